```python
import math
import jax
import jax.numpy as jnp
from jax import lax
import numpy as np

D_MODEL = 1024
BATCH = 8
SEQ = 4096
DEPTH = 2
DEC_BATCH = 32
DEC_SEQ = 64
PAST_LEN = 2048

CHUNK = 64
Q_BLOCK = 128
NORM_EPS = 1e-6

MLA_HEADS = 8
MLA_NOPE = 64
MLA_ROPE = 32
MLA_V = 64
MLA_Q_LORA = 384
MLA_KV_LORA = 256
MLA_QK_HEAD = MLA_NOPE + MLA_ROPE
MLA_VW = MLA_HEADS * MLA_V
ROPE_THETA = 10000.0

GDN_HEADS = 4
GDN_DK = 128
GDN_DV = 128
GDN_QK = GDN_HEADS * GDN_DK
GDN_VW = GDN_HEADS * GDN_DV
CONV_W = 4
GDN_CONV_CH = 2 * GDN_QK + GDN_VW

HGRN_HEADS = 4
HGRN_DK = 128
HGRN_DV = 128
HGRN_KW = HGRN_HEADS * HGRN_DK
HGRN_VW = HGRN_HEADS * HGRN_DV

N_BRANCH = 3
SPLIT_SIZES = (MLA_Q_LORA, MLA_KV_LORA, MLA_ROPE,
               GDN_QK, GDN_QK, GDN_VW, GDN_VW, GDN_HEADS, GDN_HEADS,
               HGRN_KW, HGRN_KW, HGRN_VW, HGRN_VW,
               N_BRANCH * D_MODEL)
IN_COLS = (MLA_Q_LORA + MLA_KV_LORA + MLA_ROPE + 2 * GDN_QK + 2 * GDN_VW + 2 * GDN_HEADS
           + 2 * HGRN_KW + 2 * HGRN_VW + N_BRANCH * D_MODEL)

FF_DENSE = 2816
N_EXPERTS = 8
TOP_K = 2
FF_EXPERT = 1408
MOE_BLOCK = 256
N_DENSE = (DEPTH + 1) // 2
N_MOE = DEPTH // 2

kernel_name = 'hybrid_mla_gdn_hgrn2_stream_step'


def rmsnorm(x, g):
    xf = x.astype(jnp.float32)
    y = xf * lax.rsqrt(jnp.mean(xf * xf, axis=-1, keepdims=True) + NORM_EPS)
    return (y * g.astype(jnp.float32)).astype(x.dtype)


def l2norm(x):
    return x * lax.rsqrt(jnp.sum(x * x, axis=-1, keepdims=True) + NORM_EPS)


def rope(x, pos):
    half = x.shape[-1] // 2
    inv = ROPE_THETA ** (-jnp.arange(half, dtype=jnp.float32) / half)
    ang = pos.astype(jnp.float32)[:, None] * inv[None, :]
    shape = (1, ang.shape[0]) + (1,) * (x.ndim - 3) + (half,)
    cos = jnp.cos(ang).reshape(shape)
    sin = jnp.sin(ang).reshape(shape)
    xf = x.astype(jnp.float32)
    x1, x2 = xf[..., :half], xf[..., half:]
    return jnp.concatenate([x1 * cos - x2 * sin, x1 * sin + x2 * cos], axis=-1).astype(x.dtype)


def causal_mask(c, strict):
    t = jnp.arange(c)
    return (t[:, None] > t[None, :]) if strict else (t[:, None] >= t[None, :])


def to_chunks(t, c):
    b, l, h = t.shape[:3]
    t = t.reshape((b, l // c, c, h) + t.shape[3:])
    return jnp.moveaxis(jnp.moveaxis(t, 1, 0), 3, 2)


def from_chunks(o):
    n, b, h, c, d = o.shape
    return jnp.transpose(o, (1, 0, 3, 2, 4)).reshape(b, n * c, h, d)


def block_causal_attention(q_nope, q_rope, q_pos, k_nope, k_rope, v, k_pos):
    b, t, h, _ = q_nope.shape
    qb = min(Q_BLOCK, t)
    nb = t // qb
    scale = MLA_QK_HEAD ** -0.5
    k_chunk = k_pos // CHUNK

    def one_block(args):
        qn, qr, qp = args
        s = (jnp.einsum('bthd,bshd->bhts', qn, k_nope, preferred_element_type=jnp.float32)
             + jnp.einsum('bthd,bsd->bhts', qr, k_rope, preferred_element_type=jnp.float32)) * scale
        allowed = k_chunk[None, :] <= (qp // CHUNK)[:, None]
        s = jnp.where(allowed[None, None], s, -jnp.inf)
        p = jax.nn.softmax(s, axis=-1).astype(v.dtype)
        return jnp.einsum('bhts,bshd->bthd', p, v)

    def blocks(x):
        return jnp.moveaxis(x.reshape((b, nb, qb) + x.shape[2:]), 1, 0)

    out = lax.map(one_block, (blocks(q_nope), blocks(q_rope), q_pos.reshape(nb, qb)))
    return jnp.moveaxis(out, 0, 1).reshape(b, t, h, -1)


def mla_branch(cq_raw, ckv_raw, kr_raw, pos, past_ckv, past_kr,
               q_norm_g, w_q_up, kv_norm_g, w_kv_up, q_head_g, k_head_g, w_o):
    b, t, _ = cq_raw.shape
    q = (rmsnorm(cq_raw, q_norm_g) @ w_q_up).reshape(b, t, MLA_HEADS, MLA_QK_HEAD)
    q_nope = rmsnorm(q[..., :MLA_NOPE], q_head_g[:MLA_NOPE])
    q_rope = rope(rmsnorm(q[..., MLA_NOPE:], q_head_g[MLA_NOPE:]), pos)
    ckv = rmsnorm(ckv_raw, kv_norm_g)
    kr = rope(rmsnorm(kr_raw, k_head_g[MLA_NOPE:]), pos)
    if past_ckv is None:
        ckv_all, kr_all, k_pos = ckv, kr, pos
    else:
        p = past_ckv.shape[1]
        ckv_all = jnp.concatenate([past_ckv.astype(ckv.dtype), ckv], axis=1)
        kr_all = jnp.concatenate([past_kr.astype(kr.dtype), kr], axis=1)
        k_pos = jnp.concatenate([jnp.arange(p, dtype=jnp.int32), pos])
    s = ckv_all.shape[1]
    kv = (ckv_all @ w_kv_up).reshape(b, s, MLA_HEADS, MLA_NOPE + MLA_V)
    k_nope = rmsnorm(kv[..., :MLA_NOPE], k_head_g[:MLA_NOPE])
    v = kv[..., MLA_NOPE:]
    o = block_causal_attention(q_nope, q_rope, pos, k_nope, kr_all, v, k_pos)
    return o.reshape(b, t, MLA_VW) @ w_o, ckv, kr


def causal_short_conv(x_raw, conv_state, w):
    t = x_raw.shape[1]
    xp = jnp.concatenate([conv_state.astype(x_raw.dtype), x_raw], axis=1)
    y = xp[:, 0:t, :] * w[0]
    for j in range(1, CONV_W):
        y = y + xp[:, j:j + t, :] * w[j]
    return jax.nn.silu(y), xp[:, t:, :]


def gated_delta_rule(q, k, v, g, beta, s0):
    b, l, h, dk = q.shape
    c = min(CHUNK, l)
    qc, kc, vc, gc, bc = (to_chunks(t, c) for t in (q, k, v, g, beta))
    gam = jnp.cumsum(gc, axis=-1)
    decay = jnp.exp(jnp.where(causal_mask(c, False), gam[..., :, None] - gam[..., None, :], -jnp.inf))
    kk = jnp.einsum('nbhtd,nbhsd->nbhts', kc, kc)
    a_mat = jnp.where(causal_mask(c, True), bc[..., :, None] * kk * decay, 0.0)
    m = a_mat + jnp.eye(c, dtype=a_mat.dtype)
    u = lax.linalg.triangular_solve(m, vc * bc[..., None], left_side=True, lower=True, unit_diagonal=True)
    w = lax.linalg.triangular_solve(m, kc * (bc * jnp.exp(gam))[..., None], left_side=True, lower=True,
                                    unit_diagonal=True)
    qk = jnp.einsum('nbhtd,nbhsd->nbhts', qc, kc) * decay
    q_dec = qc * jnp.exp(gam)[..., None]
    k_dec = kc * jnp.exp(gam[..., -1:] - gam)[..., None]
    g_last = jnp.exp(gam[..., -1])

    def step(s, xs):
        u_n, w_n, qk_n, qd_n, kd_n, gl_n = xs
        v_new = u_n - jnp.einsum('bhcd,bhde->bhce', w_n, s)
        o = jnp.einsum('bhcd,bhde->bhce', qd_n, s) + jnp.einsum('bhts,bhse->bhte', qk_n, v_new)
        s = gl_n[..., None, None] * s + jnp.einsum('bhcd,bhce->bhde', kd_n, v_new)
        return s, o

    s_fin, o = lax.scan(step, s0, (u, w, qk, q_dec, k_dec, g_last))
    return from_chunks(o), s_fin


def gla_recurrence(q, k, v, g, s0):
    c = min(CHUNK, q.shape[1])
    qc, kc, vc, gc = (to_chunks(t, c) for t in (q, k, v, g))
    incl = causal_mask(c, False)[:, :, None]

    def step(s, xs):
        q_n, k_n, v_n, g_n = xs
        cb = jnp.cumsum(g_n, axis=2)
        dec = jnp.exp(jnp.where(incl, cb[:, :, :, None, :] - cb[:, :, None, :, :], -jnp.inf))
        att = jnp.einsum('bhtd,bhsd,bhtsd->bhts', q_n, k_n, dec)
        o = jnp.einsum('bhtd,bhde->bhte', q_n * jnp.exp(cb), s) + jnp.einsum('bhts,bhse->bhte', att, v_n)
        c_last = cb[:, :, -1:, :]
        s = jnp.exp(c_last[:, :, 0, :])[..., None] * s + jnp.einsum('bhsd,bhse->bhde', k_n * jnp.exp(c_last - cb), v_n)
        return s, o

    s_fin, o = lax.scan(step, s0, (qc, kc, vc, gc))
    return from_chunks(o), s_fin


def gdn_branch(q_raw, k_raw, v_raw, z_raw, a_raw, beta_raw, conv_state, s0,
               conv_w, a_log, dt_bias, norm_g, w_o):
    b, t, _ = q_raw.shape
    f32 = jnp.float32
    qkv, conv_new = causal_short_conv(jnp.concatenate([q_raw, k_raw, v_raw], axis=-1), conv_state, conv_w)
    qkv = qkv.astype(f32)
    q = l2norm(qkv[..., :GDN_QK].reshape(b, t, GDN_HEADS, GDN_DK)) * (GDN_DK ** -0.5)
    k = l2norm(qkv[..., GDN_QK:2 * GDN_QK].reshape(b, t, GDN_HEADS, GDN_DK))
    v = qkv[..., 2 * GDN_QK:].reshape(b, t, GDN_HEADS, GDN_DV)
    beta = jax.nn.sigmoid(beta_raw.astype(f32))
    g = -jnp.exp(a_log.astype(f32)) * jax.nn.softplus(a_raw.astype(f32) + dt_bias.astype(f32))
    o, s_new = gated_delta_rule(q, k, v, g, beta, s0.astype(f32))
    o = rmsnorm(o, norm_g) * jax.nn.silu(z_raw.astype(f32).reshape(b, t, GDN_HEADS, GDN_DV))
    return o.reshape(b, t, GDN_VW).astype(w_o.dtype) @ w_o, s_new.astype(q_raw.dtype), conv_new


def hgrn2_branch(q_raw, f_raw, i_raw, gate_raw, lb, s0, norm_g, w_o):
    b, t, _ = q_raw.shape
    f32 = jnp.float32
    shp = (b, t, HGRN_HEADS, HGRN_DK)
    q = jax.nn.silu(q_raw.astype(f32)).reshape(shp) * (HGRN_DK ** -0.5)
    f = lb + (1.0 - lb) * jax.nn.sigmoid(f_raw.astype(f32))
    k = (1.0 - f).reshape(shp)
    g = jnp.log(f).reshape(shp)
    v = i_raw.astype(f32).reshape(b, t, HGRN_HEADS, HGRN_DV)
    o, s_new = gla_recurrence(q, k, v, g, s0.astype(f32))
    o = rmsnorm(o, norm_g) * jax.nn.silu(gate_raw.astype(f32).reshape(b, t, HGRN_HEADS, HGRN_DV))
    return o.reshape(b, t, HGRN_VW).astype(w_o.dtype) @ w_o, s_new.astype(q_raw.dtype)


def swiglu(h, w_gate, w_up, w_down):
    return (jax.nn.silu(h @ w_gate) * (h @ w_up)) @ w_down


def moe_swiglu(h, w_router, b_router, w_gate, w_up, w_down):
    bsz, t, d = h.shape
    xt = h.reshape(-1, d)
    n = xt.shape[0]
    logits = jnp.dot(xt, w_router, preferred_element_type=jnp.float32) + b_router.astype(jnp.float32)
    top_logit, top_e = lax.top_k(logits, TOP_K)
    top_w = jax.nn.softmax(top_logit, axis=-1)
    n_assign = n * TOP_K
    flat_e = top_e.reshape(-1)
    flat_tok = jnp.repeat(jnp.arange(n, dtype=jnp.int32), TOP_K)
    flat_w = top_w.reshape(-1)
    order = jnp.argsort(flat_e)
    se, stok, sw = flat_e[order], flat_tok[order], flat_w[order]
    counts = jnp.bincount(flat_e, length=N_EXPERTS)
    starts = jnp.cumsum(counts) - counts
    padded = (counts + MOE_BLOCK - 1) // MOE_BLOCK * MOE_BLOCK
    pad_end = jnp.cumsum(padded)
    dest = pad_end[se] - padded[se] + (jnp.arange(n_assign, dtype=jnp.int32) - starts[se])
    n_blocks = -(-(n_assign + N_EXPERTS * (MOE_BLOCK - 1)) // MOE_BLOCK)
    tok_buf = jnp.full((n_blocks * MOE_BLOCK,), n, jnp.int32).at[dest].set(stok)
    w_buf = jnp.zeros((n_blocks * MOE_BLOCK,), jnp.float32).at[dest].set(sw)
    blk_start = jnp.arange(n_blocks, dtype=jnp.int32) * MOE_BLOCK
    blk_e = jnp.minimum(jnp.sum(blk_start[:, None] >= pad_end[None, :], axis=1), N_EXPERTS - 1)
    x_pad = jnp.concatenate([xt, jnp.zeros((1, d), xt.dtype)], axis=0)

    def expert_block(args):
        idx, e = args
        xb = x_pad[idx]
        return (jax.nn.silu(xb @ w_gate[e]) * (xb @ w_up[e])) @ w_down[e]

    yb = lax.map(expert_block, (tok_buf.reshape(n_blocks, MOE_BLOCK), blk_e))
    y = jnp.zeros((n + 1, d), jnp.float32).at[tok_buf].add(yb.reshape(-1, d).astype(jnp.float32) * w_buf[:, None])
    return y[:n].astype(h.dtype).reshape(bsz, t, d)


def trunk_layer(x, pos, lw, past):
    b, t, _ = x.shape
    h = rmsnorm(x, lw['mixer_norm_g'])
    proj = h @ lw['w_in']
    (cq, ckv, kr, gq, gk, gv, gz, ga, gb, hq, hf, hi, hgate, gate_logits) = jnp.split(
        proj, np.cumsum(SPLIT_SIZES)[:-1].tolist(), axis=-1)
    o_a, new_ckv, new_kr = mla_branch(cq, ckv, kr, pos, past['ckv'], past['kr'],
                                      lw['mla_q_norm_g'], lw['mla_w_q_up'], lw['mla_kv_norm_g'],
                                      lw['mla_w_kv_up'], lw['mla_q_head_norm_g'], lw['mla_k_head_norm_g'],
                                      lw['mla_w_o'])
    o_b, gdn_s, gdn_conv = gdn_branch(gq, gk, gv, gz, ga, gb, past['conv'], past['gdn'],
                                      lw['gdn_conv_w'], lw['gdn_a_log'], lw['gdn_dt_bias'],
                                      lw['gdn_norm_g'], lw['gdn_w_o'])
    o_c, hgrn_s = hgrn2_branch(hq, hf, hi, hgate, lw['hgrn_lb'], past['hgrn'], lw['hgrn_norm_g'], lw['hgrn_w_o'])
    gates = jax.nn.sigmoid(gate_logits.astype(jnp.float32)).reshape(b, t, N_BRANCH, D_MODEL)
    mixed = gates[:, :, 0] * o_a + gates[:, :, 1] * o_b + gates[:, :, 2] * o_c
    x = x + mixed.astype(x.dtype) @ lw['w_out']
    h2 = rmsnorm(x, lw['ffn_norm_g'])
    if lw['moe'] is None:
        y = swiglu(h2, *lw['dense'])
    else:
        y = moe_swiglu(h2, *lw['moe'])
    return x + y, (new_ckv, new_kr, gdn_s, gdn_conv, hgrn_s)


def setup_inputs(seed: int = 0) -> dict:
    key = jax.random.key(seed)
    ks = iter(jax.random.split(key, 48))

    def nrm(shape, scale):
        return scale * jax.random.normal(next(ks), shape, jnp.float32)

    def gain(shape):
        return 1.0 + 0.02 * jax.random.normal(next(ks), shape, jnp.float32)

    d = {}
    d['x_prompt'] = nrm((BATCH, SEQ, D_MODEL), 1.0)
    d['x_sample'] = nrm((DEC_BATCH, DEC_SEQ, D_MODEL), 1.0)
    d['cache_mla_ckv'] = nrm((DEPTH, DEC_BATCH, PAST_LEN, MLA_KV_LORA), 1.0)
    d['cache_mla_krope'] = nrm((DEPTH, DEC_BATCH, PAST_LEN, MLA_ROPE), 1.0)
    d['state_gdn'] = nrm((DEPTH, DEC_BATCH, GDN_HEADS, GDN_DK, GDN_DV), 0.3)
    d['state_gdn_conv'] = nrm((DEPTH, DEC_BATCH, CONV_W - 1, GDN_CONV_CH), 1.0)
    d['state_hgrn'] = nrm((DEPTH, DEC_BATCH, HGRN_HEADS, HGRN_DK, HGRN_DV), 0.3)
    d['mixer_norm_g'] = gain((DEPTH, D_MODEL))
    d['w_in'] = nrm((DEPTH, D_MODEL, IN_COLS), D_MODEL ** -0.5)
    d['mla_q_norm_g'] = gain((DEPTH, MLA_Q_LORA))
    d['mla_w_q_up'] = nrm((DEPTH, MLA_Q_LORA, MLA_HEADS * MLA_QK_HEAD), MLA_Q_LORA ** -0.5)
    d['mla_kv_norm_g'] = gain((DEPTH, MLA_KV_LORA))
    d['mla_w_kv_up'] = nrm((DEPTH, MLA_KV_LORA, MLA_HEADS * (MLA_NOPE + MLA_V)), MLA_KV_LORA ** -0.5)
    d['mla_q_head_norm_g'] = gain((DEPTH, MLA_QK_HEAD))
    d['mla_k_head_norm_g'] = gain((DEPTH, MLA_QK_HEAD))
    d['mla_w_o'] = nrm((DEPTH, MLA_VW, D_MODEL), MLA_VW ** -0.5)
    d['gdn_conv_w'] = nrm((DEPTH, CONV_W, GDN_CONV_CH), CONV_W ** -0.5)
    d['gdn_a_log'] = jnp.log(jax.random.uniform(next(ks), (DEPTH, GDN_HEADS), jnp.float32, 1.0, 16.0))
    dt = jnp.exp(jax.random.uniform(next(ks), (DEPTH, GDN_HEADS), jnp.float32, math.log(1e-3), math.log(1e-1)))
    d['gdn_dt_bias'] = dt + jnp.log(-jnp.expm1(-dt))
    d['gdn_norm_g'] = gain((DEPTH, GDN_DV))
    d['gdn_w_o'] = nrm((DEPTH, GDN_VW, D_MODEL), GDN_VW ** -0.5)
    d['hgrn_lb_logits'] = nrm((DEPTH, HGRN_KW), 0.5)
    d['hgrn_norm_g'] = gain((DEPTH, HGRN_DV))
    d['hgrn_w_o'] = nrm((DEPTH, HGRN_VW, D_MODEL), HGRN_VW ** -0.5)
    d['w_out'] = nrm((DEPTH, D_MODEL, D_MODEL), D_MODEL ** -0.5)
    d['ffn_norm_g'] = gain((DEPTH, D_MODEL))
    d['dense_w_gate'] = nrm((N_DENSE, D_MODEL, FF_DENSE), D_MODEL ** -0.5)
    d['dense_w_up'] = nrm((N_DENSE, D_MODEL, FF_DENSE), D_MODEL ** -0.5)
    d['dense_w_down'] = nrm((N_DENSE, FF_DENSE, D_MODEL), FF_DENSE ** -0.5)
    d['moe_w_router'] = nrm((N_MOE, D_MODEL, N_EXPERTS), D_MODEL ** -0.5)
    d['moe_b_router'] = nrm((N_MOE, N_EXPERTS), 0.01)
    d['moe_w_gate'] = nrm((N_MOE, N_EXPERTS, D_MODEL, FF_EXPERT), D_MODEL ** -0.5)
    d['moe_w_up'] = nrm((N_MOE, N_EXPERTS, D_MODEL, FF_EXPERT), D_MODEL ** -0.5)
    d['moe_w_down'] = nrm((N_MOE, N_EXPERTS, FF_EXPERT, D_MODEL), FF_EXPERT ** -0.5)
    return d


def reference(x_prompt, x_sample, cache_mla_ckv, cache_mla_krope, state_gdn, state_gdn_conv, state_hgrn,
              mixer_norm_g, w_in, mla_q_norm_g, mla_w_q_up, mla_kv_norm_g, mla_w_kv_up,
              mla_q_head_norm_g, mla_k_head_norm_g, mla_w_o,
              gdn_conv_w, gdn_a_log, gdn_dt_bias, gdn_norm_g, gdn_w_o,
              hgrn_lb_logits, hgrn_norm_g, hgrn_w_o, w_out, ffn_norm_g,
              dense_w_gate, dense_w_up, dense_w_down,
              moe_w_router, moe_b_router, moe_w_gate, moe_w_up, moe_w_down):
    lb_soft = jax.nn.softmax(hgrn_lb_logits.astype(jnp.float32), axis=0)
    hgrn_lb = jnp.cumsum(lb_soft, axis=0) - lb_soft[0]
    b_p, t_p = x_prompt.shape[:2]
    t_s = x_sample.shape[1]
    past_len = cache_mla_ckv.shape[2]
    pos_p = jnp.arange(t_p, dtype=jnp.int32)
    pos_s = past_len + jnp.arange(t_s, dtype=jnp.int32)
    past_p = dict(ckv=None, kr=None,
                  gdn=jnp.zeros((b_p, GDN_HEADS, GDN_DK, GDN_DV), x_prompt.dtype),
                  conv=jnp.zeros((b_p, CONV_W - 1, GDN_CONV_CH), x_prompt.dtype),
                  hgrn=jnp.zeros((b_p, HGRN_HEADS, HGRN_DK, HGRN_DV), x_prompt.dtype))
    xp, xs = x_prompt, x_sample
    st_p, st_s = [], []
    for l in range(DEPTH):
        lw = dict(
            mixer_norm_g=mixer_norm_g[l], w_in=w_in[l],
            mla_q_norm_g=mla_q_norm_g[l], mla_w_q_up=mla_w_q_up[l],
            mla_kv_norm_g=mla_kv_norm_g[l], mla_w_kv_up=mla_w_kv_up[l],
            mla_q_head_norm_g=mla_q_head_norm_g[l], mla_k_head_norm_g=mla_k_head_norm_g[l],
            mla_w_o=mla_w_o[l],
            gdn_conv_w=gdn_conv_w[l], gdn_a_log=gdn_a_log[l], gdn_dt_bias=gdn_dt_bias[l],
            gdn_norm_g=gdn_norm_g[l], gdn_w_o=gdn_w_o[l],
            hgrn_lb=hgrn_lb[l], hgrn_norm_g=hgrn_norm_g[l], hgrn_w_o=hgrn_w_o[l],
            w_out=w_out[l], ffn_norm_g=ffn_norm_g[l],
            dense=(dense_w_gate[l // 2], dense_w_up[l // 2], dense_w_down[l // 2]) if l % 2 == 0 else None,
            moe=None if l % 2 == 0 else (moe_w_router[l // 2], moe_b_router[l // 2], moe_w_gate[l // 2],
                                        moe_w_up[l // 2], moe_w_down[l // 2]),
        )
        past_s = dict(ckv=cache_mla_ckv[l], kr=cache_mla_krope[l], gdn=state_gdn[l],
                      conv=state_gdn_conv[l], hgrn=state_hgrn[l])
        xp, sp = trunk_layer(xp, pos_p, lw, past_p)
        xs, ss = trunk_layer(xs, pos_s, lw, past_s)
        st_p.append(sp)
        st_s.append(ss)

    def stack(lst, i):
        return jnp.stack([s[i] for s in lst], axis=0)

    return (xp, xs,
            stack(st_p, 0), stack(st_p, 1), stack(st_p, 2), stack(st_p, 3), stack(st_p, 4),
            stack(st_s, 0), stack(st_s, 1), stack(st_s, 2), stack(st_s, 3), stack(st_s, 4))
```

```python
import functools

import jax
import jax.numpy as jnp
import numpy as np
from jax import lax
from jax.experimental import pallas as pl
from jax.experimental.pallas import tpu as pltpu

F32 = jnp.float32
BF16 = jnp.bfloat16

D_MODEL = 1024
CHUNK = 64
NORM_EPS = 1e-6

MLA_HEADS = 8
MLA_NOPE = 64
MLA_ROPE = 32
MLA_V = 64
MLA_Q_LORA = 384
MLA_KV_LORA = 256
MLA_QK_HEAD = MLA_NOPE + MLA_ROPE
MLA_VW = MLA_HEADS * MLA_V
ROPE_THETA = 10000.0
HEAD_PAD = 128
QK_W = MLA_HEADS * HEAD_PAD

GDN_HEADS = 4
GDN_DK = 128
GDN_DV = 128
GDN_QK = GDN_HEADS * GDN_DK
GDN_VW = GDN_HEADS * GDN_DV
CONV_W = 4

HGRN_HEADS = 4
HGRN_DK = 128
HGRN_DV = 128
HGRN_KW = HGRN_HEADS * HGRN_DK

N_BRANCH = 3
SPLIT_SIZES = (MLA_Q_LORA, MLA_KV_LORA, MLA_ROPE,
               GDN_QK, GDN_QK, GDN_VW, GDN_VW, GDN_HEADS, GDN_HEADS,
               HGRN_KW, HGRN_KW, HGRN_KW, HGRN_KW,
               N_BRANCH * D_MODEL)

N_EXPERTS = 8
FF_EXPERT = 1408

P_COLS = 8192
P_TN = 1024
COL_CKV = MLA_Q_LORA
COL_KR = MLA_Q_LORA + MLA_KV_LORA
COL_GAB = COL_KR + 128
COL_GDN = 1024
ROPE_LANE0 = MLA_NOPE

VMEM_LIMIT = 56 * 1024 * 1024


def _cparams(sem):
    return pltpu.CompilerParams(dimension_semantics=sem, vmem_limit_bytes=VMEM_LIMIT)


def _row_tile(n, cap):
    for t in (2048, 1024, 512, 256, 128, 64, 32, 16, 8):
        if t <= cap and n % t == 0:
            return t
    raise ValueError(f"no row tile for {n}")


def _sigmoid(x):
    return 1.0 / (1.0 + jnp.exp(-x))


def _silu(x):
    return x * _sigmoid(x)


def _rms(x, g):
    ms = jnp.mean(x * x, axis=-1, keepdims=True)
    return x * lax.rsqrt(ms + NORM_EPS) * g


def _in_proj_kernel(x_ref, g_ref, w_ref, o_ref, h_ref):
    @pl.when(pl.program_id(1) == 0)
    def _():
        h_ref[...] = _rms(x_ref[...], g_ref[...]).astype(BF16)

    o_ref[...] = jnp.dot(h_ref[...], w_ref[...], preferred_element_type=F32)


def _in_proj(x, g, w):
    n = x.shape[0]
    tm = _row_tile(n, 1024)
    return pl.pallas_call(
        _in_proj_kernel,
        grid=(n // tm, P_COLS // P_TN),
        in_specs=[pl.BlockSpec((tm, D_MODEL), lambda i, j: (i, 0)),
                  pl.BlockSpec((1, D_MODEL), lambda i, j: (0, 0)),
                  pl.BlockSpec((D_MODEL, P_TN), lambda i, j: (0, j))],
        out_specs=pl.BlockSpec((tm, P_TN), lambda i, j: (i, j)),
        out_shape=jax.ShapeDtypeStruct((n, P_COLS), F32),
        scratch_shapes=[pltpu.VMEM((tm, D_MODEL), BF16)],
        compiler_params=_cparams(("parallel", "arbitrary")),
        name="in_proj",
    )(x, g, w)


def _rope(x, c, s1, s2):
    return x * c + pltpu.roll(x, HEAD_PAD - 16, 1) * s1 + pltpu.roll(x, 16, 1) * s2


def _mla_pre_kernel(p_ref, gq_ref, gkv_ref, wq_ref, hq_ref, hk_ref, c_ref, s1_ref, s2_ref,
                    q_ref, ckv_ref, kr_ref, *, scale):
    c, s1, s2 = c_ref[...], s1_ref[...], s2_ref[...]
    ckv_ref[...] = _rms(p_ref[:, COL_CKV:COL_KR], gkv_ref[...])

    kr = p_ref[:, COL_KR:COL_KR + HEAD_PAD]
    kr_ms = jnp.sum(kr * kr, axis=-1, keepdims=True) * (1.0 / MLA_ROPE)
    kr_ref[...] = _rope(kr * lax.rsqrt(kr_ms + NORM_EPS) * hk_ref[...], c, s1, s2)

    cq = _rms(p_ref[:, 0:MLA_Q_LORA], gq_ref[...]).astype(BF16)
    q = jnp.dot(cq, wq_ref[...], preferred_element_type=F32)
    lane = lax.broadcasted_iota(jnp.int32, (1, HEAD_PAD), 1)
    is_nope = lane < MLA_NOPE
    hq = hq_ref[...]
    for h in range(MLA_HEADS):
        qh = q[:, h * HEAD_PAD:(h + 1) * HEAD_PAD]
        sq = qh * qh
        ms_n = jnp.sum(jnp.where(is_nope, sq, 0.0), axis=-1, keepdims=True) * (1.0 / MLA_NOPE)
        ms_r = jnp.sum(jnp.where(is_nope, 0.0, sq), axis=-1, keepdims=True) * (1.0 / MLA_ROPE)
        inv = jnp.where(is_nope, lax.rsqrt(ms_n + NORM_EPS), lax.rsqrt(ms_r + NORM_EPS))
        qh = _rope(qh * inv * hq, c, s1, s2) * scale
        q_ref[:, h * HEAD_PAD:(h + 1) * HEAD_PAD] = qh.astype(BF16)


def _mla_pre(p, t_seq, gq, gkv, wq, hq, hk, tabs):
    n = p.shape[0]
    tm = _row_tile(n, 512)
    c, s1, s2 = tabs
    if tm > t_seq:
        c, s1, s2 = (jnp.tile(t, (tm // t_seq, 1)) for t in (c, s1, s2))
    n_tab = c.shape[0] // tm
    tab_spec = pl.BlockSpec((tm, HEAD_PAD), lambda i: (i % n_tab, 0))
    vec = lambda w: pl.BlockSpec((1, w), lambda i: (0, 0))
    return pl.pallas_call(
        functools.partial(_mla_pre_kernel, scale=MLA_QK_HEAD ** -0.5),
        grid=(n // tm,),
        in_specs=[pl.BlockSpec((tm, 1024), lambda i: (i, 0)),
                  vec(MLA_Q_LORA), vec(MLA_KV_LORA),
                  pl.BlockSpec((MLA_Q_LORA, QK_W), lambda i: (0, 0)),
                  vec(HEAD_PAD), vec(HEAD_PAD), tab_spec, tab_spec, tab_spec],
        out_specs=[pl.BlockSpec((tm, QK_W), lambda i: (i, 0)),
                   pl.BlockSpec((tm, MLA_KV_LORA), lambda i: (i, 0)),
                   pl.BlockSpec((tm, HEAD_PAD), lambda i: (i, 0))],
        out_shape=[jax.ShapeDtypeStruct((n, QK_W), BF16),
                   jax.ShapeDtypeStruct((n, MLA_KV_LORA), F32),
                   jax.ShapeDtypeStruct((n, HEAD_PAD), F32)],
        compiler_params=_cparams(("parallel",)),
        name="mla_pre",
    )(p, gq, gkv, wq, hq, hk, c, s1, s2)


def _kv_up_kernel(ckv_ref, kr_ref, wk_ref, wv_ref, hk_ref, k_ref, v_ref):
    c = ckv_ref[...].astype(BF16)
    k = jnp.dot(c, wk_ref[...], preferred_element_type=F32)
    kr = kr_ref[...]
    hk = hk_ref[...]
    for h in range(MLA_HEADS):
        kh = k[:, h * HEAD_PAD:(h + 1) * HEAD_PAD]
        ms = jnp.sum(kh * kh, axis=-1, keepdims=True) * (1.0 / MLA_NOPE)
        k_ref[:, h * HEAD_PAD:(h + 1) * HEAD_PAD] = (kh * lax.rsqrt(ms + NORM_EPS) * hk + kr).astype(BF16)
    v_ref[...] = jnp.dot(c, wv_ref[...], preferred_element_type=F32).astype(BF16)


def _kv_up(ckv, kr, wk, wv, hk):
    n = ckv.shape[0]
    tm = _row_tile(n, 512)
    return pl.pallas_call(
        _kv_up_kernel,
        grid=(n // tm,),
        in_specs=[pl.BlockSpec((tm, MLA_KV_LORA), lambda i: (i, 0)),
                  pl.BlockSpec((tm, HEAD_PAD), lambda i: (i, 0)),
                  pl.BlockSpec((MLA_KV_LORA, QK_W), lambda i: (0, 0)),
                  pl.BlockSpec((MLA_KV_LORA, MLA_VW), lambda i: (0, 0)),
                  pl.BlockSpec((1, HEAD_PAD), lambda i: (0, 0))],
        out_specs=[pl.BlockSpec((tm, QK_W), lambda i: (i, 0)),
                   pl.BlockSpec((tm, MLA_VW), lambda i: (i, 0))],
        out_shape=[jax.ShapeDtypeStruct((n, QK_W), BF16),
                   jax.ShapeDtypeStruct((n, MLA_VW), BF16)],
        compiler_params=_cparams(("parallel",)),
        name="kv_up",
    )(ckv, kr, wk, wv, hk)


def _attn_kernel(q_ref, k_ref, v_ref, o_ref, m_ref, l_ref, acc_ref, *, tq, tk, causal):
    m_ref[...] = jnp.full(m_ref.shape, -jnp.inf, F32)
    l_ref[...] = jnp.zeros(l_ref.shape, F32)
    acc_ref[...] = jnp.zeros(acc_ref.shape, F32)

    def block(start, masked):
        if masked:
            rq = lax.broadcasted_iota(jnp.int32, (tq, tk), 0) // CHUNK
            ck = lax.broadcasted_iota(jnp.int32, (tq, tk), 1) // CHUNK
            allowed = ck <= rq
        for h in range(MLA_HEADS):
            qh = q_ref[0, :, h * HEAD_PAD:(h + 1) * HEAD_PAD]
            kh = k_ref[0, pl.ds(start, tk), h * HEAD_PAD:(h + 1) * HEAD_PAD]
            vh = v_ref[0, pl.ds(start, tk), h * MLA_V:(h + 1) * MLA_V]
            s = lax.dot_general(qh, kh, (((1,), (1,)), ((), ())), preferred_element_type=F32)
            if masked:
                s = jnp.where(allowed, s, -jnp.inf)
            m_prev = m_ref[h]
            m_new = jnp.maximum(m_prev, jnp.max(s, axis=-1, keepdims=True))
            alpha = jnp.exp(m_prev - m_new)
            p = jnp.exp(s - m_new)
            l_ref[h] = alpha * l_ref[h] + jnp.sum(p, axis=-1, keepdims=True)
            pv = jnp.dot(p.astype(BF16), vh, preferred_element_type=F32)
            acc_ref[h] = alpha * acc_ref[h] + pv
            m_ref[h] = m_new

    if causal:
        qi = pl.program_id(1)

        def body(j, carry):
            block(pl.multiple_of(j * tk, tk), False)
            return carry

        lax.fori_loop(0, qi, body, 0)
        block(pl.multiple_of(qi * tk, tk), True)
    else:
        block(0, False)

    for h in range(MLA_HEADS):
        o_ref[0, :, h * MLA_V:(h + 1) * MLA_V] = acc_ref[h] / l_ref[h]


def _attention(q, k, v, causal):
    b, t, _ = q.shape
    s = k.shape[1]
    if causal:
        tq = tk = min(256, t)
    else:
        tq, tk = t, s
    return pl.pallas_call(
        functools.partial(_attn_kernel, tq=tq, tk=tk, causal=causal),
        grid=(b, t // tq),
        in_specs=[pl.BlockSpec((1, tq, QK_W), lambda i, j: (i, j, 0)),
                  pl.BlockSpec((1, s, QK_W), lambda i, j: (i, 0, 0)),
                  pl.BlockSpec((1, s, MLA_VW), lambda i, j: (i, 0, 0))],
        out_specs=pl.BlockSpec((1, tq, MLA_VW), lambda i, j: (i, j, 0)),
        out_shape=jax.ShapeDtypeStruct((b, t, MLA_VW), F32),
        scratch_shapes=[pltpu.VMEM((MLA_HEADS, tq, 1), F32),
                        pltpu.VMEM((MLA_HEADS, tq, 1), F32),
                        pltpu.VMEM((MLA_HEADS, tq, MLA_V), F32)],
        compiler_params=_cparams(("parallel", "arbitrary")),
        name="mla_attn",
    )(q, k, v)


def _row_iota(shape):
    return lax.broadcasted_iota(jnp.int32, shape, 0)


def _segment_scans(g):
    row = _row_iota(g.shape)
    pre, tot = g, g
    out = [(pre, tot)]
    m = 1
    while m < CHUNK:
        upper = (row // m) % 2 == 1
        from_lower = pltpu.roll(tot, m, 0)
        from_upper = pltpu.roll(tot, CHUNK - m, 0)
        pre = pre + jnp.where(upper, from_lower, 0.0)
        tot = tot + jnp.where(upper, from_lower, from_upper)
        out.append((pre, tot))
        m *= 2
    return out


def _dot_nt(a, b):
    return lax.dot_general(a, b, (((1,), (1,)), ((), ())), preferred_element_type=F32)


def _dot_tn(a, b):
    return lax.dot_general(a, b, (((0,), (0,)), ((), ())), preferred_element_type=F32)


def _gdn_kernel(q_ref, k_ref, v_ref, z_ref, ab_ref, cw_ref, cs_ref, s0_ref, alog_ref, dt_ref, ng_ref,
                o_ref, sf_ref, s_ref, carry_ref):
    c_idx = pl.program_id(1)

    @pl.when(c_idx == 0)
    def _():
        s_ref[...] = s0_ref[0]
        carry_ref[...] = cs_ref[0]

    def conv(x_ref, j):
        x = x_ref[0]
        xp = jnp.concatenate([carry_ref[j], x], axis=0)
        w = cw_ref[:, j * GDN_QK:(j + 1) * GDN_QK]
        y = x * w[3:4]
        for d in range(1, CONV_W):
            y = y + xp[8 - d:8 - d + CHUNK] * w[3 - d:4 - d]
        carry_ref[j] = x[CHUNK - 8:]
        return _silu(y)

    q_all, k_all, v_all = conv(q_ref, 0), conv(k_ref, 1), conv(v_ref, 2)
    z_all = z_ref[0]

    ab = ab_ref[0]
    x = ab + dt_ref[...]
    softplus = jnp.maximum(x, 0.0) + jnp.log(1.0 + jnp.exp(-jnp.abs(x)))
    g_blk = -jnp.exp(alog_ref[...]) * softplus
    gam_blk = _segment_scans(g_blk)[-1][0]
    beta_blk = _sigmoid(ab)

    row = lax.broadcasted_iota(jnp.int32, (CHUNK, CHUNK), 0)
    col = lax.broadcasted_iota(jnp.int32, (CHUNK, CHUNK), 1)
    eye = (row == col).astype(F32)
    ones = jnp.ones((CHUNK, CHUNK), F32)

    for h in range(GDN_HEADS):
        sl = slice(h * GDN_DK, (h + 1) * GDN_DK)
        q, k, v = q_all[:, sl], k_all[:, sl], v_all[:, sl]
        q = q * lax.rsqrt(jnp.sum(q * q, axis=-1, keepdims=True) + NORM_EPS) * (GDN_DK ** -0.5)
        k = k * lax.rsqrt(jnp.sum(k * k, axis=-1, keepdims=True) + NORM_EPS)
        gam = gam_blk[:, h:h + 1]
        beta = beta_blk[:, GDN_HEADS + h:GDN_HEADS + h + 1]
        gam_row = jnp.dot(ones, eye * gam, preferred_element_type=F32, precision=lax.Precision.HIGHEST)
        decay = jnp.where(row >= col, jnp.exp(jnp.minimum(gam - gam_row, 0.0)), 0.0)
        kk = _dot_nt(k, k)
        a = jnp.where(row > col, beta * kk * decay, 0.0)
        t_inv = eye - a
        pw = a
        n = 1
        while 2 * n < CHUNK:
            pw = jnp.dot(pw, pw, preferred_element_type=F32)
            t_inv = t_inv + jnp.dot(t_inv, pw, preferred_element_type=F32)
            n *= 2
        e_gam = jnp.exp(gam)
        u = jnp.dot(t_inv, v * beta, preferred_element_type=F32)
        w = jnp.dot(t_inv, k * (beta * e_gam), preferred_element_type=F32)
        qk = _dot_nt(q, k) * decay
        gam_last = gam[CHUNK - 1:CHUNK]
        k_dec = k * jnp.exp(gam_last - gam)
        s = s_ref[h]
        v_new = u - jnp.dot(w, s, preferred_element_type=F32)
        o = jnp.dot(q * e_gam, s, preferred_element_type=F32) + jnp.dot(qk, v_new, preferred_element_type=F32)
        s_ref[h] = jnp.exp(gam_last) * s + _dot_tn(k_dec, v_new)
        o_ref[0, :, sl] = _rms(o, ng_ref[...]) * _silu(z_all[:, sl])

    @pl.when(c_idx == pl.num_programs(1) - 1)
    def _():
        sf_ref[0] = s_ref[...]


def _gdn(p3, conv_w, conv_state8, s0, alog, dt, ng):
    b, t, _ = p3.shape
    nc = t // CHUNK
    blk = lambda j: pl.BlockSpec((1, CHUNK, GDN_QK), lambda i, c: (i, c, COL_GDN // GDN_QK + j))
    vec = pl.BlockSpec((1, 128), lambda i, c: (0, 0))
    return pl.pallas_call(
        _gdn_kernel,
        grid=(b, nc),
        in_specs=[blk(0), blk(1), blk(2), blk(3),
                  pl.BlockSpec((1, CHUNK, 128), lambda i, c: (i, c, COL_GAB // 128)),
                  pl.BlockSpec((CONV_W, 3 * GDN_QK), lambda i, c: (0, 0)),
                  pl.BlockSpec((1, 3, 8, GDN_QK), lambda i, c: (i, 0, 0, 0)),
                  pl.BlockSpec((1, GDN_HEADS, GDN_DK, GDN_DV), lambda i, c: (i, 0, 0, 0)),
                  vec, vec, vec],
        out_specs=[pl.BlockSpec((1, CHUNK, GDN_VW), lambda i, c: (i, c, 0)),
                   pl.BlockSpec((1, GDN_HEADS, GDN_DK, GDN_DV), lambda i, c: (i, 0, 0, 0))],
        out_shape=[jax.ShapeDtypeStruct((b, t, GDN_VW), F32),
                   jax.ShapeDtypeStruct((b, GDN_HEADS, GDN_DK, GDN_DV), F32)],
        scratch_shapes=[pltpu.VMEM((GDN_HEADS, GDN_DK, GDN_DV), F32),
                        pltpu.VMEM((3, 8, GDN_QK), F32)],
        compiler_params=_cparams(("parallel", "arbitrary")),
        name="gdn",
    )(p3, p3, p3, p3, p3, conv_w, conv_state8, s0, alog, dt, ng)


def _hgrn_kernel(q_ref, f_ref, i_ref, gate_ref, lb_ref, s0_ref, ng_ref, o_ref, sf_ref, st_ref):
    c_idx = pl.program_id(1)

    @pl.when(c_idx == 0)
    def _():
        st_ref[...] = s0_ref[0]

    row = lax.broadcasted_iota(jnp.int32, (CHUNK, CHUNK), 0)
    col = lax.broadcasted_iota(jnp.int32, (CHUNK, CHUNK), 1)
    rowv = _row_iota((CHUNK, HGRN_DK))

    for h in range(HGRN_HEADS):
        sl = slice(h * HGRN_DK, (h + 1) * HGRN_DK)
        lb = lb_ref[:, sl]
        q = _silu(q_ref[0, :, sl]) * (HGRN_DK ** -0.5)
        f = lb + (1.0 - lb) * _sigmoid(f_ref[0, :, sl])
        k = 1.0 - f
        g = jnp.log(f)
        v = i_ref[0, :, sl]
        scans = _segment_scans(g)
        cb, c_tot = scans[-1]

        att = jnp.where(row == col, _dot_nt(q, k), 0.0)
        m = 1
        lvl = 0
        while m < CHUNK:
            upper = (rowv // m) % 2 == 1
            pre_m, tot_m = scans[lvl]
            qs = jnp.where(upper, q * jnp.exp(pre_m), 0.0)
            ks = jnp.where(upper, 0.0, k * jnp.exp(tot_m - pre_m))
            att = att + jnp.where(row // (2 * m) == col // (2 * m), _dot_nt(qs, ks), 0.0)
            m *= 2
            lvl += 1

        st = st_ref[h]
        o = _dot_nt(q * jnp.exp(cb), st) + jnp.dot(att, v, preferred_element_type=F32)
        st_ref[h] = st * jnp.exp(c_tot[0:1]) + _dot_tn(v, k * jnp.exp(c_tot - cb))
        o_ref[0, :, sl] = _rms(o, ng_ref[...]) * _silu(gate_ref[0, :, sl])

    @pl.when(c_idx == pl.num_programs(1) - 1)
    def _():
        sf_ref[0] = st_ref[...]


def _hgrn(p3, lb, s0t, ng):
    b, t, _ = p3.shape
    nc = t // CHUNK
    col0 = (COL_GDN + 4 * GDN_QK) // HGRN_KW
    blk = lambda j: pl.BlockSpec((1, CHUNK, HGRN_KW), lambda i, c: (i, c, col0 + j))
    return pl.pallas_call(
        _hgrn_kernel,
        grid=(b, nc),
        in_specs=[blk(0), blk(1), blk(2), blk(3),
                  pl.BlockSpec((1, HGRN_KW), lambda i, c: (0, 0)),
                  pl.BlockSpec((1, HGRN_HEADS, HGRN_DV, HGRN_DK), lambda i, c: (i, 0, 0, 0)),
                  pl.BlockSpec((1, HGRN_DV), lambda i, c: (0, 0))],
        out_specs=[pl.BlockSpec((1, CHUNK, HGRN_KW), lambda i, c: (i, c, 0)),
                   pl.BlockSpec((1, HGRN_HEADS, HGRN_DV, HGRN_DK), lambda i, c: (i, 0, 0, 0))],
        out_shape=[jax.ShapeDtypeStruct((b, t, HGRN_KW), F32),
                   jax.ShapeDtypeStruct((b, HGRN_HEADS, HGRN_DV, HGRN_DK), F32)],
        scratch_shapes=[pltpu.VMEM((HGRN_HEADS, HGRN_DV, HGRN_DK), F32)],
        compiler_params=_cparams(("parallel", "arbitrary")),
        name="hgrn",
    )(p3, p3, p3, p3, lb, s0t, ng)


def _merge_kernel(x_ref, oa_ref, ob_ref, oc_ref, g0_ref, g1_ref, g2_ref, wa_ref, wb_ref, wc_ref, wo_ref, y_ref):
    def branch(o_ref, w_ref, g_ref):
        return _sigmoid(g_ref[...]) * jnp.dot(o_ref[...].astype(BF16), w_ref[...], preferred_element_type=F32)

    mixed = branch(oa_ref, wa_ref, g0_ref) + branch(ob_ref, wb_ref, g1_ref) + branch(oc_ref, wc_ref, g2_ref)
    y_ref[...] = x_ref[...] + jnp.dot(mixed.astype(BF16), wo_ref[...], preferred_element_type=F32)


def _merge(x, oa, ob, oc, p, wa, wb, wc, wo):
    n = x.shape[0]
    tm = _row_tile(n, 512)
    row = lambda w: pl.BlockSpec((tm, w), lambda i: (i, 0))
    gate = lambda j: pl.BlockSpec((tm, D_MODEL), lambda i: (i, (P_COLS - N_BRANCH * D_MODEL) // D_MODEL + j))
    wsp = lambda k: pl.BlockSpec((k, D_MODEL), lambda i: (0, 0))
    return pl.pallas_call(
        _merge_kernel,
        grid=(n // tm,),
        in_specs=[row(D_MODEL), row(MLA_VW), row(GDN_VW), row(HGRN_KW), gate(0), gate(1), gate(2),
                  wsp(MLA_VW), wsp(GDN_VW), wsp(HGRN_KW), wsp(D_MODEL)],
        out_specs=row(D_MODEL),
        out_shape=jax.ShapeDtypeStruct((n, D_MODEL), F32),
        compiler_params=_cparams(("parallel",)),
        name="merge_out",
    )(x, oa, ob, oc, p, p, p, wa, wb, wc, wo)


def _router_kernel(x_ref, g_ref, w_ref, b_ref, cw_ref):
    h = _rms(x_ref[...], g_ref[...])
    logits = jnp.dot(h, w_ref[...], preferred_element_type=F32, precision=lax.Precision.HIGHEST) + b_ref[...]
    lane = lax.broadcasted_iota(jnp.int32, logits.shape, 1)
    valid = lane < N_EXPERTS
    neg = -jnp.inf
    l1 = jnp.where(valid, logits, neg)
    m1 = jnp.max(l1, axis=-1, keepdims=True)
    i1 = jnp.min(jnp.where(l1 == m1, lane, 128), axis=-1, keepdims=True)
    l2 = jnp.where(lane == i1, neg, l1)
    m2 = jnp.max(l2, axis=-1, keepdims=True)
    i2 = jnp.min(jnp.where(l2 == m2, lane, 128), axis=-1, keepdims=True)
    e2 = jnp.exp(m2 - m1)
    den = 1.0 + e2
    cw_ref[...] = jnp.where(lane == i1, 1.0 / den, 0.0) + jnp.where(lane == i2, e2 / den, 0.0)


def _router(x, g, w, b):
    n = x.shape[0]
    tm = _row_tile(n, 512)
    return pl.pallas_call(
        _router_kernel,
        grid=(n // tm,),
        in_specs=[pl.BlockSpec((tm, D_MODEL), lambda i: (i, 0)),
                  pl.BlockSpec((1, D_MODEL), lambda i: (0, 0)),
                  pl.BlockSpec((D_MODEL, 128), lambda i: (0, 0)),
                  pl.BlockSpec((1, 128), lambda i: (0, 0))],
        out_specs=pl.BlockSpec((tm, 128), lambda i: (i, 0)),
        out_shape=jax.ShapeDtypeStruct((n, 128), F32),
        compiler_params=_cparams(("parallel",)),
        name="moe_router",
    )(x, g, w, b)


def _ffn_kernel(x_ref, g_ref, cw_ref, wg_ref, wu_ref, wd_ref, y_ref, h_ref, acc_ref, *, weighted):
    e = pl.program_id(1)

    @pl.when(e == 0)
    def _():
        h_ref[...] = _rms(x_ref[...], g_ref[...]).astype(BF16)
        acc_ref[...] = jnp.zeros(acc_ref.shape, F32)

    h = h_ref[...]
    a = jnp.dot(h, wg_ref[0], preferred_element_type=F32)
    b = jnp.dot(h, wu_ref[0], preferred_element_type=F32)
    y = jnp.dot((_silu(a) * b).astype(BF16), wd_ref[0], preferred_element_type=F32)
    if weighted:
        lane = lax.broadcasted_iota(jnp.int32, cw_ref.shape, 1)
        y = y * jnp.sum(jnp.where(lane == e, cw_ref[...], 0.0), axis=-1, keepdims=True)
    acc_ref[...] += y

    @pl.when(e == pl.num_programs(1) - 1)
    def _():
        y_ref[...] = x_ref[...] + acc_ref[...]


def _ffn(x, g, cw, wg, wu, wd, weighted):
    n = x.shape[0]
    ne, _, ff = wg.shape
    tm = _row_tile(n, 512)
    return pl.pallas_call(
        functools.partial(_ffn_kernel, weighted=weighted),
        grid=(n // tm, ne),
        in_specs=[pl.BlockSpec((tm, D_MODEL), lambda i, e: (i, 0)),
                  pl.BlockSpec((1, D_MODEL), lambda i, e: (0, 0)),
                  pl.BlockSpec((tm, 128), lambda i, e: (i, 0)),
                  pl.BlockSpec((1, D_MODEL, ff), lambda i, e: (e, 0, 0)),
                  pl.BlockSpec((1, D_MODEL, ff), lambda i, e: (e, 0, 0)),
                  pl.BlockSpec((1, ff, D_MODEL), lambda i, e: (e, 0, 0))],
        out_specs=pl.BlockSpec((tm, D_MODEL), lambda i, e: (i, 0)),
        out_shape=jax.ShapeDtypeStruct((n, D_MODEL), F32),
        scratch_shapes=[pltpu.VMEM((tm, D_MODEL), BF16), pltpu.VMEM((tm, D_MODEL), F32)],
        compiler_params=_cparams(("parallel", "arbitrary")),
        name="ffn",
    )(x, g, cw, wg, wu, wd)


def _pad_lanes(x, left, total):
    return jnp.pad(x, [(0, 0)] * (x.ndim - 1) + [(left, total - left - x.shape[-1])])


def _pack_w_in(w):
    cq, ckv, kr, gq, gk, gv, gz, ga, gb, hq, hf, hi, hg, gates = jnp.split(
        w, np.cumsum(SPLIT_SIZES)[:-1].tolist(), axis=-1)
    kr_blk = _pad_lanes(kr, ROPE_LANE0, 128)
    ab_blk = _pad_lanes(jnp.concatenate([ga, gb], axis=-1), 0, 256)
    return jnp.concatenate([cq, ckv, kr_blk, ab_blk, gq, gk, gv, gz, hq, hf, hi, hg, gates], axis=-1).astype(BF16)


def _rope_tables(n_pos):
    half = MLA_ROPE // 2
    inv = ROPE_THETA ** (-jnp.arange(half, dtype=F32) / half)
    ang = jnp.arange(n_pos, dtype=F32)[:, None] * inv[None, :]
    cos, sin = jnp.cos(ang), jnp.sin(ang)
    one = jnp.ones((n_pos, MLA_NOPE), F32)
    zero = jnp.zeros((n_pos, MLA_NOPE), F32)
    tail = jnp.zeros((n_pos, HEAD_PAD - MLA_QK_HEAD), F32)
    z16 = jnp.zeros((n_pos, half), F32)
    c = jnp.concatenate([one, cos, cos, tail], axis=-1)
    s1 = jnp.concatenate([zero, -sin, z16, tail], axis=-1)
    s2 = jnp.concatenate([zero, z16, sin, tail], axis=-1)
    return c, s1, s2


def _layer_weights(l, a):
    f = {}
    f['mixer_g'] = a['mixer_norm_g'][l][None]
    f['w_in'] = _pack_w_in(a['w_in'][l])
    f['gq'] = a['mla_q_norm_g'][l][None]
    f['gkv'] = a['mla_kv_norm_g'][l][None]
    wq = a['mla_w_q_up'][l].reshape(MLA_Q_LORA, MLA_HEADS, MLA_QK_HEAD)
    f['wq'] = _pad_lanes(wq, 0, HEAD_PAD).reshape(MLA_Q_LORA, QK_W).astype(BF16)
    wkv = a['mla_w_kv_up'][l].reshape(MLA_KV_LORA, MLA_HEADS, MLA_NOPE + MLA_V)
    f['wk'] = _pad_lanes(wkv[:, :, :MLA_NOPE], 0, HEAD_PAD).reshape(MLA_KV_LORA, QK_W).astype(BF16)
    f['wv'] = wkv[:, :, MLA_NOPE:].reshape(MLA_KV_LORA, MLA_VW).astype(BF16)
    f['hq'] = _pad_lanes(a['mla_q_head_norm_g'][l][None], 0, HEAD_PAD)
    hk = a['mla_k_head_norm_g'][l][None]
    f['hk_nope'] = _pad_lanes(hk[:, :MLA_NOPE], 0, HEAD_PAD)
    f['hk_rope'] = _pad_lanes(hk[:, MLA_NOPE:], ROPE_LANE0, HEAD_PAD)
    f['wo_a'] = a['mla_w_o'][l].astype(BF16)
    f['conv_w'] = a['gdn_conv_w'][l]
    f['alog'] = _pad_lanes(a['gdn_a_log'][l][None], 0, 128)
    f['dt'] = _pad_lanes(a['gdn_dt_bias'][l][None], 0, 128)
    f['gdn_g'] = a['gdn_norm_g'][l][None]
    f['wo_b'] = a['gdn_w_o'][l].astype(BF16)
    f['hgrn_g'] = a['hgrn_norm_g'][l][None]
    f['wo_c'] = a['hgrn_w_o'][l].astype(BF16)
    f['w_out'] = a['w_out'][l].astype(BF16)
    f['ffn_g'] = a['ffn_norm_g'][l][None]
    if l % 2 == 0:
        wg, wu, wd = a['dense_w_gate'][l // 2], a['dense_w_up'][l // 2], a['dense_w_down'][l // 2]
        ff = wg.shape[1]
        half = ff // 2
        f['ffn'] = (jnp.moveaxis(wg.reshape(D_MODEL, 2, half), 1, 0).astype(BF16),
                    jnp.moveaxis(wu.reshape(D_MODEL, 2, half), 1, 0).astype(BF16),
                    wd.reshape(2, half, D_MODEL).astype(BF16))
        f['router'] = None
    else:
        f['ffn'] = (a['moe_w_gate'][l // 2].astype(BF16), a['moe_w_up'][l // 2].astype(BF16),
                    a['moe_w_down'][l // 2].astype(BF16))
        f['router'] = (_pad_lanes(a['moe_w_router'][l // 2], 0, 128),
                       _pad_lanes(a['moe_b_router'][l // 2][None], 0, 128))
    return f


def _trunk_layer(x, b, t, f, lb, tabs, past):
    n = b * t
    p = _in_proj(x, f['mixer_g'], f['w_in'])
    p3 = p.reshape(b, t, P_COLS)

    q, ckv, kr = _mla_pre(p, t, f['gq'], f['gkv'], f['wq'], f['hq'], f['hk_rope'], tabs)
    if past['ckv'] is None:
        ckv_all, kr_all, s = ckv, kr, t
    else:
        s = past['ckv'].shape[1] + t
        ckv_all = jnp.concatenate([past['ckv'], ckv.reshape(b, t, -1)], axis=1).reshape(b * s, -1)
        kr_past = _pad_lanes(past['kr'], ROPE_LANE0, HEAD_PAD)
        kr_all = jnp.concatenate([kr_past, kr.reshape(b, t, -1)], axis=1).reshape(b * s, -1)
    k_all, v_all = _kv_up(ckv_all, kr_all, f['wk'], f['wv'], f['hk_nope'])
    o_a = _attention(q.reshape(b, t, QK_W), k_all.reshape(b, s, QK_W), v_all.reshape(b, s, MLA_VW),
                     causal=past['ckv'] is None)

    conv8 = jnp.pad(past['conv'].reshape(b, CONV_W - 1, 3, GDN_QK).transpose(0, 2, 1, 3),
                    ((0, 0), (0, 0), (8 - (CONV_W - 1), 0), (0, 0)))
    o_b, gdn_s = _gdn(p3, f['conv_w'], conv8, past['gdn'], f['alog'], f['dt'], f['gdn_g'])
    gdn_conv = p3[:, t - (CONV_W - 1):, COL_GDN:COL_GDN + 3 * GDN_QK]

    o_c, hgrn_st = _hgrn(p3, lb, jnp.swapaxes(past['hgrn'], -1, -2), f['hgrn_g'])
    hgrn_s = jnp.swapaxes(hgrn_st, -1, -2)

    x = _merge(x, o_a.reshape(n, -1), o_b.reshape(n, -1), o_c.reshape(n, -1), p,
               f['wo_a'], f['wo_b'], f['wo_c'], f['w_out'])

    wg, wu, wd = f['ffn']
    if f['router'] is None:
        cw = jnp.ones((n, 128), F32)
        x = _ffn(x, f['ffn_g'], cw, wg, wu, wd, weighted=False)
    else:
        cw = _router(x, f['ffn_g'], *f['router'])
        x = _ffn(x, f['ffn_g'], cw, wg, wu, wd, weighted=True)

    new_ckv = ckv.reshape(b, t, MLA_KV_LORA)
    new_kr = kr.reshape(b, t, HEAD_PAD)[:, :, ROPE_LANE0:ROPE_LANE0 + MLA_ROPE]
    return x, (new_ckv, new_kr, gdn_s, gdn_conv, hgrn_s)


def kernel(x_prompt, x_sample, cache_mla_ckv, cache_mla_krope, state_gdn, state_gdn_conv, state_hgrn,
           mixer_norm_g, w_in, mla_q_norm_g, mla_w_q_up, mla_kv_norm_g, mla_w_kv_up,
           mla_q_head_norm_g, mla_k_head_norm_g, mla_w_o,
           gdn_conv_w, gdn_a_log, gdn_dt_bias, gdn_norm_g, gdn_w_o,
           hgrn_lb_logits, hgrn_norm_g, hgrn_w_o, w_out, ffn_norm_g,
           dense_w_gate, dense_w_up, dense_w_down,
           moe_w_router, moe_b_router, moe_w_gate, moe_w_up, moe_w_down):
    a = dict(mixer_norm_g=mixer_norm_g, w_in=w_in, mla_q_norm_g=mla_q_norm_g, mla_w_q_up=mla_w_q_up,
             mla_kv_norm_g=mla_kv_norm_g, mla_w_kv_up=mla_w_kv_up, mla_q_head_norm_g=mla_q_head_norm_g,
             mla_k_head_norm_g=mla_k_head_norm_g, mla_w_o=mla_w_o, gdn_conv_w=gdn_conv_w,
             gdn_a_log=gdn_a_log, gdn_dt_bias=gdn_dt_bias, gdn_norm_g=gdn_norm_g, gdn_w_o=gdn_w_o,
             hgrn_norm_g=hgrn_norm_g, hgrn_w_o=hgrn_w_o, w_out=w_out, ffn_norm_g=ffn_norm_g,
             dense_w_gate=dense_w_gate, dense_w_up=dense_w_up, dense_w_down=dense_w_down,
             moe_w_router=moe_w_router, moe_b_router=moe_b_router, moe_w_gate=moe_w_gate,
             moe_w_up=moe_w_up, moe_w_down=moe_w_down)
    depth = w_in.shape[0]
    lb_soft = jax.nn.softmax(hgrn_lb_logits.astype(F32), axis=0)
    hgrn_lb = jnp.cumsum(lb_soft, axis=0) - lb_soft[0]

    b_p, t_p = x_prompt.shape[:2]
    b_s, t_s = x_sample.shape[:2]
    past_len = cache_mla_ckv.shape[2]
    tab_all = _rope_tables(max(t_p, past_len + t_s))
    tabs_p = tuple(tb[:t_p] for tb in tab_all)
    tabs_s = tuple(tb[past_len:past_len + t_s] for tb in tab_all)

    xp = x_prompt.reshape(b_p * t_p, D_MODEL)
    xs = x_sample.reshape(b_s * t_s, D_MODEL)
    past_p = dict(ckv=None, kr=None,
                  gdn=jnp.zeros((b_p, GDN_HEADS, GDN_DK, GDN_DV), F32),
                  conv=jnp.zeros((b_p, CONV_W - 1, 3 * GDN_QK), F32),
                  hgrn=jnp.zeros((b_p, HGRN_HEADS, HGRN_DK, HGRN_DV), F32))
    st_p, st_s = [], []
    for l in range(depth):
        f = _layer_weights(l, a)
        lb = hgrn_lb[l][None]
        past_s = dict(ckv=cache_mla_ckv[l], kr=cache_mla_krope[l], gdn=state_gdn[l],
                      conv=state_gdn_conv[l], hgrn=state_hgrn[l])
        xp, sp = _trunk_layer(xp, b_p, t_p, f, lb, tabs_p, past_p)
        xs, ss = _trunk_layer(xs, b_s, t_s, f, lb, tabs_s, past_s)
        st_p.append(sp)
        st_s.append(ss)

    def stack(lst, i):
        return jnp.stack([s[i] for s in lst], axis=0)

    return (xp.reshape(b_p, t_p, D_MODEL), xs.reshape(b_s, t_s, D_MODEL),
            stack(st_p, 0), stack(st_p, 1), stack(st_p, 2), stack(st_p, 3), stack(st_p, 4),
            stack(st_s, 0), stack(st_s, 1), stack(st_s, 2), stack(st_s, 3), stack(st_s, 4))
```

```python
import functools

import jax
import jax.numpy as jnp
import numpy as np
from jax import lax
from jax.experimental import pallas as pl
from jax.experimental.pallas import tpu as pltpu

F32 = jnp.float32
BF16 = jnp.bfloat16

D_MODEL = 1024
CHUNK = 64
NORM_EPS = 1e-6

MLA_HEADS = 8
MLA_NOPE = 64
MLA_ROPE = 32
MLA_V = 64
MLA_Q_LORA = 384
MLA_KV_LORA = 256
MLA_QK_HEAD = MLA_NOPE + MLA_ROPE
MLA_VW = MLA_HEADS * MLA_V
ROPE_THETA = 10000.0
LOG2_E = 1.4426950408889634
HEAD_PAD = 128
QK_W = MLA_HEADS * HEAD_PAD

GDN_HEADS = 4
GDN_DK = 128
GDN_DV = 128
GDN_QK = GDN_HEADS * GDN_DK
GDN_VW = GDN_HEADS * GDN_DV
CONV_W = 4

HGRN_HEADS = 4
HGRN_DK = 128
HGRN_DV = 128
HGRN_KW = HGRN_HEADS * HGRN_DK

N_BRANCH = 3
SPLIT_SIZES = (MLA_Q_LORA, MLA_KV_LORA, MLA_ROPE,
               GDN_QK, GDN_QK, GDN_VW, GDN_VW, GDN_HEADS, GDN_HEADS,
               HGRN_KW, HGRN_KW, HGRN_KW, HGRN_KW,
               N_BRANCH * D_MODEL)

N_EXPERTS = 8
FF_EXPERT = 1408

P_COLS = 8192
P_TN = 1024
COL_CKV = MLA_Q_LORA
COL_KR = MLA_Q_LORA + MLA_KV_LORA
COL_GAB = COL_KR + 128
COL_GDN = 1024
ROPE_LANE0 = MLA_NOPE

VMEM_LIMIT = 56 * 1024 * 1024


def _cparams(sem):
    return pltpu.CompilerParams(dimension_semantics=sem, vmem_limit_bytes=VMEM_LIMIT)


def _row_tile(n, cap):
    for t in (2048, 1024, 512, 256, 128, 64, 32, 16, 8):
        if t <= cap and n % t == 0:
            return t
    raise ValueError(f"no row tile for {n}")


def _sigmoid(x):
    return 1.0 / (1.0 + jnp.exp(-x))


def _silu(x):
    return x * _sigmoid(x)


def _rms(x, g):
    ms = jnp.mean(x * x, axis=-1, keepdims=True)
    return x * lax.rsqrt(ms + NORM_EPS) * g


def _in_proj_kernel(x_ref, g_ref, w_ref, o_ref, h_ref):
    @pl.when(pl.program_id(1) == 0)
    def _():
        h_ref[...] = _rms(x_ref[...], g_ref[...]).astype(BF16)

    o_ref[...] = jnp.dot(h_ref[...], w_ref[...], preferred_element_type=F32)


def _in_proj(x, g, w):
    n = x.shape[0]
    tm = _row_tile(n, 1024)
    return pl.pallas_call(
        _in_proj_kernel,
        grid=(n // tm, P_COLS // P_TN),
        in_specs=[pl.BlockSpec((tm, D_MODEL), lambda i, j: (i, 0)),
                  pl.BlockSpec((1, D_MODEL), lambda i, j: (0, 0)),
                  pl.BlockSpec((D_MODEL, P_TN), lambda i, j: (0, j))],
        out_specs=pl.BlockSpec((tm, P_TN), lambda i, j: (i, j)),
        out_shape=jax.ShapeDtypeStruct((n, P_COLS), F32),
        scratch_shapes=[pltpu.VMEM((tm, D_MODEL), BF16)],
        compiler_params=_cparams(("parallel", "arbitrary")),
        name="in_proj",
    )(x, g, w)


def _rope(x, c, s1, s2):
    return x * c + pltpu.roll(x, HEAD_PAD - 16, 1) * s1 + pltpu.roll(x, 16, 1) * s2


def _mla_pre_kernel(p_ref, gq_ref, gkv_ref, wq_ref, hq_ref, hk_ref, c_ref, s1_ref, s2_ref,
                    q_ref, ckv_ref, kr_ref, *, scale, transposed):
    c, s1, s2 = c_ref[...], s1_ref[...], s2_ref[...]
    ckv_ref[...] = _rms(p_ref[:, COL_CKV:COL_KR], gkv_ref[...])

    kr = p_ref[:, COL_KR:COL_KR + HEAD_PAD]
    kr_ms = jnp.sum(kr * kr, axis=-1, keepdims=True) * (1.0 / MLA_ROPE)
    kr_ref[...] = _rope(kr * lax.rsqrt(kr_ms + NORM_EPS) * hk_ref[...], c, s1, s2)

    cq = _rms(p_ref[:, 0:MLA_Q_LORA], gq_ref[...]).astype(BF16)
    q = jnp.dot(cq, wq_ref[...], preferred_element_type=F32)
    lane = lax.broadcasted_iota(jnp.int32, (1, HEAD_PAD), 1)
    is_nope = lane < MLA_NOPE
    hq = hq_ref[...]
    for h in range(MLA_HEADS):
        qh = q[:, h * HEAD_PAD:(h + 1) * HEAD_PAD]
        sq = qh * qh
        ms_n = jnp.sum(jnp.where(is_nope, sq, 0.0), axis=-1, keepdims=True) * (1.0 / MLA_NOPE)
        ms_r = jnp.sum(jnp.where(is_nope, 0.0, sq), axis=-1, keepdims=True) * (1.0 / MLA_ROPE)
        inv = jnp.where(is_nope, lax.rsqrt(ms_n + NORM_EPS), lax.rsqrt(ms_r + NORM_EPS))
        qh = _rope(qh * inv * hq, c, s1, s2) * scale
        if transposed:
            q_ref[0, h] = qh.T.astype(BF16)
        else:
            q_ref[:, h * HEAD_PAD:(h + 1) * HEAD_PAD] = qh.astype(BF16)


def _mla_pre(p, b, t_seq, gq, gkv, wq, hq, hk, tabs, transposed):
    n = p.shape[0]
    tm = _row_tile(n, 512)
    c, s1, s2 = tabs
    if tm > t_seq:
        c, s1, s2 = (jnp.tile(t, (tm // t_seq, 1)) for t in (c, s1, s2))
    n_tab = c.shape[0] // tm
    tab_spec = pl.BlockSpec((tm, HEAD_PAD), lambda i: (i % n_tab, 0))
    vec = lambda w: pl.BlockSpec((1, w), lambda i: (0, 0))
    scale = MLA_QK_HEAD ** -0.5
    if transposed:
        scale *= LOG2_E
        q_spec = pl.BlockSpec((1, MLA_HEADS, HEAD_PAD, tm), lambda i: (i // n_tab, 0, 0, i % n_tab))
        q_shape = jax.ShapeDtypeStruct((b, MLA_HEADS, HEAD_PAD, t_seq), BF16)
    else:
        q_spec = pl.BlockSpec((tm, QK_W), lambda i: (i, 0))
        q_shape = jax.ShapeDtypeStruct((n, QK_W), BF16)
    return pl.pallas_call(
        functools.partial(_mla_pre_kernel, scale=scale, transposed=transposed),
        grid=(n // tm,),
        in_specs=[pl.BlockSpec((tm, 1024), lambda i: (i, 0)),
                  vec(MLA_Q_LORA), vec(MLA_KV_LORA),
                  pl.BlockSpec((MLA_Q_LORA, QK_W), lambda i: (0, 0)),
                  vec(HEAD_PAD), vec(HEAD_PAD), tab_spec, tab_spec, tab_spec],
        out_specs=[q_spec,
                   pl.BlockSpec((tm, MLA_KV_LORA), lambda i: (i, 0)),
                   pl.BlockSpec((tm, HEAD_PAD), lambda i: (i, 0))],
        out_shape=[q_shape,
                   jax.ShapeDtypeStruct((n, MLA_KV_LORA), F32),
                   jax.ShapeDtypeStruct((n, HEAD_PAD), F32)],
        compiler_params=_cparams(("parallel",)),
        name="mla_pre",
    )(p, gq, gkv, wq, hq, hk, c, s1, s2)


def _kv_up_kernel(ckv_ref, kr_ref, wk_ref, wv_ref, hk_ref, k_ref, v_ref, *, transposed):
    c = ckv_ref[...].astype(BF16)
    k = jnp.dot(c, wk_ref[...], preferred_element_type=F32)
    kr = kr_ref[...]
    hk = hk_ref[...]
    for h in range(MLA_HEADS):
        kh = k[:, h * HEAD_PAD:(h + 1) * HEAD_PAD]
        ms = jnp.sum(kh * kh, axis=-1, keepdims=True) * (1.0 / MLA_NOPE)
        k_ref[:, h * HEAD_PAD:(h + 1) * HEAD_PAD] = (kh * lax.rsqrt(ms + NORM_EPS) * hk + kr).astype(BF16)
    v = jnp.dot(c, wv_ref[...], preferred_element_type=F32)
    if transposed:
        lane = lax.broadcasted_iota(jnp.int32, (1, HEAD_PAD), 1)
        for h in range(MLA_HEADS):
            vh = jnp.where(lane < MLA_V, v[:, h * HEAD_PAD:(h + 1) * HEAD_PAD], 1.0)
            v_ref[0, h] = vh.T.astype(BF16)
    else:
        v_ref[...] = v.astype(BF16)


def _kv_up(ckv, kr, b, s_len, wk, wv, hk, transposed):
    n = ckv.shape[0]
    tm = _row_tile(n, 512)
    if transposed:
        n_t = s_len // tm
        v_spec = pl.BlockSpec((1, MLA_HEADS, HEAD_PAD, tm), lambda i: (i // n_t, 0, 0, i % n_t))
        v_shape = jax.ShapeDtypeStruct((b, MLA_HEADS, HEAD_PAD, s_len), BF16)
    else:
        v_spec = pl.BlockSpec((tm, MLA_VW), lambda i: (i, 0))
        v_shape = jax.ShapeDtypeStruct((n, MLA_VW), BF16)
    return pl.pallas_call(
        functools.partial(_kv_up_kernel, transposed=transposed),
        grid=(n // tm,),
        in_specs=[pl.BlockSpec((tm, MLA_KV_LORA), lambda i: (i, 0)),
                  pl.BlockSpec((tm, HEAD_PAD), lambda i: (i, 0)),
                  pl.BlockSpec((MLA_KV_LORA, QK_W), lambda i: (0, 0)),
                  pl.BlockSpec((MLA_KV_LORA, wv.shape[1]), lambda i: (0, 0)),
                  pl.BlockSpec((1, HEAD_PAD), lambda i: (0, 0))],
        out_specs=[pl.BlockSpec((tm, QK_W), lambda i: (i, 0)), v_spec],
        out_shape=[jax.ShapeDtypeStruct((n, QK_W), BF16), v_shape],
        compiler_params=_cparams(("parallel",)),
        name="kv_up",
    )(ckv, kr, wk, wv, hk)


def _attn_kernel(q_ref, k_ref, v_ref, o_ref, m_ref, l_ref, acc_ref, *, tq, tk, causal):
    m_ref[...] = jnp.full(m_ref.shape, -jnp.inf, F32)
    l_ref[...] = jnp.zeros(l_ref.shape, F32)
    acc_ref[...] = jnp.zeros(acc_ref.shape, F32)

    def block(start, masked):
        if masked:
            rq = lax.broadcasted_iota(jnp.int32, (tq, tk), 0) // CHUNK
            ck = lax.broadcasted_iota(jnp.int32, (tq, tk), 1) // CHUNK
            allowed = ck <= rq
        def scores(h):
            qh = q_ref[0, :, h * HEAD_PAD:(h + 1) * HEAD_PAD]
            kh = k_ref[0, pl.ds(start, tk), h * HEAD_PAD:(h + 1) * HEAD_PAD]
            return lax.dot_general(qh, kh, (((1,), (1,)), ((), ())), preferred_element_type=F32)

        all_scores = [scores(h) for h in range(MLA_HEADS)]
        for h in range(MLA_HEADS):
            vh = v_ref[0, pl.ds(start, tk), h * MLA_V:(h + 1) * MLA_V]
            s = all_scores[h]
            if masked:
                s = jnp.where(allowed, s, -jnp.inf)
            m_prev = m_ref[h]
            m_new = jnp.maximum(m_prev, jnp.max(s, axis=-1, keepdims=True))
            alpha = jnp.exp(m_prev - m_new)
            p = jnp.exp(s - m_new)
            l_ref[h] = alpha * l_ref[h] + jnp.sum(p, axis=-1, keepdims=True)
            pv = jnp.dot(p.astype(BF16), vh, preferred_element_type=F32)
            acc_ref[h] = alpha * acc_ref[h] + pv
            m_ref[h] = m_new

    if causal:
        qi = pl.program_id(1)

        def body(j, carry):
            block(pl.multiple_of(j * tk, tk), False)
            return carry

        lax.fori_loop(0, qi, body, 0)
        block(pl.multiple_of(qi * tk, tk), True)
    else:
        block(0, False)

    for h in range(MLA_HEADS):
        o_ref[0, :, h * MLA_V:(h + 1) * MLA_V] = acc_ref[h] / l_ref[h]


def _attention(q, k, v, causal):
    b, t, _ = q.shape
    s = k.shape[1]
    if causal:
        tq = tk = min(256, t)
    else:
        tq, tk = t, s
    return pl.pallas_call(
        functools.partial(_attn_kernel, tq=tq, tk=tk, causal=causal),
        grid=(b, t // tq),
        in_specs=[pl.BlockSpec((1, tq, QK_W), lambda i, j: (i, j, 0)),
                  pl.BlockSpec((1, s, QK_W), lambda i, j: (i, 0, 0)),
                  pl.BlockSpec((1, s, MLA_VW), lambda i, j: (i, 0, 0))],
        out_specs=pl.BlockSpec((1, tq, MLA_VW), lambda i, j: (i, j, 0)),
        out_shape=jax.ShapeDtypeStruct((b, t, MLA_VW), F32),
        scratch_shapes=[pltpu.VMEM((MLA_HEADS, tq, 1), F32),
                        pltpu.VMEM((MLA_HEADS, tq, 1), F32),
                        pltpu.VMEM((MLA_HEADS, tq, MLA_V), F32)],
        compiler_params=_cparams(("parallel", "arbitrary")),
        name="mla_attn",
    )(q, k, v)


def _attn_t_kernel(qt_ref, k_ref, vt_ref, o_ref, m_ref, acc_ref, *, tq, tk):
    m_ref[...] = jnp.full(m_ref.shape, -jnp.inf, F32)
    acc_ref[...] = jnp.zeros(acc_ref.shape, F32)

    def block(start, masked):
        if masked:
            ck = lax.broadcasted_iota(jnp.int32, (tk, tq), 0) // CHUNK
            cq = lax.broadcasted_iota(jnp.int32, (tk, tq), 1) // CHUNK
            allowed = ck <= cq
        def scores(h):
            kh = k_ref[0, pl.ds(start, tk), h * HEAD_PAD:(h + 1) * HEAD_PAD]
            return jnp.dot(kh, qt_ref[0, h], preferred_element_type=F32)

        def update(h, s):
            if masked:
                s = jnp.where(allowed, s, -jnp.inf)
            m_prev = m_ref[h]
            m_new = jnp.maximum(m_prev, jnp.max(s, axis=0, keepdims=True))
            alpha = jnp.exp2(m_prev - m_new)
            p = jnp.exp2(s - m_new).astype(BF16)
            pv = jnp.dot(vt_ref[0, h, :, pl.ds(start, tk)], p, preferred_element_type=F32)
            acc_ref[h] = alpha * acc_ref[h] + pv
            m_ref[h] = m_new

        ahead = 3
        pending = [scores(h) for h in range(ahead)]
        for h in range(MLA_HEADS):
            s = pending.pop(0)
            if h + ahead < MLA_HEADS:
                pending.append(scores(h + ahead))
            update(h, s)

    qi = pl.program_id(1)

    def body(j, carry):
        block(pl.multiple_of(j * tk, tk), False)
        return carry

    lax.fori_loop(0, qi, body, 0)
    block(pl.multiple_of(qi * tk, tk), True)

    ot = jnp.concatenate([acc_ref[h, 0:MLA_V] / acc_ref[h, MLA_V:MLA_V + 1] for h in range(MLA_HEADS)], axis=0)
    o_ref[0] = ot.T


def _attention_t(qt, k, vt):
    b, _, _, t = qt.shape
    s = k.shape[1]
    tq = tk = min(256, t)
    return pl.pallas_call(
        functools.partial(_attn_t_kernel, tq=tq, tk=tk),
        grid=(b, t // tq),
        in_specs=[pl.BlockSpec((1, MLA_HEADS, HEAD_PAD, tq), lambda i, j: (i, 0, 0, j)),
                  pl.BlockSpec((1, s, QK_W), lambda i, j: (i, 0, 0)),
                  pl.BlockSpec((1, MLA_HEADS, HEAD_PAD, s), lambda i, j: (i, 0, 0, 0))],
        out_specs=pl.BlockSpec((1, tq, MLA_VW), lambda i, j: (i, j, 0)),
        out_shape=jax.ShapeDtypeStruct((b, t, MLA_VW), F32),
        scratch_shapes=[pltpu.VMEM((MLA_HEADS, 1, tq), F32),
                        pltpu.VMEM((MLA_HEADS, HEAD_PAD, tq), F32)],
        compiler_params=_cparams(("parallel", "arbitrary")),
        name="mla_attn_t",
    )(qt, k, vt)


def _row_iota(shape):
    return lax.broadcasted_iota(jnp.int32, shape, 0)


def _segment_scans(g):
    row = _row_iota(g.shape)
    rows = g.shape[0]
    pre, tot = g, g
    out = [(pre, tot)]
    m = 1
    while m < CHUNK:
        upper = (row // m) % 2 == 1
        from_lower = pltpu.roll(tot, m, 0)
        from_upper = pltpu.roll(tot, rows - m, 0)
        pre = pre + jnp.where(upper, from_lower, 0.0)
        tot = tot + jnp.where(upper, from_lower, from_upper)
        out.append((pre, tot))
        m *= 2
    return out


def _dot_nt(a, b):
    return lax.dot_general(a, b, (((1,), (1,)), ((), ())), preferred_element_type=F32)


def _dot_tn(a, b):
    return lax.dot_general(a, b, (((0,), (0,)), ((), ())), preferred_element_type=F32)


def _gdn_prep_kernel(q_ref, k_ref, v_ref, ab_ref, cw_ref, cs_ref, alog_ref, dt_ref,
                     u_ref, w_ref, qd_ref, kd_ref, qk_ref, gl_ref, carry_ref, *, nb, r):
    @pl.when(pl.program_id(1) == 0)
    def _():
        carry_ref[...] = cs_ref[...]

    n = nb * r
    nc = n // CHUNK

    def conv(x_ref, j):
        w = cw_ref[:, j * GDN_QK:(j + 1) * GDN_QK]
        outs = []
        for b in range(nb):
            x = x_ref[b]
            xp = jnp.concatenate([carry_ref[b, j], x], axis=0)
            y = x * w[3:4]
            for d in range(1, CONV_W):
                y = y + xp[8 - d:8 - d + r] * w[3 - d:4 - d]
            carry_ref[b, j] = x[r - 8:]
            outs.append(_silu(y))
        return outs[0] if nb == 1 else jnp.concatenate(outs, axis=0)

    q_all, k_all, v_all = conv(q_ref, 0), conv(k_ref, 1), conv(v_ref, 2)

    ab = ab_ref[...].reshape(n, 128)
    x = ab + dt_ref[...]
    softplus = jnp.maximum(x, 0.0) + jnp.log(1.0 + jnp.exp(-jnp.abs(x)))
    g_blk = -jnp.exp(alog_ref[...]) * softplus
    gam_blk = _segment_scans(g_blk)[-1][0]
    gam_t = gam_blk.T
    gam3_blk = gam_blk.reshape(nc, CHUNK, 128)
    beta3_blk = _sigmoid(ab).reshape(nc, CHUNK, 128)

    row = lax.broadcasted_iota(jnp.int32, (1, CHUNK, CHUNK), 1)
    col = lax.broadcasted_iota(jnp.int32, (1, CHUNK, CHUNK), 2)
    eye = (row == col).astype(F32)

    def bmm(a, b):
        return jnp.einsum('cij,cjk->cik', a.astype(BF16), b.astype(BF16), preferred_element_type=F32)

    def bmm_nt(a, b):
        return jnp.einsum('cid,cjd->cij', a.astype(BF16), b.astype(BF16), preferred_element_type=F32)

    for h in range(GDN_HEADS):
        sl = slice(h * GDN_DK, (h + 1) * GDN_DK)
        q, k, v = q_all[:, sl], k_all[:, sl], v_all[:, sl]
        q = q * lax.rsqrt(jnp.sum(q * q, axis=-1, keepdims=True) + NORM_EPS) * (GDN_DK ** -0.5)
        k = k * lax.rsqrt(jnp.sum(k * k, axis=-1, keepdims=True) + NORM_EPS)
        q, k, v = (a.reshape(nc, CHUNK, GDN_DK) for a in (q, k, v))
        gam = gam3_blk[:, :, h:h + 1]
        beta = beta3_blk[:, :, GDN_HEADS + h:GDN_HEADS + h + 1]
        gam_row = jnp.stack([gam_t[h:h + 1, c * CHUNK:(c + 1) * CHUNK] for c in range(nc)], axis=0)
        decay = jnp.where(row >= col, jnp.exp(jnp.minimum(gam - gam_row, 0.0)), 0.0)
        a = jnp.where(row > col, beta * bmm_nt(k, k) * decay, 0.0)
        t_inv = eye - a
        pw = a
        m = 1
        while 2 * m < CHUNK:
            pw = bmm(pw, pw)
            t_inv = t_inv + bmm(t_inv, pw)
            m *= 2
        e_gam = jnp.exp(gam)
        gam_last = gam[:, CHUNK - 1:CHUNK, :]
        u_ref[:, h] = bmm(t_inv, v * beta).reshape(nb, r, GDN_DV)
        w_ref[:, h] = bmm(t_inv, k * (beta * e_gam)).astype(BF16).reshape(nb, r, GDN_DK)
        qd_ref[:, h] = (q * e_gam).astype(BF16).reshape(nb, r, GDN_DK)
        kd_ref[:, h] = (k * jnp.exp(gam_last - gam)).astype(BF16).reshape(nb, r, GDN_DK)
        qk_ref[:, h] = (bmm_nt(q, k) * decay).astype(BF16).reshape(nb, r, CHUNK)
        gl_ref[:, h] = jnp.broadcast_to(jnp.exp(gam_last), (nc, 1, 128)).reshape(nb, r // CHUNK, 1, 128)


def _gdn_scan_kernel(u_ref, w_ref, qd_ref, kd_ref, qk_ref, gl_ref, s0_ref, o_ref, sf_ref, s_ref, *, nb):
    c_idx = pl.program_id(1)

    @pl.when(c_idx == 0)
    def _():
        s_ref[...] = s0_ref[...]

    chains = [(b, h) for b in range(nb) for h in range(GDN_HEADS)]
    s_old = [s_ref[b, h] for b, h in chains]
    s_bf = [s.astype(BF16) for s in s_old]
    v_new = [u_ref[b, h] - jnp.dot(w_ref[b, h], sb, preferred_element_type=F32)
             for (b, h), sb in zip(chains, s_bf)]
    v_bf = [v.astype(BF16) for v in v_new]
    for (b, h), s, sb, vb in zip(chains, s_old, s_bf, v_bf):
        o = (jnp.dot(qd_ref[b, h], sb, preferred_element_type=F32)
             + jnp.dot(qk_ref[b, h], vb, preferred_element_type=F32))
        o_ref[b, :, h * GDN_DV:(h + 1) * GDN_DV] = o
        s_ref[b, h] = gl_ref[b, h, 0] * s + _dot_tn(kd_ref[b, h], vb)

    @pl.when(c_idx == pl.num_programs(1) - 1)
    def _():
        sf_ref[...] = s_ref[...]


def _gdn(p3, conv_w, conv_state8, s0, alog, dt):
    b, t, _ = p3.shape
    r = min(512, t)
    nb = max(1, min(b, 512 // r))
    nt = t // r
    nc = t // CHUNK
    blk = lambda j: pl.BlockSpec((nb, r, GDN_QK), lambda i, c: (i, c, COL_GDN // GDN_QK + j))
    vec = pl.BlockSpec((1, 128), lambda i, c: (0, 0))
    head_spec = lambda w: pl.BlockSpec((nb, GDN_HEADS, r, w), lambda i, c: (i, 0, c, 0))
    head_shape = lambda w, dt_: jax.ShapeDtypeStruct((b, GDN_HEADS, t, w), dt_)
    u, w, qd, kd, qk, gl = pl.pallas_call(
        functools.partial(_gdn_prep_kernel, nb=nb, r=r),
        grid=(b // nb, nt),
        in_specs=[blk(0), blk(1), blk(2),
                  pl.BlockSpec((nb, r, 128), lambda i, c: (i, c, COL_GAB // 128)),
                  pl.BlockSpec((CONV_W, 3 * GDN_QK), lambda i, c: (0, 0)),
                  pl.BlockSpec((nb, 3, 8, GDN_QK), lambda i, c: (i, 0, 0, 0)),
                  vec, vec],
        out_specs=[head_spec(GDN_DV), head_spec(GDN_DK), head_spec(GDN_DK), head_spec(GDN_DK),
                   head_spec(CHUNK),
                   pl.BlockSpec((nb, GDN_HEADS, r // CHUNK, 1, 128), lambda i, c: (i, 0, c, 0, 0))],
        out_shape=[head_shape(GDN_DV, F32), head_shape(GDN_DK, BF16), head_shape(GDN_DK, BF16),
                   head_shape(GDN_DK, BF16), head_shape(CHUNK, BF16),
                   jax.ShapeDtypeStruct((b, GDN_HEADS, nc, 1, 128), F32)],
        scratch_shapes=[pltpu.VMEM((nb, 3, 8, GDN_QK), F32)],
        compiler_params=_cparams(("parallel", "arbitrary")),
        name="gdn_prep",
    )(p3, p3, p3, p3, conv_w, conv_state8, alog, dt)

    sb = min(b, 8)
    chunk_spec = lambda w: pl.BlockSpec((sb, GDN_HEADS, CHUNK, w), lambda i, c: (i, 0, c, 0))
    state_spec = pl.BlockSpec((sb, GDN_HEADS, GDN_DK, GDN_DV), lambda i, c: (i, 0, 0, 0))
    return pl.pallas_call(
        functools.partial(_gdn_scan_kernel, nb=sb),
        grid=(b // sb, nc),
        in_specs=[chunk_spec(GDN_DV), chunk_spec(GDN_DK), chunk_spec(GDN_DK), chunk_spec(GDN_DK),
                  chunk_spec(CHUNK),
                  pl.BlockSpec((sb, GDN_HEADS, 1, 1, 128), lambda i, c: (i, 0, c, 0, 0)),
                  state_spec],
        out_specs=[pl.BlockSpec((sb, CHUNK, GDN_VW), lambda i, c: (i, c, 0)), state_spec],
        out_shape=[jax.ShapeDtypeStruct((b, t, GDN_VW), F32),
                   jax.ShapeDtypeStruct((b, GDN_HEADS, GDN_DK, GDN_DV), F32)],
        scratch_shapes=[pltpu.VMEM((sb, GDN_HEADS, GDN_DK, GDN_DV), F32)],
        compiler_params=_cparams(("parallel", "arbitrary")),
        name="gdn_scan",
    )(u, w, qd, kd, qk, gl, s0)


def _hgrn_kernel(q_ref, f_ref, i_ref, gate_ref, lb_ref, s0_ref, ng_ref, o_ref, sf_ref, st_ref):
    c_idx = pl.program_id(1)

    @pl.when(c_idx == 0)
    def _():
        st_ref[...] = s0_ref[0]

    row = lax.broadcasted_iota(jnp.int32, (CHUNK, CHUNK), 0)
    col = lax.broadcasted_iota(jnp.int32, (CHUNK, CHUNK), 1)
    rowv = _row_iota((CHUNK, HGRN_DK))

    heads = range(HGRN_HEADS)
    sls = [slice(h * HGRN_DK, (h + 1) * HGRN_DK) for h in heads]
    qs_, ks_, vs_, scans_ = [], [], [], []
    for sl in sls:
        lb = lb_ref[:, sl]
        f = lb + (1.0 - lb) * _sigmoid(f_ref[0, :, sl])
        qs_.append((_silu(q_ref[0, :, sl]) * (HGRN_DK ** -0.5)))
        ks_.append(1.0 - f)
        vs_.append(i_ref[0, :, sl].astype(BF16))
        scans_.append(_segment_scans(jnp.log(f)))

    att = [jnp.where(row == col, _dot_nt(q.astype(BF16), k.astype(BF16)), 0.0) for q, k in zip(qs_, ks_)]
    m = 1
    lvl = 0
    while m < CHUNK:
        upper = (rowv // m) % 2 == 1
        same_block = row // (2 * m) == col // (2 * m)
        for h in heads:
            pre_m, tot_m = scans_[h][lvl]
            q_lvl = jnp.where(upper, qs_[h] * jnp.exp(pre_m), 0.0).astype(BF16)
            k_lvl = jnp.where(upper, 0.0, ks_[h] * jnp.exp(tot_m - pre_m)).astype(BF16)
            att[h] = att[h] + jnp.where(same_block, _dot_nt(q_lvl, k_lvl), 0.0)
        m *= 2
        lvl += 1

    st_old = [st_ref[h] for h in heads]
    o_st = [_dot_nt((qs_[h] * jnp.exp(scans_[h][-1][0])).astype(BF16), st_old[h].astype(BF16)) for h in heads]
    o_att = [jnp.dot(att[h].astype(BF16), vs_[h], preferred_element_type=F32) for h in heads]
    for h in heads:
        cb, c_tot = scans_[h][-1]
        k_dec = (ks_[h] * jnp.exp(c_tot - cb)).astype(BF16)
        st_ref[h] = st_old[h] * jnp.exp(c_tot[0:1]) + _dot_tn(vs_[h], k_dec)
        o_ref[0, :, sls[h]] = _rms(o_st[h] + o_att[h], ng_ref[...]) * _silu(gate_ref[0, :, sls[h]])

    @pl.when(c_idx == pl.num_programs(1) - 1)
    def _():
        sf_ref[0] = st_ref[...]


def _hgrn(p3, lb, s0t, ng):
    b, t, _ = p3.shape
    nc = t // CHUNK
    col0 = (COL_GDN + 4 * GDN_QK) // HGRN_KW
    blk = lambda j: pl.BlockSpec((1, CHUNK, HGRN_KW), lambda i, c: (i, c, col0 + j))
    return pl.pallas_call(
        _hgrn_kernel,
        grid=(b, nc),
        in_specs=[blk(0), blk(1), blk(2), blk(3),
                  pl.BlockSpec((1, HGRN_KW), lambda i, c: (0, 0)),
                  pl.BlockSpec((1, HGRN_HEADS, HGRN_DV, HGRN_DK), lambda i, c: (i, 0, 0, 0)),
                  pl.BlockSpec((1, HGRN_DV), lambda i, c: (0, 0))],
        out_specs=[pl.BlockSpec((1, CHUNK, HGRN_KW), lambda i, c: (i, c, 0)),
                   pl.BlockSpec((1, HGRN_HEADS, HGRN_DV, HGRN_DK), lambda i, c: (i, 0, 0, 0))],
        out_shape=[jax.ShapeDtypeStruct((b, t, HGRN_KW), F32),
                   jax.ShapeDtypeStruct((b, HGRN_HEADS, HGRN_DV, HGRN_DK), F32)],
        scratch_shapes=[pltpu.VMEM((HGRN_HEADS, HGRN_DV, HGRN_DK), F32)],
        compiler_params=_cparams(("parallel", "arbitrary")),
        name="hgrn",
    )(p3, p3, p3, p3, lb, s0t, ng)


def _merge_kernel(x_ref, oa_ref, ob_ref, z_ref, oc_ref, g0_ref, g1_ref, g2_ref, ng_ref,
                  wa_ref, wb_ref, wc_ref, wo_ref, y_ref):
    def branch(o, w_ref, g_ref):
        return _sigmoid(g_ref[...]) * jnp.dot(o.astype(BF16), w_ref[...], preferred_element_type=F32)

    ob = jnp.concatenate(
        [_rms(ob_ref[:, h * GDN_DV:(h + 1) * GDN_DV], ng_ref[...]) * _silu(z_ref[:, h * GDN_DV:(h + 1) * GDN_DV])
         for h in range(GDN_HEADS)], axis=-1)
    mixed = branch(oa_ref[...], wa_ref, g0_ref) + branch(ob, wb_ref, g1_ref) + branch(oc_ref[...], wc_ref, g2_ref)
    y_ref[...] = x_ref[...] + jnp.dot(mixed.astype(BF16), wo_ref[...], preferred_element_type=F32)


def _merge(x, oa, ob, oc, p, ng, wa, wb, wc, wo):
    n = x.shape[0]
    tm = _row_tile(n, 512)
    row = lambda w: pl.BlockSpec((tm, w), lambda i: (i, 0))
    gate = lambda j: pl.BlockSpec((tm, D_MODEL), lambda i: (i, (P_COLS - N_BRANCH * D_MODEL) // D_MODEL + j))
    z_spec = pl.BlockSpec((tm, GDN_VW), lambda i: (i, COL_GDN // GDN_VW + 3))
    wsp = lambda k: pl.BlockSpec((k, D_MODEL), lambda i: (0, 0))
    return pl.pallas_call(
        _merge_kernel,
        grid=(n // tm,),
        in_specs=[row(D_MODEL), row(MLA_VW), row(GDN_VW), z_spec, row(HGRN_KW), gate(0), gate(1), gate(2),
                  pl.BlockSpec((1, GDN_DV), lambda i: (0, 0)),
                  wsp(MLA_VW), wsp(GDN_VW), wsp(HGRN_KW), wsp(D_MODEL)],
        out_specs=row(D_MODEL),
        out_shape=jax.ShapeDtypeStruct((n, D_MODEL), F32),
        compiler_params=_cparams(("parallel",)),
        name="merge_out",
    )(x, oa, ob, p, oc, p, p, p, ng, wa, wb, wc, wo)


def _router_kernel(x_ref, g_ref, w_ref, b_ref, cw_ref):
    h = _rms(x_ref[...], g_ref[...])
    logits = jnp.dot(h, w_ref[...], preferred_element_type=F32, precision=lax.Precision.HIGHEST) + b_ref[...]
    lane = lax.broadcasted_iota(jnp.int32, logits.shape, 1)
    valid = lane < N_EXPERTS
    neg = -jnp.inf
    l1 = jnp.where(valid, logits, neg)
    m1 = jnp.max(l1, axis=-1, keepdims=True)
    i1 = jnp.min(jnp.where(l1 == m1, lane, 128), axis=-1, keepdims=True)
    l2 = jnp.where(lane == i1, neg, l1)
    m2 = jnp.max(l2, axis=-1, keepdims=True)
    i2 = jnp.min(jnp.where(l2 == m2, lane, 128), axis=-1, keepdims=True)
    e2 = jnp.exp(m2 - m1)
    den = 1.0 + e2
    cw_ref[...] = jnp.where(lane == i1, 1.0 / den, 0.0) + jnp.where(lane == i2, e2 / den, 0.0)


def _router(x, g, w, b):
    n = x.shape[0]
    tm = _row_tile(n, 512)
    return pl.pallas_call(
        _router_kernel,
        grid=(n // tm,),
        in_specs=[pl.BlockSpec((tm, D_MODEL), lambda i: (i, 0)),
                  pl.BlockSpec((1, D_MODEL), lambda i: (0, 0)),
                  pl.BlockSpec((D_MODEL, 128), lambda i: (0, 0)),
                  pl.BlockSpec((1, 128), lambda i: (0, 0))],
        out_specs=pl.BlockSpec((tm, 128), lambda i: (i, 0)),
        out_shape=jax.ShapeDtypeStruct((n, 128), F32),
        compiler_params=_cparams(("parallel",)),
        name="moe_router",
    )(x, g, w, b)


def _ffn_kernel(x_ref, g_ref, cw_ref, wg_ref, wu_ref, wd_ref, y_ref, h_ref, acc_ref, *, weighted):
    e = pl.program_id(1)

    @pl.when(e == 0)
    def _():
        h_ref[...] = _rms(x_ref[...], g_ref[...]).astype(BF16)
        acc_ref[...] = jnp.zeros(acc_ref.shape, F32)

    h = h_ref[...]
    a = jnp.dot(h, wg_ref[0], preferred_element_type=F32)
    b = jnp.dot(h, wu_ref[0], preferred_element_type=F32)
    y = jnp.dot((_silu(a) * b).astype(BF16), wd_ref[0], preferred_element_type=F32)
    if weighted:
        lane = lax.broadcasted_iota(jnp.int32, cw_ref.shape, 1)
        y = y * jnp.sum(jnp.where(lane == e, cw_ref[...], 0.0), axis=-1, keepdims=True)
    acc_ref[...] += y

    @pl.when(e == pl.num_programs(1) - 1)
    def _():
        y_ref[...] = x_ref[...] + acc_ref[...]


def _ffn(x, g, cw, wg, wu, wd, weighted):
    n = x.shape[0]
    ne, _, ff = wg.shape
    tm = _row_tile(n, 512)
    return pl.pallas_call(
        functools.partial(_ffn_kernel, weighted=weighted),
        grid=(n // tm, ne),
        in_specs=[pl.BlockSpec((tm, D_MODEL), lambda i, e: (i, 0)),
                  pl.BlockSpec((1, D_MODEL), lambda i, e: (0, 0)),
                  pl.BlockSpec((tm, 128), lambda i, e: (i, 0)),
                  pl.BlockSpec((1, D_MODEL, ff), lambda i, e: (e, 0, 0)),
                  pl.BlockSpec((1, D_MODEL, ff), lambda i, e: (e, 0, 0)),
                  pl.BlockSpec((1, ff, D_MODEL), lambda i, e: (e, 0, 0))],
        out_specs=pl.BlockSpec((tm, D_MODEL), lambda i, e: (i, 0)),
        out_shape=jax.ShapeDtypeStruct((n, D_MODEL), F32),
        scratch_shapes=[pltpu.VMEM((tm, D_MODEL), BF16), pltpu.VMEM((tm, D_MODEL), F32)],
        compiler_params=_cparams(("parallel", "arbitrary")),
        name="ffn",
    )(x, g, cw, wg, wu, wd)


def _pad_lanes(x, left, total):
    return jnp.pad(x, [(0, 0)] * (x.ndim - 1) + [(left, total - left - x.shape[-1])])


def _pack_w_in(w):
    cq, ckv, kr, gq, gk, gv, gz, ga, gb, hq, hf, hi, hg, gates = jnp.split(
        w, np.cumsum(SPLIT_SIZES)[:-1].tolist(), axis=-1)
    kr_blk = _pad_lanes(kr, ROPE_LANE0, 128)
    ab_blk = _pad_lanes(jnp.concatenate([ga, gb], axis=-1), 0, 256)
    return jnp.concatenate([cq, ckv, kr_blk, ab_blk, gq, gk, gv, gz, hq, hf, hi, hg, gates], axis=-1).astype(BF16)


def _rope_tables(n_pos):
    half = MLA_ROPE // 2
    inv = ROPE_THETA ** (-jnp.arange(half, dtype=F32) / half)
    ang = jnp.arange(n_pos, dtype=F32)[:, None] * inv[None, :]
    cos, sin = jnp.cos(ang), jnp.sin(ang)
    one = jnp.ones((n_pos, MLA_NOPE), F32)
    zero = jnp.zeros((n_pos, MLA_NOPE), F32)
    tail = jnp.zeros((n_pos, HEAD_PAD - MLA_QK_HEAD), F32)
    z16 = jnp.zeros((n_pos, half), F32)
    c = jnp.concatenate([one, cos, cos, tail], axis=-1)
    s1 = jnp.concatenate([zero, -sin, z16, tail], axis=-1)
    s2 = jnp.concatenate([zero, z16, sin, tail], axis=-1)
    return c, s1, s2


def _layer_weights(l, a):
    f = {}
    f['mixer_g'] = a['mixer_norm_g'][l][None]
    f['w_in'] = _pack_w_in(a['w_in'][l])
    f['gq'] = a['mla_q_norm_g'][l][None]
    f['gkv'] = a['mla_kv_norm_g'][l][None]
    wq = a['mla_w_q_up'][l].reshape(MLA_Q_LORA, MLA_HEADS, MLA_QK_HEAD)
    f['wq'] = _pad_lanes(wq, 0, HEAD_PAD).reshape(MLA_Q_LORA, QK_W).astype(BF16)
    wkv = a['mla_w_kv_up'][l].reshape(MLA_KV_LORA, MLA_HEADS, MLA_NOPE + MLA_V)
    f['wk'] = _pad_lanes(wkv[:, :, :MLA_NOPE], 0, HEAD_PAD).reshape(MLA_KV_LORA, QK_W).astype(BF16)
    f['wv'] = wkv[:, :, MLA_NOPE:].reshape(MLA_KV_LORA, MLA_VW).astype(BF16)
    f['wv_pad'] = _pad_lanes(wkv[:, :, MLA_NOPE:], 0, HEAD_PAD).reshape(MLA_KV_LORA, QK_W).astype(BF16)
    f['hq'] = _pad_lanes(a['mla_q_head_norm_g'][l][None], 0, HEAD_PAD)
    hk = a['mla_k_head_norm_g'][l][None]
    f['hk_nope'] = _pad_lanes(hk[:, :MLA_NOPE], 0, HEAD_PAD)
    f['hk_rope'] = _pad_lanes(hk[:, MLA_NOPE:], ROPE_LANE0, HEAD_PAD)
    f['wo_a'] = a['mla_w_o'][l].astype(BF16)
    f['conv_w'] = a['gdn_conv_w'][l]
    f['alog'] = _pad_lanes(a['gdn_a_log'][l][None], 0, 128)
    f['dt'] = _pad_lanes(a['gdn_dt_bias'][l][None], 0, 128)
    f['gdn_g'] = a['gdn_norm_g'][l][None]
    f['wo_b'] = a['gdn_w_o'][l].astype(BF16)
    f['hgrn_g'] = a['hgrn_norm_g'][l][None]
    f['wo_c'] = a['hgrn_w_o'][l].astype(BF16)
    f['w_out'] = a['w_out'][l].astype(BF16)
    f['ffn_g'] = a['ffn_norm_g'][l][None]
    if l % 2 == 0:
        wg, wu, wd = a['dense_w_gate'][l // 2], a['dense_w_up'][l // 2], a['dense_w_down'][l // 2]
        ff = wg.shape[1]
        half = ff // 2
        f['ffn'] = (jnp.moveaxis(wg.reshape(D_MODEL, 2, half), 1, 0).astype(BF16),
                    jnp.moveaxis(wu.reshape(D_MODEL, 2, half), 1, 0).astype(BF16),
                    wd.reshape(2, half, D_MODEL).astype(BF16))
        f['router'] = None
    else:
        f['ffn'] = (a['moe_w_gate'][l // 2].astype(BF16), a['moe_w_up'][l // 2].astype(BF16),
                    a['moe_w_down'][l // 2].astype(BF16))
        f['router'] = (_pad_lanes(a['moe_w_router'][l // 2], 0, 128),
                       _pad_lanes(a['moe_b_router'][l // 2][None], 0, 128))
    return f


def _trunk_layer(x, b, t, f, lb, tabs, past):
    n = b * t
    p = _in_proj(x, f['mixer_g'], f['w_in'])
    p3 = p.reshape(b, t, P_COLS)

    fresh = past['ckv'] is None
    q, ckv, kr = _mla_pre(p, b, t, f['gq'], f['gkv'], f['wq'], f['hq'], f['hk_rope'], tabs, transposed=fresh)
    if fresh:
        k_all, vt_all = _kv_up(ckv, kr, b, t, f['wk'], f['wv_pad'], f['hk_nope'], transposed=True)
        o_a = _attention_t(q, k_all.reshape(b, t, QK_W), vt_all)
    else:
        s = past['ckv'].shape[1] + t
        ckv_all = jnp.concatenate([past['ckv'], ckv.reshape(b, t, -1)], axis=1).reshape(b * s, -1)
        kr_past = _pad_lanes(past['kr'], ROPE_LANE0, HEAD_PAD)
        kr_all = jnp.concatenate([kr_past, kr.reshape(b, t, -1)], axis=1).reshape(b * s, -1)
        k_all, v_all = _kv_up(ckv_all, kr_all, b, s, f['wk'], f['wv'], f['hk_nope'], transposed=False)
        o_a = _attention(q.reshape(b, t, QK_W), k_all.reshape(b, s, QK_W), v_all.reshape(b, s, MLA_VW),
                         causal=False)

    conv8 = jnp.pad(past['conv'].reshape(b, CONV_W - 1, 3, GDN_QK).transpose(0, 2, 1, 3),
                    ((0, 0), (0, 0), (8 - (CONV_W - 1), 0), (0, 0)))
    o_b, gdn_s = _gdn(p3, f['conv_w'], conv8, past['gdn'], f['alog'], f['dt'])
    gdn_conv = p3[:, t - (CONV_W - 1):, COL_GDN:COL_GDN + 3 * GDN_QK]

    o_c, hgrn_st = _hgrn(p3, lb, jnp.swapaxes(past['hgrn'], -1, -2), f['hgrn_g'])
    hgrn_s = jnp.swapaxes(hgrn_st, -1, -2)

    x = _merge(x, o_a.reshape(n, -1), o_b.reshape(n, -1), o_c.reshape(n, -1), p, f['gdn_g'],
               f['wo_a'], f['wo_b'], f['wo_c'], f['w_out'])

    wg, wu, wd = f['ffn']
    if f['router'] is None:
        cw = jnp.ones((n, 128), F32)
        x = _ffn(x, f['ffn_g'], cw, wg, wu, wd, weighted=False)
    else:
        cw = _router(x, f['ffn_g'], *f['router'])
        x = _ffn(x, f['ffn_g'], cw, wg, wu, wd, weighted=True)

    new_ckv = ckv.reshape(b, t, MLA_KV_LORA)
    new_kr = kr.reshape(b, t, HEAD_PAD)[:, :, ROPE_LANE0:ROPE_LANE0 + MLA_ROPE]
    return x, (new_ckv, new_kr, gdn_s, gdn_conv, hgrn_s)


def kernel(x_prompt, x_sample, cache_mla_ckv, cache_mla_krope, state_gdn, state_gdn_conv, state_hgrn,
           mixer_norm_g, w_in, mla_q_norm_g, mla_w_q_up, mla_kv_norm_g, mla_w_kv_up,
           mla_q_head_norm_g, mla_k_head_norm_g, mla_w_o,
           gdn_conv_w, gdn_a_log, gdn_dt_bias, gdn_norm_g, gdn_w_o,
           hgrn_lb_logits, hgrn_norm_g, hgrn_w_o, w_out, ffn_norm_g,
           dense_w_gate, dense_w_up, dense_w_down,
           moe_w_router, moe_b_router, moe_w_gate, moe_w_up, moe_w_down):
    a = dict(mixer_norm_g=mixer_norm_g, w_in=w_in, mla_q_norm_g=mla_q_norm_g, mla_w_q_up=mla_w_q_up,
             mla_kv_norm_g=mla_kv_norm_g, mla_w_kv_up=mla_w_kv_up, mla_q_head_norm_g=mla_q_head_norm_g,
             mla_k_head_norm_g=mla_k_head_norm_g, mla_w_o=mla_w_o, gdn_conv_w=gdn_conv_w,
             gdn_a_log=gdn_a_log, gdn_dt_bias=gdn_dt_bias, gdn_norm_g=gdn_norm_g, gdn_w_o=gdn_w_o,
             hgrn_norm_g=hgrn_norm_g, hgrn_w_o=hgrn_w_o, w_out=w_out, ffn_norm_g=ffn_norm_g,
             dense_w_gate=dense_w_gate, dense_w_up=dense_w_up, dense_w_down=dense_w_down,
             moe_w_router=moe_w_router, moe_b_router=moe_b_router, moe_w_gate=moe_w_gate,
             moe_w_up=moe_w_up, moe_w_down=moe_w_down)
    depth = w_in.shape[0]
    lb_soft = jax.nn.softmax(hgrn_lb_logits.astype(F32), axis=0)
    hgrn_lb = jnp.cumsum(lb_soft, axis=0) - lb_soft[0]

    b_p, t_p = x_prompt.shape[:2]
    b_s, t_s = x_sample.shape[:2]
    past_len = cache_mla_ckv.shape[2]
    tab_all = _rope_tables(max(t_p, past_len + t_s))
    tabs_p = tuple(tb[:t_p] for tb in tab_all)
    tabs_s = tuple(tb[past_len:past_len + t_s] for tb in tab_all)

    xp = x_prompt.reshape(b_p * t_p, D_MODEL)
    xs = x_sample.reshape(b_s * t_s, D_MODEL)
    past_p = dict(ckv=None, kr=None,
                  gdn=jnp.zeros((b_p, GDN_HEADS, GDN_DK, GDN_DV), F32),
                  conv=jnp.zeros((b_p, CONV_W - 1, 3 * GDN_QK), F32),
                  hgrn=jnp.zeros((b_p, HGRN_HEADS, HGRN_DK, HGRN_DV), F32))
    st_p, st_s = [], []
    for l in range(depth):
        f = _layer_weights(l, a)
        lb = hgrn_lb[l][None]
        past_s = dict(ckv=cache_mla_ckv[l], kr=cache_mla_krope[l], gdn=state_gdn[l],
                      conv=state_gdn_conv[l], hgrn=state_hgrn[l])
        xp, sp = _trunk_layer(xp, b_p, t_p, f, lb, tabs_p, past_p)
        xs, ss = _trunk_layer(xs, b_s, t_s, f, lb, tabs_s, past_s)
        st_p.append(sp)
        st_s.append(ss)

    def stack(lst, i):
        return jnp.stack([s[i] for s in lst], axis=0)

    return (xp.reshape(b_p, t_p, D_MODEL), xs.reshape(b_s, t_s, D_MODEL),
            stack(st_p, 0), stack(st_p, 1), stack(st_p, 2), stack(st_p, 3), stack(st_p, 4),
            stack(st_s, 0), stack(st_s, 1), stack(st_s, 2), stack(st_s, 3), stack(st_s, 4))
```

```python
import functools

import jax
import jax.numpy as jnp
import numpy as np
from jax import lax
from jax.experimental import pallas as pl
from jax.experimental.pallas import tpu as pltpu

F32 = jnp.float32
BF16 = jnp.bfloat16

D_MODEL = 1024
CHUNK = 64
NORM_EPS = 1e-6

MLA_HEADS = 8
MLA_NOPE = 64
MLA_ROPE = 32
MLA_V = 64
MLA_Q_LORA = 384
MLA_KV_LORA = 256
MLA_QK_HEAD = MLA_NOPE + MLA_ROPE
MLA_VW = MLA_HEADS * MLA_V
ROPE_THETA = 10000.0
LOG2_E = 1.4426950408889634
HEAD_PAD = 128
QK_W = MLA_HEADS * HEAD_PAD

GDN_HEADS = 4
GDN_DK = 128
GDN_DV = 128
GDN_QK = GDN_HEADS * GDN_DK
GDN_VW = GDN_HEADS * GDN_DV
CONV_W = 4

HGRN_HEADS = 4
HGRN_DK = 128
HGRN_DV = 128
HGRN_KW = HGRN_HEADS * HGRN_DK

N_BRANCH = 3
SPLIT_SIZES = (MLA_Q_LORA, MLA_KV_LORA, MLA_ROPE,
               GDN_QK, GDN_QK, GDN_VW, GDN_VW, GDN_HEADS, GDN_HEADS,
               HGRN_KW, HGRN_KW, HGRN_KW, HGRN_KW,
               N_BRANCH * D_MODEL)

N_EXPERTS = 8
FF_EXPERT = 1408
MOE_TILE = 1024
MOE_BLOCK_ROWS = 288

P_COLS = 8192
P_TN = 1024
COL_CKV = MLA_Q_LORA
COL_KR = MLA_Q_LORA + MLA_KV_LORA
COL_GAB = COL_KR + 128
COL_GDN = 1024
ROPE_LANE0 = MLA_NOPE

VMEM_LIMIT = 56 * 1024 * 1024


def _cparams(sem):
    return pltpu.CompilerParams(dimension_semantics=sem, vmem_limit_bytes=VMEM_LIMIT)


def _row_tile(n, cap):
    for t in (2048, 1024, 512, 256, 128, 64, 32, 16, 8):
        if t <= cap and n % t == 0:
            return t
    raise ValueError(f"no row tile for {n}")


def _sigmoid(x):
    return 1.0 / (1.0 + jnp.exp(-x))


def _silu(x):
    return x * _sigmoid(x)


def _rms(x, g):
    ms = jnp.mean(x * x, axis=-1, keepdims=True)
    return x * lax.rsqrt(ms + NORM_EPS) * g


def _in_proj_kernel(x_ref, g_ref, w_ref, o_ref, h_ref):
    @pl.when(pl.program_id(1) == 0)
    def _():
        h_ref[...] = _rms(x_ref[...], g_ref[...]).astype(BF16)

    o_ref[...] = jnp.dot(h_ref[...], w_ref[...], preferred_element_type=F32)


def _in_proj(x, g, w):
    n = x.shape[0]
    tm = _row_tile(n, 1024)
    return pl.pallas_call(
        _in_proj_kernel,
        grid=(n // tm, P_COLS // P_TN),
        in_specs=[pl.BlockSpec((tm, D_MODEL), lambda i, j: (i, 0)),
                  pl.BlockSpec((1, D_MODEL), lambda i, j: (0, 0)),
                  pl.BlockSpec((D_MODEL, P_TN), lambda i, j: (0, j))],
        out_specs=pl.BlockSpec((tm, P_TN), lambda i, j: (i, j)),
        out_shape=jax.ShapeDtypeStruct((n, P_COLS), F32),
        scratch_shapes=[pltpu.VMEM((tm, D_MODEL), BF16)],
        compiler_params=_cparams(("parallel", "arbitrary")),
        name="in_proj",
    )(x, g, w)


def _rope(x, c, s1, s2):
    return x * c + pltpu.roll(x, HEAD_PAD - 16, 1) * s1 + pltpu.roll(x, 16, 1) * s2


def _mla_pre_kernel(p_ref, gq_ref, gkv_ref, wq_ref, hq_ref, hk_ref, c_ref, s1_ref, s2_ref,
                    q_ref, ckv_ref, kr_ref, *, scale, transposed):
    c, s1, s2 = c_ref[...], s1_ref[...], s2_ref[...]
    ckv_ref[...] = _rms(p_ref[:, COL_CKV:COL_KR], gkv_ref[...])

    kr = p_ref[:, COL_KR:COL_KR + HEAD_PAD]
    kr_ms = jnp.sum(kr * kr, axis=-1, keepdims=True) * (1.0 / MLA_ROPE)
    kr_ref[...] = _rope(kr * lax.rsqrt(kr_ms + NORM_EPS) * hk_ref[...], c, s1, s2)

    cq = _rms(p_ref[:, 0:MLA_Q_LORA], gq_ref[...]).astype(BF16)
    q = jnp.dot(cq, wq_ref[...], preferred_element_type=F32)
    lane = lax.broadcasted_iota(jnp.int32, (1, HEAD_PAD), 1)
    is_nope = lane < MLA_NOPE
    hq = hq_ref[...]
    for h in range(MLA_HEADS):
        qh = q[:, h * HEAD_PAD:(h + 1) * HEAD_PAD]
        sq = qh * qh
        ms_n = jnp.sum(jnp.where(is_nope, sq, 0.0), axis=-1, keepdims=True) * (1.0 / MLA_NOPE)
        ms_r = jnp.sum(jnp.where(is_nope, 0.0, sq), axis=-1, keepdims=True) * (1.0 / MLA_ROPE)
        inv = jnp.where(is_nope, lax.rsqrt(ms_n + NORM_EPS), lax.rsqrt(ms_r + NORM_EPS))
        qh = _rope(qh * inv * hq, c, s1, s2) * scale
        if transposed:
            q_ref[0, h] = qh.T.astype(BF16)
        else:
            q_ref[:, h * HEAD_PAD:(h + 1) * HEAD_PAD] = qh.astype(BF16)


def _mla_pre(p, b, t_seq, gq, gkv, wq, hq, hk, tabs, transposed):
    n = p.shape[0]
    tm = _row_tile(n, 512)
    c, s1, s2 = tabs
    if tm > t_seq:
        c, s1, s2 = (jnp.tile(t, (tm // t_seq, 1)) for t in (c, s1, s2))
    n_tab = c.shape[0] // tm
    tab_spec = pl.BlockSpec((tm, HEAD_PAD), lambda i: (i % n_tab, 0))
    vec = lambda w: pl.BlockSpec((1, w), lambda i: (0, 0))
    scale = MLA_QK_HEAD ** -0.5
    if transposed:
        scale *= LOG2_E
        q_spec = pl.BlockSpec((1, MLA_HEADS, HEAD_PAD, tm), lambda i: (i // n_tab, 0, 0, i % n_tab))
        q_shape = jax.ShapeDtypeStruct((b, MLA_HEADS, HEAD_PAD, t_seq), BF16)
    else:
        q_spec = pl.BlockSpec((tm, QK_W), lambda i: (i, 0))
        q_shape = jax.ShapeDtypeStruct((n, QK_W), BF16)
    return pl.pallas_call(
        functools.partial(_mla_pre_kernel, scale=scale, transposed=transposed),
        grid=(n // tm,),
        in_specs=[pl.BlockSpec((tm, 1024), lambda i: (i, 0)),
                  vec(MLA_Q_LORA), vec(MLA_KV_LORA),
                  pl.BlockSpec((MLA_Q_LORA, QK_W), lambda i: (0, 0)),
                  vec(HEAD_PAD), vec(HEAD_PAD), tab_spec, tab_spec, tab_spec],
        out_specs=[q_spec,
                   pl.BlockSpec((tm, MLA_KV_LORA), lambda i: (i, 0)),
                   pl.BlockSpec((tm, HEAD_PAD), lambda i: (i, 0))],
        out_shape=[q_shape,
                   jax.ShapeDtypeStruct((n, MLA_KV_LORA), F32),
                   jax.ShapeDtypeStruct((n, HEAD_PAD), F32)],
        compiler_params=_cparams(("parallel",)),
        name="mla_pre",
    )(p, gq, gkv, wq, hq, hk, c, s1, s2)


def _kv_up_kernel(ckv_ref, kr_ref, wk_ref, wv_ref, hk_ref, k_ref, v_ref, *, transposed):
    c = ckv_ref[...].astype(BF16)
    k = jnp.dot(c, wk_ref[...], preferred_element_type=F32)
    kr = kr_ref[...]
    hk = hk_ref[...]
    for h in range(MLA_HEADS):
        kh = k[:, h * HEAD_PAD:(h + 1) * HEAD_PAD]
        ms = jnp.sum(kh * kh, axis=-1, keepdims=True) * (1.0 / MLA_NOPE)
        k_ref[:, h * HEAD_PAD:(h + 1) * HEAD_PAD] = (kh * lax.rsqrt(ms + NORM_EPS) * hk + kr).astype(BF16)
    v = jnp.dot(c, wv_ref[...], preferred_element_type=F32)
    if transposed:
        lane = lax.broadcasted_iota(jnp.int32, (1, HEAD_PAD), 1)
        for h in range(MLA_HEADS):
            vh = jnp.where(lane < MLA_V, v[:, h * HEAD_PAD:(h + 1) * HEAD_PAD], 1.0)
            v_ref[0, h] = vh.T.astype(BF16)
    else:
        v_ref[...] = v.astype(BF16)


def _kv_up(ckv, kr, b, s_len, wk, wv, hk, transposed):
    n = ckv.shape[0]
    tm = _row_tile(n, 512)
    if transposed:
        n_t = s_len // tm
        v_spec = pl.BlockSpec((1, MLA_HEADS, HEAD_PAD, tm), lambda i: (i // n_t, 0, 0, i % n_t))
        v_shape = jax.ShapeDtypeStruct((b, MLA_HEADS, HEAD_PAD, s_len), BF16)
    else:
        v_spec = pl.BlockSpec((tm, MLA_VW), lambda i: (i, 0))
        v_shape = jax.ShapeDtypeStruct((n, MLA_VW), BF16)
    return pl.pallas_call(
        functools.partial(_kv_up_kernel, transposed=transposed),
        grid=(n // tm,),
        in_specs=[pl.BlockSpec((tm, MLA_KV_LORA), lambda i: (i, 0)),
                  pl.BlockSpec((tm, HEAD_PAD), lambda i: (i, 0)),
                  pl.BlockSpec((MLA_KV_LORA, QK_W), lambda i: (0, 0)),
                  pl.BlockSpec((MLA_KV_LORA, wv.shape[1]), lambda i: (0, 0)),
                  pl.BlockSpec((1, HEAD_PAD), lambda i: (0, 0))],
        out_specs=[pl.BlockSpec((tm, QK_W), lambda i: (i, 0)), v_spec],
        out_shape=[jax.ShapeDtypeStruct((n, QK_W), BF16), v_shape],
        compiler_params=_cparams(("parallel",)),
        name="kv_up",
    )(ckv, kr, wk, wv, hk)


def _attn_kernel(q_ref, k_ref, v_ref, o_ref, m_ref, l_ref, acc_ref, *, tq, tk, causal):
    m_ref[...] = jnp.full(m_ref.shape, -jnp.inf, F32)
    l_ref[...] = jnp.zeros(l_ref.shape, F32)
    acc_ref[...] = jnp.zeros(acc_ref.shape, F32)

    def block(start, masked):
        if masked:
            rq = lax.broadcasted_iota(jnp.int32, (tq, tk), 0) // CHUNK
            ck = lax.broadcasted_iota(jnp.int32, (tq, tk), 1) // CHUNK
            allowed = ck <= rq
        def scores(h):
            qh = q_ref[0, :, h * HEAD_PAD:(h + 1) * HEAD_PAD]
            kh = k_ref[0, pl.ds(start, tk), h * HEAD_PAD:(h + 1) * HEAD_PAD]
            return lax.dot_general(qh, kh, (((1,), (1,)), ((), ())), preferred_element_type=F32)

        all_scores = [scores(h) for h in range(MLA_HEADS)]
        for h in range(MLA_HEADS):
            vh = v_ref[0, pl.ds(start, tk), h * MLA_V:(h + 1) * MLA_V]
            s = all_scores[h]
            if masked:
                s = jnp.where(allowed, s, -jnp.inf)
            m_prev = m_ref[h]
            m_new = jnp.maximum(m_prev, jnp.max(s, axis=-1, keepdims=True))
            alpha = jnp.exp(m_prev - m_new)
            p = jnp.exp(s - m_new)
            l_ref[h] = alpha * l_ref[h] + jnp.sum(p, axis=-1, keepdims=True)
            pv = jnp.dot(p.astype(BF16), vh, preferred_element_type=F32)
            acc_ref[h] = alpha * acc_ref[h] + pv
            m_ref[h] = m_new

    if causal:
        qi = pl.program_id(1)

        def body(j, carry):
            block(pl.multiple_of(j * tk, tk), False)
            return carry

        lax.fori_loop(0, qi, body, 0)
        block(pl.multiple_of(qi * tk, tk), True)
    else:
        block(0, False)

    for h in range(MLA_HEADS):
        o_ref[0, :, h * MLA_V:(h + 1) * MLA_V] = acc_ref[h] / l_ref[h]


def _attention(q, k, v, causal):
    b, t, _ = q.shape
    s = k.shape[1]
    if causal:
        tq = tk = min(256, t)
    else:
        tq, tk = t, s
    return pl.pallas_call(
        functools.partial(_attn_kernel, tq=tq, tk=tk, causal=causal),
        grid=(b, t // tq),
        in_specs=[pl.BlockSpec((1, tq, QK_W), lambda i, j: (i, j, 0)),
                  pl.BlockSpec((1, s, QK_W), lambda i, j: (i, 0, 0)),
                  pl.BlockSpec((1, s, MLA_VW), lambda i, j: (i, 0, 0))],
        out_specs=pl.BlockSpec((1, tq, MLA_VW), lambda i, j: (i, j, 0)),
        out_shape=jax.ShapeDtypeStruct((b, t, MLA_VW), F32),
        scratch_shapes=[pltpu.VMEM((MLA_HEADS, tq, 1), F32),
                        pltpu.VMEM((MLA_HEADS, tq, 1), F32),
                        pltpu.VMEM((MLA_HEADS, tq, MLA_V), F32)],
        compiler_params=_cparams(("parallel", "arbitrary")),
        name="mla_attn",
    )(q, k, v)


def _attn_t_kernel(qt_ref, k_ref, vt_ref, o_ref, m_ref, acc_ref, *, tq, tk):
    m_ref[...] = jnp.full(m_ref.shape, -jnp.inf, F32)
    acc_ref[...] = jnp.zeros(acc_ref.shape, F32)

    def blocks(starts, masked):
        if masked:
            ck = lax.broadcasted_iota(jnp.int32, (tk, tq), 0) // CHUNK
            cq = lax.broadcasted_iota(jnp.int32, (tk, tq), 1) // CHUNK
            allowed = ck <= cq

        def scores(start, h):
            kh = k_ref[0, pl.ds(start, tk), h * HEAD_PAD:(h + 1) * HEAD_PAD]
            return jnp.dot(kh, qt_ref[0, h], preferred_element_type=F32)

        def update(start, h, s):
            if masked:
                s = jnp.where(allowed, s, -jnp.inf)
            m_prev = m_ref[h]
            m_new = jnp.maximum(m_prev, jnp.max(s, axis=0, keepdims=True))
            alpha = jnp.exp2(m_prev - m_new)
            p = jnp.exp2(s - m_new).astype(BF16)
            pv = jnp.dot(vt_ref[0, h, :, pl.ds(start, tk)], p, preferred_element_type=F32)
            acc_ref[h] = alpha * acc_ref[h] + pv
            m_ref[h] = m_new

        items = [(start, h) for start in starts for h in range(MLA_HEADS)]
        ahead = 3
        pending = [scores(*it) for it in items[:ahead]]
        for n, it in enumerate(items):
            s = pending.pop(0)
            if n + ahead < len(items):
                pending.append(scores(*items[n + ahead]))
            update(*it, s)

    qi = pl.program_id(1)

    def body(j, carry):
        first = pl.multiple_of(2 * j * tk, tk)
        blocks([first, pl.multiple_of(first + tk, tk)], False)
        return carry

    lax.fori_loop(0, qi // 2, body, 0)

    @pl.when(qi % 2 == 1)
    def _():
        blocks([pl.multiple_of((qi - 1) * tk, tk)], False)

    blocks([pl.multiple_of(qi * tk, tk)], True)

    ot = jnp.concatenate([acc_ref[h, 0:MLA_V] / acc_ref[h, MLA_V:MLA_V + 1] for h in range(MLA_HEADS)], axis=0)
    o_ref[0] = ot.T


def _attention_t(qt, k, vt):
    b, _, _, t = qt.shape
    s = k.shape[1]
    tq = tk = min(256, t)
    return pl.pallas_call(
        functools.partial(_attn_t_kernel, tq=tq, tk=tk),
        grid=(b, t // tq),
        in_specs=[pl.BlockSpec((1, MLA_HEADS, HEAD_PAD, tq), lambda i, j: (i, 0, 0, j)),
                  pl.BlockSpec((1, s, QK_W), lambda i, j: (i, 0, 0)),
                  pl.BlockSpec((1, MLA_HEADS, HEAD_PAD, s), lambda i, j: (i, 0, 0, 0))],
        out_specs=pl.BlockSpec((1, tq, MLA_VW), lambda i, j: (i, j, 0)),
        out_shape=jax.ShapeDtypeStruct((b, t, MLA_VW), F32),
        scratch_shapes=[pltpu.VMEM((MLA_HEADS, 1, tq), F32),
                        pltpu.VMEM((MLA_HEADS, HEAD_PAD, tq), F32)],
        compiler_params=_cparams(("parallel", "arbitrary")),
        name="mla_attn_t",
    )(qt, k, vt)


def _row_iota(shape):
    return lax.broadcasted_iota(jnp.int32, shape, 0)


def _segment_scans(g):
    row = _row_iota(g.shape)
    rows = g.shape[0]
    pre, tot = g, g
    out = [(pre, tot)]
    m = 1
    while m < CHUNK:
        upper = (row // m) % 2 == 1
        from_lower = pltpu.roll(tot, m, 0)
        from_upper = pltpu.roll(tot, rows - m, 0)
        pre = pre + jnp.where(upper, from_lower, 0.0)
        tot = tot + jnp.where(upper, from_lower, from_upper)
        out.append((pre, tot))
        m *= 2
    return out


def _dot_nt(a, b):
    return lax.dot_general(a, b, (((1,), (1,)), ((), ())), preferred_element_type=F32)


def _dot_tn(a, b):
    return lax.dot_general(a, b, (((0,), (0,)), ((), ())), preferred_element_type=F32)


def _gdn_prep_kernel(q_ref, k_ref, v_ref, ab_ref, cw_ref, cs_ref, alog_ref, dt_ref,
                     u_ref, w_ref, qd_ref, kd_ref, qk_ref, gl_ref, carry_ref, *, nb, r):
    @pl.when(pl.program_id(1) == 0)
    def _():
        carry_ref[...] = cs_ref[...]

    n = nb * r
    nc = n // CHUNK

    def conv(x_ref, j):
        w = cw_ref[:, j * GDN_QK:(j + 1) * GDN_QK]
        outs = []
        for b in range(nb):
            x = x_ref[b]
            xp = jnp.concatenate([carry_ref[b, j], x], axis=0)
            y = x * w[3:4]
            for d in range(1, CONV_W):
                y = y + xp[8 - d:8 - d + r] * w[3 - d:4 - d]
            carry_ref[b, j] = x[r - 8:]
            outs.append(_silu(y))
        return outs[0] if nb == 1 else jnp.concatenate(outs, axis=0)

    q_all, k_all, v_all = conv(q_ref, 0), conv(k_ref, 1), conv(v_ref, 2)

    ab = ab_ref[...].reshape(n, 128)
    x = ab + dt_ref[...]
    softplus = jnp.maximum(x, 0.0) + jnp.log(1.0 + jnp.exp(-jnp.abs(x)))
    g_blk = -jnp.exp(alog_ref[...]) * softplus
    gam_blk = _segment_scans(g_blk)[-1][0]
    gam_t = gam_blk.T
    gam3_blk = gam_blk.reshape(nc, CHUNK, 128)
    beta3_blk = _sigmoid(ab).reshape(nc, CHUNK, 128)

    row = lax.broadcasted_iota(jnp.int32, (1, CHUNK, CHUNK), 1)
    col = lax.broadcasted_iota(jnp.int32, (1, CHUNK, CHUNK), 2)
    eye = (row == col).astype(F32)

    def bmm(a, b):
        return jnp.einsum('cij,cjk->cik', a.astype(BF16), b.astype(BF16), preferred_element_type=F32)

    def bmm_nt(a, b):
        return jnp.einsum('cid,cjd->cij', a.astype(BF16), b.astype(BF16), preferred_element_type=F32)

    for h in range(GDN_HEADS):
        sl = slice(h * GDN_DK, (h + 1) * GDN_DK)
        q, k, v = q_all[:, sl], k_all[:, sl], v_all[:, sl]
        q = q * lax.rsqrt(jnp.sum(q * q, axis=-1, keepdims=True) + NORM_EPS) * (GDN_DK ** -0.5)
        k = k * lax.rsqrt(jnp.sum(k * k, axis=-1, keepdims=True) + NORM_EPS)
        q, k, v = (a.reshape(nc, CHUNK, GDN_DK) for a in (q, k, v))
        gam = gam3_blk[:, :, h:h + 1]
        beta = beta3_blk[:, :, GDN_HEADS + h:GDN_HEADS + h + 1]
        gam_row = jnp.stack([gam_t[h:h + 1, c * CHUNK:(c + 1) * CHUNK] for c in range(nc)], axis=0)
        decay = jnp.where(row >= col, jnp.exp(jnp.minimum(gam - gam_row, 0.0)), 0.0)
        a = jnp.where(row > col, beta * bmm_nt(k, k) * decay, 0.0)
        t_inv = eye - a
        pw = a
        m = 1
        while 2 * m < CHUNK:
            pw = bmm(pw, pw)
            t_inv = t_inv + bmm(t_inv, pw)
            m *= 2
        e_gam = jnp.exp(gam)
        gam_last = gam[:, CHUNK - 1:CHUNK, :]
        u_ref[:, h] = bmm(t_inv, v * beta).reshape(nb, r, GDN_DV)
        w_ref[:, h] = bmm(t_inv, k * (beta * e_gam)).astype(BF16).reshape(nb, r, GDN_DK)
        qd_ref[:, h] = (q * e_gam).astype(BF16).reshape(nb, r, GDN_DK)
        kd_ref[:, h] = (k * jnp.exp(gam_last - gam)).astype(BF16).reshape(nb, r, GDN_DK)
        qk_ref[:, h] = (bmm_nt(q, k) * decay).astype(BF16).reshape(nb, r, CHUNK)
        gl_ref[:, h] = jnp.broadcast_to(jnp.exp(gam_last), (nc, 1, 128)).reshape(nb, r // CHUNK, 1, 128)


def _gdn_scan_kernel(u_ref, w_ref, qd_ref, kd_ref, qk_ref, gl_ref, s0_ref, o_ref, sf_ref, s_ref, *, nb):
    c_idx = pl.program_id(1)

    @pl.when(c_idx == 0)
    def _():
        s_ref[...] = s0_ref[...]

    chains = [(b, h) for b in range(nb) for h in range(GDN_HEADS)]
    s_old = [s_ref[b, h] for b, h in chains]
    s_bf = [s.astype(BF16) for s in s_old]
    v_new = [u_ref[b, h] - jnp.dot(w_ref[b, h], sb, preferred_element_type=F32)
             for (b, h), sb in zip(chains, s_bf)]
    v_bf = [v.astype(BF16) for v in v_new]
    for (b, h), s, sb, vb in zip(chains, s_old, s_bf, v_bf):
        o = (jnp.dot(qd_ref[b, h], sb, preferred_element_type=F32)
             + jnp.dot(qk_ref[b, h], vb, preferred_element_type=F32))
        o_ref[b, :, h * GDN_DV:(h + 1) * GDN_DV] = o
        s_ref[b, h] = gl_ref[b, h, 0] * s + _dot_tn(kd_ref[b, h], vb)

    @pl.when(c_idx == pl.num_programs(1) - 1)
    def _():
        sf_ref[...] = s_ref[...]


def _gdn(p3, conv_w, conv_state8, s0, alog, dt):
    b, t, _ = p3.shape
    r = min(512, t)
    nb = max(1, min(b, 512 // r))
    nt = t // r
    nc = t // CHUNK
    blk = lambda j: pl.BlockSpec((nb, r, GDN_QK), lambda i, c: (i, c, COL_GDN // GDN_QK + j))
    vec = pl.BlockSpec((1, 128), lambda i, c: (0, 0))
    head_spec = lambda w: pl.BlockSpec((nb, GDN_HEADS, r, w), lambda i, c: (i, 0, c, 0))
    head_shape = lambda w, dt_: jax.ShapeDtypeStruct((b, GDN_HEADS, t, w), dt_)
    u, w, qd, kd, qk, gl = pl.pallas_call(
        functools.partial(_gdn_prep_kernel, nb=nb, r=r),
        grid=(b // nb, nt),
        in_specs=[blk(0), blk(1), blk(2),
                  pl.BlockSpec((nb, r, 128), lambda i, c: (i, c, COL_GAB // 128)),
                  pl.BlockSpec((CONV_W, 3 * GDN_QK), lambda i, c: (0, 0)),
                  pl.BlockSpec((nb, 3, 8, GDN_QK), lambda i, c: (i, 0, 0, 0)),
                  vec, vec],
        out_specs=[head_spec(GDN_DV), head_spec(GDN_DK), head_spec(GDN_DK), head_spec(GDN_DK),
                   head_spec(CHUNK),
                   pl.BlockSpec((nb, GDN_HEADS, r // CHUNK, 1, 128), lambda i, c: (i, 0, c, 0, 0))],
        out_shape=[head_shape(GDN_DV, F32), head_shape(GDN_DK, BF16), head_shape(GDN_DK, BF16),
                   head_shape(GDN_DK, BF16), head_shape(CHUNK, BF16),
                   jax.ShapeDtypeStruct((b, GDN_HEADS, nc, 1, 128), F32)],
        scratch_shapes=[pltpu.VMEM((nb, 3, 8, GDN_QK), F32)],
        compiler_params=_cparams(("parallel", "arbitrary")),
        name="gdn_prep",
    )(p3, p3, p3, p3, conv_w, conv_state8, alog, dt)

    sb = min(b, 8)
    chunk_spec = lambda w: pl.BlockSpec((sb, GDN_HEADS, CHUNK, w), lambda i, c: (i, 0, c, 0))
    state_spec = pl.BlockSpec((sb, GDN_HEADS, GDN_DK, GDN_DV), lambda i, c: (i, 0, 0, 0))
    return pl.pallas_call(
        functools.partial(_gdn_scan_kernel, nb=sb),
        grid=(b // sb, nc),
        in_specs=[chunk_spec(GDN_DV), chunk_spec(GDN_DK), chunk_spec(GDN_DK), chunk_spec(GDN_DK),
                  chunk_spec(CHUNK),
                  pl.BlockSpec((sb, GDN_HEADS, 1, 1, 128), lambda i, c: (i, 0, c, 0, 0)),
                  state_spec],
        out_specs=[pl.BlockSpec((sb, CHUNK, GDN_VW), lambda i, c: (i, c, 0)), state_spec],
        out_shape=[jax.ShapeDtypeStruct((b, t, GDN_VW), F32),
                   jax.ShapeDtypeStruct((b, GDN_HEADS, GDN_DK, GDN_DV), F32)],
        scratch_shapes=[pltpu.VMEM((sb, GDN_HEADS, GDN_DK, GDN_DV), F32)],
        compiler_params=_cparams(("parallel", "arbitrary")),
        name="gdn_scan",
    )(u, w, qd, kd, qk, gl, s0)


def _hgrn_kernel(q_ref, f_ref, i_ref, gate_ref, lb_ref, s0_ref, ng_ref, o_ref, sf_ref, st_ref):
    c_idx = pl.program_id(1)

    @pl.when(c_idx == 0)
    def _():
        st_ref[...] = s0_ref[0]

    row = lax.broadcasted_iota(jnp.int32, (CHUNK, CHUNK), 0)
    col = lax.broadcasted_iota(jnp.int32, (CHUNK, CHUNK), 1)
    rowv = _row_iota((CHUNK, HGRN_DK))

    heads = range(HGRN_HEADS)
    sls = [slice(h * HGRN_DK, (h + 1) * HGRN_DK) for h in heads]
    qs_, ks_, vs_, scans_ = [], [], [], []
    for sl in sls:
        lb = lb_ref[:, sl]
        f = lb + (1.0 - lb) * _sigmoid(f_ref[0, :, sl])
        qs_.append((_silu(q_ref[0, :, sl]) * (HGRN_DK ** -0.5)))
        ks_.append(1.0 - f)
        vs_.append(i_ref[0, :, sl].astype(BF16))
        scans_.append(_segment_scans(jnp.log(f)))

    att = [jnp.where(row == col, _dot_nt(q.astype(BF16), k.astype(BF16)), 0.0) for q, k in zip(qs_, ks_)]
    m = 1
    lvl = 0
    while m < CHUNK:
        upper = (rowv // m) % 2 == 1
        same_block = row // (2 * m) == col // (2 * m)
        for h in heads:
            pre_m, tot_m = scans_[h][lvl]
            q_lvl = jnp.where(upper, qs_[h] * jnp.exp(pre_m), 0.0).astype(BF16)
            k_lvl = jnp.where(upper, 0.0, ks_[h] * jnp.exp(tot_m - pre_m)).astype(BF16)
            att[h] = att[h] + jnp.where(same_block, _dot_nt(q_lvl, k_lvl), 0.0)
        m *= 2
        lvl += 1

    st_old = [st_ref[h] for h in heads]
    o_st = [_dot_nt((qs_[h] * jnp.exp(scans_[h][-1][0])).astype(BF16), st_old[h].astype(BF16)) for h in heads]
    o_att = [jnp.dot(att[h].astype(BF16), vs_[h], preferred_element_type=F32) for h in heads]
    for h in heads:
        cb, c_tot = scans_[h][-1]
        k_dec = (ks_[h] * jnp.exp(c_tot - cb)).astype(BF16)
        st_ref[h] = st_old[h] * jnp.exp(c_tot[0:1]) + _dot_tn(vs_[h], k_dec)
        o_ref[0, :, sls[h]] = _rms(o_st[h] + o_att[h], ng_ref[...]) * _silu(gate_ref[0, :, sls[h]])

    @pl.when(c_idx == pl.num_programs(1) - 1)
    def _():
        sf_ref[0] = st_ref[...]


def _hgrn(p3, lb, s0t, ng):
    b, t, _ = p3.shape
    nc = t // CHUNK
    col0 = (COL_GDN + 4 * GDN_QK) // HGRN_KW
    blk = lambda j: pl.BlockSpec((1, CHUNK, HGRN_KW), lambda i, c: (i, c, col0 + j))
    return pl.pallas_call(
        _hgrn_kernel,
        grid=(b, nc),
        in_specs=[blk(0), blk(1), blk(2), blk(3),
                  pl.BlockSpec((1, HGRN_KW), lambda i, c: (0, 0)),
                  pl.BlockSpec((1, HGRN_HEADS, HGRN_DV, HGRN_DK), lambda i, c: (i, 0, 0, 0)),
                  pl.BlockSpec((1, HGRN_DV), lambda i, c: (0, 0))],
        out_specs=[pl.BlockSpec((1, CHUNK, HGRN_KW), lambda i, c: (i, c, 0)),
                   pl.BlockSpec((1, HGRN_HEADS, HGRN_DV, HGRN_DK), lambda i, c: (i, 0, 0, 0))],
        out_shape=[jax.ShapeDtypeStruct((b, t, HGRN_KW), F32),
                   jax.ShapeDtypeStruct((b, HGRN_HEADS, HGRN_DV, HGRN_DK), F32)],
        scratch_shapes=[pltpu.VMEM((HGRN_HEADS, HGRN_DV, HGRN_DK), F32)],
        compiler_params=_cparams(("parallel", "arbitrary")),
        name="hgrn",
    )(p3, p3, p3, p3, lb, s0t, ng)


def _merge_kernel(x_ref, oa_ref, ob_ref, z_ref, oc_ref, g0_ref, g1_ref, g2_ref, ng_ref,
                  wa_ref, wb_ref, wc_ref, wo_ref, y_ref):
    def branch(o, w_ref, g_ref):
        return _sigmoid(g_ref[...]) * jnp.dot(o.astype(BF16), w_ref[...], preferred_element_type=F32)

    ob = jnp.concatenate(
        [_rms(ob_ref[:, h * GDN_DV:(h + 1) * GDN_DV], ng_ref[...]) * _silu(z_ref[:, h * GDN_DV:(h + 1) * GDN_DV])
         for h in range(GDN_HEADS)], axis=-1)
    mixed = branch(oa_ref[...], wa_ref, g0_ref) + branch(ob, wb_ref, g1_ref) + branch(oc_ref[...], wc_ref, g2_ref)
    y_ref[...] = x_ref[...] + jnp.dot(mixed.astype(BF16), wo_ref[...], preferred_element_type=F32)


def _merge(x, oa, ob, oc, p, ng, wa, wb, wc, wo):
    n = x.shape[0]
    tm = _row_tile(n, 512)
    row = lambda w: pl.BlockSpec((tm, w), lambda i: (i, 0))
    gate = lambda j: pl.BlockSpec((tm, D_MODEL), lambda i: (i, (P_COLS - N_BRANCH * D_MODEL) // D_MODEL + j))
    z_spec = pl.BlockSpec((tm, GDN_VW), lambda i: (i, COL_GDN // GDN_VW + 3))
    wsp = lambda k: pl.BlockSpec((k, D_MODEL), lambda i: (0, 0))
    return pl.pallas_call(
        _merge_kernel,
        grid=(n // tm,),
        in_specs=[row(D_MODEL), row(MLA_VW), row(GDN_VW), z_spec, row(HGRN_KW), gate(0), gate(1), gate(2),
                  pl.BlockSpec((1, GDN_DV), lambda i: (0, 0)),
                  wsp(MLA_VW), wsp(GDN_VW), wsp(HGRN_KW), wsp(D_MODEL)],
        out_specs=row(D_MODEL),
        out_shape=jax.ShapeDtypeStruct((n, D_MODEL), F32),
        compiler_params=_cparams(("parallel",)),
        name="merge_out",
    )(x, oa, ob, p, oc, p, p, p, ng, wa, wb, wc, wo)


def _router_kernel(x_ref, g_ref, w_ref, b_ref, cw_ref, cwt_ref, cnt_ref):
    h = _rms(x_ref[...], g_ref[...])
    logits = jnp.dot(h, w_ref[...], preferred_element_type=F32, precision=lax.Precision.HIGHEST) + b_ref[...]
    lane = lax.broadcasted_iota(jnp.int32, logits.shape, 1)
    valid = lane < N_EXPERTS
    neg = -jnp.inf
    l1 = jnp.where(valid, logits, neg)
    m1 = jnp.max(l1, axis=-1, keepdims=True)
    i1 = jnp.min(jnp.where(l1 == m1, lane, 128), axis=-1, keepdims=True)
    l2 = jnp.where(lane == i1, neg, l1)
    m2 = jnp.max(l2, axis=-1, keepdims=True)
    i2 = jnp.min(jnp.where(l2 == m2, lane, 128), axis=-1, keepdims=True)
    e2 = jnp.exp(m2 - m1)
    den = 1.0 + e2
    cw = jnp.where(lane == i1, 1.0 / den, 0.0) + jnp.where(lane == i2, e2 / den, 0.0)
    cw_ref[...] = cw
    cwt_ref[...] = cw.T
    cnt_ref[0] = jnp.sum((cw > 0.0).astype(F32), axis=0, keepdims=True)


def _router(x, g, w, b, tm):
    n = x.shape[0]
    return pl.pallas_call(
        _router_kernel,
        grid=(n // tm,),
        in_specs=[pl.BlockSpec((tm, D_MODEL), lambda i: (i, 0)),
                  pl.BlockSpec((1, D_MODEL), lambda i: (0, 0)),
                  pl.BlockSpec((D_MODEL, 128), lambda i: (0, 0)),
                  pl.BlockSpec((1, 128), lambda i: (0, 0))],
        out_specs=[pl.BlockSpec((tm, 128), lambda i: (i, 0)),
                   pl.BlockSpec((128, tm), lambda i: (0, i)),
                   pl.BlockSpec((1, 1, 128), lambda i: (i, 0, 0))],
        out_shape=[jax.ShapeDtypeStruct((n, 128), F32),
                   jax.ShapeDtypeStruct((128, n), F32),
                   jax.ShapeDtypeStruct((n // tm, 1, 128), F32)],
        compiler_params=_cparams(("parallel",)),
        name="moe_router",
    )(x, g, w, b)


def _moe_kernel(cnt_ref, x_ref, g_ref, cw_ref, cwt_ref, wg_ref, wu_ref, wd_ref, y_ref,
                h_ref, rcol_ref, rrow_ref, *, t, br):
    i, e = pl.program_id(0), pl.program_id(1)

    @pl.when(e == 0)
    def _():
        x = x_ref[...]
        h_ref[...] = _rms(x, g_ref[...]).astype(BF16)
        y_ref[...] = x
        r = lax.broadcasted_iota(jnp.int32, (t, t), 0)
        c = lax.broadcasted_iota(jnp.int32, (t, t), 1)
        on = cw_ref[...] > 0.0
        rank = jnp.dot((c < r).astype(BF16), on.astype(BF16), preferred_element_type=F32)
        rcol_ref[...] = jnp.where(on, rank.astype(jnp.int32), -1)
        on_t = cwt_ref[...] > 0.0
        rank_t = jnp.dot(on_t.astype(BF16), (r < c).astype(BF16), preferred_element_type=F32)
        rrow_ref[...] = jnp.where(on_t, rank_t.astype(jnp.int32), -1)

    count = cnt_ref[i * N_EXPERTS + e]
    n_blocks = lax.div(count + (br - 1), br)
    sel = lax.broadcasted_iota(jnp.int32, (1, 128), 1) == e
    rank_c = jnp.sum(jnp.where(sel, rcol_ref[...], 0), axis=-1, keepdims=True)
    w_c = jnp.sum(jnp.where(sel, cw_ref[...], 0.0), axis=-1, keepdims=True)
    rank_r = rrow_ref[pl.ds(e, 1), :]

    def body(j, carry):
        base = j * br
        rows = lax.broadcasted_iota(jnp.int32, (br, t), 0) + base
        xc = jnp.dot((rows == rank_r).astype(BF16), h_ref[...], preferred_element_type=F32).astype(BF16)
        a = jnp.dot(xc, wg_ref[0], preferred_element_type=F32)
        b = jnp.dot(xc, wu_ref[0], preferred_element_type=F32)
        yc = jnp.dot((_silu(a) * b).astype(BF16), wd_ref[0], preferred_element_type=F32).astype(BF16)
        cols = lax.broadcasted_iota(jnp.int32, (t, br), 1) + base
        z = jnp.dot((cols == rank_c).astype(BF16), yc, preferred_element_type=F32)
        y_ref[...] += w_c * z
        return carry

    lax.fori_loop(0, n_blocks, body, 0)


def _moe(x, g, cw, cwt, counts, wg, wu, wd, t):
    n = x.shape[0]
    ne, _, ff = wg.shape
    br = MOE_BLOCK_ROWS
    grid_spec = pltpu.PrefetchScalarGridSpec(
        num_scalar_prefetch=1,
        grid=(n // t, ne),
        in_specs=[pl.BlockSpec((t, D_MODEL), lambda i, e, cnt: (i, 0)),
                  pl.BlockSpec((1, D_MODEL), lambda i, e, cnt: (0, 0)),
                  pl.BlockSpec((t, 128), lambda i, e, cnt: (i, 0)),
                  pl.BlockSpec((128, t), lambda i, e, cnt: (0, i)),
                  pl.BlockSpec((1, D_MODEL, ff), lambda i, e, cnt: (e, 0, 0)),
                  pl.BlockSpec((1, D_MODEL, ff), lambda i, e, cnt: (e, 0, 0)),
                  pl.BlockSpec((1, ff, D_MODEL), lambda i, e, cnt: (e, 0, 0))],
        out_specs=pl.BlockSpec((t, D_MODEL), lambda i, e, cnt: (i, 0)),
        scratch_shapes=[pltpu.VMEM((t, D_MODEL), BF16),
                        pltpu.VMEM((t, 128), jnp.int32),
                        pltpu.VMEM((128, t), jnp.int32)])
    return pl.pallas_call(
        functools.partial(_moe_kernel, t=t, br=br),
        grid_spec=grid_spec,
        out_shape=jax.ShapeDtypeStruct((n, D_MODEL), F32),
        compiler_params=_cparams(("parallel", "arbitrary")),
        name="moe",
    )(counts, x, g, cw, cwt, wg, wu, wd)


def _ffn_kernel(x_ref, g_ref, wg_ref, wu_ref, wd_ref, y_ref, h_ref):
    e = pl.program_id(1)

    @pl.when(e == 0)
    def _():
        x = x_ref[...]
        h_ref[...] = _rms(x, g_ref[...]).astype(BF16)
        y_ref[...] = x

    h = h_ref[...]
    a = jnp.dot(h, wg_ref[0], preferred_element_type=F32)
    b = jnp.dot(h, wu_ref[0], preferred_element_type=F32)
    y_ref[...] += jnp.dot((_silu(a) * b).astype(BF16), wd_ref[0], preferred_element_type=F32)


def _ffn(x, g, wg, wu, wd):
    n = x.shape[0]
    ne, _, ff = wg.shape
    tm = _row_tile(n, 512)
    return pl.pallas_call(
        _ffn_kernel,
        grid=(n // tm, ne),
        in_specs=[pl.BlockSpec((tm, D_MODEL), lambda i, e: (i, 0)),
                  pl.BlockSpec((1, D_MODEL), lambda i, e: (0, 0)),
                  pl.BlockSpec((1, D_MODEL, ff), lambda i, e: (e, 0, 0)),
                  pl.BlockSpec((1, D_MODEL, ff), lambda i, e: (e, 0, 0)),
                  pl.BlockSpec((1, ff, D_MODEL), lambda i, e: (e, 0, 0))],
        out_specs=pl.BlockSpec((tm, D_MODEL), lambda i, e: (i, 0)),
        out_shape=jax.ShapeDtypeStruct((n, D_MODEL), F32),
        scratch_shapes=[pltpu.VMEM((tm, D_MODEL), BF16)],
        compiler_params=_cparams(("parallel", "arbitrary")),
        name="ffn",
    )(x, g, wg, wu, wd)


def _pad_lanes(x, left, total):
    return jnp.pad(x, [(0, 0)] * (x.ndim - 1) + [(left, total - left - x.shape[-1])])


def _pack_w_in(w):
    cq, ckv, kr, gq, gk, gv, gz, ga, gb, hq, hf, hi, hg, gates = jnp.split(
        w, np.cumsum(SPLIT_SIZES)[:-1].tolist(), axis=-1)
    kr_blk = _pad_lanes(kr, ROPE_LANE0, 128)
    ab_blk = _pad_lanes(jnp.concatenate([ga, gb], axis=-1), 0, 256)
    return jnp.concatenate([cq, ckv, kr_blk, ab_blk, gq, gk, gv, gz, hq, hf, hi, hg, gates], axis=-1).astype(BF16)


def _rope_tables(n_pos):
    half = MLA_ROPE // 2
    inv = ROPE_THETA ** (-jnp.arange(half, dtype=F32) / half)
    ang = jnp.arange(n_pos, dtype=F32)[:, None] * inv[None, :]
    cos, sin = jnp.cos(ang), jnp.sin(ang)
    one = jnp.ones((n_pos, MLA_NOPE), F32)
    zero = jnp.zeros((n_pos, MLA_NOPE), F32)
    tail = jnp.zeros((n_pos, HEAD_PAD - MLA_QK_HEAD), F32)
    z16 = jnp.zeros((n_pos, half), F32)
    c = jnp.concatenate([one, cos, cos, tail], axis=-1)
    s1 = jnp.concatenate([zero, -sin, z16, tail], axis=-1)
    s2 = jnp.concatenate([zero, z16, sin, tail], axis=-1)
    return c, s1, s2


def _layer_weights(l, a):
    f = {}
    f['mixer_g'] = a['mixer_norm_g'][l][None]
    f['w_in'] = _pack_w_in(a['w_in'][l])
    f['gq'] = a['mla_q_norm_g'][l][None]
    f['gkv'] = a['mla_kv_norm_g'][l][None]
    wq = a['mla_w_q_up'][l].reshape(MLA_Q_LORA, MLA_HEADS, MLA_QK_HEAD)
    f['wq'] = _pad_lanes(wq, 0, HEAD_PAD).reshape(MLA_Q_LORA, QK_W).astype(BF16)
    wkv = a['mla_w_kv_up'][l].reshape(MLA_KV_LORA, MLA_HEADS, MLA_NOPE + MLA_V)
    f['wk'] = _pad_lanes(wkv[:, :, :MLA_NOPE], 0, HEAD_PAD).reshape(MLA_KV_LORA, QK_W).astype(BF16)
    f['wv'] = wkv[:, :, MLA_NOPE:].reshape(MLA_KV_LORA, MLA_VW).astype(BF16)
    f['wv_pad'] = _pad_lanes(wkv[:, :, MLA_NOPE:], 0, HEAD_PAD).reshape(MLA_KV_LORA, QK_W).astype(BF16)
    f['hq'] = _pad_lanes(a['mla_q_head_norm_g'][l][None], 0, HEAD_PAD)
    hk = a['mla_k_head_norm_g'][l][None]
    f['hk_nope'] = _pad_lanes(hk[:, :MLA_NOPE], 0, HEAD_PAD)
    f['hk_rope'] = _pad_lanes(hk[:, MLA_NOPE:], ROPE_LANE0, HEAD_PAD)
    f['wo_a'] = a['mla_w_o'][l].astype(BF16)
    f['conv_w'] = a['gdn_conv_w'][l]
    f['alog'] = _pad_lanes(a['gdn_a_log'][l][None], 0, 128)
    f['dt'] = _pad_lanes(a['gdn_dt_bias'][l][None], 0, 128)
    f['gdn_g'] = a['gdn_norm_g'][l][None]
    f['wo_b'] = a['gdn_w_o'][l].astype(BF16)
    f['hgrn_g'] = a['hgrn_norm_g'][l][None]
    f['wo_c'] = a['hgrn_w_o'][l].astype(BF16)
    f['w_out'] = a['w_out'][l].astype(BF16)
    f['ffn_g'] = a['ffn_norm_g'][l][None]
    if l % 2 == 0:
        wg, wu, wd = a['dense_w_gate'][l // 2], a['dense_w_up'][l // 2], a['dense_w_down'][l // 2]
        ff = wg.shape[1]
        half = ff // 2
        f['ffn'] = (jnp.moveaxis(wg.reshape(D_MODEL, 2, half), 1, 0).astype(BF16),
                    jnp.moveaxis(wu.reshape(D_MODEL, 2, half), 1, 0).astype(BF16),
                    wd.reshape(2, half, D_MODEL).astype(BF16))
        f['router'] = None
    else:
        f['ffn'] = (a['moe_w_gate'][l // 2].astype(BF16), a['moe_w_up'][l // 2].astype(BF16),
                    a['moe_w_down'][l // 2].astype(BF16))
        f['router'] = (_pad_lanes(a['moe_w_router'][l // 2], 0, 128),
                       _pad_lanes(a['moe_b_router'][l // 2][None], 0, 128))
    return f


def _trunk_layer(x, b, t, f, lb, tabs, past):
    n = b * t
    p = _in_proj(x, f['mixer_g'], f['w_in'])
    p3 = p.reshape(b, t, P_COLS)

    fresh = past['ckv'] is None
    q, ckv, kr = _mla_pre(p, b, t, f['gq'], f['gkv'], f['wq'], f['hq'], f['hk_rope'], tabs, transposed=fresh)
    if fresh:
        k_all, vt_all = _kv_up(ckv, kr, b, t, f['wk'], f['wv_pad'], f['hk_nope'], transposed=True)
        o_a = _attention_t(q, k_all.reshape(b, t, QK_W), vt_all)
    else:
        s = past['ckv'].shape[1] + t
        ckv_all = jnp.concatenate([past['ckv'], ckv.reshape(b, t, -1)], axis=1).reshape(b * s, -1)
        kr_past = _pad_lanes(past['kr'], ROPE_LANE0, HEAD_PAD)
        kr_all = jnp.concatenate([kr_past, kr.reshape(b, t, -1)], axis=1).reshape(b * s, -1)
        k_all, v_all = _kv_up(ckv_all, kr_all, b, s, f['wk'], f['wv'], f['hk_nope'], transposed=False)
        o_a = _attention(q.reshape(b, t, QK_W), k_all.reshape(b, s, QK_W), v_all.reshape(b, s, MLA_VW),
                         causal=False)

    conv8 = jnp.pad(past['conv'].reshape(b, CONV_W - 1, 3, GDN_QK).transpose(0, 2, 1, 3),
                    ((0, 0), (0, 0), (8 - (CONV_W - 1), 0), (0, 0)))
    o_b, gdn_s = _gdn(p3, f['conv_w'], conv8, past['gdn'], f['alog'], f['dt'])
    gdn_conv = p3[:, t - (CONV_W - 1):, COL_GDN:COL_GDN + 3 * GDN_QK]

    o_c, hgrn_st = _hgrn(p3, lb, jnp.swapaxes(past['hgrn'], -1, -2), f['hgrn_g'])
    hgrn_s = jnp.swapaxes(hgrn_st, -1, -2)

    x = _merge(x, o_a.reshape(n, -1), o_b.reshape(n, -1), o_c.reshape(n, -1), p, f['gdn_g'],
               f['wo_a'], f['wo_b'], f['wo_c'], f['w_out'])

    wg, wu, wd = f['ffn']
    if f['router'] is None:
        x = _ffn(x, f['ffn_g'], wg, wu, wd)
    else:
        t_moe = _row_tile(n, MOE_TILE)
        cw, cwt, cnt = _router(x, f['ffn_g'], *f['router'], t_moe)
        counts = cnt[:, 0, :N_EXPERTS].astype(jnp.int32).reshape(-1)
        x = _moe(x, f['ffn_g'], cw, cwt, counts, wg, wu, wd, t_moe)

    new_ckv = ckv.reshape(b, t, MLA_KV_LORA)
    new_kr = kr.reshape(b, t, HEAD_PAD)[:, :, ROPE_LANE0:ROPE_LANE0 + MLA_ROPE]
    return x, (new_ckv, new_kr, gdn_s, gdn_conv, hgrn_s)


def kernel(x_prompt, x_sample, cache_mla_ckv, cache_mla_krope, state_gdn, state_gdn_conv, state_hgrn,
           mixer_norm_g, w_in, mla_q_norm_g, mla_w_q_up, mla_kv_norm_g, mla_w_kv_up,
           mla_q_head_norm_g, mla_k_head_norm_g, mla_w_o,
           gdn_conv_w, gdn_a_log, gdn_dt_bias, gdn_norm_g, gdn_w_o,
           hgrn_lb_logits, hgrn_norm_g, hgrn_w_o, w_out, ffn_norm_g,
           dense_w_gate, dense_w_up, dense_w_down,
           moe_w_router, moe_b_router, moe_w_gate, moe_w_up, moe_w_down):
    a = dict(mixer_norm_g=mixer_norm_g, w_in=w_in, mla_q_norm_g=mla_q_norm_g, mla_w_q_up=mla_w_q_up,
             mla_kv_norm_g=mla_kv_norm_g, mla_w_kv_up=mla_w_kv_up, mla_q_head_norm_g=mla_q_head_norm_g,
             mla_k_head_norm_g=mla_k_head_norm_g, mla_w_o=mla_w_o, gdn_conv_w=gdn_conv_w,
             gdn_a_log=gdn_a_log, gdn_dt_bias=gdn_dt_bias, gdn_norm_g=gdn_norm_g, gdn_w_o=gdn_w_o,
             hgrn_norm_g=hgrn_norm_g, hgrn_w_o=hgrn_w_o, w_out=w_out, ffn_norm_g=ffn_norm_g,
             dense_w_gate=dense_w_gate, dense_w_up=dense_w_up, dense_w_down=dense_w_down,
             moe_w_router=moe_w_router, moe_b_router=moe_b_router, moe_w_gate=moe_w_gate,
             moe_w_up=moe_w_up, moe_w_down=moe_w_down)
    depth = w_in.shape[0]
    lb_soft = jax.nn.softmax(hgrn_lb_logits.astype(F32), axis=0)
    hgrn_lb = jnp.cumsum(lb_soft, axis=0) - lb_soft[0]

    b_p, t_p = x_prompt.shape[:2]
    b_s, t_s = x_sample.shape[:2]
    past_len = cache_mla_ckv.shape[2]
    tab_all = _rope_tables(max(t_p, past_len + t_s))
    tabs_p = tuple(tb[:t_p] for tb in tab_all)
    tabs_s = tuple(tb[past_len:past_len + t_s] for tb in tab_all)

    xp = x_prompt.reshape(b_p * t_p, D_MODEL)
    xs = x_sample.reshape(b_s * t_s, D_MODEL)
    past_p = dict(ckv=None, kr=None,
                  gdn=jnp.zeros((b_p, GDN_HEADS, GDN_DK, GDN_DV), F32),
                  conv=jnp.zeros((b_p, CONV_W - 1, 3 * GDN_QK), F32),
                  hgrn=jnp.zeros((b_p, HGRN_HEADS, HGRN_DK, HGRN_DV), F32))
    st_p, st_s = [], []
    for l in range(depth):
        f = _layer_weights(l, a)
        lb = hgrn_lb[l][None]
        past_s = dict(ckv=cache_mla_ckv[l], kr=cache_mla_krope[l], gdn=state_gdn[l],
                      conv=state_gdn_conv[l], hgrn=state_hgrn[l])
        xp, sp = _trunk_layer(xp, b_p, t_p, f, lb, tabs_p, past_p)
        xs, ss = _trunk_layer(xs, b_s, t_s, f, lb, tabs_s, past_s)
        st_p.append(sp)
        st_s.append(ss)

    def stack(lst, i):
        return jnp.stack([s[i] for s in lst], axis=0)

    return (xp.reshape(b_p, t_p, D_MODEL), xs.reshape(b_s, t_s, D_MODEL),
            stack(st_p, 0), stack(st_p, 1), stack(st_p, 2), stack(st_p, 3), stack(st_p, 4),
            stack(st_s, 0), stack(st_s, 1), stack(st_s, 2), stack(st_s, 3), stack(st_s, 4))
```

```python
import functools

import jax
import jax.numpy as jnp
import numpy as np
from jax import lax
from jax.experimental import pallas as pl
from jax.experimental.pallas import tpu as pltpu

F32 = jnp.float32
BF16 = jnp.bfloat16

D_MODEL = 1024
CHUNK = 64
NORM_EPS = 1e-6

MLA_HEADS = 8
MLA_NOPE = 64
MLA_ROPE = 32
MLA_V = 64
MLA_Q_LORA = 384
MLA_KV_LORA = 256
MLA_QK_HEAD = MLA_NOPE + MLA_ROPE
MLA_VW = MLA_HEADS * MLA_V
ROPE_THETA = 10000.0
LOG2_E = 1.4426950408889634
HEAD_PAD = 128
QK_W = MLA_HEADS * HEAD_PAD
VT_ROWS = MLA_V + 16

GDN_HEADS = 4
GDN_DK = 128
GDN_DV = 128
GDN_QK = GDN_HEADS * GDN_DK
GDN_VW = GDN_HEADS * GDN_DV
CONV_W = 4

HGRN_HEADS = 4
HGRN_DK = 128
HGRN_DV = 128
HGRN_KW = HGRN_HEADS * HGRN_DK

N_BRANCH = 3
SPLIT_SIZES = (MLA_Q_LORA, MLA_KV_LORA, MLA_ROPE,
               GDN_QK, GDN_QK, GDN_VW, GDN_VW, GDN_HEADS, GDN_HEADS,
               HGRN_KW, HGRN_KW, HGRN_KW, HGRN_KW,
               N_BRANCH * D_MODEL)

N_EXPERTS = 8
FF_EXPERT = 1408
MOE_TILE = 1024
MOE_BLOCK_ROWS = 288

P_COLS = 8192
P_MAIN = P_COLS - N_BRANCH * D_MODEL
P_TN = 1024
COL_CKV = MLA_Q_LORA
COL_KR = MLA_Q_LORA + MLA_KV_LORA
COL_GAB = COL_KR + 128
COL_GDN = 1024
ROPE_LANE0 = MLA_NOPE

VMEM_LIMIT = 56 * 1024 * 1024


def _cparams(sem):
    return pltpu.CompilerParams(dimension_semantics=sem, vmem_limit_bytes=VMEM_LIMIT)


def _row_tile(n, cap):
    for t in (2048, 1024, 512, 256, 128, 64, 32, 16, 8):
        if t <= cap and n % t == 0:
            return t
    raise ValueError(f"no row tile for {n}")


def _sigmoid(x):
    return 1.0 / (1.0 + jnp.exp(-x))


def _silu(x):
    return x * _sigmoid(x)


def _rms(x, g):
    ms = jnp.mean(x * x, axis=-1, keepdims=True)
    return x * lax.rsqrt(ms + NORM_EPS) * g


def _in_proj_kernel(x_ref, g_ref, w_ref, o_ref, gate_ref, h_ref):
    j = pl.program_id(1)

    @pl.when(j == 0)
    def _():
        h_ref[...] = _rms(x_ref[...], g_ref[...]).astype(BF16)

    @pl.when(j < P_MAIN // P_TN)
    def _():
        o_ref[...] = jnp.dot(h_ref[...], w_ref[...], preferred_element_type=F32)

    @pl.when(j >= P_MAIN // P_TN)
    def _():
        gate_ref[...] = jnp.dot(h_ref[...], w_ref[...], preferred_element_type=F32).astype(BF16)


def _in_proj(x, g, w):
    n = x.shape[0]
    tm = _row_tile(n, 1024)
    n_main = P_MAIN // P_TN
    return pl.pallas_call(
        _in_proj_kernel,
        grid=(n // tm, P_COLS // P_TN),
        in_specs=[pl.BlockSpec((tm, D_MODEL), lambda i, j: (i, 0)),
                  pl.BlockSpec((1, D_MODEL), lambda i, j: (0, 0)),
                  pl.BlockSpec((D_MODEL, P_TN), lambda i, j: (0, j))],
        out_specs=[pl.BlockSpec((tm, P_TN), lambda i, j: (i, jnp.minimum(j, n_main - 1))),
                   pl.BlockSpec((tm, P_TN), lambda i, j: (i, jnp.maximum(j - n_main, 0)))],
        out_shape=[jax.ShapeDtypeStruct((n, P_MAIN), F32),
                   jax.ShapeDtypeStruct((n, P_COLS - P_MAIN), BF16)],
        scratch_shapes=[pltpu.VMEM((tm, D_MODEL), BF16)],
        compiler_params=_cparams(("parallel", "arbitrary")),
        name="in_proj",
    )(x, g, w)


def _rope(x, c, s1, s2):
    return x * c + pltpu.roll(x, HEAD_PAD - 16, 1) * s1 + pltpu.roll(x, 16, 1) * s2


def _mla_pre_kernel(p_ref, gq_ref, gkv_ref, wq_ref, hq_ref, hk_ref, c_ref, s1_ref, s2_ref,
                    q_ref, ckv_ref, kr_ref, *, scale, transposed):
    c, s1, s2 = c_ref[...], s1_ref[...], s2_ref[...]
    ckv_ref[...] = _rms(p_ref[:, COL_CKV:COL_KR], gkv_ref[...])

    kr = p_ref[:, COL_KR:COL_KR + HEAD_PAD]
    kr_ms = jnp.sum(kr * kr, axis=-1, keepdims=True) * (1.0 / MLA_ROPE)
    kr_ref[...] = _rope(kr * lax.rsqrt(kr_ms + NORM_EPS) * hk_ref[...], c, s1, s2)

    cq = _rms(p_ref[:, 0:MLA_Q_LORA], gq_ref[...]).astype(BF16)
    q = jnp.dot(cq, wq_ref[...], preferred_element_type=F32)
    lane = lax.broadcasted_iota(jnp.int32, (1, HEAD_PAD), 1)
    is_nope = lane < MLA_NOPE
    hq = hq_ref[...]
    for h in range(MLA_HEADS):
        qh = q[:, h * HEAD_PAD:(h + 1) * HEAD_PAD]
        sq = qh * qh
        ms_n = jnp.sum(jnp.where(is_nope, sq, 0.0), axis=-1, keepdims=True) * (1.0 / MLA_NOPE)
        ms_r = jnp.sum(jnp.where(is_nope, 0.0, sq), axis=-1, keepdims=True) * (1.0 / MLA_ROPE)
        inv = jnp.where(is_nope, lax.rsqrt(ms_n + NORM_EPS), lax.rsqrt(ms_r + NORM_EPS))
        qh = _rope(qh * inv * hq, c, s1, s2) * scale
        if transposed:
            q_ref[0, h] = qh.T.astype(BF16)
        else:
            q_ref[:, h * HEAD_PAD:(h + 1) * HEAD_PAD] = qh.astype(BF16)


def _mla_pre(p, b, t_seq, gq, gkv, wq, hq, hk, tabs, transposed):
    n = p.shape[0]
    tm = _row_tile(n, 512)
    c, s1, s2 = tabs
    if tm > t_seq:
        c, s1, s2 = (jnp.tile(t, (tm // t_seq, 1)) for t in (c, s1, s2))
    n_tab = c.shape[0] // tm
    tab_spec = pl.BlockSpec((tm, HEAD_PAD), lambda i: (i % n_tab, 0))
    vec = lambda w: pl.BlockSpec((1, w), lambda i: (0, 0))
    scale = MLA_QK_HEAD ** -0.5
    if transposed:
        scale *= LOG2_E
        q_spec = pl.BlockSpec((1, MLA_HEADS, HEAD_PAD, tm), lambda i: (i // n_tab, 0, 0, i % n_tab))
        q_shape = jax.ShapeDtypeStruct((b, MLA_HEADS, HEAD_PAD, t_seq), BF16)
    else:
        q_spec = pl.BlockSpec((tm, QK_W), lambda i: (i, 0))
        q_shape = jax.ShapeDtypeStruct((n, QK_W), BF16)
    return pl.pallas_call(
        functools.partial(_mla_pre_kernel, scale=scale, transposed=transposed),
        grid=(n // tm,),
        in_specs=[pl.BlockSpec((tm, 1024), lambda i: (i, 0)),
                  vec(MLA_Q_LORA), vec(MLA_KV_LORA),
                  pl.BlockSpec((MLA_Q_LORA, QK_W), lambda i: (0, 0)),
                  vec(HEAD_PAD), vec(HEAD_PAD), tab_spec, tab_spec, tab_spec],
        out_specs=[q_spec,
                   pl.BlockSpec((tm, MLA_KV_LORA), lambda i: (i, 0)),
                   pl.BlockSpec((tm, HEAD_PAD), lambda i: (i, 0))],
        out_shape=[q_shape,
                   jax.ShapeDtypeStruct((n, MLA_KV_LORA), F32),
                   jax.ShapeDtypeStruct((n, HEAD_PAD), F32)],
        compiler_params=_cparams(("parallel",)),
        name="mla_pre",
    )(p, gq, gkv, wq, hq, hk, c, s1, s2)


def _kv_up_kernel(ckv_ref, kr_ref, wk_ref, wv_ref, hk_ref, k_ref, v_ref, *, transposed):
    c = ckv_ref[...].astype(BF16)
    k = jnp.dot(c, wk_ref[...], preferred_element_type=F32)
    kr = kr_ref[...]
    hk = hk_ref[...]
    for h in range(MLA_HEADS):
        kh = k[:, h * HEAD_PAD:(h + 1) * HEAD_PAD]
        ms = jnp.sum(kh * kh, axis=-1, keepdims=True) * (1.0 / MLA_NOPE)
        k_ref[:, h * HEAD_PAD:(h + 1) * HEAD_PAD] = (kh * lax.rsqrt(ms + NORM_EPS) * hk + kr).astype(BF16)
    v = jnp.dot(c, wv_ref[...], preferred_element_type=F32)
    if transposed:
        lane = lax.broadcasted_iota(jnp.int32, (1, HEAD_PAD), 1)
        for h in range(MLA_HEADS):
            vh = jnp.where(lane < MLA_V, v[:, h * HEAD_PAD:(h + 1) * HEAD_PAD], 1.0)
            v_ref[0, h] = vh.T[0:VT_ROWS].astype(BF16)
    else:
        v_ref[...] = v.astype(BF16)


def _kv_up(ckv, kr, b, s_len, wk, wv, hk, transposed):
    n = ckv.shape[0]
    tm = _row_tile(n, 512)
    if transposed:
        n_t = s_len // tm
        v_spec = pl.BlockSpec((1, MLA_HEADS, VT_ROWS, tm), lambda i: (i // n_t, 0, 0, i % n_t))
        v_shape = jax.ShapeDtypeStruct((b, MLA_HEADS, VT_ROWS, s_len), BF16)
    else:
        v_spec = pl.BlockSpec((tm, MLA_VW), lambda i: (i, 0))
        v_shape = jax.ShapeDtypeStruct((n, MLA_VW), BF16)
    return pl.pallas_call(
        functools.partial(_kv_up_kernel, transposed=transposed),
        grid=(n // tm,),
        in_specs=[pl.BlockSpec((tm, MLA_KV_LORA), lambda i: (i, 0)),
                  pl.BlockSpec((tm, HEAD_PAD), lambda i: (i, 0)),
                  pl.BlockSpec((MLA_KV_LORA, QK_W), lambda i: (0, 0)),
                  pl.BlockSpec((MLA_KV_LORA, wv.shape[1]), lambda i: (0, 0)),
                  pl.BlockSpec((1, HEAD_PAD), lambda i: (0, 0))],
        out_specs=[pl.BlockSpec((tm, QK_W), lambda i: (i, 0)), v_spec],
        out_shape=[jax.ShapeDtypeStruct((n, QK_W), BF16), v_shape],
        compiler_params=_cparams(("parallel",)),
        name="kv_up",
    )(ckv, kr, wk, wv, hk)


def _attn_kernel(q_ref, k_ref, v_ref, o_ref, m_ref, l_ref, acc_ref, *, tq, tk, causal):
    m_ref[...] = jnp.full(m_ref.shape, -jnp.inf, F32)
    l_ref[...] = jnp.zeros(l_ref.shape, F32)
    acc_ref[...] = jnp.zeros(acc_ref.shape, F32)

    def block(start, masked):
        if masked:
            rq = lax.broadcasted_iota(jnp.int32, (tq, tk), 0) // CHUNK
            ck = lax.broadcasted_iota(jnp.int32, (tq, tk), 1) // CHUNK
            allowed = ck <= rq
        def scores(h):
            qh = q_ref[0, :, h * HEAD_PAD:(h + 1) * HEAD_PAD]
            kh = k_ref[0, pl.ds(start, tk), h * HEAD_PAD:(h + 1) * HEAD_PAD]
            return lax.dot_general(qh, kh, (((1,), (1,)), ((), ())), preferred_element_type=F32)

        all_scores = [scores(h) for h in range(MLA_HEADS)]
        for h in range(MLA_HEADS):
            vh = v_ref[0, pl.ds(start, tk), h * MLA_V:(h + 1) * MLA_V]
            s = all_scores[h]
            if masked:
                s = jnp.where(allowed, s, -jnp.inf)
            m_prev = m_ref[h]
            m_new = jnp.maximum(m_prev, jnp.max(s, axis=-1, keepdims=True))
            alpha = jnp.exp(m_prev - m_new)
            p = jnp.exp(s - m_new)
            l_ref[h] = alpha * l_ref[h] + jnp.sum(p, axis=-1, keepdims=True)
            pv = jnp.dot(p.astype(BF16), vh, preferred_element_type=F32)
            acc_ref[h] = alpha * acc_ref[h] + pv
            m_ref[h] = m_new

    if causal:
        qi = pl.program_id(1)

        def body(j, carry):
            block(pl.multiple_of(j * tk, tk), False)
            return carry

        lax.fori_loop(0, qi, body, 0)
        block(pl.multiple_of(qi * tk, tk), True)
    else:
        block(0, False)

    for h in range(MLA_HEADS):
        o_ref[0, :, h * MLA_V:(h + 1) * MLA_V] = acc_ref[h] / l_ref[h]


def _attention(q, k, v, causal):
    b, t, _ = q.shape
    s = k.shape[1]
    if causal:
        tq = tk = min(256, t)
    else:
        tq, tk = t, s
    return pl.pallas_call(
        functools.partial(_attn_kernel, tq=tq, tk=tk, causal=causal),
        grid=(b, t // tq),
        in_specs=[pl.BlockSpec((1, tq, QK_W), lambda i, j: (i, j, 0)),
                  pl.BlockSpec((1, s, QK_W), lambda i, j: (i, 0, 0)),
                  pl.BlockSpec((1, s, MLA_VW), lambda i, j: (i, 0, 0))],
        out_specs=pl.BlockSpec((1, tq, MLA_VW), lambda i, j: (i, j, 0)),
        out_shape=jax.ShapeDtypeStruct((b, t, MLA_VW), F32),
        scratch_shapes=[pltpu.VMEM((MLA_HEADS, tq, 1), F32),
                        pltpu.VMEM((MLA_HEADS, tq, 1), F32),
                        pltpu.VMEM((MLA_HEADS, tq, MLA_V), F32)],
        compiler_params=_cparams(("parallel", "arbitrary")),
        name="mla_attn",
    )(q, k, v)


def _attn_t_kernel(qt_ref, k_ref, vt_ref, o_ref, m_ref, acc_ref, *, tq, tk):
    m_ref[...] = jnp.full(m_ref.shape, -jnp.inf, F32)
    acc_ref[...] = jnp.zeros(acc_ref.shape, F32)

    def blocks(starts, masked):
        if masked:
            ck = lax.broadcasted_iota(jnp.int32, (tk, tq), 0) // CHUNK
            cq = lax.broadcasted_iota(jnp.int32, (tk, tq), 1) // CHUNK
            allowed = ck <= cq

        def scores(start, h):
            kh = k_ref[0, pl.ds(start, tk), h * HEAD_PAD:(h + 1) * HEAD_PAD]
            return jnp.dot(kh, qt_ref[0, h], preferred_element_type=F32)

        def update(start, h, s):
            if masked:
                s = jnp.where(allowed, s, -jnp.inf)
            m_prev = m_ref[h]
            m_new = jnp.maximum(m_prev, jnp.max(s, axis=0, keepdims=True))
            alpha = jnp.exp2(m_prev - m_new)
            p = jnp.exp2(s - m_new).astype(BF16)
            pv = jnp.dot(vt_ref[0, h, :, pl.ds(start, tk)], p, preferred_element_type=F32)
            acc_ref[h] = alpha * acc_ref[h] + pv
            m_ref[h] = m_new

        items = [(start, h) for start in starts for h in range(MLA_HEADS)]
        ahead = 3
        pending = [scores(*it) for it in items[:ahead]]
        for n, it in enumerate(items):
            s = pending.pop(0)
            if n + ahead < len(items):
                pending.append(scores(*items[n + ahead]))
            update(*it, s)

    qi = pl.program_id(1)

    def body(j, carry):
        first = pl.multiple_of(2 * j * tk, tk)
        blocks([first, pl.multiple_of(first + tk, tk)], False)
        return carry

    lax.fori_loop(0, qi // 2, body, 0)

    @pl.when(qi % 2 == 1)
    def _():
        blocks([pl.multiple_of((qi - 1) * tk, tk)], False)

    blocks([pl.multiple_of(qi * tk, tk)], True)

    ot = jnp.concatenate([acc_ref[h, 0:MLA_V] / acc_ref[h, MLA_V:MLA_V + 1] for h in range(MLA_HEADS)], axis=0)
    o_ref[0] = ot.T


def _attention_t(qt, k, vt):
    b, _, _, t = qt.shape
    s = k.shape[1]
    tq = tk = min(256, t)
    return pl.pallas_call(
        functools.partial(_attn_t_kernel, tq=tq, tk=tk),
        grid=(b, t // tq),
        in_specs=[pl.BlockSpec((1, MLA_HEADS, HEAD_PAD, tq), lambda i, j: (i, 0, 0, j)),
                  pl.BlockSpec((1, s, QK_W), lambda i, j: (i, 0, 0)),
                  pl.BlockSpec((1, MLA_HEADS, VT_ROWS, s), lambda i, j: (i, 0, 0, 0))],
        out_specs=pl.BlockSpec((1, tq, MLA_VW), lambda i, j: (i, j, 0)),
        out_shape=jax.ShapeDtypeStruct((b, t, MLA_VW), F32),
        scratch_shapes=[pltpu.VMEM((MLA_HEADS, 1, tq), F32),
                        pltpu.VMEM((MLA_HEADS, VT_ROWS, tq), F32)],
        compiler_params=_cparams(("parallel", "arbitrary")),
        name="mla_attn_t",
    )(qt, k, vt)


def _row_iota(shape):
    return lax.broadcasted_iota(jnp.int32, shape, 0)


def _segment_scans(g):
    row = _row_iota(g.shape)
    rows = g.shape[0]
    pre, tot = g, g
    out = [(pre, tot)]
    m = 1
    while m < CHUNK:
        upper = (row // m) % 2 == 1
        from_lower = pltpu.roll(tot, m, 0)
        from_upper = pltpu.roll(tot, rows - m, 0)
        pre = pre + jnp.where(upper, from_lower, 0.0)
        tot = tot + jnp.where(upper, from_lower, from_upper)
        out.append((pre, tot))
        m *= 2
    return out


def _dot_nt(a, b):
    return lax.dot_general(a, b, (((1,), (1,)), ((), ())), preferred_element_type=F32)


def _dot_tn(a, b):
    return lax.dot_general(a, b, (((0,), (0,)), ((), ())), preferred_element_type=F32)


def _gdn_prep_kernel(q_ref, k_ref, v_ref, ab_ref, cw_ref, cs_ref, alog_ref, dt_ref,
                     u_ref, w_ref, qd_ref, kd_ref, qk_ref, gl_ref, carry_ref, *, nb, r):
    @pl.when(pl.program_id(1) == 0)
    def _():
        carry_ref[...] = cs_ref[...]

    n = nb * r
    nc = n // CHUNK

    def conv(x_ref, j):
        w = cw_ref[:, j * GDN_QK:(j + 1) * GDN_QK]
        outs = []
        for b in range(nb):
            x = x_ref[b]
            xp = jnp.concatenate([carry_ref[b, j], x], axis=0)
            y = x * w[3:4]
            for d in range(1, CONV_W):
                y = y + xp[8 - d:8 - d + r] * w[3 - d:4 - d]
            carry_ref[b, j] = x[r - 8:]
            outs.append(_silu(y))
        return outs[0] if nb == 1 else jnp.concatenate(outs, axis=0)

    q_all, k_all, v_all = conv(q_ref, 0), conv(k_ref, 1), conv(v_ref, 2)

    ab = ab_ref[...].reshape(n, 128)
    x = ab + dt_ref[...]
    softplus = jnp.maximum(x, 0.0) + jnp.log(1.0 + jnp.exp(-jnp.abs(x)))
    g_blk = -jnp.exp(alog_ref[...]) * softplus
    gam_blk = _segment_scans(g_blk)[-1][0]
    gam_t = gam_blk.T
    gam3_blk = gam_blk.reshape(nc, CHUNK, 128)
    beta3_blk = _sigmoid(ab).reshape(nc, CHUNK, 128)

    row = lax.broadcasted_iota(jnp.int32, (1, CHUNK, CHUNK), 1)
    col = lax.broadcasted_iota(jnp.int32, (1, CHUNK, CHUNK), 2)
    eye = (row == col).astype(F32)

    def bmm(a, b):
        return jnp.einsum('cij,cjk->cik', a.astype(BF16), b.astype(BF16), preferred_element_type=F32)

    def bmm_nt(a, b):
        return jnp.einsum('cid,cjd->cij', a.astype(BF16), b.astype(BF16), preferred_element_type=F32)

    for h in range(GDN_HEADS):
        sl = slice(h * GDN_DK, (h + 1) * GDN_DK)
        q, k, v = q_all[:, sl], k_all[:, sl], v_all[:, sl]
        q = q * lax.rsqrt(jnp.sum(q * q, axis=-1, keepdims=True) + NORM_EPS) * (GDN_DK ** -0.5)
        k = k * lax.rsqrt(jnp.sum(k * k, axis=-1, keepdims=True) + NORM_EPS)
        q, k, v = (a.reshape(nc, CHUNK, GDN_DK) for a in (q, k, v))
        gam = gam3_blk[:, :, h:h + 1]
        beta = beta3_blk[:, :, GDN_HEADS + h:GDN_HEADS + h + 1]
        gam_row = jnp.stack([gam_t[h:h + 1, c * CHUNK:(c + 1) * CHUNK] for c in range(nc)], axis=0)
        decay = jnp.where(row >= col, jnp.exp(jnp.minimum(gam - gam_row, 0.0)), 0.0)
        a = jnp.where(row > col, beta * bmm_nt(k, k) * decay, 0.0)
        t_inv = eye - a
        pw = a
        m = 1
        while 2 * m < CHUNK:
            pw = bmm(pw, pw)
            t_inv = t_inv + bmm(t_inv, pw)
            m *= 2
        e_gam = jnp.exp(gam)
        gam_last = gam[:, CHUNK - 1:CHUNK, :]
        u_ref[:, h] = bmm(t_inv, v * beta).reshape(nb, r, GDN_DV)
        w_ref[:, h] = bmm(t_inv, k * (beta * e_gam)).astype(BF16).reshape(nb, r, GDN_DK)
        qd_ref[:, h] = (q * e_gam).astype(BF16).reshape(nb, r, GDN_DK)
        kd_ref[:, h] = (k * jnp.exp(gam_last - gam)).astype(BF16).reshape(nb, r, GDN_DK)
        qk_ref[:, h] = (bmm_nt(q, k) * decay).astype(BF16).reshape(nb, r, CHUNK)
        gl_ref[:, h] = jnp.broadcast_to(jnp.exp(gam_last), (nc, 1, 128)).reshape(nb, r // CHUNK, 1, 128)


def _gdn_scan_kernel(u_ref, w_ref, qd_ref, kd_ref, qk_ref, gl_ref, s0_ref, o_ref, sf_ref, s_ref, *, nb):
    c_idx = pl.program_id(1)

    @pl.when(c_idx == 0)
    def _():
        s_ref[...] = s0_ref[...]

    chains = [(b, h) for b in range(nb) for h in range(GDN_HEADS)]
    s_old = [s_ref[b, h] for b, h in chains]
    s_bf = [s.astype(BF16) for s in s_old]
    v_new = [u_ref[b, h] - jnp.dot(w_ref[b, h], sb, preferred_element_type=F32)
             for (b, h), sb in zip(chains, s_bf)]
    v_bf = [v.astype(BF16) for v in v_new]
    for (b, h), s, sb, vb in zip(chains, s_old, s_bf, v_bf):
        o = (jnp.dot(qd_ref[b, h], sb, preferred_element_type=F32)
             + jnp.dot(qk_ref[b, h], vb, preferred_element_type=F32))
        o_ref[b, :, h * GDN_DV:(h + 1) * GDN_DV] = o
        s_ref[b, h] = gl_ref[b, h, 0] * s + _dot_tn(kd_ref[b, h], vb)

    @pl.when(c_idx == pl.num_programs(1) - 1)
    def _():
        sf_ref[...] = s_ref[...]


def _gdn(p3, conv_w, conv_state8, s0, alog, dt):
    b, t, _ = p3.shape
    r = min(512, t)
    nb = max(1, min(b, 512 // r))
    nt = t // r
    nc = t // CHUNK
    blk = lambda j: pl.BlockSpec((nb, r, GDN_QK), lambda i, c: (i, c, COL_GDN // GDN_QK + j))
    vec = pl.BlockSpec((1, 128), lambda i, c: (0, 0))
    head_spec = lambda w: pl.BlockSpec((nb, GDN_HEADS, r, w), lambda i, c: (i, 0, c, 0))
    head_shape = lambda w, dt_: jax.ShapeDtypeStruct((b, GDN_HEADS, t, w), dt_)
    u, w, qd, kd, qk, gl = pl.pallas_call(
        functools.partial(_gdn_prep_kernel, nb=nb, r=r),
        grid=(b // nb, nt),
        in_specs=[blk(0), blk(1), blk(2),
                  pl.BlockSpec((nb, r, 128), lambda i, c: (i, c, COL_GAB // 128)),
                  pl.BlockSpec((CONV_W, 3 * GDN_QK), lambda i, c: (0, 0)),
                  pl.BlockSpec((nb, 3, 8, GDN_QK), lambda i, c: (i, 0, 0, 0)),
                  vec, vec],
        out_specs=[head_spec(GDN_DV), head_spec(GDN_DK), head_spec(GDN_DK), head_spec(GDN_DK),
                   head_spec(CHUNK),
                   pl.BlockSpec((nb, GDN_HEADS, r // CHUNK, 1, 128), lambda i, c: (i, 0, c, 0, 0))],
        out_shape=[head_shape(GDN_DV, F32), head_shape(GDN_DK, BF16), head_shape(GDN_DK, BF16),
                   head_shape(GDN_DK, BF16), head_shape(CHUNK, BF16),
                   jax.ShapeDtypeStruct((b, GDN_HEADS, nc, 1, 128), F32)],
        scratch_shapes=[pltpu.VMEM((nb, 3, 8, GDN_QK), F32)],
        compiler_params=_cparams(("parallel", "arbitrary")),
        name="gdn_prep",
    )(p3, p3, p3, p3, conv_w, conv_state8, alog, dt)

    sb = min(b, 8)
    chunk_spec = lambda w: pl.BlockSpec((sb, GDN_HEADS, CHUNK, w), lambda i, c: (i, 0, c, 0))
    state_spec = pl.BlockSpec((sb, GDN_HEADS, GDN_DK, GDN_DV), lambda i, c: (i, 0, 0, 0))
    return pl.pallas_call(
        functools.partial(_gdn_scan_kernel, nb=sb),
        grid=(b // sb, nc),
        in_specs=[chunk_spec(GDN_DV), chunk_spec(GDN_DK), chunk_spec(GDN_DK), chunk_spec(GDN_DK),
                  chunk_spec(CHUNK),
                  pl.BlockSpec((sb, GDN_HEADS, 1, 1, 128), lambda i, c: (i, 0, c, 0, 0)),
                  state_spec],
        out_specs=[pl.BlockSpec((sb, CHUNK, GDN_VW), lambda i, c: (i, c, 0)), state_spec],
        out_shape=[jax.ShapeDtypeStruct((b, t, GDN_VW), F32),
                   jax.ShapeDtypeStruct((b, GDN_HEADS, GDN_DK, GDN_DV), F32)],
        scratch_shapes=[pltpu.VMEM((sb, GDN_HEADS, GDN_DK, GDN_DV), F32)],
        compiler_params=_cparams(("parallel", "arbitrary")),
        name="gdn_scan",
    )(u, w, qd, kd, qk, gl, s0)


def _hgrn_kernel(q_ref, f_ref, i_ref, gate_ref, lb_ref, s0_ref, ng_ref, o_ref, sf_ref, st_ref):
    c_idx = pl.program_id(1)

    @pl.when(c_idx == 0)
    def _():
        st_ref[...] = s0_ref[0]

    row = lax.broadcasted_iota(jnp.int32, (CHUNK, CHUNK), 0)
    col = lax.broadcasted_iota(jnp.int32, (CHUNK, CHUNK), 1)
    rowv = _row_iota((CHUNK, HGRN_DK))

    heads = range(HGRN_HEADS)
    sls = [slice(h * HGRN_DK, (h + 1) * HGRN_DK) for h in heads]
    qs_, ks_, vs_, scans_ = [], [], [], []
    for sl in sls:
        lb = lb_ref[:, sl]
        f = lb + (1.0 - lb) * _sigmoid(f_ref[0, :, sl])
        qs_.append((_silu(q_ref[0, :, sl]) * (HGRN_DK ** -0.5)))
        ks_.append(1.0 - f)
        vs_.append(i_ref[0, :, sl].astype(BF16))
        scans_.append(_segment_scans(jnp.log(f)))

    att = [jnp.where(row == col, _dot_nt(q.astype(BF16), k.astype(BF16)), 0.0) for q, k in zip(qs_, ks_)]
    m = 1
    lvl = 0
    while m < CHUNK:
        upper = (rowv // m) % 2 == 1
        same_block = row // (2 * m) == col // (2 * m)
        for h in heads:
            pre_m, tot_m = scans_[h][lvl]
            q_lvl = jnp.where(upper, qs_[h] * jnp.exp(pre_m), 0.0).astype(BF16)
            k_lvl = jnp.where(upper, 0.0, ks_[h] * jnp.exp(tot_m - pre_m)).astype(BF16)
            att[h] = att[h] + jnp.where(same_block, _dot_nt(q_lvl, k_lvl), 0.0)
        m *= 2
        lvl += 1

    st_old = [st_ref[h] for h in heads]
    o_st = [_dot_nt((qs_[h] * jnp.exp(scans_[h][-1][0])).astype(BF16), st_old[h].astype(BF16)) for h in heads]
    o_att = [jnp.dot(att[h].astype(BF16), vs_[h], preferred_element_type=F32) for h in heads]
    for h in heads:
        cb, c_tot = scans_[h][-1]
        k_dec = (ks_[h] * jnp.exp(c_tot - cb)).astype(BF16)
        st_ref[h] = st_old[h] * jnp.exp(c_tot[0:1]) + _dot_tn(vs_[h], k_dec)
        o_ref[0, :, sls[h]] = _rms(o_st[h] + o_att[h], ng_ref[...]) * _silu(gate_ref[0, :, sls[h]])

    @pl.when(c_idx == pl.num_programs(1) - 1)
    def _():
        sf_ref[0] = st_ref[...]


def _hgrn(p3, lb, s0t, ng):
    b, t, _ = p3.shape
    nc = t // CHUNK
    col0 = (COL_GDN + 4 * GDN_QK) // HGRN_KW
    blk = lambda j: pl.BlockSpec((1, CHUNK, HGRN_KW), lambda i, c: (i, c, col0 + j))
    return pl.pallas_call(
        _hgrn_kernel,
        grid=(b, nc),
        in_specs=[blk(0), blk(1), blk(2), blk(3),
                  pl.BlockSpec((1, HGRN_KW), lambda i, c: (0, 0)),
                  pl.BlockSpec((1, HGRN_HEADS, HGRN_DV, HGRN_DK), lambda i, c: (i, 0, 0, 0)),
                  pl.BlockSpec((1, HGRN_DV), lambda i, c: (0, 0))],
        out_specs=[pl.BlockSpec((1, CHUNK, HGRN_KW), lambda i, c: (i, c, 0)),
                   pl.BlockSpec((1, HGRN_HEADS, HGRN_DV, HGRN_DK), lambda i, c: (i, 0, 0, 0))],
        out_shape=[jax.ShapeDtypeStruct((b, t, HGRN_KW), F32),
                   jax.ShapeDtypeStruct((b, HGRN_HEADS, HGRN_DV, HGRN_DK), F32)],
        scratch_shapes=[pltpu.VMEM((HGRN_HEADS, HGRN_DV, HGRN_DK), F32)],
        compiler_params=_cparams(("parallel", "arbitrary")),
        name="hgrn",
    )(p3, p3, p3, p3, lb, s0t, ng)


def _merge_kernel(x_ref, oa_ref, ob_ref, z_ref, oc_ref, g0_ref, g1_ref, g2_ref, ng_ref,
                  wa_ref, wb_ref, wc_ref, wo_ref, y_ref):
    def branch(o, w_ref, g_ref):
        gate = _sigmoid(g_ref[...].astype(F32))
        return gate * jnp.dot(o.astype(BF16), w_ref[...], preferred_element_type=F32)

    ob = jnp.concatenate(
        [_rms(ob_ref[:, h * GDN_DV:(h + 1) * GDN_DV], ng_ref[...]) * _silu(z_ref[:, h * GDN_DV:(h + 1) * GDN_DV])
         for h in range(GDN_HEADS)], axis=-1)
    mixed = branch(oa_ref[...], wa_ref, g0_ref) + branch(ob, wb_ref, g1_ref) + branch(oc_ref[...], wc_ref, g2_ref)
    y_ref[...] = x_ref[...] + jnp.dot(mixed.astype(BF16), wo_ref[...], preferred_element_type=F32)


def _merge(x, oa, ob, oc, p, pg, ng, wa, wb, wc, wo):
    n = x.shape[0]
    tm = _row_tile(n, 512)
    row = lambda w: pl.BlockSpec((tm, w), lambda i: (i, 0))
    gate = lambda j: pl.BlockSpec((tm, D_MODEL), lambda i: (i, j))
    z_spec = pl.BlockSpec((tm, GDN_VW), lambda i: (i, COL_GDN // GDN_VW + 3))
    wsp = lambda k: pl.BlockSpec((k, D_MODEL), lambda i: (0, 0))
    return pl.pallas_call(
        _merge_kernel,
        grid=(n // tm,),
        in_specs=[row(D_MODEL), row(MLA_VW), row(GDN_VW), z_spec, row(HGRN_KW), gate(0), gate(1), gate(2),
                  pl.BlockSpec((1, GDN_DV), lambda i: (0, 0)),
                  wsp(MLA_VW), wsp(GDN_VW), wsp(HGRN_KW), wsp(D_MODEL)],
        out_specs=row(D_MODEL),
        out_shape=jax.ShapeDtypeStruct((n, D_MODEL), F32),
        compiler_params=_cparams(("parallel",)),
        name="merge_out",
    )(x, oa, ob, p, oc, pg, pg, pg, ng, wa, wb, wc, wo)


def _router_kernel(x_ref, g_ref, w_ref, b_ref, cw_ref, cwt_ref, cnt_ref):
    h = _rms(x_ref[...], g_ref[...])
    logits = jnp.dot(h, w_ref[...], preferred_element_type=F32, precision=lax.Precision.HIGHEST) + b_ref[...]
    lane = lax.broadcasted_iota(jnp.int32, logits.shape, 1)
    valid = lane < N_EXPERTS
    neg = -jnp.inf
    l1 = jnp.where(valid, logits, neg)
    m1 = jnp.max(l1, axis=-1, keepdims=True)
    i1 = jnp.min(jnp.where(l1 == m1, lane, 128), axis=-1, keepdims=True)
    l2 = jnp.where(lane == i1, neg, l1)
    m2 = jnp.max(l2, axis=-1, keepdims=True)
    i2 = jnp.min(jnp.where(l2 == m2, lane, 128), axis=-1, keepdims=True)
    e2 = jnp.exp(m2 - m1)
    den = 1.0 + e2
    cw = jnp.where(lane == i1, 1.0 / den, 0.0) + jnp.where(lane == i2, e2 / den, 0.0)
    cw_ref[...] = cw
    cwt_ref[...] = cw.T
    cnt_ref[0] = jnp.sum((cw > 0.0).astype(F32), axis=0, keepdims=True)


def _router(x, g, w, b, tm):
    n = x.shape[0]
    return pl.pallas_call(
        _router_kernel,
        grid=(n // tm,),
        in_specs=[pl.BlockSpec((tm, D_MODEL), lambda i: (i, 0)),
                  pl.BlockSpec((1, D_MODEL), lambda i: (0, 0)),
                  pl.BlockSpec((D_MODEL, 128), lambda i: (0, 0)),
                  pl.BlockSpec((1, 128), lambda i: (0, 0))],
        out_specs=[pl.BlockSpec((tm, 128), lambda i: (i, 0)),
                   pl.BlockSpec((128, tm), lambda i: (0, i)),
                   pl.BlockSpec((1, 1, 128), lambda i: (i, 0, 0))],
        out_shape=[jax.ShapeDtypeStruct((n, 128), F32),
                   jax.ShapeDtypeStruct((128, n), F32),
                   jax.ShapeDtypeStruct((n // tm, 1, 128), F32)],
        compiler_params=_cparams(("parallel",)),
        name="moe_router",
    )(x, g, w, b)


def _moe_kernel(cnt_ref, x_ref, g_ref, cw_ref, cwt_ref, wg_ref, wu_ref, wd_ref, y_ref,
                h_ref, rcol_ref, rrow_ref, *, t, br):
    i, e = pl.program_id(0), pl.program_id(1)

    @pl.when(e == 0)
    def _():
        x = x_ref[...]
        h_ref[...] = _rms(x, g_ref[...]).astype(BF16)
        y_ref[...] = x
        r = lax.broadcasted_iota(jnp.int32, (t, t), 0)
        c = lax.broadcasted_iota(jnp.int32, (t, t), 1)
        on = cw_ref[...] > 0.0
        rank = jnp.dot((c < r).astype(BF16), on.astype(BF16), preferred_element_type=F32)
        rcol_ref[...] = jnp.where(on, rank.astype(jnp.int32), -1)
        on_t = cwt_ref[...] > 0.0
        rank_t = jnp.dot(on_t.astype(BF16), (r < c).astype(BF16), preferred_element_type=F32)
        rrow_ref[...] = jnp.where(on_t, rank_t.astype(jnp.int32), -1)

    count = cnt_ref[i * N_EXPERTS + e]
    n_blocks = lax.div(count + (br - 1), br)
    sel = lax.broadcasted_iota(jnp.int32, (1, 128), 1) == e
    rank_c = jnp.sum(jnp.where(sel, rcol_ref[...], 0), axis=-1, keepdims=True)
    w_c = jnp.sum(jnp.where(sel, cw_ref[...], 0.0), axis=-1, keepdims=True)
    rank_r = rrow_ref[pl.ds(e, 1), :]

    def body(j, carry):
        base = j * br
        rows = lax.broadcasted_iota(jnp.int32, (br, t), 0) + base
        xc = jnp.dot((rows == rank_r).astype(BF16), h_ref[...], preferred_element_type=F32).astype(BF16)
        a = jnp.dot(xc, wg_ref[0], preferred_element_type=F32)
        b = jnp.dot(xc, wu_ref[0], preferred_element_type=F32)
        yc = jnp.dot((_silu(a) * b).astype(BF16), wd_ref[0], preferred_element_type=F32).astype(BF16)
        cols = lax.broadcasted_iota(jnp.int32, (t, br), 1) + base
        z = jnp.dot((cols == rank_c).astype(BF16), yc, preferred_element_type=F32)
        y_ref[...] += w_c * z
        return carry

    lax.fori_loop(0, n_blocks, body, 0)


def _moe(x, g, cw, cwt, counts, wg, wu, wd, t):
    n = x.shape[0]
    ne, _, ff = wg.shape
    br = MOE_BLOCK_ROWS
    grid_spec = pltpu.PrefetchScalarGridSpec(
        num_scalar_prefetch=1,
        grid=(n // t, ne),
        in_specs=[pl.BlockSpec((t, D_MODEL), lambda i, e, cnt: (i, 0)),
                  pl.BlockSpec((1, D_MODEL), lambda i, e, cnt: (0, 0)),
                  pl.BlockSpec((t, 128), lambda i, e, cnt: (i, 0)),
                  pl.BlockSpec((128, t), lambda i, e, cnt: (0, i)),
                  pl.BlockSpec((1, D_MODEL, ff), lambda i, e, cnt: (e, 0, 0)),
                  pl.BlockSpec((1, D_MODEL, ff), lambda i, e, cnt: (e, 0, 0)),
                  pl.BlockSpec((1, ff, D_MODEL), lambda i, e, cnt: (e, 0, 0))],
        out_specs=pl.BlockSpec((t, D_MODEL), lambda i, e, cnt: (i, 0)),
        scratch_shapes=[pltpu.VMEM((t, D_MODEL), BF16),
                        pltpu.VMEM((t, 128), jnp.int32),
                        pltpu.VMEM((128, t), jnp.int32)])
    return pl.pallas_call(
        functools.partial(_moe_kernel, t=t, br=br),
        grid_spec=grid_spec,
        out_shape=jax.ShapeDtypeStruct((n, D_MODEL), F32),
        compiler_params=_cparams(("parallel", "arbitrary")),
        name="moe",
    )(counts, x, g, cw, cwt, wg, wu, wd)


def _ffn_kernel(x_ref, g_ref, wg_ref, wu_ref, wd_ref, y_ref, h_ref):
    e = pl.program_id(1)

    @pl.when(e == 0)
    def _():
        x = x_ref[...]
        h_ref[...] = _rms(x, g_ref[...]).astype(BF16)
        y_ref[...] = x

    h = h_ref[...]
    a = jnp.dot(h, wg_ref[0], preferred_element_type=F32)
    b = jnp.dot(h, wu_ref[0], preferred_element_type=F32)
    y_ref[...] += jnp.dot((_silu(a) * b).astype(BF16), wd_ref[0], preferred_element_type=F32)


def _ffn(x, g, wg, wu, wd):
    n = x.shape[0]
    ne, _, ff = wg.shape
    tm = _row_tile(n, 512)
    return pl.pallas_call(
        _ffn_kernel,
        grid=(n // tm, ne),
        in_specs=[pl.BlockSpec((tm, D_MODEL), lambda i, e: (i, 0)),
                  pl.BlockSpec((1, D_MODEL), lambda i, e: (0, 0)),
                  pl.BlockSpec((1, D_MODEL, ff), lambda i, e: (e, 0, 0)),
                  pl.BlockSpec((1, D_MODEL, ff), lambda i, e: (e, 0, 0)),
                  pl.BlockSpec((1, ff, D_MODEL), lambda i, e: (e, 0, 0))],
        out_specs=pl.BlockSpec((tm, D_MODEL), lambda i, e: (i, 0)),
        out_shape=jax.ShapeDtypeStruct((n, D_MODEL), F32),
        scratch_shapes=[pltpu.VMEM((tm, D_MODEL), BF16)],
        compiler_params=_cparams(("parallel", "arbitrary")),
        name="ffn",
    )(x, g, wg, wu, wd)


def _pad_lanes(x, left, total):
    return jnp.pad(x, [(0, 0)] * (x.ndim - 1) + [(left, total - left - x.shape[-1])])


def _pack_w_in(w):
    cq, ckv, kr, gq, gk, gv, gz, ga, gb, hq, hf, hi, hg, gates = jnp.split(
        w, np.cumsum(SPLIT_SIZES)[:-1].tolist(), axis=-1)
    kr_blk = _pad_lanes(kr, ROPE_LANE0, 128)
    ab_blk = _pad_lanes(jnp.concatenate([ga, gb], axis=-1), 0, 256)
    return jnp.concatenate([cq, ckv, kr_blk, ab_blk, gq, gk, gv, gz, hq, hf, hi, hg, gates], axis=-1).astype(BF16)


def _rope_tables(n_pos):
    half = MLA_ROPE // 2
    inv = ROPE_THETA ** (-jnp.arange(half, dtype=F32) / half)
    ang = jnp.arange(n_pos, dtype=F32)[:, None] * inv[None, :]
    cos, sin = jnp.cos(ang), jnp.sin(ang)
    one = jnp.ones((n_pos, MLA_NOPE), F32)
    zero = jnp.zeros((n_pos, MLA_NOPE), F32)
    tail = jnp.zeros((n_pos, HEAD_PAD - MLA_QK_HEAD), F32)
    z16 = jnp.zeros((n_pos, half), F32)
    c = jnp.concatenate([one, cos, cos, tail], axis=-1)
    s1 = jnp.concatenate([zero, -sin, z16, tail], axis=-1)
    s2 = jnp.concatenate([zero, z16, sin, tail], axis=-1)
    return c, s1, s2


def _layer_weights(l, a):
    f = {}
    f['mixer_g'] = a['mixer_norm_g'][l][None]
    f['w_in'] = _pack_w_in(a['w_in'][l])
    f['gq'] = a['mla_q_norm_g'][l][None]
    f['gkv'] = a['mla_kv_norm_g'][l][None]
    wq = a['mla_w_q_up'][l].reshape(MLA_Q_LORA, MLA_HEADS, MLA_QK_HEAD)
    f['wq'] = _pad_lanes(wq, 0, HEAD_PAD).reshape(MLA_Q_LORA, QK_W).astype(BF16)
    wkv = a['mla_w_kv_up'][l].reshape(MLA_KV_LORA, MLA_HEADS, MLA_NOPE + MLA_V)
    f['wk'] = _pad_lanes(wkv[:, :, :MLA_NOPE], 0, HEAD_PAD).reshape(MLA_KV_LORA, QK_W).astype(BF16)
    f['wv'] = wkv[:, :, MLA_NOPE:].reshape(MLA_KV_LORA, MLA_VW).astype(BF16)
    f['wv_pad'] = _pad_lanes(wkv[:, :, MLA_NOPE:], 0, HEAD_PAD).reshape(MLA_KV_LORA, QK_W).astype(BF16)
    f['hq'] = _pad_lanes(a['mla_q_head_norm_g'][l][None], 0, HEAD_PAD)
    hk = a['mla_k_head_norm_g'][l][None]
    f['hk_nope'] = _pad_lanes(hk[:, :MLA_NOPE], 0, HEAD_PAD)
    f['hk_rope'] = _pad_lanes(hk[:, MLA_NOPE:], ROPE_LANE0, HEAD_PAD)
    f['wo_a'] = a['mla_w_o'][l].astype(BF16)
    f['conv_w'] = a['gdn_conv_w'][l]
    f['alog'] = _pad_lanes(a['gdn_a_log'][l][None], 0, 128)
    f['dt'] = _pad_lanes(a['gdn_dt_bias'][l][None], 0, 128)
    f['gdn_g'] = a['gdn_norm_g'][l][None]
    f['wo_b'] = a['gdn_w_o'][l].astype(BF16)
    f['hgrn_g'] = a['hgrn_norm_g'][l][None]
    f['wo_c'] = a['hgrn_w_o'][l].astype(BF16)
    f['w_out'] = a['w_out'][l].astype(BF16)
    f['ffn_g'] = a['ffn_norm_g'][l][None]
    if l % 2 == 0:
        wg, wu, wd = a['dense_w_gate'][l // 2], a['dense_w_up'][l // 2], a['dense_w_down'][l // 2]
        ff = wg.shape[1]
        half = ff // 2
        f['ffn'] = (jnp.moveaxis(wg.reshape(D_MODEL, 2, half), 1, 0).astype(BF16),
                    jnp.moveaxis(wu.reshape(D_MODEL, 2, half), 1, 0).astype(BF16),
                    wd.reshape(2, half, D_MODEL).astype(BF16))
        f['router'] = None
    else:
        f['ffn'] = (a['moe_w_gate'][l // 2].astype(BF16), a['moe_w_up'][l // 2].astype(BF16),
                    a['moe_w_down'][l // 2].astype(BF16))
        f['router'] = (_pad_lanes(a['moe_w_router'][l // 2], 0, 128),
                       _pad_lanes(a['moe_b_router'][l // 2][None], 0, 128))
    return f


def _trunk_layer(x, b, t, f, lb, tabs, past):
    n = b * t
    p, p_gate = _in_proj(x, f['mixer_g'], f['w_in'])
    p3 = p.reshape(b, t, P_MAIN)

    fresh = past['ckv'] is None
    q, ckv, kr = _mla_pre(p, b, t, f['gq'], f['gkv'], f['wq'], f['hq'], f['hk_rope'], tabs, transposed=fresh)
    if fresh:
        k_all, vt_all = _kv_up(ckv, kr, b, t, f['wk'], f['wv_pad'], f['hk_nope'], transposed=True)
        o_a = _attention_t(q, k_all.reshape(b, t, QK_W), vt_all)
    else:
        s = past['ckv'].shape[1] + t
        ckv_all = jnp.concatenate([past['ckv'], ckv.reshape(b, t, -1)], axis=1).reshape(b * s, -1)
        kr_past = _pad_lanes(past['kr'], ROPE_LANE0, HEAD_PAD)
        kr_all = jnp.concatenate([kr_past, kr.reshape(b, t, -1)], axis=1).reshape(b * s, -1)
        k_all, v_all = _kv_up(ckv_all, kr_all, b, s, f['wk'], f['wv'], f['hk_nope'], transposed=False)
        o_a = _attention(q.reshape(b, t, QK_W), k_all.reshape(b, s, QK_W), v_all.reshape(b, s, MLA_VW),
                         causal=False)

    conv8 = jnp.pad(past['conv'].reshape(b, CONV_W - 1, 3, GDN_QK).transpose(0, 2, 1, 3),
                    ((0, 0), (0, 0), (8 - (CONV_W - 1), 0), (0, 0)))
    o_b, gdn_s = _gdn(p3, f['conv_w'], conv8, past['gdn'], f['alog'], f['dt'])
    gdn_conv = p3[:, t - (CONV_W - 1):, COL_GDN:COL_GDN + 3 * GDN_QK]

    o_c, hgrn_st = _hgrn(p3, lb, jnp.swapaxes(past['hgrn'], -1, -2), f['hgrn_g'])
    hgrn_s = jnp.swapaxes(hgrn_st, -1, -2)

    x = _merge(x, o_a.reshape(n, -1), o_b.reshape(n, -1), o_c.reshape(n, -1), p, p_gate, f['gdn_g'],
               f['wo_a'], f['wo_b'], f['wo_c'], f['w_out'])

    wg, wu, wd = f['ffn']
    if f['router'] is None:
        x = _ffn(x, f['ffn_g'], wg, wu, wd)
    else:
        t_moe = _row_tile(n, MOE_TILE)
        cw, cwt, cnt = _router(x, f['ffn_g'], *f['router'], t_moe)
        counts = cnt[:, 0, :N_EXPERTS].astype(jnp.int32).reshape(-1)
        x = _moe(x, f['ffn_g'], cw, cwt, counts, wg, wu, wd, t_moe)

    new_ckv = ckv.reshape(b, t, MLA_KV_LORA)
    new_kr = kr.reshape(b, t, HEAD_PAD)[:, :, ROPE_LANE0:ROPE_LANE0 + MLA_ROPE]
    return x, (new_ckv, new_kr, gdn_s, gdn_conv, hgrn_s)


def kernel(x_prompt, x_sample, cache_mla_ckv, cache_mla_krope, state_gdn, state_gdn_conv, state_hgrn,
           mixer_norm_g, w_in, mla_q_norm_g, mla_w_q_up, mla_kv_norm_g, mla_w_kv_up,
           mla_q_head_norm_g, mla_k_head_norm_g, mla_w_o,
           gdn_conv_w, gdn_a_log, gdn_dt_bias, gdn_norm_g, gdn_w_o,
           hgrn_lb_logits, hgrn_norm_g, hgrn_w_o, w_out, ffn_norm_g,
           dense_w_gate, dense_w_up, dense_w_down,
           moe_w_router, moe_b_router, moe_w_gate, moe_w_up, moe_w_down):
    a = dict(mixer_norm_g=mixer_norm_g, w_in=w_in, mla_q_norm_g=mla_q_norm_g, mla_w_q_up=mla_w_q_up,
             mla_kv_norm_g=mla_kv_norm_g, mla_w_kv_up=mla_w_kv_up, mla_q_head_norm_g=mla_q_head_norm_g,
             mla_k_head_norm_g=mla_k_head_norm_g, mla_w_o=mla_w_o, gdn_conv_w=gdn_conv_w,
             gdn_a_log=gdn_a_log, gdn_dt_bias=gdn_dt_bias, gdn_norm_g=gdn_norm_g, gdn_w_o=gdn_w_o,
             hgrn_norm_g=hgrn_norm_g, hgrn_w_o=hgrn_w_o, w_out=w_out, ffn_norm_g=ffn_norm_g,
             dense_w_gate=dense_w_gate, dense_w_up=dense_w_up, dense_w_down=dense_w_down,
             moe_w_router=moe_w_router, moe_b_router=moe_b_router, moe_w_gate=moe_w_gate,
             moe_w_up=moe_w_up, moe_w_down=moe_w_down)
    depth = w_in.shape[0]
    lb_soft = jax.nn.softmax(hgrn_lb_logits.astype(F32), axis=0)
    hgrn_lb = jnp.cumsum(lb_soft, axis=0) - lb_soft[0]

    b_p, t_p = x_prompt.shape[:2]
    b_s, t_s = x_sample.shape[:2]
    past_len = cache_mla_ckv.shape[2]
    tab_all = _rope_tables(max(t_p, past_len + t_s))
    tabs_p = tuple(tb[:t_p] for tb in tab_all)
    tabs_s = tuple(tb[past_len:past_len + t_s] for tb in tab_all)

    xp = x_prompt.reshape(b_p * t_p, D_MODEL)
    xs = x_sample.reshape(b_s * t_s, D_MODEL)
    past_p = dict(ckv=None, kr=None,
                  gdn=jnp.zeros((b_p, GDN_HEADS, GDN_DK, GDN_DV), F32),
                  conv=jnp.zeros((b_p, CONV_W - 1, 3 * GDN_QK), F32),
                  hgrn=jnp.zeros((b_p, HGRN_HEADS, HGRN_DK, HGRN_DV), F32))
    st_p, st_s = [], []
    for l in range(depth):
        f = _layer_weights(l, a)
        lb = hgrn_lb[l][None]
        past_s = dict(ckv=cache_mla_ckv[l], kr=cache_mla_krope[l], gdn=state_gdn[l],
                      conv=state_gdn_conv[l], hgrn=state_hgrn[l])
        xp, sp = _trunk_layer(xp, b_p, t_p, f, lb, tabs_p, past_p)
        xs, ss = _trunk_layer(xs, b_s, t_s, f, lb, tabs_s, past_s)
        st_p.append(sp)
        st_s.append(ss)

    def stack(lst, i):
        return jnp.stack([s[i] for s in lst], axis=0)

    return (xp.reshape(b_p, t_p, D_MODEL), xs.reshape(b_s, t_s, D_MODEL),
            stack(st_p, 0), stack(st_p, 1), stack(st_p, 2), stack(st_p, 3), stack(st_p, 4),
            stack(st_s, 0), stack(st_s, 1), stack(st_s, 2), stack(st_s, 3), stack(st_s, 4))
```

```python
import functools

import jax
import jax.numpy as jnp
import numpy as np
from jax import lax
from jax.experimental import pallas as pl
from jax.experimental.pallas import tpu as pltpu

F32 = jnp.float32
BF16 = jnp.bfloat16

D_MODEL = 1024
CHUNK = 64
NORM_EPS = 1e-6

MLA_HEADS = 8
MLA_NOPE = 64
MLA_ROPE = 32
MLA_V = 64
MLA_Q_LORA = 384
MLA_KV_LORA = 256
MLA_QK_HEAD = MLA_NOPE + MLA_ROPE
MLA_VW = MLA_HEADS * MLA_V
ROPE_THETA = 10000.0
LOG2_E = 1.4426950408889634
HEAD_PAD = 128
QK_W = MLA_HEADS * HEAD_PAD
VT_ROWS = MLA_V + 16

GDN_HEADS = 4
GDN_DK = 128
GDN_DV = 128
GDN_QK = GDN_HEADS * GDN_DK
GDN_VW = GDN_HEADS * GDN_DV
CONV_W = 4
GDN_PREP_ROWS = 1024

HGRN_HEADS = 4
HGRN_DK = 128
HGRN_DV = 128
HGRN_KW = HGRN_HEADS * HGRN_DK

N_BRANCH = 3
SPLIT_SIZES = (MLA_Q_LORA, MLA_KV_LORA, MLA_ROPE,
               GDN_QK, GDN_QK, GDN_VW, GDN_VW, GDN_HEADS, GDN_HEADS,
               HGRN_KW, HGRN_KW, HGRN_KW, HGRN_KW,
               N_BRANCH * D_MODEL)

N_EXPERTS = 8
FF_EXPERT = 1408
MOE_TILE = 1024
MOE_BLOCK_ROWS = 288

P_COLS = 8192
P_MAIN = P_COLS - N_BRANCH * D_MODEL
P_TN = 512
COL_CKV = MLA_Q_LORA
COL_KR = MLA_Q_LORA + MLA_KV_LORA
COL_GAB = COL_KR + 128
COL_GDN = 1024
ROPE_LANE0 = MLA_NOPE

VMEM_LIMIT = 56 * 1024 * 1024


def _cparams(sem):
    return pltpu.CompilerParams(dimension_semantics=sem, vmem_limit_bytes=VMEM_LIMIT)


def _row_tile(n, cap):
    for t in (2048, 1024, 512, 256, 128, 64, 32, 16, 8):
        if t <= cap and n % t == 0:
            return t
    raise ValueError(f"no row tile for {n}")


def _sigmoid(x):
    return 1.0 / (1.0 + jnp.exp(-x))


def _silu(x):
    return x * (0.5 * jnp.tanh(0.5 * x) + 0.5)


def _rms(x, g):
    ms = jnp.mean(x * x, axis=-1, keepdims=True)
    return x * lax.rsqrt(ms + NORM_EPS) * g


def _in_proj_kernel(x_ref, g_ref, w_ref, o_ref, gate_ref, h_ref):
    j = pl.program_id(1)

    @pl.when(j == 0)
    def _():
        h_ref[...] = _rms(x_ref[...], g_ref[...]).astype(BF16)

    @pl.when(j < P_MAIN // P_TN)
    def _():
        o_ref[...] = jnp.dot(h_ref[...], w_ref[...], preferred_element_type=F32)

    @pl.when(j >= P_MAIN // P_TN)
    def _():
        gate_ref[...] = jnp.dot(h_ref[...], w_ref[...], preferred_element_type=F32).astype(BF16)


def _in_proj(x, g, w):
    n = x.shape[0]
    tm = _row_tile(n, 2048)
    n_main = P_MAIN // P_TN
    return pl.pallas_call(
        _in_proj_kernel,
        grid=(n // tm, P_COLS // P_TN),
        in_specs=[pl.BlockSpec((tm, D_MODEL), lambda i, j: (i, 0)),
                  pl.BlockSpec((1, D_MODEL), lambda i, j: (0, 0)),
                  pl.BlockSpec((D_MODEL, P_TN), lambda i, j: (0, j))],
        out_specs=[pl.BlockSpec((tm, P_TN), lambda i, j: (i, jnp.minimum(j, n_main - 1))),
                   pl.BlockSpec((tm, P_TN), lambda i, j: (i, jnp.maximum(j - n_main, 0)))],
        out_shape=[jax.ShapeDtypeStruct((n, P_MAIN), F32),
                   jax.ShapeDtypeStruct((n, P_COLS - P_MAIN), BF16)],
        scratch_shapes=[pltpu.VMEM((tm, D_MODEL), BF16)],
        compiler_params=_cparams(("parallel", "arbitrary")),
        name="in_proj",
    )(x, g, w)


def _rope(x, c, s1, s2):
    return x * c + pltpu.roll(x, HEAD_PAD - 16, 1) * s1 + pltpu.roll(x, 16, 1) * s2


def _mla_pre_kernel(p_ref, gq_ref, gkv_ref, wq_ref, hq_ref, hk_ref, c_ref, s1_ref, s2_ref,
                    q_ref, ckv_ref, kr_ref, *, scale, transposed):
    c, s1, s2 = c_ref[...], s1_ref[...], s2_ref[...]
    ckv_ref[...] = _rms(p_ref[:, COL_CKV:COL_KR], gkv_ref[...])

    kr = p_ref[:, COL_KR:COL_KR + HEAD_PAD]
    kr_ms = jnp.sum(kr * kr, axis=-1, keepdims=True) * (1.0 / MLA_ROPE)
    kr_ref[...] = _rope(kr * lax.rsqrt(kr_ms + NORM_EPS) * hk_ref[...], c, s1, s2)

    cq = _rms(p_ref[:, 0:MLA_Q_LORA], gq_ref[...]).astype(BF16)
    q = jnp.dot(cq, wq_ref[...], preferred_element_type=F32)
    lane = lax.broadcasted_iota(jnp.int32, (1, HEAD_PAD), 1)
    is_nope = lane < MLA_NOPE
    hq = hq_ref[...]
    for h in range(MLA_HEADS):
        qh = q[:, h * HEAD_PAD:(h + 1) * HEAD_PAD]
        sq = qh * qh
        ms_n = jnp.sum(jnp.where(is_nope, sq, 0.0), axis=-1, keepdims=True) * (1.0 / MLA_NOPE)
        ms_r = jnp.sum(jnp.where(is_nope, 0.0, sq), axis=-1, keepdims=True) * (1.0 / MLA_ROPE)
        inv = jnp.where(is_nope, lax.rsqrt(ms_n + NORM_EPS), lax.rsqrt(ms_r + NORM_EPS))
        qh = _rope(qh * inv * hq, c, s1, s2) * scale
        if transposed:
            q_ref[0, h] = qh.T.astype(BF16)
        else:
            q_ref[:, h * HEAD_PAD:(h + 1) * HEAD_PAD] = qh.astype(BF16)


def _mla_pre(p, b, t_seq, gq, gkv, wq, hq, hk, tabs, transposed):
    n = p.shape[0]
    tm = _row_tile(n, 512)
    c, s1, s2 = tabs
    if tm > t_seq:
        c, s1, s2 = (jnp.tile(t, (tm // t_seq, 1)) for t in (c, s1, s2))
    n_tab = c.shape[0] // tm
    tab_spec = pl.BlockSpec((tm, HEAD_PAD), lambda i: (i % n_tab, 0))
    vec = lambda w: pl.BlockSpec((1, w), lambda i: (0, 0))
    scale = MLA_QK_HEAD ** -0.5
    if transposed:
        scale *= LOG2_E
        q_spec = pl.BlockSpec((1, MLA_HEADS, HEAD_PAD, tm), lambda i: (i // n_tab, 0, 0, i % n_tab))
        q_shape = jax.ShapeDtypeStruct((b, MLA_HEADS, HEAD_PAD, t_seq), BF16)
    else:
        q_spec = pl.BlockSpec((tm, QK_W), lambda i: (i, 0))
        q_shape = jax.ShapeDtypeStruct((n, QK_W), BF16)
    return pl.pallas_call(
        functools.partial(_mla_pre_kernel, scale=scale, transposed=transposed),
        grid=(n // tm,),
        in_specs=[pl.BlockSpec((tm, 1024), lambda i: (i, 0)),
                  vec(MLA_Q_LORA), vec(MLA_KV_LORA),
                  pl.BlockSpec((MLA_Q_LORA, QK_W), lambda i: (0, 0)),
                  vec(HEAD_PAD), vec(HEAD_PAD), tab_spec, tab_spec, tab_spec],
        out_specs=[q_spec,
                   pl.BlockSpec((tm, MLA_KV_LORA), lambda i: (i, 0)),
                   pl.BlockSpec((tm, HEAD_PAD), lambda i: (i, 0))],
        out_shape=[q_shape,
                   jax.ShapeDtypeStruct((n, MLA_KV_LORA), F32),
                   jax.ShapeDtypeStruct((n, HEAD_PAD), F32)],
        compiler_params=_cparams(("parallel",)),
        name="mla_pre",
    )(p, gq, gkv, wq, hq, hk, c, s1, s2)


def _kv_up_kernel(ckv_ref, kr_ref, wk_ref, wv_ref, hk_ref, k_ref, v_ref, *, transposed):
    c = ckv_ref[...].astype(BF16)
    k = jnp.dot(c, wk_ref[...], preferred_element_type=F32)
    kr = kr_ref[...]
    hk = hk_ref[...]
    for h in range(MLA_HEADS):
        kh = k[:, h * HEAD_PAD:(h + 1) * HEAD_PAD]
        ms = jnp.sum(kh * kh, axis=-1, keepdims=True) * (1.0 / MLA_NOPE)
        k_ref[:, h * HEAD_PAD:(h + 1) * HEAD_PAD] = (kh * lax.rsqrt(ms + NORM_EPS) * hk + kr).astype(BF16)
    v = jnp.dot(c, wv_ref[...], preferred_element_type=F32)
    if transposed:
        lane = lax.broadcasted_iota(jnp.int32, (1, HEAD_PAD), 1)
        for h in range(MLA_HEADS):
            vh = jnp.where(lane < MLA_V, v[:, h * HEAD_PAD:(h + 1) * HEAD_PAD], 1.0)
            v_ref[0, h] = vh.T[0:VT_ROWS].astype(BF16)
    else:
        v_ref[...] = v.astype(BF16)


def _kv_up(ckv, kr, b, s_len, wk, wv, hk, transposed):
    n = ckv.shape[0]
    tm = _row_tile(n, 512)
    if transposed:
        n_t = s_len // tm
        v_spec = pl.BlockSpec((1, MLA_HEADS, VT_ROWS, tm), lambda i: (i // n_t, 0, 0, i % n_t))
        v_shape = jax.ShapeDtypeStruct((b, MLA_HEADS, VT_ROWS, s_len), BF16)
    else:
        v_spec = pl.BlockSpec((tm, MLA_VW), lambda i: (i, 0))
        v_shape = jax.ShapeDtypeStruct((n, MLA_VW), BF16)
    return pl.pallas_call(
        functools.partial(_kv_up_kernel, transposed=transposed),
        grid=(n // tm,),
        in_specs=[pl.BlockSpec((tm, MLA_KV_LORA), lambda i: (i, 0)),
                  pl.BlockSpec((tm, HEAD_PAD), lambda i: (i, 0)),
                  pl.BlockSpec((MLA_KV_LORA, QK_W), lambda i: (0, 0)),
                  pl.BlockSpec((MLA_KV_LORA, wv.shape[1]), lambda i: (0, 0)),
                  pl.BlockSpec((1, HEAD_PAD), lambda i: (0, 0))],
        out_specs=[pl.BlockSpec((tm, QK_W), lambda i: (i, 0)), v_spec],
        out_shape=[jax.ShapeDtypeStruct((n, QK_W), BF16), v_shape],
        compiler_params=_cparams(("parallel",)),
        name="kv_up",
    )(ckv, kr, wk, wv, hk)


def _attn_kernel(q_ref, k_ref, v_ref, o_ref, m_ref, l_ref, acc_ref, *, tq, tk, causal):
    m_ref[...] = jnp.full(m_ref.shape, -jnp.inf, F32)
    l_ref[...] = jnp.zeros(l_ref.shape, F32)
    acc_ref[...] = jnp.zeros(acc_ref.shape, F32)

    def block(start, masked):
        if masked:
            rq = lax.broadcasted_iota(jnp.int32, (tq, tk), 0) // CHUNK
            ck = lax.broadcasted_iota(jnp.int32, (tq, tk), 1) // CHUNK
            allowed = ck <= rq
        def scores(h):
            qh = q_ref[0, :, h * HEAD_PAD:(h + 1) * HEAD_PAD]
            kh = k_ref[0, pl.ds(start, tk), h * HEAD_PAD:(h + 1) * HEAD_PAD]
            return lax.dot_general(qh, kh, (((1,), (1,)), ((), ())), preferred_element_type=F32)

        all_scores = [scores(h) for h in range(MLA_HEADS)]
        for h in range(MLA_HEADS):
            vh = v_ref[0, pl.ds(start, tk), h * MLA_V:(h + 1) * MLA_V]
            s = all_scores[h]
            if masked:
                s = jnp.where(allowed, s, -jnp.inf)
            m_prev = m_ref[h]
            m_new = jnp.maximum(m_prev, jnp.max(s, axis=-1, keepdims=True))
            alpha = jnp.exp(m_prev - m_new)
            p = jnp.exp(s - m_new)
            l_ref[h] = alpha * l_ref[h] + jnp.sum(p, axis=-1, keepdims=True)
            pv = jnp.dot(p.astype(BF16), vh, preferred_element_type=F32)
            acc_ref[h] = alpha * acc_ref[h] + pv
            m_ref[h] = m_new

    if causal:
        qi = pl.program_id(1)

        def body(j, carry):
            block(pl.multiple_of(j * tk, tk), False)
            return carry

        lax.fori_loop(0, qi, body, 0)
        block(pl.multiple_of(qi * tk, tk), True)
    else:
        block(0, False)

    for h in range(MLA_HEADS):
        o_ref[0, :, h * MLA_V:(h + 1) * MLA_V] = acc_ref[h] / l_ref[h]


def _attention(q, k, v, causal):
    b, t, _ = q.shape
    s = k.shape[1]
    if causal:
        tq = tk = min(256, t)
    else:
        tq, tk = t, s
    return pl.pallas_call(
        functools.partial(_attn_kernel, tq=tq, tk=tk, causal=causal),
        grid=(b, t // tq),
        in_specs=[pl.BlockSpec((1, tq, QK_W), lambda i, j: (i, j, 0)),
                  pl.BlockSpec((1, s, QK_W), lambda i, j: (i, 0, 0)),
                  pl.BlockSpec((1, s, MLA_VW), lambda i, j: (i, 0, 0))],
        out_specs=pl.BlockSpec((1, tq, MLA_VW), lambda i, j: (i, j, 0)),
        out_shape=jax.ShapeDtypeStruct((b, t, MLA_VW), F32),
        scratch_shapes=[pltpu.VMEM((MLA_HEADS, tq, 1), F32),
                        pltpu.VMEM((MLA_HEADS, tq, 1), F32),
                        pltpu.VMEM((MLA_HEADS, tq, MLA_V), F32)],
        compiler_params=_cparams(("parallel", "arbitrary")),
        name="mla_attn",
    )(q, k, v)


def _attn_t_kernel(qt_ref, k_ref, vt_ref, o_ref, m_ref, acc_ref, *, tq, tk):
    m_ref[...] = jnp.full(m_ref.shape, -jnp.inf, F32)
    acc_ref[...] = jnp.zeros(acc_ref.shape, F32)

    def blocks(starts, masked):
        if masked:
            ck = lax.broadcasted_iota(jnp.int32, (tk, tq), 0) // CHUNK
            cq = lax.broadcasted_iota(jnp.int32, (tk, tq), 1) // CHUNK
            allowed = ck <= cq

        def scores(start, h):
            kh = k_ref[0, pl.ds(start, tk), h * HEAD_PAD:(h + 1) * HEAD_PAD]
            return jnp.dot(kh, qt_ref[0, h], preferred_element_type=F32)

        def update(start, h, s):
            if masked:
                s = jnp.where(allowed, s, -jnp.inf)
            m_prev = m_ref[h]
            m_new = jnp.maximum(m_prev, jnp.max(s, axis=0, keepdims=True))
            alpha = jnp.exp2(m_prev - m_new)
            p = jnp.exp2(s - m_new).astype(BF16)
            pv = jnp.dot(vt_ref[0, h, :, pl.ds(start, tk)], p, preferred_element_type=F32)
            acc_ref[h] = alpha * acc_ref[h] + pv
            m_ref[h] = m_new

        items = [(start, h) for start in starts for h in range(MLA_HEADS)]
        ahead = 5
        pending = [scores(*it) for it in items[:ahead]]
        for n, it in enumerate(items):
            s = pending.pop(0)
            if n + ahead < len(items):
                pending.append(scores(*items[n + ahead]))
            update(*it, s)

    qi = pl.program_id(1)

    def body(j, carry):
        first = pl.multiple_of(2 * j * tk, tk)
        blocks([first, pl.multiple_of(first + tk, tk)], False)
        return carry

    lax.fori_loop(0, qi // 2, body, 0)

    @pl.when(qi % 2 == 1)
    def _():
        blocks([pl.multiple_of((qi - 1) * tk, tk)], False)

    blocks([pl.multiple_of(qi * tk, tk)], True)

    ot = jnp.concatenate([acc_ref[h, 0:MLA_V] / acc_ref[h, MLA_V:MLA_V + 1] for h in range(MLA_HEADS)], axis=0)
    o_ref[0] = ot.T


def _attention_t(qt, k, vt):
    b, _, _, t = qt.shape
    s = k.shape[1]
    tq = tk = min(256, t)
    return pl.pallas_call(
        functools.partial(_attn_t_kernel, tq=tq, tk=tk),
        grid=(b, t // tq),
        in_specs=[pl.BlockSpec((1, MLA_HEADS, HEAD_PAD, tq), lambda i, j: (i, 0, 0, j)),
                  pl.BlockSpec((1, s, QK_W), lambda i, j: (i, 0, 0)),
                  pl.BlockSpec((1, MLA_HEADS, VT_ROWS, s), lambda i, j: (i, 0, 0, 0))],
        out_specs=pl.BlockSpec((1, tq, MLA_VW), lambda i, j: (i, j, 0)),
        out_shape=jax.ShapeDtypeStruct((b, t, MLA_VW), F32),
        scratch_shapes=[pltpu.VMEM((MLA_HEADS, 1, tq), F32),
                        pltpu.VMEM((MLA_HEADS, VT_ROWS, tq), F32)],
        compiler_params=_cparams(("parallel", "arbitrary")),
        name="mla_attn_t",
    )(qt, k, vt)


def _row_iota(shape):
    return lax.broadcasted_iota(jnp.int32, shape, 0)


def _segment_scans(g):
    row = _row_iota(g.shape)
    rows = g.shape[0]
    pre, tot = g, g
    out = [(pre, tot)]
    m = 1
    while m < CHUNK:
        upper = (row // m) % 2 == 1
        from_lower = pltpu.roll(tot, m, 0)
        from_upper = pltpu.roll(tot, rows - m, 0)
        pre = pre + jnp.where(upper, from_lower, 0.0)
        tot = tot + jnp.where(upper, from_lower, from_upper)
        out.append((pre, tot))
        m *= 2
    return out


def _dot_nt(a, b):
    return lax.dot_general(a, b, (((1,), (1,)), ((), ())), preferred_element_type=F32)


def _dot_tn(a, b):
    return lax.dot_general(a, b, (((0,), (0,)), ((), ())), preferred_element_type=F32)


def _gdn_prep_kernel(q_ref, k_ref, v_ref, ab_ref, cw_ref, cs_ref, alog_ref, dt_ref,
                     u_ref, w_ref, qd_ref, kd_ref, qk_ref, gl_ref, carry_ref, *, nb, r):
    @pl.when(pl.program_id(1) == 0)
    def _():
        carry_ref[...] = cs_ref[...]

    n = nb * r
    nc = n // CHUNK

    def conv(x_ref, j):
        w = cw_ref[:, j * GDN_QK:(j + 1) * GDN_QK]
        outs = []
        for b in range(nb):
            x = x_ref[b]
            xp = jnp.concatenate([carry_ref[b, j], x], axis=0)
            y = x * w[3:4]
            for d in range(1, CONV_W):
                y = y + xp[8 - d:8 - d + r] * w[3 - d:4 - d]
            carry_ref[b, j] = x[r - 8:]
            outs.append(_silu(y))
        return outs[0] if nb == 1 else jnp.concatenate(outs, axis=0)

    q_all, k_all, v_all = conv(q_ref, 0), conv(k_ref, 1), conv(v_ref, 2)

    ab = ab_ref[...].reshape(n, 128)
    x = ab + dt_ref[...]
    softplus = jnp.maximum(x, 0.0) + jnp.log(1.0 + jnp.exp(-jnp.abs(x)))
    g_blk = -jnp.exp(alog_ref[...]) * softplus
    gam_blk = _segment_scans(g_blk)[-1][0]
    gam_t = gam_blk.T
    gam3_blk = gam_blk.reshape(nc, CHUNK, 128)
    beta3_blk = _sigmoid(ab).reshape(nc, CHUNK, 128)

    row = lax.broadcasted_iota(jnp.int32, (1, CHUNK, CHUNK), 1)
    col = lax.broadcasted_iota(jnp.int32, (1, CHUNK, CHUNK), 2)
    eye = (row == col).astype(F32)

    def bmm(a, b):
        return jnp.einsum('cij,cjk->cik', a.astype(BF16), b.astype(BF16), preferred_element_type=F32)

    def bmm_nt(a, b):
        return jnp.einsum('cid,cjd->cij', a.astype(BF16), b.astype(BF16), preferred_element_type=F32)

    for h in range(GDN_HEADS):
        sl = slice(h * GDN_DK, (h + 1) * GDN_DK)
        q, k, v = q_all[:, sl], k_all[:, sl], v_all[:, sl]
        q = q * lax.rsqrt(jnp.sum(q * q, axis=-1, keepdims=True) + NORM_EPS) * (GDN_DK ** -0.5)
        k = k * lax.rsqrt(jnp.sum(k * k, axis=-1, keepdims=True) + NORM_EPS)
        q, k, v = (a.reshape(nc, CHUNK, GDN_DK) for a in (q, k, v))
        gam = gam3_blk[:, :, h:h + 1]
        beta = beta3_blk[:, :, GDN_HEADS + h:GDN_HEADS + h + 1]
        gam_row = jnp.stack([gam_t[h:h + 1, c * CHUNK:(c + 1) * CHUNK] for c in range(nc)], axis=0)
        decay = jnp.where(row >= col, jnp.exp(jnp.minimum(gam - gam_row, 0.0)), 0.0)
        a = jnp.where(row > col, beta * bmm_nt(k, k) * decay, 0.0)
        t_inv = eye - a
        pw = a
        m = 1
        while 2 * m < CHUNK:
            pw = bmm(pw, pw)
            t_inv = t_inv + bmm(t_inv, pw)
            m *= 2
        e_gam = jnp.exp(gam)
        gam_last = gam[:, CHUNK - 1:CHUNK, :]
        u_ref[:, h] = bmm(t_inv, v * beta).reshape(nb, r, GDN_DV)
        w_ref[:, h] = bmm(t_inv, k * (beta * e_gam)).astype(BF16).reshape(nb, r, GDN_DK)
        qd_ref[:, h] = (q * e_gam).astype(BF16).reshape(nb, r, GDN_DK)
        kd_ref[:, h] = (k * jnp.exp(gam_last - gam)).astype(BF16).reshape(nb, r, GDN_DK)
        qk_ref[:, h] = (bmm_nt(q, k) * decay).astype(BF16).reshape(nb, r, CHUNK)
        gl_ref[:, h] = jnp.broadcast_to(jnp.exp(gam_last), (nc, 1, 128)).reshape(nb, r // CHUNK, 1, 128)


def _gdn_scan_kernel(u_ref, w_ref, qd_ref, kd_ref, qk_ref, gl_ref, s0_ref, o_ref, sf_ref, s_ref, *, nb):
    c_idx = pl.program_id(1)

    @pl.when(c_idx == 0)
    def _():
        s_ref[...] = s0_ref[...]

    chains = [(b, h) for b in range(nb) for h in range(GDN_HEADS)]
    s_old = [s_ref[b, h] for b, h in chains]
    s_bf = [s.astype(BF16) for s in s_old]
    v_new = [u_ref[b, h] - jnp.dot(w_ref[b, h], sb, preferred_element_type=F32)
             for (b, h), sb in zip(chains, s_bf)]
    v_bf = [v.astype(BF16) for v in v_new]
    for (b, h), s, sb, vb in zip(chains, s_old, s_bf, v_bf):
        o = (jnp.dot(qd_ref[b, h], sb, preferred_element_type=F32)
             + jnp.dot(qk_ref[b, h], vb, preferred_element_type=F32))
        o_ref[b, :, h * GDN_DV:(h + 1) * GDN_DV] = o
        s_ref[b, h] = gl_ref[b, h, 0] * s + _dot_tn(kd_ref[b, h], vb)

    @pl.when(c_idx == pl.num_programs(1) - 1)
    def _():
        sf_ref[...] = s_ref[...]


def _gdn(p3, conv_w, conv_state8, s0, alog, dt):
    b, t, _ = p3.shape
    r = min(GDN_PREP_ROWS, t)
    nb = max(1, min(b, GDN_PREP_ROWS // r))
    nt = t // r
    nc = t // CHUNK
    blk = lambda j: pl.BlockSpec((nb, r, GDN_QK), lambda i, c: (i, c, COL_GDN // GDN_QK + j))
    vec = pl.BlockSpec((1, 128), lambda i, c: (0, 0))
    head_spec = lambda w: pl.BlockSpec((nb, GDN_HEADS, r, w), lambda i, c: (i, 0, c, 0))
    head_shape = lambda w, dt_: jax.ShapeDtypeStruct((b, GDN_HEADS, t, w), dt_)
    u, w, qd, kd, qk, gl = pl.pallas_call(
        functools.partial(_gdn_prep_kernel, nb=nb, r=r),
        grid=(b // nb, nt),
        in_specs=[blk(0), blk(1), blk(2),
                  pl.BlockSpec((nb, r, 128), lambda i, c: (i, c, COL_GAB // 128)),
                  pl.BlockSpec((CONV_W, 3 * GDN_QK), lambda i, c: (0, 0)),
                  pl.BlockSpec((nb, 3, 8, GDN_QK), lambda i, c: (i, 0, 0, 0)),
                  vec, vec],
        out_specs=[head_spec(GDN_DV), head_spec(GDN_DK), head_spec(GDN_DK), head_spec(GDN_DK),
                   head_spec(CHUNK),
                   pl.BlockSpec((nb, GDN_HEADS, r // CHUNK, 1, 128), lambda i, c: (i, 0, c, 0, 0))],
        out_shape=[head_shape(GDN_DV, F32), head_shape(GDN_DK, BF16), head_shape(GDN_DK, BF16),
                   head_shape(GDN_DK, BF16), head_shape(CHUNK, BF16),
                   jax.ShapeDtypeStruct((b, GDN_HEADS, nc, 1, 128), F32)],
        scratch_shapes=[pltpu.VMEM((nb, 3, 8, GDN_QK), F32)],
        compiler_params=_cparams(("parallel", "arbitrary")),
        name="gdn_prep",
    )(p3, p3, p3, p3, conv_w, conv_state8, alog, dt)

    sb = min(b, 8)
    chunk_spec = lambda w: pl.BlockSpec((sb, GDN_HEADS, CHUNK, w), lambda i, c: (i, 0, c, 0))
    state_spec = pl.BlockSpec((sb, GDN_HEADS, GDN_DK, GDN_DV), lambda i, c: (i, 0, 0, 0))
    return pl.pallas_call(
        functools.partial(_gdn_scan_kernel, nb=sb),
        grid=(b // sb, nc),
        in_specs=[chunk_spec(GDN_DV), chunk_spec(GDN_DK), chunk_spec(GDN_DK), chunk_spec(GDN_DK),
                  chunk_spec(CHUNK),
                  pl.BlockSpec((sb, GDN_HEADS, 1, 1, 128), lambda i, c: (i, 0, c, 0, 0)),
                  state_spec],
        out_specs=[pl.BlockSpec((sb, CHUNK, GDN_VW), lambda i, c: (i, c, 0)), state_spec],
        out_shape=[jax.ShapeDtypeStruct((b, t, GDN_VW), F32),
                   jax.ShapeDtypeStruct((b, GDN_HEADS, GDN_DK, GDN_DV), F32)],
        scratch_shapes=[pltpu.VMEM((sb, GDN_HEADS, GDN_DK, GDN_DV), F32)],
        compiler_params=_cparams(("parallel", "arbitrary")),
        name="gdn_scan",
    )(u, w, qd, kd, qk, gl, s0)


def _hgrn_kernel(q_ref, f_ref, i_ref, gate_ref, lb_ref, s0_ref, ng_ref, o_ref, sf_ref, st_ref):
    c_idx = pl.program_id(1)

    @pl.when(c_idx == 0)
    def _():
        st_ref[...] = s0_ref[0]

    row = lax.broadcasted_iota(jnp.int32, (CHUNK, CHUNK), 0)
    col = lax.broadcasted_iota(jnp.int32, (CHUNK, CHUNK), 1)
    rowv = _row_iota((CHUNK, HGRN_DK))

    heads = range(HGRN_HEADS)
    sls = [slice(h * HGRN_DK, (h + 1) * HGRN_DK) for h in heads]
    qs_, ks_, vs_, scans_ = [], [], [], []
    for sl in sls:
        lb = lb_ref[:, sl]
        f = lb + (1.0 - lb) * _sigmoid(f_ref[0, :, sl])
        qs_.append((_silu(q_ref[0, :, sl]) * (HGRN_DK ** -0.5)))
        ks_.append(1.0 - f)
        vs_.append(i_ref[0, :, sl].astype(BF16))
        scans_.append(_segment_scans(jnp.log(f)))

    att = [jnp.where(row == col, _dot_nt(q.astype(BF16), k.astype(BF16)), 0.0) for q, k in zip(qs_, ks_)]
    m = 1
    lvl = 0
    while m < CHUNK:
        upper = (rowv // m) % 2 == 1
        same_block = row // (2 * m) == col // (2 * m)
        for h in heads:
            pre_m, tot_m = scans_[h][lvl]
            q_lvl = jnp.where(upper, qs_[h] * jnp.exp(pre_m), 0.0).astype(BF16)
            k_lvl = jnp.where(upper, 0.0, ks_[h] * jnp.exp(tot_m - pre_m)).astype(BF16)
            att[h] = att[h] + jnp.where(same_block, _dot_nt(q_lvl, k_lvl), 0.0)
        m *= 2
        lvl += 1

    st_old = [st_ref[h] for h in heads]
    o_st = [_dot_nt((qs_[h] * jnp.exp(scans_[h][-1][0])).astype(BF16), st_old[h].astype(BF16)) for h in heads]
    o_att = [jnp.dot(att[h].astype(BF16), vs_[h], preferred_element_type=F32) for h in heads]
    for h in heads:
        cb, c_tot = scans_[h][-1]
        k_dec = (ks_[h] * jnp.exp(c_tot - cb)).astype(BF16)
        st_ref[h] = st_old[h] * jnp.exp(c_tot[0:1]) + _dot_tn(vs_[h], k_dec)
        o_ref[0, :, sls[h]] = _rms(o_st[h] + o_att[h], ng_ref[...]) * _silu(gate_ref[0, :, sls[h]])

    @pl.when(c_idx == pl.num_programs(1) - 1)
    def _():
        sf_ref[0] = st_ref[...]


def _hgrn(p3, lb, s0t, ng):
    b, t, _ = p3.shape
    nc = t // CHUNK
    col0 = (COL_GDN + 4 * GDN_QK) // HGRN_KW
    blk = lambda j: pl.BlockSpec((1, CHUNK, HGRN_KW), lambda i, c: (i, c, col0 + j))
    return pl.pallas_call(
        _hgrn_kernel,
        grid=(b, nc),
        in_specs=[blk(0), blk(1), blk(2), blk(3),
                  pl.BlockSpec((1, HGRN_KW), lambda i, c: (0, 0)),
                  pl.BlockSpec((1, HGRN_HEADS, HGRN_DV, HGRN_DK), lambda i, c: (i, 0, 0, 0)),
                  pl.BlockSpec((1, HGRN_DV), lambda i, c: (0, 0))],
        out_specs=[pl.BlockSpec((1, CHUNK, HGRN_KW), lambda i, c: (i, c, 0)),
                   pl.BlockSpec((1, HGRN_HEADS, HGRN_DV, HGRN_DK), lambda i, c: (i, 0, 0, 0))],
        out_shape=[jax.ShapeDtypeStruct((b, t, HGRN_KW), F32),
                   jax.ShapeDtypeStruct((b, HGRN_HEADS, HGRN_DV, HGRN_DK), F32)],
        scratch_shapes=[pltpu.VMEM((HGRN_HEADS, HGRN_DV, HGRN_DK), F32)],
        compiler_params=_cparams(("parallel", "arbitrary")),
        name="hgrn",
    )(p3, p3, p3, p3, lb, s0t, ng)


def _merge_kernel(x_ref, oa_ref, ob_ref, z_ref, oc_ref, g0_ref, g1_ref, g2_ref, ng_ref,
                  wa_ref, wb_ref, wc_ref, wo_ref, y_ref):
    def branch(o, w_ref, g_ref):
        gate = _sigmoid(g_ref[...].astype(F32))
        return gate * jnp.dot(o.astype(BF16), w_ref[...], preferred_element_type=F32)

    ob = jnp.concatenate(
        [_rms(ob_ref[:, h * GDN_DV:(h + 1) * GDN_DV], ng_ref[...]) * _silu(z_ref[:, h * GDN_DV:(h + 1) * GDN_DV])
         for h in range(GDN_HEADS)], axis=-1)
    mixed = branch(oa_ref[...], wa_ref, g0_ref) + branch(ob, wb_ref, g1_ref) + branch(oc_ref[...], wc_ref, g2_ref)
    y_ref[...] = x_ref[...] + jnp.dot(mixed.astype(BF16), wo_ref[...], preferred_element_type=F32)


def _merge(x, oa, ob, oc, p, pg, ng, wa, wb, wc, wo):
    n = x.shape[0]
    tm = _row_tile(n, 512)
    row = lambda w: pl.BlockSpec((tm, w), lambda i: (i, 0))
    gate = lambda j: pl.BlockSpec((tm, D_MODEL), lambda i: (i, j))
    z_spec = pl.BlockSpec((tm, GDN_VW), lambda i: (i, COL_GDN // GDN_VW + 3))
    wsp = lambda k: pl.BlockSpec((k, D_MODEL), lambda i: (0, 0))
    return pl.pallas_call(
        _merge_kernel,
        grid=(n // tm,),
        in_specs=[row(D_MODEL), row(MLA_VW), row(GDN_VW), z_spec, row(HGRN_KW), gate(0), gate(1), gate(2),
                  pl.BlockSpec((1, GDN_DV), lambda i: (0, 0)),
                  wsp(MLA_VW), wsp(GDN_VW), wsp(HGRN_KW), wsp(D_MODEL)],
        out_specs=row(D_MODEL),
        out_shape=jax.ShapeDtypeStruct((n, D_MODEL), F32),
        compiler_params=_cparams(("parallel",)),
        name="merge_out",
    )(x, oa, ob, p, oc, pg, pg, pg, ng, wa, wb, wc, wo)


def _router_kernel(x_ref, g_ref, w_ref, b_ref, cw_ref, cwt_ref, cnt_ref):
    h = _rms(x_ref[...], g_ref[...])
    logits = jnp.dot(h, w_ref[...], preferred_element_type=F32, precision=lax.Precision.HIGHEST) + b_ref[...]
    lane = lax.broadcasted_iota(jnp.int32, logits.shape, 1)
    valid = lane < N_EXPERTS
    neg = -jnp.inf
    l1 = jnp.where(valid, logits, neg)
    m1 = jnp.max(l1, axis=-1, keepdims=True)
    i1 = jnp.min(jnp.where(l1 == m1, lane, 128), axis=-1, keepdims=True)
    l2 = jnp.where(lane == i1, neg, l1)
    m2 = jnp.max(l2, axis=-1, keepdims=True)
    i2 = jnp.min(jnp.where(l2 == m2, lane, 128), axis=-1, keepdims=True)
    e2 = jnp.exp(m2 - m1)
    den = 1.0 + e2
    cw = jnp.where(lane == i1, 1.0 / den, 0.0) + jnp.where(lane == i2, e2 / den, 0.0)
    cw_ref[...] = cw
    cwt_ref[...] = cw.T
    cnt_ref[0] = jnp.sum((cw > 0.0).astype(F32), axis=0, keepdims=True)


def _router(x, g, w, b, tm):
    n = x.shape[0]
    return pl.pallas_call(
        _router_kernel,
        grid=(n // tm,),
        in_specs=[pl.BlockSpec((tm, D_MODEL), lambda i: (i, 0)),
                  pl.BlockSpec((1, D_MODEL), lambda i: (0, 0)),
                  pl.BlockSpec((D_MODEL, 128), lambda i: (0, 0)),
                  pl.BlockSpec((1, 128), lambda i: (0, 0))],
        out_specs=[pl.BlockSpec((tm, 128), lambda i: (i, 0)),
                   pl.BlockSpec((128, tm), lambda i: (0, i)),
                   pl.BlockSpec((1, 1, 128), lambda i: (i, 0, 0))],
        out_shape=[jax.ShapeDtypeStruct((n, 128), F32),
                   jax.ShapeDtypeStruct((128, n), F32),
                   jax.ShapeDtypeStruct((n // tm, 1, 128), F32)],
        compiler_params=_cparams(("parallel",)),
        name="moe_router",
    )(x, g, w, b)


def _moe_kernel(cnt_ref, x_ref, g_ref, cw_ref, cwt_ref, wg_ref, wu_ref, wd_ref, y_ref,
                h_ref, rcol_ref, rrow_ref, *, t, br):
    i, e = pl.program_id(0), pl.program_id(1)

    @pl.when(e == 0)
    def _():
        x = x_ref[...]
        h_ref[...] = _rms(x, g_ref[...]).astype(BF16)
        y_ref[...] = x
        r = lax.broadcasted_iota(jnp.int32, (t, t), 0)
        c = lax.broadcasted_iota(jnp.int32, (t, t), 1)
        on = cw_ref[...] > 0.0
        rank = jnp.dot((c < r).astype(BF16), on.astype(BF16), preferred_element_type=F32)
        rcol_ref[...] = jnp.where(on, rank.astype(jnp.int32), -1)
        on_t = cwt_ref[...] > 0.0
        rank_t = jnp.dot(on_t.astype(BF16), (r < c).astype(BF16), preferred_element_type=F32)
        rrow_ref[...] = jnp.where(on_t, rank_t.astype(jnp.int32), -1)

    count = cnt_ref[i * N_EXPERTS + e]
    n_blocks = lax.div(count + (br - 1), br)
    sel = lax.broadcasted_iota(jnp.int32, (1, 128), 1) == e
    rank_c = jnp.sum(jnp.where(sel, rcol_ref[...], 0), axis=-1, keepdims=True)
    rank_r = rrow_ref[pl.ds(e, 1), :]
    w_r = cwt_ref[pl.ds(e, 1), :]

    def body(j, carry):
        base = j * br
        rows = lax.broadcasted_iota(jnp.int32, (br, t), 0) + base
        pick = rows == rank_r
        xc = jnp.dot(pick.astype(BF16), h_ref[...], preferred_element_type=F32).astype(BF16)
        a = jnp.dot(xc, wg_ref[0], preferred_element_type=F32)
        b = jnp.dot(xc, wu_ref[0], preferred_element_type=F32)
        yc = jnp.dot((_silu(a) * b).astype(BF16), wd_ref[0], preferred_element_type=F32)
        w_rows = jnp.sum(jnp.where(pick, w_r, 0.0), axis=-1, keepdims=True)
        cols = lax.broadcasted_iota(jnp.int32, (t, br), 1) + base
        y_ref[...] += jnp.dot((cols == rank_c).astype(BF16), (yc * w_rows).astype(BF16),
                              preferred_element_type=F32)
        return carry

    lax.fori_loop(0, n_blocks, body, 0)


def _moe(x, g, cw, cwt, counts, wg, wu, wd, t):
    n = x.shape[0]
    ne, _, ff = wg.shape
    br = MOE_BLOCK_ROWS
    grid_spec = pltpu.PrefetchScalarGridSpec(
        num_scalar_prefetch=1,
        grid=(n // t, ne),
        in_specs=[pl.BlockSpec((t, D_MODEL), lambda i, e, cnt: (i, 0)),
                  pl.BlockSpec((1, D_MODEL), lambda i, e, cnt: (0, 0)),
                  pl.BlockSpec((t, 128), lambda i, e, cnt: (i, 0)),
                  pl.BlockSpec((128, t), lambda i, e, cnt: (0, i)),
                  pl.BlockSpec((1, D_MODEL, ff), lambda i, e, cnt: (e, 0, 0)),
                  pl.BlockSpec((1, D_MODEL, ff), lambda i, e, cnt: (e, 0, 0)),
                  pl.BlockSpec((1, ff, D_MODEL), lambda i, e, cnt: (e, 0, 0))],
        out_specs=pl.BlockSpec((t, D_MODEL), lambda i, e, cnt: (i, 0)),
        scratch_shapes=[pltpu.VMEM((t, D_MODEL), BF16),
                        pltpu.VMEM((t, 128), jnp.int32),
                        pltpu.VMEM((128, t), jnp.int32)])
    return pl.pallas_call(
        functools.partial(_moe_kernel, t=t, br=br),
        grid_spec=grid_spec,
        out_shape=jax.ShapeDtypeStruct((n, D_MODEL), F32),
        compiler_params=_cparams(("parallel", "arbitrary")),
        name="moe",
    )(counts, x, g, cw, cwt, wg, wu, wd)


def _ffn_kernel(x_ref, g_ref, wg_ref, wu_ref, wd_ref, y_ref, h_ref):
    e = pl.program_id(1)

    @pl.when(e == 0)
    def _():
        x = x_ref[...]
        h_ref[...] = _rms(x, g_ref[...]).astype(BF16)
        y_ref[...] = x

    h = h_ref[...]
    a = jnp.dot(h, wg_ref[0], preferred_element_type=F32)
    b = jnp.dot(h, wu_ref[0], preferred_element_type=F32)
    y_ref[...] += jnp.dot((_silu(a) * b).astype(BF16), wd_ref[0], preferred_element_type=F32)


def _ffn(x, g, wg, wu, wd):
    n = x.shape[0]
    ne, _, ff = wg.shape
    tm = _row_tile(n, 512)
    return pl.pallas_call(
        _ffn_kernel,
        grid=(n // tm, ne),
        in_specs=[pl.BlockSpec((tm, D_MODEL), lambda i, e: (i, 0)),
                  pl.BlockSpec((1, D_MODEL), lambda i, e: (0, 0)),
                  pl.BlockSpec((1, D_MODEL, ff), lambda i, e: (e, 0, 0)),
                  pl.BlockSpec((1, D_MODEL, ff), lambda i, e: (e, 0, 0)),
                  pl.BlockSpec((1, ff, D_MODEL), lambda i, e: (e, 0, 0))],
        out_specs=pl.BlockSpec((tm, D_MODEL), lambda i, e: (i, 0)),
        out_shape=jax.ShapeDtypeStruct((n, D_MODEL), F32),
        scratch_shapes=[pltpu.VMEM((tm, D_MODEL), BF16)],
        compiler_params=_cparams(("parallel", "arbitrary")),
        name="ffn",
    )(x, g, wg, wu, wd)


def _pad_lanes(x, left, total):
    return jnp.pad(x, [(0, 0)] * (x.ndim - 1) + [(left, total - left - x.shape[-1])])


def _pack_w_in(w):
    cq, ckv, kr, gq, gk, gv, gz, ga, gb, hq, hf, hi, hg, gates = jnp.split(
        w, np.cumsum(SPLIT_SIZES)[:-1].tolist(), axis=-1)
    kr_blk = _pad_lanes(kr, ROPE_LANE0, 128)
    ab_blk = _pad_lanes(jnp.concatenate([ga, gb], axis=-1), 0, 256)
    return jnp.concatenate([cq, ckv, kr_blk, ab_blk, gq, gk, gv, gz, hq, hf, hi, hg, gates], axis=-1).astype(BF16)


def _rope_tables(n_pos):
    half = MLA_ROPE // 2
    inv = ROPE_THETA ** (-jnp.arange(half, dtype=F32) / half)
    ang = jnp.arange(n_pos, dtype=F32)[:, None] * inv[None, :]
    cos, sin = jnp.cos(ang), jnp.sin(ang)
    one = jnp.ones((n_pos, MLA_NOPE), F32)
    zero = jnp.zeros((n_pos, MLA_NOPE), F32)
    tail = jnp.zeros((n_pos, HEAD_PAD - MLA_QK_HEAD), F32)
    z16 = jnp.zeros((n_pos, half), F32)
    c = jnp.concatenate([one, cos, cos, tail], axis=-1)
    s1 = jnp.concatenate([zero, -sin, z16, tail], axis=-1)
    s2 = jnp.concatenate([zero, z16, sin, tail], axis=-1)
    return c, s1, s2


def _layer_weights(l, a):
    f = {}
    f['mixer_g'] = a['mixer_norm_g'][l][None]
    f['w_in'] = _pack_w_in(a['w_in'][l])
    f['gq'] = a['mla_q_norm_g'][l][None]
    f['gkv'] = a['mla_kv_norm_g'][l][None]
    wq = a['mla_w_q_up'][l].reshape(MLA_Q_LORA, MLA_HEADS, MLA_QK_HEAD)
    f['wq'] = _pad_lanes(wq, 0, HEAD_PAD).reshape(MLA_Q_LORA, QK_W).astype(BF16)
    wkv = a['mla_w_kv_up'][l].reshape(MLA_KV_LORA, MLA_HEADS, MLA_NOPE + MLA_V)
    f['wk'] = _pad_lanes(wkv[:, :, :MLA_NOPE], 0, HEAD_PAD).reshape(MLA_KV_LORA, QK_W).astype(BF16)
    f['wv'] = wkv[:, :, MLA_NOPE:].reshape(MLA_KV_LORA, MLA_VW).astype(BF16)
    f['wv_pad'] = _pad_lanes(wkv[:, :, MLA_NOPE:], 0, HEAD_PAD).reshape(MLA_KV_LORA, QK_W).astype(BF16)
    f['hq'] = _pad_lanes(a['mla_q_head_norm_g'][l][None], 0, HEAD_PAD)
    hk = a['mla_k_head_norm_g'][l][None]
    f['hk_nope'] = _pad_lanes(hk[:, :MLA_NOPE], 0, HEAD_PAD)
    f['hk_rope'] = _pad_lanes(hk[:, MLA_NOPE:], ROPE_LANE0, HEAD_PAD)
    f['wo_a'] = a['mla_w_o'][l].astype(BF16)
    f['conv_w'] = a['gdn_conv_w'][l]
    f['alog'] = _pad_lanes(a['gdn_a_log'][l][None], 0, 128)
    f['dt'] = _pad_lanes(a['gdn_dt_bias'][l][None], 0, 128)
    f['gdn_g'] = a['gdn_norm_g'][l][None]
    f['wo_b'] = a['gdn_w_o'][l].astype(BF16)
    f['hgrn_g'] = a['hgrn_norm_g'][l][None]
    f['wo_c'] = a['hgrn_w_o'][l].astype(BF16)
    f['w_out'] = a['w_out'][l].astype(BF16)
    f['ffn_g'] = a['ffn_norm_g'][l][None]
    if l % 2 == 0:
        wg, wu, wd = a['dense_w_gate'][l // 2], a['dense_w_up'][l // 2], a['dense_w_down'][l // 2]
        ff = wg.shape[1]
        half = ff // 2
        f['ffn'] = (jnp.moveaxis(wg.reshape(D_MODEL, 2, half), 1, 0).astype(BF16),
                    jnp.moveaxis(wu.reshape(D_MODEL, 2, half), 1, 0).astype(BF16),
                    wd.reshape(2, half, D_MODEL).astype(BF16))
        f['router'] = None
    else:
        f['ffn'] = (a['moe_w_gate'][l // 2].astype(BF16), a['moe_w_up'][l // 2].astype(BF16),
                    a['moe_w_down'][l // 2].astype(BF16))
        f['router'] = (_pad_lanes(a['moe_w_router'][l // 2], 0, 128),
                       _pad_lanes(a['moe_b_router'][l // 2][None], 0, 128))
    return f


def _trunk_layer(x, b, t, f, lb, tabs, past):
    n = b * t
    p, p_gate = _in_proj(x, f['mixer_g'], f['w_in'])
    p3 = p.reshape(b, t, P_MAIN)

    fresh = past['ckv'] is None
    q, ckv, kr = _mla_pre(p, b, t, f['gq'], f['gkv'], f['wq'], f['hq'], f['hk_rope'], tabs, transposed=fresh)
    if fresh:
        k_all, vt_all = _kv_up(ckv, kr, b, t, f['wk'], f['wv_pad'], f['hk_nope'], transposed=True)
        o_a = _attention_t(q, k_all.reshape(b, t, QK_W), vt_all)
    else:
        s = past['ckv'].shape[1] + t
        ckv_all = jnp.concatenate([past['ckv'], ckv.reshape(b, t, -1)], axis=1).reshape(b * s, -1)
        kr_past = _pad_lanes(past['kr'], ROPE_LANE0, HEAD_PAD)
        kr_all = jnp.concatenate([kr_past, kr.reshape(b, t, -1)], axis=1).reshape(b * s, -1)
        k_all, v_all = _kv_up(ckv_all, kr_all, b, s, f['wk'], f['wv'], f['hk_nope'], transposed=False)
        o_a = _attention(q.reshape(b, t, QK_W), k_all.reshape(b, s, QK_W), v_all.reshape(b, s, MLA_VW),
                         causal=False)

    conv8 = jnp.pad(past['conv'].reshape(b, CONV_W - 1, 3, GDN_QK).transpose(0, 2, 1, 3),
                    ((0, 0), (0, 0), (8 - (CONV_W - 1), 0), (0, 0)))
    o_b, gdn_s = _gdn(p3, f['conv_w'], conv8, past['gdn'], f['alog'], f['dt'])
    gdn_conv = p3[:, t - (CONV_W - 1):, COL_GDN:COL_GDN + 3 * GDN_QK]

    o_c, hgrn_st = _hgrn(p3, lb, jnp.swapaxes(past['hgrn'], -1, -2), f['hgrn_g'])
    hgrn_s = jnp.swapaxes(hgrn_st, -1, -2)

    x = _merge(x, o_a.reshape(n, -1), o_b.reshape(n, -1), o_c.reshape(n, -1), p, p_gate, f['gdn_g'],
               f['wo_a'], f['wo_b'], f['wo_c'], f['w_out'])

    wg, wu, wd = f['ffn']
    if f['router'] is None:
        x = _ffn(x, f['ffn_g'], wg, wu, wd)
    else:
        t_moe = _row_tile(n, MOE_TILE)
        cw, cwt, cnt = _router(x, f['ffn_g'], *f['router'], t_moe)
        counts = cnt[:, 0, :N_EXPERTS].astype(jnp.int32).reshape(-1)
        x = _moe(x, f['ffn_g'], cw, cwt, counts, wg, wu, wd, t_moe)

    new_ckv = ckv.reshape(b, t, MLA_KV_LORA)
    new_kr = kr.reshape(b, t, HEAD_PAD)[:, :, ROPE_LANE0:ROPE_LANE0 + MLA_ROPE]
    return x, (new_ckv, new_kr, gdn_s, gdn_conv, hgrn_s)


def kernel(x_prompt, x_sample, cache_mla_ckv, cache_mla_krope, state_gdn, state_gdn_conv, state_hgrn,
           mixer_norm_g, w_in, mla_q_norm_g, mla_w_q_up, mla_kv_norm_g, mla_w_kv_up,
           mla_q_head_norm_g, mla_k_head_norm_g, mla_w_o,
           gdn_conv_w, gdn_a_log, gdn_dt_bias, gdn_norm_g, gdn_w_o,
           hgrn_lb_logits, hgrn_norm_g, hgrn_w_o, w_out, ffn_norm_g,
           dense_w_gate, dense_w_up, dense_w_down,
           moe_w_router, moe_b_router, moe_w_gate, moe_w_up, moe_w_down):
    a = dict(mixer_norm_g=mixer_norm_g, w_in=w_in, mla_q_norm_g=mla_q_norm_g, mla_w_q_up=mla_w_q_up,
             mla_kv_norm_g=mla_kv_norm_g, mla_w_kv_up=mla_w_kv_up, mla_q_head_norm_g=mla_q_head_norm_g,
             mla_k_head_norm_g=mla_k_head_norm_g, mla_w_o=mla_w_o, gdn_conv_w=gdn_conv_w,
             gdn_a_log=gdn_a_log, gdn_dt_bias=gdn_dt_bias, gdn_norm_g=gdn_norm_g, gdn_w_o=gdn_w_o,
             hgrn_norm_g=hgrn_norm_g, hgrn_w_o=hgrn_w_o, w_out=w_out, ffn_norm_g=ffn_norm_g,
             dense_w_gate=dense_w_gate, dense_w_up=dense_w_up, dense_w_down=dense_w_down,
             moe_w_router=moe_w_router, moe_b_router=moe_b_router, moe_w_gate=moe_w_gate,
             moe_w_up=moe_w_up, moe_w_down=moe_w_down)
    depth = w_in.shape[0]
    lb_soft = jax.nn.softmax(hgrn_lb_logits.astype(F32), axis=0)
    hgrn_lb = jnp.cumsum(lb_soft, axis=0) - lb_soft[0]

    b_p, t_p = x_prompt.shape[:2]
    b_s, t_s = x_sample.shape[:2]
    past_len = cache_mla_ckv.shape[2]
    tab_all = _rope_tables(max(t_p, past_len + t_s))
    tabs_p = tuple(tb[:t_p] for tb in tab_all)
    tabs_s = tuple(tb[past_len:past_len + t_s] for tb in tab_all)

    xp = x_prompt.reshape(b_p * t_p, D_MODEL)
    xs = x_sample.reshape(b_s * t_s, D_MODEL)
    past_p = dict(ckv=None, kr=None,
                  gdn=jnp.zeros((b_p, GDN_HEADS, GDN_DK, GDN_DV), F32),
                  conv=jnp.zeros((b_p, CONV_W - 1, 3 * GDN_QK), F32),
                  hgrn=jnp.zeros((b_p, HGRN_HEADS, HGRN_DK, HGRN_DV), F32))
    st_p, st_s = [], []
    for l in range(depth):
        f = _layer_weights(l, a)
        lb = hgrn_lb[l][None]
        past_s = dict(ckv=cache_mla_ckv[l], kr=cache_mla_krope[l], gdn=state_gdn[l],
                      conv=state_gdn_conv[l], hgrn=state_hgrn[l])
        xp, sp = _trunk_layer(xp, b_p, t_p, f, lb, tabs_p, past_p)
        xs, ss = _trunk_layer(xs, b_s, t_s, f, lb, tabs_s, past_s)
        st_p.append(sp)
        st_s.append(ss)

    def stack(lst, i):
        return jnp.stack([s[i] for s in lst], axis=0)

    return (xp.reshape(b_p, t_p, D_MODEL), xs.reshape(b_s, t_s, D_MODEL),
            stack(st_p, 0), stack(st_p, 1), stack(st_p, 2), stack(st_p, 3), stack(st_p, 4),
            stack(st_s, 0), stack(st_s, 1), stack(st_s, 2), stack(st_s, 3), stack(st_s, 4))
```

```python
import functools

import jax
import jax.numpy as jnp
import numpy as np
from jax import lax
from jax.experimental import pallas as pl
from jax.experimental.pallas import tpu as pltpu

F32 = jnp.float32
BF16 = jnp.bfloat16

D_MODEL = 1024
CHUNK = 64
NORM_EPS = 1e-6

MLA_HEADS = 8
MLA_NOPE = 64
MLA_ROPE = 32
MLA_V = 64
MLA_Q_LORA = 384
MLA_KV_LORA = 256
MLA_QK_HEAD = MLA_NOPE + MLA_ROPE
MLA_VW = MLA_HEADS * MLA_V
ROPE_THETA = 10000.0
LOG2_E = 1.4426950408889634
HEAD_PAD = 128
QK_W = MLA_HEADS * HEAD_PAD
VT_ROWS = MLA_V + 16

GDN_HEADS = 4
GDN_DK = 128
GDN_DV = 128
GDN_QK = GDN_HEADS * GDN_DK
GDN_VW = GDN_HEADS * GDN_DV
CONV_W = 4
HGRN_PREP_ROWS = 512
GDN_PREP_ROWS = 1024

HGRN_HEADS = 4
HGRN_DK = 128
HGRN_DV = 128
HGRN_KW = HGRN_HEADS * HGRN_DK

N_BRANCH = 3
SPLIT_SIZES = (MLA_Q_LORA, MLA_KV_LORA, MLA_ROPE,
               GDN_QK, GDN_QK, GDN_VW, GDN_VW, GDN_HEADS, GDN_HEADS,
               HGRN_KW, HGRN_KW, HGRN_KW, HGRN_KW,
               N_BRANCH * D_MODEL)

N_EXPERTS = 8
FF_EXPERT = 1408
MOE_TILE = 1024
MOE_BLOCK_ROWS = 288

P_COLS = 8192
P_MAIN = P_COLS - N_BRANCH * D_MODEL
P_TN = 512
COL_CKV = MLA_Q_LORA
COL_KR = MLA_Q_LORA + MLA_KV_LORA
COL_GAB = COL_KR + 128
COL_GDN = 1024
ROPE_LANE0 = MLA_NOPE

VMEM_LIMIT = 56 * 1024 * 1024


def _cparams(sem):
    return pltpu.CompilerParams(dimension_semantics=sem, vmem_limit_bytes=VMEM_LIMIT)


def _row_tile(n, cap):
    for t in (2048, 1024, 512, 256, 128, 64, 32, 16, 8):
        if t <= cap and n % t == 0:
            return t
    raise ValueError(f"no row tile for {n}")


def _sigmoid(x):
    return 1.0 / (1.0 + jnp.exp(-x))


def _silu(x):
    return x * (0.5 * jnp.tanh(0.5 * x) + 0.5)


def _rms(x, g):
    ms = jnp.mean(x * x, axis=-1, keepdims=True)
    return x * lax.rsqrt(ms + NORM_EPS) * g


def _in_proj_kernel(x_ref, g_ref, w_ref, o_ref, gate_ref, h_ref):
    j = pl.program_id(1)

    @pl.when(j == 0)
    def _():
        h_ref[...] = _rms(x_ref[...], g_ref[...]).astype(BF16)

    @pl.when(j < P_MAIN // P_TN)
    def _():
        o_ref[...] = jnp.dot(h_ref[...], w_ref[...], preferred_element_type=F32)

    @pl.when(j >= P_MAIN // P_TN)
    def _():
        gate_ref[...] = jnp.dot(h_ref[...], w_ref[...], preferred_element_type=F32).astype(BF16)


def _in_proj(x, g, w):
    n = x.shape[0]
    tm = _row_tile(n, 2048)
    n_main = P_MAIN // P_TN
    return pl.pallas_call(
        _in_proj_kernel,
        grid=(n // tm, P_COLS // P_TN),
        in_specs=[pl.BlockSpec((tm, D_MODEL), lambda i, j: (i, 0)),
                  pl.BlockSpec((1, D_MODEL), lambda i, j: (0, 0)),
                  pl.BlockSpec((D_MODEL, P_TN), lambda i, j: (0, j))],
        out_specs=[pl.BlockSpec((tm, P_TN), lambda i, j: (i, jnp.minimum(j, n_main - 1))),
                   pl.BlockSpec((tm, P_TN), lambda i, j: (i, jnp.maximum(j - n_main, 0)))],
        out_shape=[jax.ShapeDtypeStruct((n, P_MAIN), F32),
                   jax.ShapeDtypeStruct((n, P_COLS - P_MAIN), BF16)],
        scratch_shapes=[pltpu.VMEM((tm, D_MODEL), BF16)],
        compiler_params=_cparams(("parallel", "arbitrary")),
        name="in_proj",
    )(x, g, w)


def _rope(x, c, s1, s2):
    return x * c + pltpu.roll(x, HEAD_PAD - 16, 1) * s1 + pltpu.roll(x, 16, 1) * s2


def _mla_pre_kernel(p_ref, gq_ref, gkv_ref, wq_ref, hq_ref, hk_ref, c_ref, s1_ref, s2_ref,
                    q_ref, ckv_ref, kr_ref, *, scale, transposed):
    c, s1, s2 = c_ref[...], s1_ref[...], s2_ref[...]
    ckv_ref[...] = _rms(p_ref[:, COL_CKV:COL_KR], gkv_ref[...])

    kr = p_ref[:, COL_KR:COL_KR + HEAD_PAD]
    kr_ms = jnp.sum(kr * kr, axis=-1, keepdims=True) * (1.0 / MLA_ROPE)
    kr_ref[...] = _rope(kr * lax.rsqrt(kr_ms + NORM_EPS) * hk_ref[...], c, s1, s2)

    cq = _rms(p_ref[:, 0:MLA_Q_LORA], gq_ref[...]).astype(BF16)
    q = jnp.dot(cq, wq_ref[...], preferred_element_type=F32)
    lane = lax.broadcasted_iota(jnp.int32, (1, HEAD_PAD), 1)
    is_nope = lane < MLA_NOPE
    hq = hq_ref[...]
    for h in range(MLA_HEADS):
        qh = q[:, h * HEAD_PAD:(h + 1) * HEAD_PAD]
        sq = qh * qh
        ms_n = jnp.sum(jnp.where(is_nope, sq, 0.0), axis=-1, keepdims=True) * (1.0 / MLA_NOPE)
        ms_r = jnp.sum(jnp.where(is_nope, 0.0, sq), axis=-1, keepdims=True) * (1.0 / MLA_ROPE)
        inv = jnp.where(is_nope, lax.rsqrt(ms_n + NORM_EPS), lax.rsqrt(ms_r + NORM_EPS))
        qh = _rope(qh * inv * hq, c, s1, s2) * scale
        if transposed:
            q_ref[0, h] = qh.T.astype(BF16)
        else:
            q_ref[:, h * HEAD_PAD:(h + 1) * HEAD_PAD] = qh.astype(BF16)


def _mla_pre(p, b, t_seq, gq, gkv, wq, hq, hk, tabs, transposed):
    n = p.shape[0]
    tm = _row_tile(n, 512)
    c, s1, s2 = tabs
    if tm > t_seq:
        c, s1, s2 = (jnp.tile(t, (tm // t_seq, 1)) for t in (c, s1, s2))
    n_tab = c.shape[0] // tm
    tab_spec = pl.BlockSpec((tm, HEAD_PAD), lambda i: (i % n_tab, 0))
    vec = lambda w: pl.BlockSpec((1, w), lambda i: (0, 0))
    scale = MLA_QK_HEAD ** -0.5
    if transposed:
        scale *= LOG2_E
        q_spec = pl.BlockSpec((1, MLA_HEADS, HEAD_PAD, tm), lambda i: (i // n_tab, 0, 0, i % n_tab))
        q_shape = jax.ShapeDtypeStruct((b, MLA_HEADS, HEAD_PAD, t_seq), BF16)
    else:
        q_spec = pl.BlockSpec((tm, QK_W), lambda i: (i, 0))
        q_shape = jax.ShapeDtypeStruct((n, QK_W), BF16)
    return pl.pallas_call(
        functools.partial(_mla_pre_kernel, scale=scale, transposed=transposed),
        grid=(n // tm,),
        in_specs=[pl.BlockSpec((tm, 1024), lambda i: (i, 0)),
                  vec(MLA_Q_LORA), vec(MLA_KV_LORA),
                  pl.BlockSpec((MLA_Q_LORA, QK_W), lambda i: (0, 0)),
                  vec(HEAD_PAD), vec(HEAD_PAD), tab_spec, tab_spec, tab_spec],
        out_specs=[q_spec,
                   pl.BlockSpec((tm, MLA_KV_LORA), lambda i: (i, 0)),
                   pl.BlockSpec((tm, HEAD_PAD), lambda i: (i, 0))],
        out_shape=[q_shape,
                   jax.ShapeDtypeStruct((n, MLA_KV_LORA), F32),
                   jax.ShapeDtypeStruct((n, HEAD_PAD), F32)],
        compiler_params=_cparams(("parallel",)),
        name="mla_pre",
    )(p, gq, gkv, wq, hq, hk, c, s1, s2)


def _kv_up_kernel(ckv_ref, kr_ref, wk_ref, wv_ref, hk_ref, k_ref, v_ref, *, transposed):
    c = ckv_ref[...].astype(BF16)
    k = jnp.dot(c, wk_ref[...], preferred_element_type=F32)
    kr = kr_ref[...]
    hk = hk_ref[...]
    for h in range(MLA_HEADS):
        kh = k[:, h * HEAD_PAD:(h + 1) * HEAD_PAD]
        ms = jnp.sum(kh * kh, axis=-1, keepdims=True) * (1.0 / MLA_NOPE)
        k_ref[:, h * HEAD_PAD:(h + 1) * HEAD_PAD] = (kh * lax.rsqrt(ms + NORM_EPS) * hk + kr).astype(BF16)
    v = jnp.dot(c, wv_ref[...], preferred_element_type=F32)
    if transposed:
        lane = lax.broadcasted_iota(jnp.int32, (1, HEAD_PAD), 1)
        for h in range(MLA_HEADS):
            vh = jnp.where(lane < MLA_V, v[:, h * HEAD_PAD:(h + 1) * HEAD_PAD], 1.0)
            v_ref[0, h] = vh.T[0:VT_ROWS].astype(BF16)
    else:
        v_ref[...] = v.astype(BF16)


def _kv_up(ckv, kr, b, s_len, wk, wv, hk, transposed):
    n = ckv.shape[0]
    tm = _row_tile(n, 512)
    if transposed:
        n_t = s_len // tm
        v_spec = pl.BlockSpec((1, MLA_HEADS, VT_ROWS, tm), lambda i: (i // n_t, 0, 0, i % n_t))
        v_shape = jax.ShapeDtypeStruct((b, MLA_HEADS, VT_ROWS, s_len), BF16)
    else:
        v_spec = pl.BlockSpec((tm, MLA_VW), lambda i: (i, 0))
        v_shape = jax.ShapeDtypeStruct((n, MLA_VW), BF16)
    return pl.pallas_call(
        functools.partial(_kv_up_kernel, transposed=transposed),
        grid=(n // tm,),
        in_specs=[pl.BlockSpec((tm, MLA_KV_LORA), lambda i: (i, 0)),
                  pl.BlockSpec((tm, HEAD_PAD), lambda i: (i, 0)),
                  pl.BlockSpec((MLA_KV_LORA, QK_W), lambda i: (0, 0)),
                  pl.BlockSpec((MLA_KV_LORA, wv.shape[1]), lambda i: (0, 0)),
                  pl.BlockSpec((1, HEAD_PAD), lambda i: (0, 0))],
        out_specs=[pl.BlockSpec((tm, QK_W), lambda i: (i, 0)), v_spec],
        out_shape=[jax.ShapeDtypeStruct((n, QK_W), BF16), v_shape],
        compiler_params=_cparams(("parallel",)),
        name="kv_up",
    )(ckv, kr, wk, wv, hk)


def _attn_kernel(q_ref, k_ref, v_ref, o_ref, m_ref, l_ref, acc_ref, *, tq, tk, causal):
    m_ref[...] = jnp.full(m_ref.shape, -jnp.inf, F32)
    l_ref[...] = jnp.zeros(l_ref.shape, F32)
    acc_ref[...] = jnp.zeros(acc_ref.shape, F32)

    def block(start, masked):
        if masked:
            rq = lax.broadcasted_iota(jnp.int32, (tq, tk), 0) // CHUNK
            ck = lax.broadcasted_iota(jnp.int32, (tq, tk), 1) // CHUNK
            allowed = ck <= rq
        def scores(h):
            qh = q_ref[0, :, h * HEAD_PAD:(h + 1) * HEAD_PAD]
            kh = k_ref[0, pl.ds(start, tk), h * HEAD_PAD:(h + 1) * HEAD_PAD]
            return lax.dot_general(qh, kh, (((1,), (1,)), ((), ())), preferred_element_type=F32)

        all_scores = [scores(h) for h in range(MLA_HEADS)]
        for h in range(MLA_HEADS):
            vh = v_ref[0, pl.ds(start, tk), h * MLA_V:(h + 1) * MLA_V]
            s = all_scores[h]
            if masked:
                s = jnp.where(allowed, s, -jnp.inf)
            m_prev = m_ref[h]
            m_new = jnp.maximum(m_prev, jnp.max(s, axis=-1, keepdims=True))
            alpha = jnp.exp(m_prev - m_new)
            p = jnp.exp(s - m_new)
            l_ref[h] = alpha * l_ref[h] + jnp.sum(p, axis=-1, keepdims=True)
            pv = jnp.dot(p.astype(BF16), vh, preferred_element_type=F32)
            acc_ref[h] = alpha * acc_ref[h] + pv
            m_ref[h] = m_new

    if causal:
        qi = pl.program_id(1)

        def body(j, carry):
            block(pl.multiple_of(j * tk, tk), False)
            return carry

        lax.fori_loop(0, qi, body, 0)
        block(pl.multiple_of(qi * tk, tk), True)
    else:
        block(0, False)

    for h in range(MLA_HEADS):
        o_ref[0, :, h * MLA_V:(h + 1) * MLA_V] = acc_ref[h] / l_ref[h]


def _attention(q, k, v, causal):
    b, t, _ = q.shape
    s = k.shape[1]
    if causal:
        tq = tk = min(256, t)
    else:
        tq, tk = t, s
    return pl.pallas_call(
        functools.partial(_attn_kernel, tq=tq, tk=tk, causal=causal),
        grid=(b, t // tq),
        in_specs=[pl.BlockSpec((1, tq, QK_W), lambda i, j: (i, j, 0)),
                  pl.BlockSpec((1, s, QK_W), lambda i, j: (i, 0, 0)),
                  pl.BlockSpec((1, s, MLA_VW), lambda i, j: (i, 0, 0))],
        out_specs=pl.BlockSpec((1, tq, MLA_VW), lambda i, j: (i, j, 0)),
        out_shape=jax.ShapeDtypeStruct((b, t, MLA_VW), F32),
        scratch_shapes=[pltpu.VMEM((MLA_HEADS, tq, 1), F32),
                        pltpu.VMEM((MLA_HEADS, tq, 1), F32),
                        pltpu.VMEM((MLA_HEADS, tq, MLA_V), F32)],
        compiler_params=_cparams(("parallel", "arbitrary")),
        name="mla_attn",
    )(q, k, v)


def _attn_t_kernel(qt_ref, k_ref, vt_ref, o_ref, m_ref, acc_ref, *, tq, tk):
    m_ref[...] = jnp.full(m_ref.shape, -jnp.inf, F32)
    acc_ref[...] = jnp.zeros(acc_ref.shape, F32)

    def blocks(starts, masked):
        if masked:
            ck = lax.broadcasted_iota(jnp.int32, (tk, tq), 0) // CHUNK
            cq = lax.broadcasted_iota(jnp.int32, (tk, tq), 1) // CHUNK
            allowed = ck <= cq

        def scores(start, h):
            kh = k_ref[0, pl.ds(start, tk), h * HEAD_PAD:(h + 1) * HEAD_PAD]
            return jnp.dot(kh, qt_ref[0, h], preferred_element_type=F32)

        def update(start, h, s):
            if masked:
                s = jnp.where(allowed, s, -jnp.inf)
            m_prev = m_ref[h]
            m_new = jnp.maximum(m_prev, jnp.max(s, axis=0, keepdims=True))
            alpha = jnp.exp2(m_prev - m_new)
            p = jnp.exp2(s - m_new).astype(BF16)
            pv = jnp.dot(vt_ref[0, h, :, pl.ds(start, tk)], p, preferred_element_type=F32)
            acc_ref[h] = alpha * acc_ref[h] + pv
            m_ref[h] = m_new

        items = [(start, h) for start in starts for h in range(MLA_HEADS)]
        ahead = 5
        pending = [scores(*it) for it in items[:ahead]]
        for n, it in enumerate(items):
            s = pending.pop(0)
            if n + ahead < len(items):
                pending.append(scores(*items[n + ahead]))
            update(*it, s)

    qi = pl.program_id(1)

    def body(j, carry):
        first = pl.multiple_of(2 * j * tk, tk)
        blocks([first, pl.multiple_of(first + tk, tk)], False)
        return carry

    lax.fori_loop(0, qi // 2, body, 0)

    @pl.when(qi % 2 == 1)
    def _():
        blocks([pl.multiple_of((qi - 1) * tk, tk)], False)

    blocks([pl.multiple_of(qi * tk, tk)], True)

    ot = jnp.concatenate([acc_ref[h, 0:MLA_V] / acc_ref[h, MLA_V:MLA_V + 1] for h in range(MLA_HEADS)], axis=0)
    o_ref[0] = ot.T


def _attention_t(qt, k, vt):
    b, _, _, t = qt.shape
    s = k.shape[1]
    tq = tk = min(256, t)
    return pl.pallas_call(
        functools.partial(_attn_t_kernel, tq=tq, tk=tk),
        grid=(b, t // tq),
        in_specs=[pl.BlockSpec((1, MLA_HEADS, HEAD_PAD, tq), lambda i, j: (i, 0, 0, j)),
                  pl.BlockSpec((1, s, QK_W), lambda i, j: (i, 0, 0)),
                  pl.BlockSpec((1, MLA_HEADS, VT_ROWS, s), lambda i, j: (i, 0, 0, 0))],
        out_specs=pl.BlockSpec((1, tq, MLA_VW), lambda i, j: (i, j, 0)),
        out_shape=jax.ShapeDtypeStruct((b, t, MLA_VW), F32),
        scratch_shapes=[pltpu.VMEM((MLA_HEADS, 1, tq), F32),
                        pltpu.VMEM((MLA_HEADS, VT_ROWS, tq), F32)],
        compiler_params=_cparams(("parallel", "arbitrary")),
        name="mla_attn_t",
    )(qt, k, vt)


def _row_iota(shape):
    return lax.broadcasted_iota(jnp.int32, shape, 0)


def _upper_half_masks(shape):
    row = _row_iota(shape)
    masks = []
    m = 1
    while m < CHUNK:
        masks.append((row // m) % 2 == 1)
        m *= 2
    return masks


def _segment_scans(g, uppers):
    rows = g.shape[0]
    pre, tot = g, g
    out = [(pre, tot)]
    m = 1
    while m < CHUNK:
        upper = uppers[len(out) - 1]
        from_lower = pltpu.roll(tot, m, 0)
        from_upper = pltpu.roll(tot, rows - m, 0)
        pre = pre + jnp.where(upper, from_lower, 0.0)
        tot = tot + jnp.where(upper, from_lower, from_upper)
        out.append((pre, tot))
        m *= 2
    return out


def _dot_nt(a, b):
    return lax.dot_general(a, b, (((1,), (1,)), ((), ())), preferred_element_type=F32)


def _dot_tn(a, b):
    return lax.dot_general(a, b, (((0,), (0,)), ((), ())), preferred_element_type=F32)


def _gdn_prep_kernel(q_ref, k_ref, v_ref, ab_ref, cw_ref, cs_ref, alog_ref, dt_ref,
                     u_ref, w_ref, qd_ref, kd_ref, qk_ref, gl_ref, carry_ref, *, nb, r):
    @pl.when(pl.program_id(1) == 0)
    def _():
        carry_ref[...] = cs_ref[...]

    n = nb * r
    nc = n // CHUNK

    def conv(x_ref, j):
        w = cw_ref[:, j * GDN_QK:(j + 1) * GDN_QK]
        outs = []
        for b in range(nb):
            x = x_ref[b]
            xp = jnp.concatenate([carry_ref[b, j], x], axis=0)
            y = x * w[3:4]
            for d in range(1, CONV_W):
                y = y + xp[8 - d:8 - d + r] * w[3 - d:4 - d]
            carry_ref[b, j] = x[r - 8:]
            outs.append(_silu(y))
        return outs[0] if nb == 1 else jnp.concatenate(outs, axis=0)

    q_all, k_all, v_all = conv(q_ref, 0), conv(k_ref, 1), conv(v_ref, 2)

    ab = ab_ref[...].reshape(n, 128)
    x = ab + dt_ref[...]
    softplus = jnp.maximum(x, 0.0) + jnp.log(1.0 + jnp.exp(-jnp.abs(x)))
    g_blk = (-LOG2_E) * jnp.exp(alog_ref[...]) * softplus
    gam_blk = _segment_scans(g_blk, _upper_half_masks(g_blk.shape))[-1][0]
    gam_t = gam_blk.T
    gam3_blk = gam_blk.reshape(nc, CHUNK, 128)
    beta3_blk = _sigmoid(ab).reshape(nc, CHUNK, 128)

    row = lax.broadcasted_iota(jnp.int32, (1, CHUNK, CHUNK), 1)
    col = lax.broadcasted_iota(jnp.int32, (1, CHUNK, CHUNK), 2)
    eye = (row == col).astype(F32)

    def bmm(a, b):
        return jnp.einsum('cij,cjk->cik', a.astype(BF16), b.astype(BF16), preferred_element_type=F32)

    def bmm_nt(a, b):
        return jnp.einsum('cid,cjd->cij', a.astype(BF16), b.astype(BF16), preferred_element_type=F32)

    for h in range(GDN_HEADS):
        sl = slice(h * GDN_DK, (h + 1) * GDN_DK)
        q, k, v = q_all[:, sl], k_all[:, sl], v_all[:, sl]
        q = q * lax.rsqrt(jnp.sum(q * q, axis=-1, keepdims=True) + NORM_EPS) * (GDN_DK ** -0.5)
        k = k * lax.rsqrt(jnp.sum(k * k, axis=-1, keepdims=True) + NORM_EPS)
        q, k, v = (a.reshape(nc, CHUNK, GDN_DK) for a in (q, k, v))
        gam = gam3_blk[:, :, h:h + 1]
        beta = beta3_blk[:, :, GDN_HEADS + h:GDN_HEADS + h + 1]
        gam_row = jnp.stack([gam_t[h:h + 1, c * CHUNK:(c + 1) * CHUNK] for c in range(nc)], axis=0)
        decay = jnp.where(row >= col, jnp.exp2(jnp.minimum(gam - gam_row, 0.0)), 0.0)
        a = jnp.where(row > col, beta * bmm_nt(k, k) * decay, 0.0)
        t_inv = eye - a
        pw = a
        m = 1
        while 2 * m < CHUNK:
            pw = bmm(pw, pw)
            t_inv = t_inv + bmm(t_inv, pw)
            m *= 2
        e_gam = jnp.exp2(gam)
        gam_last = gam[:, CHUNK - 1:CHUNK, :]
        u_ref[:, h] = bmm(t_inv, v * beta).reshape(nb, r, GDN_DV)
        w_ref[:, h] = bmm(t_inv, k * (beta * e_gam)).astype(BF16).reshape(nb, r, GDN_DK)
        qd_ref[:, h] = (q * e_gam).astype(BF16).reshape(nb, r, GDN_DK)
        kd_ref[:, h] = (k * jnp.exp2(gam_last - gam)).astype(BF16).reshape(nb, r, GDN_DK)
        qk_ref[:, h] = (bmm_nt(q, k) * decay).astype(BF16).reshape(nb, r, CHUNK)
        gl_ref[:, h] = jnp.broadcast_to(jnp.exp2(gam_last), (nc, 1, 128)).reshape(nb, r // CHUNK, 1, 128)


def _gdn_scan_kernel(u_ref, w_ref, qd_ref, kd_ref, qk_ref, gl_ref, s0_ref, o_ref, sf_ref, s_ref, *, nb):
    c_idx = pl.program_id(1)

    @pl.when(c_idx == 0)
    def _():
        s_ref[...] = s0_ref[...]

    chains = [(b, h) for b in range(nb) for h in range(GDN_HEADS)]
    s_old = [s_ref[b, h] for b, h in chains]
    s_bf = [s.astype(BF16) for s in s_old]
    v_new = [u_ref[b, h] - jnp.dot(w_ref[b, h], sb, preferred_element_type=F32)
             for (b, h), sb in zip(chains, s_bf)]
    v_bf = [v.astype(BF16) for v in v_new]
    for (b, h), s, sb, vb in zip(chains, s_old, s_bf, v_bf):
        o = (jnp.dot(qd_ref[b, h], sb, preferred_element_type=F32)
             + jnp.dot(qk_ref[b, h], vb, preferred_element_type=F32))
        o_ref[b, :, h * GDN_DV:(h + 1) * GDN_DV] = o
        s_ref[b, h] = gl_ref[b, h, 0] * s + _dot_tn(kd_ref[b, h], vb)

    @pl.when(c_idx == pl.num_programs(1) - 1)
    def _():
        sf_ref[...] = s_ref[...]


def _gdn(p3, conv_w, conv_state8, s0, alog, dt):
    b, t, _ = p3.shape
    r = min(GDN_PREP_ROWS, t)
    nb = max(1, min(b, GDN_PREP_ROWS // r))
    nt = t // r
    nc = t // CHUNK
    blk = lambda j: pl.BlockSpec((nb, r, GDN_QK), lambda i, c: (i, c, COL_GDN // GDN_QK + j))
    vec = pl.BlockSpec((1, 128), lambda i, c: (0, 0))
    head_spec = lambda w: pl.BlockSpec((nb, GDN_HEADS, r, w), lambda i, c: (i, 0, c, 0))
    head_shape = lambda w, dt_: jax.ShapeDtypeStruct((b, GDN_HEADS, t, w), dt_)
    u, w, qd, kd, qk, gl = pl.pallas_call(
        functools.partial(_gdn_prep_kernel, nb=nb, r=r),
        grid=(b // nb, nt),
        in_specs=[blk(0), blk(1), blk(2),
                  pl.BlockSpec((nb, r, 128), lambda i, c: (i, c, COL_GAB // 128)),
                  pl.BlockSpec((CONV_W, 3 * GDN_QK), lambda i, c: (0, 0)),
                  pl.BlockSpec((nb, 3, 8, GDN_QK), lambda i, c: (i, 0, 0, 0)),
                  vec, vec],
        out_specs=[head_spec(GDN_DV), head_spec(GDN_DK), head_spec(GDN_DK), head_spec(GDN_DK),
                   head_spec(CHUNK),
                   pl.BlockSpec((nb, GDN_HEADS, r // CHUNK, 1, 128), lambda i, c: (i, 0, c, 0, 0))],
        out_shape=[head_shape(GDN_DV, F32), head_shape(GDN_DK, BF16), head_shape(GDN_DK, BF16),
                   head_shape(GDN_DK, BF16), head_shape(CHUNK, BF16),
                   jax.ShapeDtypeStruct((b, GDN_HEADS, nc, 1, 128), F32)],
        scratch_shapes=[pltpu.VMEM((nb, 3, 8, GDN_QK), F32)],
        compiler_params=_cparams(("parallel", "arbitrary")),
        name="gdn_prep",
    )(p3, p3, p3, p3, conv_w, conv_state8, alog, dt)

    sb = min(b, 8)
    chunk_spec = lambda w: pl.BlockSpec((sb, GDN_HEADS, CHUNK, w), lambda i, c: (i, 0, c, 0))
    state_spec = pl.BlockSpec((sb, GDN_HEADS, GDN_DK, GDN_DV), lambda i, c: (i, 0, 0, 0))
    return pl.pallas_call(
        functools.partial(_gdn_scan_kernel, nb=sb),
        grid=(b // sb, nc),
        in_specs=[chunk_spec(GDN_DV), chunk_spec(GDN_DK), chunk_spec(GDN_DK), chunk_spec(GDN_DK),
                  chunk_spec(CHUNK),
                  pl.BlockSpec((sb, GDN_HEADS, 1, 1, 128), lambda i, c: (i, 0, c, 0, 0)),
                  state_spec],
        out_specs=[pl.BlockSpec((sb, CHUNK, GDN_VW), lambda i, c: (i, c, 0)), state_spec],
        out_shape=[jax.ShapeDtypeStruct((b, t, GDN_VW), F32),
                   jax.ShapeDtypeStruct((b, GDN_HEADS, GDN_DK, GDN_DV), F32)],
        scratch_shapes=[pltpu.VMEM((sb, GDN_HEADS, GDN_DK, GDN_DV), F32)],
        compiler_params=_cparams(("parallel", "arbitrary")),
        name="gdn_scan",
    )(u, w, qd, kd, qk, gl, s0)


def _hgrn_prep_kernel(q_ref, f_ref, lb_ref, att_ref, qe_ref, ke_ref, dec_ref, *, nb, r):
    n = nb * r
    nc = n // CHUNK
    row = lax.broadcasted_iota(jnp.int32, (1, CHUNK, CHUNK), 1)
    col = lax.broadcasted_iota(jnp.int32, (1, CHUNK, CHUNK), 2)
    uppers = _upper_half_masks((n, HGRN_DK))
    halves = [1 << lvl for lvl in range(len(uppers))]
    pairs = [(row // (2 * m) == col // (2 * m)) & ((row // m) % 2 == 1) & ((col // m) % 2 == 0) for m in halves]

    def bmm_nt(a, b):
        a3, b3 = (x.astype(BF16).reshape(nc, CHUNK, HGRN_DK) for x in (a, b))
        return jnp.einsum('cid,cjd->cij', a3, b3, preferred_element_type=F32)

    for h in range(HGRN_HEADS):
        sl = slice(h * HGRN_DK, (h + 1) * HGRN_DK)
        lb = lb_ref[:, sl]
        f = lb + (1.0 - lb) * _sigmoid(f_ref[:, :, sl].reshape(n, HGRN_DK))
        q = _silu(q_ref[:, :, sl].reshape(n, HGRN_DK)) * (HGRN_DK ** -0.5)
        k = 1.0 - f
        scans = _segment_scans(jnp.log2(f), uppers)
        cb, c_tot = scans[-1]

        att = jnp.where(row == col, bmm_nt(q, k), 0.0)
        for lvl in range(len(halves)):
            pre_m, tot_m = scans[lvl]
            att = att + jnp.where(pairs[lvl], bmm_nt(q * jnp.exp2(pre_m), k * jnp.exp2(tot_m - pre_m)), 0.0)

        att_ref[:, h] = att.astype(BF16).reshape(nb, r, CHUNK)
        qe_ref[:, h] = (q * jnp.exp2(cb)).astype(BF16).reshape(nb, r, HGRN_DK)
        ke_ref[:, h] = (k * jnp.exp2(c_tot - cb)).astype(BF16).reshape(nb, r, HGRN_DK)
        dec_ref[:, h] = jnp.exp2(c_tot).reshape(nc, CHUNK, HGRN_DK)[:, 0:1, :].reshape(nb, r // CHUNK, 1, HGRN_DK)


def _hgrn_scan_kernel(att_ref, qe_ref, ke_ref, dec_ref, v_ref, s0_ref, o_ref, sf_ref, st_ref, *, nb):
    c_idx = pl.program_id(1)

    @pl.when(c_idx == 0)
    def _():
        st_ref[...] = s0_ref[...]

    chains = [(b, h) for b in range(nb) for h in range(HGRN_HEADS)]
    sls = [slice(h * HGRN_DV, (h + 1) * HGRN_DV) for h in range(HGRN_HEADS)]
    st_old = [st_ref[b, h] for b, h in chains]
    vs = [v_ref[b, :, sls[h]].astype(BF16) for b, h in chains]
    o_st = [_dot_nt(qe_ref[b, h], st.astype(BF16)) for (b, h), st in zip(chains, st_old)]
    for (b, h), st, v, o1 in zip(chains, st_old, vs, o_st):
        o_ref[b, :, sls[h]] = o1 + jnp.dot(att_ref[b, h], v, preferred_element_type=F32)
        st_ref[b, h] = st * dec_ref[b, h, 0] + _dot_tn(v, ke_ref[b, h])

    @pl.when(c_idx == pl.num_programs(1) - 1)
    def _():
        sf_ref[...] = st_ref[...]


def _hgrn(p3, lb, s0t):
    b, t, _ = p3.shape
    r = min(HGRN_PREP_ROWS, t)
    nb = max(1, min(b, HGRN_PREP_ROWS // r))
    nc = t // CHUNK
    col0 = (COL_GDN + 4 * GDN_QK) // HGRN_KW
    blk = lambda j: pl.BlockSpec((nb, r, HGRN_KW), lambda i, c: (i, c, col0 + j))
    head_spec = lambda w: pl.BlockSpec((nb, HGRN_HEADS, r, w), lambda i, c: (i, 0, c, 0))
    head_shape = lambda w: jax.ShapeDtypeStruct((b, HGRN_HEADS, t, w), BF16)
    att, qe, ke, dec = pl.pallas_call(
        functools.partial(_hgrn_prep_kernel, nb=nb, r=r),
        grid=(b // nb, t // r),
        in_specs=[blk(0), blk(1), pl.BlockSpec((1, HGRN_KW), lambda i, c: (0, 0))],
        out_specs=[head_spec(CHUNK), head_spec(HGRN_DK), head_spec(HGRN_DK),
                   pl.BlockSpec((nb, HGRN_HEADS, r // CHUNK, 1, HGRN_DK), lambda i, c: (i, 0, c, 0, 0))],
        out_shape=[head_shape(CHUNK), head_shape(HGRN_DK), head_shape(HGRN_DK),
                   jax.ShapeDtypeStruct((b, HGRN_HEADS, nc, 1, HGRN_DK), F32)],
        compiler_params=_cparams(("parallel", "parallel")),
        name="hgrn_prep",
    )(p3, p3, lb)

    sb = min(b, 8)
    chunk_spec = lambda w: pl.BlockSpec((sb, HGRN_HEADS, CHUNK, w), lambda i, c: (i, 0, c, 0))
    state_spec = pl.BlockSpec((sb, HGRN_HEADS, HGRN_DV, HGRN_DK), lambda i, c: (i, 0, 0, 0))
    return pl.pallas_call(
        functools.partial(_hgrn_scan_kernel, nb=sb),
        grid=(b // sb, nc),
        in_specs=[chunk_spec(CHUNK), chunk_spec(HGRN_DK), chunk_spec(HGRN_DK),
                  pl.BlockSpec((sb, HGRN_HEADS, 1, 1, HGRN_DK), lambda i, c: (i, 0, c, 0, 0)),
                  pl.BlockSpec((sb, CHUNK, HGRN_KW), lambda i, c: (i, c, col0 + 2)),
                  state_spec],
        out_specs=[pl.BlockSpec((sb, CHUNK, HGRN_KW), lambda i, c: (i, c, 0)), state_spec],
        out_shape=[jax.ShapeDtypeStruct((b, t, HGRN_KW), F32),
                   jax.ShapeDtypeStruct((b, HGRN_HEADS, HGRN_DV, HGRN_DK), F32)],
        scratch_shapes=[pltpu.VMEM((sb, HGRN_HEADS, HGRN_DV, HGRN_DK), F32)],
        compiler_params=_cparams(("parallel", "arbitrary")),
        name="hgrn_scan",
    )(att, qe, ke, dec, p3, s0t)


def _merge_kernel(x_ref, oa_ref, ob_ref, zb_ref, oc_ref, zc_ref, g0_ref, g1_ref, g2_ref, ngb_ref, ngc_ref,
                  wa_ref, wb_ref, wc_ref, wo_ref, y_ref):
    def branch(o, w_ref, g_ref):
        gate = _sigmoid(g_ref[...].astype(F32))
        return gate * jnp.dot(o.astype(BF16), w_ref[...], preferred_element_type=F32)

    def normed(o_ref, z_ref, ng_ref):
        return jnp.concatenate(
            [_rms(o_ref[:, h * 128:(h + 1) * 128], ng_ref[...]) * _silu(z_ref[:, h * 128:(h + 1) * 128])
             for h in range(GDN_HEADS)], axis=-1)

    mixed = (branch(oa_ref[...], wa_ref, g0_ref) + branch(normed(ob_ref, zb_ref, ngb_ref), wb_ref, g1_ref)
             + branch(normed(oc_ref, zc_ref, ngc_ref), wc_ref, g2_ref))
    y_ref[...] = x_ref[...] + jnp.dot(mixed.astype(BF16), wo_ref[...], preferred_element_type=F32)


def _merge(x, oa, ob, oc, p, pg, ngb, ngc, wa, wb, wc, wo):
    n = x.shape[0]
    tm = _row_tile(n, 512)
    row = lambda w: pl.BlockSpec((tm, w), lambda i: (i, 0))
    gate = lambda j: pl.BlockSpec((tm, D_MODEL), lambda i: (i, j))
    zb_spec = pl.BlockSpec((tm, GDN_VW), lambda i: (i, COL_GDN // GDN_VW + 3))
    zc_spec = pl.BlockSpec((tm, HGRN_KW), lambda i: (i, (COL_GDN + 4 * GDN_QK) // HGRN_KW + 3))
    vec = pl.BlockSpec((1, 128), lambda i: (0, 0))
    wsp = lambda k: pl.BlockSpec((k, D_MODEL), lambda i: (0, 0))
    return pl.pallas_call(
        _merge_kernel,
        grid=(n // tm,),
        in_specs=[row(D_MODEL), row(MLA_VW), row(GDN_VW), zb_spec, row(HGRN_KW), zc_spec,
                  gate(0), gate(1), gate(2), vec, vec,
                  wsp(MLA_VW), wsp(GDN_VW), wsp(HGRN_KW), wsp(D_MODEL)],
        out_specs=row(D_MODEL),
        out_shape=jax.ShapeDtypeStruct((n, D_MODEL), F32),
        compiler_params=_cparams(("parallel",)),
        name="merge_out",
    )(x, oa, ob, p, oc, p, pg, pg, pg, ngb, ngc, wa, wb, wc, wo)


def _router_kernel(x_ref, g_ref, w_ref, b_ref, cw_ref, cwt_ref, cnt_ref):
    h = _rms(x_ref[...], g_ref[...])
    logits = jnp.dot(h, w_ref[...], preferred_element_type=F32, precision=lax.Precision.HIGHEST) + b_ref[...]
    lane = lax.broadcasted_iota(jnp.int32, logits.shape, 1)
    valid = lane < N_EXPERTS
    neg = -jnp.inf
    l1 = jnp.where(valid, logits, neg)
    m1 = jnp.max(l1, axis=-1, keepdims=True)
    i1 = jnp.min(jnp.where(l1 == m1, lane, 128), axis=-1, keepdims=True)
    l2 = jnp.where(lane == i1, neg, l1)
    m2 = jnp.max(l2, axis=-1, keepdims=True)
    i2 = jnp.min(jnp.where(l2 == m2, lane, 128), axis=-1, keepdims=True)
    e2 = jnp.exp(m2 - m1)
    den = 1.0 + e2
    cw = jnp.where(lane == i1, 1.0 / den, 0.0) + jnp.where(lane == i2, e2 / den, 0.0)
    cw_ref[...] = cw
    cwt_ref[...] = cw.T
    cnt_ref[0] = jnp.sum((cw > 0.0).astype(F32), axis=0, keepdims=True)


def _router(x, g, w, b, tm):
    n = x.shape[0]
    return pl.pallas_call(
        _router_kernel,
        grid=(n // tm,),
        in_specs=[pl.BlockSpec((tm, D_MODEL), lambda i: (i, 0)),
                  pl.BlockSpec((1, D_MODEL), lambda i: (0, 0)),
                  pl.BlockSpec((D_MODEL, 128), lambda i: (0, 0)),
                  pl.BlockSpec((1, 128), lambda i: (0, 0))],
        out_specs=[pl.BlockSpec((tm, 128), lambda i: (i, 0)),
                   pl.BlockSpec((128, tm), lambda i: (0, i)),
                   pl.BlockSpec((1, 1, 128), lambda i: (i, 0, 0))],
        out_shape=[jax.ShapeDtypeStruct((n, 128), F32),
                   jax.ShapeDtypeStruct((128, n), F32),
                   jax.ShapeDtypeStruct((n // tm, 1, 128), F32)],
        compiler_params=_cparams(("parallel",)),
        name="moe_router",
    )(x, g, w, b)


def _moe_kernel(cnt_ref, x_ref, g_ref, cw_ref, cwt_ref, wg_ref, wu_ref, wd_ref, y_ref,
                h_ref, rcol_ref, rrow_ref, *, t, br):
    i, e = pl.program_id(0), pl.program_id(1)

    @pl.when(e == 0)
    def _():
        x = x_ref[...]
        h_ref[...] = _rms(x, g_ref[...]).astype(BF16)
        y_ref[...] = x
        r = lax.broadcasted_iota(jnp.int32, (t, t), 0)
        c = lax.broadcasted_iota(jnp.int32, (t, t), 1)
        on = cw_ref[...] > 0.0
        rank = jnp.dot((c < r).astype(BF16), on.astype(BF16), preferred_element_type=F32)
        rcol_ref[...] = jnp.where(on, rank.astype(jnp.int32), -1)
        on_t = cwt_ref[...] > 0.0
        rank_t = jnp.dot(on_t.astype(BF16), (r < c).astype(BF16), preferred_element_type=F32)
        rrow_ref[...] = jnp.where(on_t, rank_t.astype(jnp.int32), -1)

    count = cnt_ref[i * N_EXPERTS + e]
    n_blocks = lax.div(count + (br - 1), br)
    sel = lax.broadcasted_iota(jnp.int32, (1, 128), 1) == e
    rank_c = jnp.sum(jnp.where(sel, rcol_ref[...], 0), axis=-1, keepdims=True)
    rank_r = rrow_ref[pl.ds(e, 1), :]
    w_r = cwt_ref[pl.ds(e, 1), :]

    def body(j, carry):
        base = j * br
        rows = lax.broadcasted_iota(jnp.int32, (br, t), 0) + base
        pick = rows == rank_r
        xc = jnp.dot(pick.astype(BF16), h_ref[...], preferred_element_type=F32).astype(BF16)
        a = jnp.dot(xc, wg_ref[0], preferred_element_type=F32)
        b = jnp.dot(xc, wu_ref[0], preferred_element_type=F32)
        yc = jnp.dot((_silu(a) * b).astype(BF16), wd_ref[0], preferred_element_type=F32)
        w_rows = jnp.sum(jnp.where(pick, w_r, 0.0), axis=-1, keepdims=True)
        cols = lax.broadcasted_iota(jnp.int32, (t, br), 1) + base
        y_ref[...] += jnp.dot((cols == rank_c).astype(BF16), (yc * w_rows).astype(BF16),
                              preferred_element_type=F32)
        return carry

    lax.fori_loop(0, n_blocks, body, 0)


def _moe(x, g, cw, cwt, counts, wg, wu, wd, t):
    n = x.shape[0]
    ne, _, ff = wg.shape
    br = MOE_BLOCK_ROWS
    grid_spec = pltpu.PrefetchScalarGridSpec(
        num_scalar_prefetch=1,
        grid=(n // t, ne),
        in_specs=[pl.BlockSpec((t, D_MODEL), lambda i, e, cnt: (i, 0)),
                  pl.BlockSpec((1, D_MODEL), lambda i, e, cnt: (0, 0)),
                  pl.BlockSpec((t, 128), lambda i, e, cnt: (i, 0)),
                  pl.BlockSpec((128, t), lambda i, e, cnt: (0, i)),
                  pl.BlockSpec((1, D_MODEL, ff), lambda i, e, cnt: (e, 0, 0)),
                  pl.BlockSpec((1, D_MODEL, ff), lambda i, e, cnt: (e, 0, 0)),
                  pl.BlockSpec((1, ff, D_MODEL), lambda i, e, cnt: (e, 0, 0))],
        out_specs=pl.BlockSpec((t, D_MODEL), lambda i, e, cnt: (i, 0)),
        scratch_shapes=[pltpu.VMEM((t, D_MODEL), BF16),
                        pltpu.VMEM((t, 128), jnp.int32),
                        pltpu.VMEM((128, t), jnp.int32)])
    return pl.pallas_call(
        functools.partial(_moe_kernel, t=t, br=br),
        grid_spec=grid_spec,
        out_shape=jax.ShapeDtypeStruct((n, D_MODEL), F32),
        compiler_params=_cparams(("parallel", "arbitrary")),
        name="moe",
    )(counts, x, g, cw, cwt, wg, wu, wd)


def _ffn_kernel(x_ref, g_ref, wg_ref, wu_ref, wd_ref, y_ref, h_ref):
    e = pl.program_id(1)

    @pl.when(e == 0)
    def _():
        x = x_ref[...]
        h_ref[...] = _rms(x, g_ref[...]).astype(BF16)
        y_ref[...] = x

    h = h_ref[...]
    a = jnp.dot(h, wg_ref[0], preferred_element_type=F32)
    b = jnp.dot(h, wu_ref[0], preferred_element_type=F32)
    y_ref[...] += jnp.dot((_silu(a) * b).astype(BF16), wd_ref[0], preferred_element_type=F32)


def _ffn(x, g, wg, wu, wd):
    n = x.shape[0]
    ne, _, ff = wg.shape
    tm = _row_tile(n, 512)
    return pl.pallas_call(
        _ffn_kernel,
        grid=(n // tm, ne),
        in_specs=[pl.BlockSpec((tm, D_MODEL), lambda i, e: (i, 0)),
                  pl.BlockSpec((1, D_MODEL), lambda i, e: (0, 0)),
                  pl.BlockSpec((1, D_MODEL, ff), lambda i, e: (e, 0, 0)),
                  pl.BlockSpec((1, D_MODEL, ff), lambda i, e: (e, 0, 0)),
                  pl.BlockSpec((1, ff, D_MODEL), lambda i, e: (e, 0, 0))],
        out_specs=pl.BlockSpec((tm, D_MODEL), lambda i, e: (i, 0)),
        out_shape=jax.ShapeDtypeStruct((n, D_MODEL), F32),
        scratch_shapes=[pltpu.VMEM((tm, D_MODEL), BF16)],
        compiler_params=_cparams(("parallel", "arbitrary")),
        name="ffn",
    )(x, g, wg, wu, wd)


def _pad_lanes(x, left, total):
    return jnp.pad(x, [(0, 0)] * (x.ndim - 1) + [(left, total - left - x.shape[-1])])


def _pack_w_in(w):
    cq, ckv, kr, gq, gk, gv, gz, ga, gb, hq, hf, hi, hg, gates = jnp.split(
        w, np.cumsum(SPLIT_SIZES)[:-1].tolist(), axis=-1)
    kr_blk = _pad_lanes(kr, ROPE_LANE0, 128)
    ab_blk = _pad_lanes(jnp.concatenate([ga, gb], axis=-1), 0, 256)
    return jnp.concatenate([cq, ckv, kr_blk, ab_blk, gq, gk, gv, gz, hq, hf, hi, hg, gates], axis=-1).astype(BF16)


def _rope_tables(n_pos):
    half = MLA_ROPE // 2
    inv = ROPE_THETA ** (-jnp.arange(half, dtype=F32) / half)
    ang = jnp.arange(n_pos, dtype=F32)[:, None] * inv[None, :]
    cos, sin = jnp.cos(ang), jnp.sin(ang)
    one = jnp.ones((n_pos, MLA_NOPE), F32)
    zero = jnp.zeros((n_pos, MLA_NOPE), F32)
    tail = jnp.zeros((n_pos, HEAD_PAD - MLA_QK_HEAD), F32)
    z16 = jnp.zeros((n_pos, half), F32)
    c = jnp.concatenate([one, cos, cos, tail], axis=-1)
    s1 = jnp.concatenate([zero, -sin, z16, tail], axis=-1)
    s2 = jnp.concatenate([zero, z16, sin, tail], axis=-1)
    return c, s1, s2


def _layer_weights(l, a):
    f = {}
    f['mixer_g'] = a['mixer_norm_g'][l][None]
    f['w_in'] = _pack_w_in(a['w_in'][l])
    f['gq'] = a['mla_q_norm_g'][l][None]
    f['gkv'] = a['mla_kv_norm_g'][l][None]
    wq = a['mla_w_q_up'][l].reshape(MLA_Q_LORA, MLA_HEADS, MLA_QK_HEAD)
    f['wq'] = _pad_lanes(wq, 0, HEAD_PAD).reshape(MLA_Q_LORA, QK_W).astype(BF16)
    wkv = a['mla_w_kv_up'][l].reshape(MLA_KV_LORA, MLA_HEADS, MLA_NOPE + MLA_V)
    f['wk'] = _pad_lanes(wkv[:, :, :MLA_NOPE], 0, HEAD_PAD).reshape(MLA_KV_LORA, QK_W).astype(BF16)
    f['wv'] = wkv[:, :, MLA_NOPE:].reshape(MLA_KV_LORA, MLA_VW).astype(BF16)
    f['wv_pad'] = _pad_lanes(wkv[:, :, MLA_NOPE:], 0, HEAD_PAD).reshape(MLA_KV_LORA, QK_W).astype(BF16)
    f['hq'] = _pad_lanes(a['mla_q_head_norm_g'][l][None], 0, HEAD_PAD)
    hk = a['mla_k_head_norm_g'][l][None]
    f['hk_nope'] = _pad_lanes(hk[:, :MLA_NOPE], 0, HEAD_PAD)
    f['hk_rope'] = _pad_lanes(hk[:, MLA_NOPE:], ROPE_LANE0, HEAD_PAD)
    f['wo_a'] = a['mla_w_o'][l].astype(BF16)
    f['conv_w'] = a['gdn_conv_w'][l]
    f['alog'] = _pad_lanes(a['gdn_a_log'][l][None], 0, 128)
    f['dt'] = _pad_lanes(a['gdn_dt_bias'][l][None], 0, 128)
    f['gdn_g'] = a['gdn_norm_g'][l][None]
    f['wo_b'] = a['gdn_w_o'][l].astype(BF16)
    f['hgrn_g'] = a['hgrn_norm_g'][l][None]
    f['wo_c'] = a['hgrn_w_o'][l].astype(BF16)
    f['w_out'] = a['w_out'][l].astype(BF16)
    f['ffn_g'] = a['ffn_norm_g'][l][None]
    if l % 2 == 0:
        wg, wu, wd = a['dense_w_gate'][l // 2], a['dense_w_up'][l // 2], a['dense_w_down'][l // 2]
        ff = wg.shape[1]
        half = ff // 2
        f['ffn'] = (jnp.moveaxis(wg.reshape(D_MODEL, 2, half), 1, 0).astype(BF16),
                    jnp.moveaxis(wu.reshape(D_MODEL, 2, half), 1, 0).astype(BF16),
                    wd.reshape(2, half, D_MODEL).astype(BF16))
        f['router'] = None
    else:
        f['ffn'] = (a['moe_w_gate'][l // 2].astype(BF16), a['moe_w_up'][l // 2].astype(BF16),
                    a['moe_w_down'][l // 2].astype(BF16))
        f['router'] = (_pad_lanes(a['moe_w_router'][l // 2], 0, 128),
                       _pad_lanes(a['moe_b_router'][l // 2][None], 0, 128))
    return f


def _trunk_layer(x, b, t, f, lb, tabs, past):
    n = b * t
    p, p_gate = _in_proj(x, f['mixer_g'], f['w_in'])
    p3 = p.reshape(b, t, P_MAIN)

    fresh = past['ckv'] is None
    q, ckv, kr = _mla_pre(p, b, t, f['gq'], f['gkv'], f['wq'], f['hq'], f['hk_rope'], tabs, transposed=fresh)
    if fresh:
        k_all, vt_all = _kv_up(ckv, kr, b, t, f['wk'], f['wv_pad'], f['hk_nope'], transposed=True)
        o_a = _attention_t(q, k_all.reshape(b, t, QK_W), vt_all)
    else:
        s = past['ckv'].shape[1] + t
        ckv_all = jnp.concatenate([past['ckv'], ckv.reshape(b, t, -1)], axis=1).reshape(b * s, -1)
        kr_past = _pad_lanes(past['kr'], ROPE_LANE0, HEAD_PAD)
        kr_all = jnp.concatenate([kr_past, kr.reshape(b, t, -1)], axis=1).reshape(b * s, -1)
        k_all, v_all = _kv_up(ckv_all, kr_all, b, s, f['wk'], f['wv'], f['hk_nope'], transposed=False)
        o_a = _attention(q.reshape(b, t, QK_W), k_all.reshape(b, s, QK_W), v_all.reshape(b, s, MLA_VW),
                         causal=False)

    conv8 = jnp.pad(past['conv'].reshape(b, CONV_W - 1, 3, GDN_QK).transpose(0, 2, 1, 3),
                    ((0, 0), (0, 0), (8 - (CONV_W - 1), 0), (0, 0)))
    o_b, gdn_s = _gdn(p3, f['conv_w'], conv8, past['gdn'], f['alog'], f['dt'])
    gdn_conv = p3[:, t - (CONV_W - 1):, COL_GDN:COL_GDN + 3 * GDN_QK]

    o_c, hgrn_st = _hgrn(p3, lb, jnp.swapaxes(past['hgrn'], -1, -2))
    hgrn_s = jnp.swapaxes(hgrn_st, -1, -2)

    x = _merge(x, o_a.reshape(n, -1), o_b.reshape(n, -1), o_c.reshape(n, -1), p, p_gate, f['gdn_g'],
               f['hgrn_g'], f['wo_a'], f['wo_b'], f['wo_c'], f['w_out'])

    wg, wu, wd = f['ffn']
    if f['router'] is None:
        x = _ffn(x, f['ffn_g'], wg, wu, wd)
    else:
        t_moe = _row_tile(n, MOE_TILE)
        cw, cwt, cnt = _router(x, f['ffn_g'], *f['router'], t_moe)
        counts = cnt[:, 0, :N_EXPERTS].astype(jnp.int32).reshape(-1)
        x = _moe(x, f['ffn_g'], cw, cwt, counts, wg, wu, wd, t_moe)

    new_ckv = ckv.reshape(b, t, MLA_KV_LORA)
    new_kr = kr.reshape(b, t, HEAD_PAD)[:, :, ROPE_LANE0:ROPE_LANE0 + MLA_ROPE]
    return x, (new_ckv, new_kr, gdn_s, gdn_conv, hgrn_s)


def kernel(x_prompt, x_sample, cache_mla_ckv, cache_mla_krope, state_gdn, state_gdn_conv, state_hgrn,
           mixer_norm_g, w_in, mla_q_norm_g, mla_w_q_up, mla_kv_norm_g, mla_w_kv_up,
           mla_q_head_norm_g, mla_k_head_norm_g, mla_w_o,
           gdn_conv_w, gdn_a_log, gdn_dt_bias, gdn_norm_g, gdn_w_o,
           hgrn_lb_logits, hgrn_norm_g, hgrn_w_o, w_out, ffn_norm_g,
           dense_w_gate, dense_w_up, dense_w_down,
           moe_w_router, moe_b_router, moe_w_gate, moe_w_up, moe_w_down):
    a = dict(mixer_norm_g=mixer_norm_g, w_in=w_in, mla_q_norm_g=mla_q_norm_g, mla_w_q_up=mla_w_q_up,
             mla_kv_norm_g=mla_kv_norm_g, mla_w_kv_up=mla_w_kv_up, mla_q_head_norm_g=mla_q_head_norm_g,
             mla_k_head_norm_g=mla_k_head_norm_g, mla_w_o=mla_w_o, gdn_conv_w=gdn_conv_w,
             gdn_a_log=gdn_a_log, gdn_dt_bias=gdn_dt_bias, gdn_norm_g=gdn_norm_g, gdn_w_o=gdn_w_o,
             hgrn_norm_g=hgrn_norm_g, hgrn_w_o=hgrn_w_o, w_out=w_out, ffn_norm_g=ffn_norm_g,
             dense_w_gate=dense_w_gate, dense_w_up=dense_w_up, dense_w_down=dense_w_down,
             moe_w_router=moe_w_router, moe_b_router=moe_b_router, moe_w_gate=moe_w_gate,
             moe_w_up=moe_w_up, moe_w_down=moe_w_down)
    depth = w_in.shape[0]
    lb_soft = jax.nn.softmax(hgrn_lb_logits.astype(F32), axis=0)
    hgrn_lb = jnp.cumsum(lb_soft, axis=0) - lb_soft[0]

    b_p, t_p = x_prompt.shape[:2]
    b_s, t_s = x_sample.shape[:2]
    past_len = cache_mla_ckv.shape[2]
    tab_all = _rope_tables(max(t_p, past_len + t_s))
    tabs_p = tuple(tb[:t_p] for tb in tab_all)
    tabs_s = tuple(tb[past_len:past_len + t_s] for tb in tab_all)

    xp = x_prompt.reshape(b_p * t_p, D_MODEL)
    xs = x_sample.reshape(b_s * t_s, D_MODEL)
    past_p = dict(ckv=None, kr=None,
                  gdn=jnp.zeros((b_p, GDN_HEADS, GDN_DK, GDN_DV), F32),
                  conv=jnp.zeros((b_p, CONV_W - 1, 3 * GDN_QK), F32),
                  hgrn=jnp.zeros((b_p, HGRN_HEADS, HGRN_DK, HGRN_DV), F32))
    st_p, st_s = [], []
    for l in range(depth):
        f = _layer_weights(l, a)
        lb = hgrn_lb[l][None]
        past_s = dict(ckv=cache_mla_ckv[l], kr=cache_mla_krope[l], gdn=state_gdn[l],
                      conv=state_gdn_conv[l], hgrn=state_hgrn[l])
        xp, sp = _trunk_layer(xp, b_p, t_p, f, lb, tabs_p, past_p)
        xs, ss = _trunk_layer(xs, b_s, t_s, f, lb, tabs_s, past_s)
        st_p.append(sp)
        st_s.append(ss)

    def stack(lst, i):
        return jnp.stack([s[i] for s in lst], axis=0)

    return (xp.reshape(b_p, t_p, D_MODEL), xs.reshape(b_s, t_s, D_MODEL),
            stack(st_p, 0), stack(st_p, 1), stack(st_p, 2), stack(st_p, 3), stack(st_p, 4),
            stack(st_s, 0), stack(st_s, 1), stack(st_s, 2), stack(st_s, 3), stack(st_s, 4))
```

```python
import functools

import jax
import jax.numpy as jnp
import numpy as np
from jax import lax
from jax.experimental import pallas as pl
from jax.experimental.pallas import tpu as pltpu

F32 = jnp.float32
BF16 = jnp.bfloat16

D_MODEL = 1024
CHUNK = 64
NORM_EPS = 1e-6

MLA_HEADS = 8
MLA_NOPE = 64
MLA_ROPE = 32
MLA_V = 64
MLA_Q_LORA = 384
MLA_KV_LORA = 256
MLA_QK_HEAD = MLA_NOPE + MLA_ROPE
MLA_VW = MLA_HEADS * MLA_V
ROPE_THETA = 10000.0
LOG2_E = 1.4426950408889634
HEAD_PAD = 128
QK_W = MLA_HEADS * HEAD_PAD
VT_ROWS = MLA_V + 16

GDN_HEADS = 4
GDN_DK = 128
GDN_DV = 128
GDN_QK = GDN_HEADS * GDN_DK
GDN_VW = GDN_HEADS * GDN_DV
CONV_W = 4
HGRN_PREP_ROWS = 512
GDN_PREP_ROWS = 1024

HGRN_HEADS = 4
HGRN_DK = 128
HGRN_DV = 128
HGRN_KW = HGRN_HEADS * HGRN_DK

N_BRANCH = 3
SPLIT_SIZES = (MLA_Q_LORA, MLA_KV_LORA, MLA_ROPE,
               GDN_QK, GDN_QK, GDN_VW, GDN_VW, GDN_HEADS, GDN_HEADS,
               HGRN_KW, HGRN_KW, HGRN_KW, HGRN_KW,
               N_BRANCH * D_MODEL)

N_EXPERTS = 8
FF_EXPERT = 1408
MOE_TILE = 1024
MOE_BLOCK_ROWS = 288

P_COLS = 8192
P_MAIN = P_COLS - N_BRANCH * D_MODEL
P_TN = 512
COL_CKV = MLA_Q_LORA
COL_KR = MLA_Q_LORA + MLA_KV_LORA
COL_GAB = COL_KR + 128
COL_GDN = 1024
ROPE_LANE0 = MLA_NOPE

VMEM_LIMIT = 56 * 1024 * 1024


def _cparams(sem):
    return pltpu.CompilerParams(dimension_semantics=sem, vmem_limit_bytes=VMEM_LIMIT)


def _row_tile(n, cap):
    for t in (2048, 1024, 512, 256, 128, 64, 32, 16, 8):
        if t <= cap and n % t == 0:
            return t
    raise ValueError(f"no row tile for {n}")


def _sigmoid(x):
    return 1.0 / (1.0 + jnp.exp(-x))


def _silu(x):
    return x * (0.5 * jnp.tanh(0.5 * x) + 0.5)


def _rms(x, g):
    ms = jnp.mean(x * x, axis=-1, keepdims=True)
    return x * lax.rsqrt(ms + NORM_EPS) * g


def _in_proj_kernel(x_ref, g_ref, w_ref, o_ref, gate_ref, h_ref):
    j = pl.program_id(1)

    @pl.when(j == 0)
    def _():
        h_ref[...] = _rms(x_ref[...], g_ref[...]).astype(BF16)

    @pl.when(j < P_MAIN // P_TN)
    def _():
        o_ref[...] = jnp.dot(h_ref[...], w_ref[...], preferred_element_type=F32)

    @pl.when(j >= P_MAIN // P_TN)
    def _():
        gate_ref[...] = jnp.dot(h_ref[...], w_ref[...], preferred_element_type=F32).astype(BF16)


def _in_proj(x, g, w):
    n = x.shape[0]
    tm = _row_tile(n, 2048)
    n_main = P_MAIN // P_TN
    return pl.pallas_call(
        _in_proj_kernel,
        grid=(n // tm, P_COLS // P_TN),
        in_specs=[pl.BlockSpec((tm, D_MODEL), lambda i, j: (i, 0)),
                  pl.BlockSpec((1, D_MODEL), lambda i, j: (0, 0)),
                  pl.BlockSpec((D_MODEL, P_TN), lambda i, j: (0, j))],
        out_specs=[pl.BlockSpec((tm, P_TN), lambda i, j: (i, jnp.minimum(j, n_main - 1))),
                   pl.BlockSpec((tm, P_TN), lambda i, j: (i, jnp.maximum(j - n_main, 0)))],
        out_shape=[jax.ShapeDtypeStruct((n, P_MAIN), F32),
                   jax.ShapeDtypeStruct((n, P_COLS - P_MAIN), BF16)],
        scratch_shapes=[pltpu.VMEM((tm, D_MODEL), BF16)],
        compiler_params=_cparams(("parallel", "arbitrary")),
        name="in_proj",
    )(x, g, w)


def _rope(x, c, s1, s2):
    return x * c + pltpu.roll(x, HEAD_PAD - 16, 1) * s1 + pltpu.roll(x, 16, 1) * s2


def _mla_pre_kernel(p_ref, gq_ref, gkv_ref, wq_ref, hq_ref, hk_ref, c_ref, s1_ref, s2_ref, *rest,
                    scale, transposed):
    if transposed:
        hqt_ref, ct_ref, s1t_ref, s2t_ref, q_ref, ckv_ref, kr_ref = rest
    else:
        q_ref, ckv_ref, kr_ref = rest
    c, s1, s2 = c_ref[...], s1_ref[...], s2_ref[...]
    ckv_ref[...] = _rms(p_ref[:, COL_CKV:COL_KR], gkv_ref[...])

    kr = p_ref[:, COL_KR:COL_KR + HEAD_PAD]
    kr_ms = jnp.sum(kr * kr, axis=-1, keepdims=True) * (1.0 / MLA_ROPE)
    kr_ref[...] = _rope(kr * lax.rsqrt(kr_ms + NORM_EPS) * hk_ref[...], c, s1, s2)

    cq = _rms(p_ref[:, 0:MLA_Q_LORA], gq_ref[...]).astype(BF16)
    q = jnp.dot(cq, wq_ref[...], preferred_element_type=F32)
    if transposed:
        row = lax.broadcasted_iota(jnp.int32, (HEAD_PAD, 1), 0)
        hqt, ct, s1t, s2t = hqt_ref[...], ct_ref[...], s1t_ref[...], s2t_ref[...]
        for h in range(MLA_HEADS):
            qt = q[:, h * HEAD_PAD:(h + 1) * HEAD_PAD].T
            sq = qt * qt
            ms_n = jnp.sum(sq[0:MLA_NOPE], axis=0, keepdims=True) * (1.0 / MLA_NOPE)
            ms_r = jnp.sum(sq[MLA_NOPE:MLA_QK_HEAD], axis=0, keepdims=True) * (1.0 / MLA_ROPE)
            inv = jnp.where(row < MLA_NOPE, lax.rsqrt(ms_n + NORM_EPS), lax.rsqrt(ms_r + NORM_EPS))
            x = qt * inv * hqt
            x = x * ct + pltpu.roll(x, HEAD_PAD - 16, 0) * s1t + pltpu.roll(x, 16, 0) * s2t
            q_ref[0, h] = x.astype(BF16)
    else:
        lane = lax.broadcasted_iota(jnp.int32, (1, HEAD_PAD), 1)
        is_nope = lane < MLA_NOPE
        hq = hq_ref[...]
        for h in range(MLA_HEADS):
            qh = q[:, h * HEAD_PAD:(h + 1) * HEAD_PAD]
            sq = qh * qh
            ms_n = jnp.sum(jnp.where(is_nope, sq, 0.0), axis=-1, keepdims=True) * (1.0 / MLA_NOPE)
            ms_r = jnp.sum(jnp.where(is_nope, 0.0, sq), axis=-1, keepdims=True) * (1.0 / MLA_ROPE)
            inv = jnp.where(is_nope, lax.rsqrt(ms_n + NORM_EPS), lax.rsqrt(ms_r + NORM_EPS))
            qh = _rope(qh * inv * hq, c, s1, s2) * scale
            q_ref[:, h * HEAD_PAD:(h + 1) * HEAD_PAD] = qh.astype(BF16)


def _mla_pre(p, b, t_seq, gq, gkv, wq, hq, hk, tabs, transposed):
    n = p.shape[0]
    tm = _row_tile(n, 512)
    c, s1, s2 = tabs
    if tm > t_seq:
        c, s1, s2 = (jnp.tile(t, (tm // t_seq, 1)) for t in (c, s1, s2))
    n_tab = c.shape[0] // tm
    tab_spec = pl.BlockSpec((tm, HEAD_PAD), lambda i: (i % n_tab, 0))
    vec = lambda w: pl.BlockSpec((1, w), lambda i: (0, 0))
    scale = MLA_QK_HEAD ** -0.5
    extra, extra_specs = (), []
    if transposed:
        scale *= LOG2_E
        q_spec = pl.BlockSpec((1, MLA_HEADS, HEAD_PAD, tm), lambda i: (i // n_tab, 0, 0, i % n_tab))
        q_shape = jax.ShapeDtypeStruct((b, MLA_HEADS, HEAD_PAD, t_seq), BF16)
        hqt = jnp.broadcast_to((hq[0] * scale)[:, None], (HEAD_PAD, tm))
        extra = (hqt, c.T, s1.T, s2.T)
        tab_t_spec = pl.BlockSpec((HEAD_PAD, tm), lambda i: (0, i % n_tab))
        extra_specs = [pl.BlockSpec((HEAD_PAD, tm), lambda i: (0, 0)), tab_t_spec, tab_t_spec, tab_t_spec]
    else:
        q_spec = pl.BlockSpec((tm, QK_W), lambda i: (i, 0))
        q_shape = jax.ShapeDtypeStruct((n, QK_W), BF16)
    return pl.pallas_call(
        functools.partial(_mla_pre_kernel, scale=scale, transposed=transposed),
        grid=(n // tm,),
        in_specs=[pl.BlockSpec((tm, 1024), lambda i: (i, 0)),
                  vec(MLA_Q_LORA), vec(MLA_KV_LORA),
                  pl.BlockSpec((MLA_Q_LORA, QK_W), lambda i: (0, 0)),
                  vec(HEAD_PAD), vec(HEAD_PAD), tab_spec, tab_spec, tab_spec, *extra_specs],
        out_specs=[q_spec,
                   pl.BlockSpec((tm, MLA_KV_LORA), lambda i: (i, 0)),
                   pl.BlockSpec((tm, HEAD_PAD), lambda i: (i, 0))],
        out_shape=[q_shape,
                   jax.ShapeDtypeStruct((n, MLA_KV_LORA), F32),
                   jax.ShapeDtypeStruct((n, HEAD_PAD), F32)],
        compiler_params=_cparams(("parallel",)),
        name="mla_pre",
    )(p, gq, gkv, wq, hq, hk, c, s1, s2, *extra)


def _kv_up_kernel(ckv_ref, kr_ref, wk_ref, wv_ref, hk_ref, k_ref, v_ref, *, transposed):
    c = ckv_ref[...].astype(BF16)
    k = jnp.dot(c, wk_ref[...], preferred_element_type=F32)
    kr = kr_ref[...]
    hk = hk_ref[...]
    for h in range(MLA_HEADS):
        kh = k[:, h * HEAD_PAD:(h + 1) * HEAD_PAD]
        ms = jnp.sum(kh * kh, axis=-1, keepdims=True) * (1.0 / MLA_NOPE)
        k_ref[:, h * HEAD_PAD:(h + 1) * HEAD_PAD] = (kh * lax.rsqrt(ms + NORM_EPS) * hk + kr).astype(BF16)
    v = jnp.dot(c, wv_ref[...], preferred_element_type=F32)
    if transposed:
        lane = lax.broadcasted_iota(jnp.int32, (1, HEAD_PAD), 1)
        for h in range(MLA_HEADS):
            vh = jnp.where(lane < MLA_V, v[:, h * HEAD_PAD:(h + 1) * HEAD_PAD], 1.0)
            v_ref[0, h] = vh.T[0:VT_ROWS].astype(BF16)
    else:
        v_ref[...] = v.astype(BF16)


def _kv_up(ckv, kr, b, s_len, wk, wv, hk, transposed):
    n = ckv.shape[0]
    tm = _row_tile(n, 512)
    if transposed:
        n_t = s_len // tm
        v_spec = pl.BlockSpec((1, MLA_HEADS, VT_ROWS, tm), lambda i: (i // n_t, 0, 0, i % n_t))
        v_shape = jax.ShapeDtypeStruct((b, MLA_HEADS, VT_ROWS, s_len), BF16)
    else:
        v_spec = pl.BlockSpec((tm, MLA_VW), lambda i: (i, 0))
        v_shape = jax.ShapeDtypeStruct((n, MLA_VW), BF16)
    return pl.pallas_call(
        functools.partial(_kv_up_kernel, transposed=transposed),
        grid=(n // tm,),
        in_specs=[pl.BlockSpec((tm, MLA_KV_LORA), lambda i: (i, 0)),
                  pl.BlockSpec((tm, HEAD_PAD), lambda i: (i, 0)),
                  pl.BlockSpec((MLA_KV_LORA, QK_W), lambda i: (0, 0)),
                  pl.BlockSpec((MLA_KV_LORA, wv.shape[1]), lambda i: (0, 0)),
                  pl.BlockSpec((1, HEAD_PAD), lambda i: (0, 0))],
        out_specs=[pl.BlockSpec((tm, QK_W), lambda i: (i, 0)), v_spec],
        out_shape=[jax.ShapeDtypeStruct((n, QK_W), BF16), v_shape],
        compiler_params=_cparams(("parallel",)),
        name="kv_up",
    )(ckv, kr, wk, wv, hk)


def _attn_kernel(q_ref, k_ref, v_ref, o_ref, m_ref, l_ref, acc_ref, *, tq, tk, causal):
    m_ref[...] = jnp.full(m_ref.shape, -jnp.inf, F32)
    l_ref[...] = jnp.zeros(l_ref.shape, F32)
    acc_ref[...] = jnp.zeros(acc_ref.shape, F32)

    def block(start, masked):
        if masked:
            rq = lax.broadcasted_iota(jnp.int32, (tq, tk), 0) // CHUNK
            ck = lax.broadcasted_iota(jnp.int32, (tq, tk), 1) // CHUNK
            allowed = ck <= rq
        def scores(h):
            qh = q_ref[0, :, h * HEAD_PAD:(h + 1) * HEAD_PAD]
            kh = k_ref[0, pl.ds(start, tk), h * HEAD_PAD:(h + 1) * HEAD_PAD]
            return lax.dot_general(qh, kh, (((1,), (1,)), ((), ())), preferred_element_type=F32)

        all_scores = [scores(h) for h in range(MLA_HEADS)]
        for h in range(MLA_HEADS):
            vh = v_ref[0, pl.ds(start, tk), h * MLA_V:(h + 1) * MLA_V]
            s = all_scores[h]
            if masked:
                s = jnp.where(allowed, s, -jnp.inf)
            m_prev = m_ref[h]
            m_new = jnp.maximum(m_prev, jnp.max(s, axis=-1, keepdims=True))
            alpha = jnp.exp(m_prev - m_new)
            p = jnp.exp(s - m_new)
            l_ref[h] = alpha * l_ref[h] + jnp.sum(p, axis=-1, keepdims=True)
            pv = jnp.dot(p.astype(BF16), vh, preferred_element_type=F32)
            acc_ref[h] = alpha * acc_ref[h] + pv
            m_ref[h] = m_new

    if causal:
        qi = pl.program_id(1)

        def body(j, carry):
            block(pl.multiple_of(j * tk, tk), False)
            return carry

        lax.fori_loop(0, qi, body, 0)
        block(pl.multiple_of(qi * tk, tk), True)
    else:
        block(0, False)

    for h in range(MLA_HEADS):
        o_ref[0, :, h * MLA_V:(h + 1) * MLA_V] = acc_ref[h] / l_ref[h]


def _attention(q, k, v, causal):
    b, t, _ = q.shape
    s = k.shape[1]
    if causal:
        tq = tk = min(256, t)
    else:
        tq, tk = t, s
    return pl.pallas_call(
        functools.partial(_attn_kernel, tq=tq, tk=tk, causal=causal),
        grid=(b, t // tq),
        in_specs=[pl.BlockSpec((1, tq, QK_W), lambda i, j: (i, j, 0)),
                  pl.BlockSpec((1, s, QK_W), lambda i, j: (i, 0, 0)),
                  pl.BlockSpec((1, s, MLA_VW), lambda i, j: (i, 0, 0))],
        out_specs=pl.BlockSpec((1, tq, MLA_VW), lambda i, j: (i, j, 0)),
        out_shape=jax.ShapeDtypeStruct((b, t, MLA_VW), F32),
        scratch_shapes=[pltpu.VMEM((MLA_HEADS, tq, 1), F32),
                        pltpu.VMEM((MLA_HEADS, tq, 1), F32),
                        pltpu.VMEM((MLA_HEADS, tq, MLA_V), F32)],
        compiler_params=_cparams(("parallel", "arbitrary")),
        name="mla_attn",
    )(q, k, v)


def _attn_t_kernel(qt_ref, k_ref, vt_ref, o_ref, m_ref, acc_ref, *, tq, tk):
    m_ref[...] = jnp.full(m_ref.shape, -jnp.inf, F32)
    acc_ref[...] = jnp.zeros(acc_ref.shape, F32)

    def blocks(starts, masked):
        if masked:
            ck = lax.broadcasted_iota(jnp.int32, (tk, tq), 0) // CHUNK
            cq = lax.broadcasted_iota(jnp.int32, (tk, tq), 1) // CHUNK
            allowed = ck <= cq

        def scores(start, h):
            kh = k_ref[0, pl.ds(start, tk), h * HEAD_PAD:(h + 1) * HEAD_PAD]
            return jnp.dot(kh, qt_ref[0, h], preferred_element_type=F32)

        def update(start, h, s):
            if masked:
                s = jnp.where(allowed, s, -jnp.inf)
            m_prev = m_ref[h]
            m_new = jnp.maximum(m_prev, jnp.max(s, axis=0, keepdims=True))
            alpha = jnp.exp2(m_prev - m_new)
            p = jnp.exp2(s - m_new).astype(BF16)
            pv = jnp.dot(vt_ref[0, h, :, pl.ds(start, tk)], p, preferred_element_type=F32)
            acc_ref[h] = alpha * acc_ref[h] + pv
            m_ref[h] = m_new

        items = [(start, h) for start in starts for h in range(MLA_HEADS)]
        ahead = 5
        pending = [scores(*it) for it in items[:ahead]]
        for n, it in enumerate(items):
            s = pending.pop(0)
            if n + ahead < len(items):
                pending.append(scores(*items[n + ahead]))
            update(*it, s)

    qi = pl.program_id(1)

    def body(j, carry):
        first = pl.multiple_of(2 * j * tk, tk)
        blocks([first, pl.multiple_of(first + tk, tk)], False)
        return carry

    lax.fori_loop(0, qi // 2, body, 0)

    @pl.when(qi % 2 == 1)
    def _():
        blocks([pl.multiple_of((qi - 1) * tk, tk)], False)

    blocks([pl.multiple_of(qi * tk, tk)], True)

    ot = jnp.concatenate([acc_ref[h, 0:MLA_V] / acc_ref[h, MLA_V:MLA_V + 1] for h in range(MLA_HEADS)], axis=0)
    o_ref[0] = ot.T


def _attention_t(qt, k, vt):
    b, _, _, t = qt.shape
    s = k.shape[1]
    tq = tk = min(256, t)
    return pl.pallas_call(
        functools.partial(_attn_t_kernel, tq=tq, tk=tk),
        grid=(b, t // tq),
        in_specs=[pl.BlockSpec((1, MLA_HEADS, HEAD_PAD, tq), lambda i, j: (i, 0, 0, j)),
                  pl.BlockSpec((1, s, QK_W), lambda i, j: (i, 0, 0)),
                  pl.BlockSpec((1, MLA_HEADS, VT_ROWS, s), lambda i, j: (i, 0, 0, 0))],
        out_specs=pl.BlockSpec((1, tq, MLA_VW), lambda i, j: (i, j, 0)),
        out_shape=jax.ShapeDtypeStruct((b, t, MLA_VW), F32),
        scratch_shapes=[pltpu.VMEM((MLA_HEADS, 1, tq), F32),
                        pltpu.VMEM((MLA_HEADS, VT_ROWS, tq), F32)],
        compiler_params=_cparams(("parallel", "arbitrary")),
        name="mla_attn_t",
    )(qt, k, vt)


def _row_iota(shape):
    return lax.broadcasted_iota(jnp.int32, shape, 0)


def _upper_half_masks(shape):
    row = _row_iota(shape)
    masks = []
    m = 1
    while m < CHUNK:
        masks.append((row // m) % 2 == 1)
        m *= 2
    return masks


def _segment_scans(g, uppers):
    rows = g.shape[0]
    pre, tot = g, g
    out = [(pre, tot)]
    m = 1
    while m < CHUNK:
        upper = uppers[len(out) - 1]
        from_lower = pltpu.roll(tot, m, 0)
        from_upper = pltpu.roll(tot, rows - m, 0)
        pre = pre + jnp.where(upper, from_lower, 0.0)
        tot = tot + jnp.where(upper, from_lower, from_upper)
        out.append((pre, tot))
        m *= 2
    return out


def _dot_nt(a, b):
    return lax.dot_general(a, b, (((1,), (1,)), ((), ())), preferred_element_type=F32)


def _dot_tn(a, b):
    return lax.dot_general(a, b, (((0,), (0,)), ((), ())), preferred_element_type=F32)


def _gdn_prep_kernel(q_ref, k_ref, v_ref, ab_ref, cw_ref, cs_ref, alog_ref, dt_ref,
                     u_ref, w_ref, qd_ref, kd_ref, qk_ref, gl_ref, carry_ref, *, nb, r):
    @pl.when(pl.program_id(1) == 0)
    def _():
        carry_ref[...] = cs_ref[...]

    n = nb * r
    nc = n // CHUNK

    def conv(x_ref, j):
        w = cw_ref[:, j * GDN_QK:(j + 1) * GDN_QK]
        outs = []
        for b in range(nb):
            x = x_ref[b]
            xp = jnp.concatenate([carry_ref[b, j], x], axis=0)
            y = x * w[3:4]
            for d in range(1, CONV_W):
                y = y + xp[8 - d:8 - d + r] * w[3 - d:4 - d]
            carry_ref[b, j] = x[r - 8:]
            outs.append(_silu(y))
        return outs[0] if nb == 1 else jnp.concatenate(outs, axis=0)

    q_all, k_all, v_all = conv(q_ref, 0), conv(k_ref, 1), conv(v_ref, 2)

    ab = ab_ref[...].reshape(n, 128)
    x = ab + dt_ref[...]
    softplus = jnp.maximum(x, 0.0) + jnp.log(1.0 + jnp.exp(-jnp.abs(x)))
    g_blk = (-LOG2_E) * jnp.exp(alog_ref[...]) * softplus
    gam_blk = _segment_scans(g_blk, _upper_half_masks(g_blk.shape))[-1][0]
    gam_t = gam_blk.T
    gam3_blk = gam_blk.reshape(nc, CHUNK, 128)
    beta3_blk = _sigmoid(ab).reshape(nc, CHUNK, 128)

    row = lax.broadcasted_iota(jnp.int32, (1, CHUNK, CHUNK), 1)
    col = lax.broadcasted_iota(jnp.int32, (1, CHUNK, CHUNK), 2)
    eye = (row == col).astype(F32)

    def bmm(a, b):
        return jnp.einsum('cij,cjk->cik', a.astype(BF16), b.astype(BF16), preferred_element_type=F32)

    def bmm_nt(a, b):
        return jnp.einsum('cid,cjd->cij', a.astype(BF16), b.astype(BF16), preferred_element_type=F32)

    for h in range(GDN_HEADS):
        sl = slice(h * GDN_DK, (h + 1) * GDN_DK)
        q, k, v = q_all[:, sl], k_all[:, sl], v_all[:, sl]
        q = q * lax.rsqrt(jnp.sum(q * q, axis=-1, keepdims=True) + NORM_EPS) * (GDN_DK ** -0.5)
        k = k * lax.rsqrt(jnp.sum(k * k, axis=-1, keepdims=True) + NORM_EPS)
        q, k, v = (a.reshape(nc, CHUNK, GDN_DK) for a in (q, k, v))
        gam = gam3_blk[:, :, h:h + 1]
        beta = beta3_blk[:, :, GDN_HEADS + h:GDN_HEADS + h + 1]
        gam_row = jnp.stack([gam_t[h:h + 1, c * CHUNK:(c + 1) * CHUNK] for c in range(nc)], axis=0)
        decay = jnp.where(row >= col, jnp.exp2(jnp.minimum(gam - gam_row, 0.0)), 0.0)
        a = jnp.where(row > col, beta * bmm_nt(k, k) * decay, 0.0)
        t_inv = eye - a
        pw = a
        m = 1
        while 2 * m < CHUNK:
            pw = bmm(pw, pw)
            t_inv = t_inv + bmm(t_inv, pw)
            m *= 2
        e_gam = jnp.exp2(gam)
        gam_last = gam[:, CHUNK - 1:CHUNK, :]
        u_ref[:, h] = bmm(t_inv, v * beta).reshape(nb, r, GDN_DV)
        w_ref[:, h] = bmm(t_inv, k * (beta * e_gam)).astype(BF16).reshape(nb, r, GDN_DK)
        qd_ref[:, h] = (q * e_gam).astype(BF16).reshape(nb, r, GDN_DK)
        kd_ref[:, h] = (k * jnp.exp2(gam_last - gam)).astype(BF16).reshape(nb, r, GDN_DK)
        qk_ref[:, h] = (bmm_nt(q, k) * decay).astype(BF16).reshape(nb, r, CHUNK)
        gl_ref[:, h] = jnp.broadcast_to(jnp.exp2(gam_last), (nc, 1, 128)).reshape(nb, r // CHUNK, 1, 128)


def _gdn_scan_kernel(u_ref, w_ref, qd_ref, kd_ref, qk_ref, gl_ref, s0_ref, o_ref, sf_ref, s_ref, *, nb):
    c_idx = pl.program_id(1)

    @pl.when(c_idx == 0)
    def _():
        s_ref[...] = s0_ref[...]

    chains = [(b, h) for b in range(nb) for h in range(GDN_HEADS)]
    s_old = [s_ref[b, h] for b, h in chains]
    s_bf = [s.astype(BF16) for s in s_old]
    v_new = [u_ref[b, h] - jnp.dot(w_ref[b, h], sb, preferred_element_type=F32)
             for (b, h), sb in zip(chains, s_bf)]
    v_bf = [v.astype(BF16) for v in v_new]
    for (b, h), s, sb, vb in zip(chains, s_old, s_bf, v_bf):
        o = (jnp.dot(qd_ref[b, h], sb, preferred_element_type=F32)
             + jnp.dot(qk_ref[b, h], vb, preferred_element_type=F32))
        o_ref[b, :, h * GDN_DV:(h + 1) * GDN_DV] = o
        s_ref[b, h] = gl_ref[b, h, 0] * s + _dot_tn(kd_ref[b, h], vb)

    @pl.when(c_idx == pl.num_programs(1) - 1)
    def _():
        sf_ref[...] = s_ref[...]


def _gdn(p3, conv_w, conv_state8, s0, alog, dt):
    b, t, _ = p3.shape
    r = min(GDN_PREP_ROWS, t)
    nb = max(1, min(b, GDN_PREP_ROWS // r))
    nt = t // r
    nc = t // CHUNK
    blk = lambda j: pl.BlockSpec((nb, r, GDN_QK), lambda i, c: (i, c, COL_GDN // GDN_QK + j))
    vec = pl.BlockSpec((1, 128), lambda i, c: (0, 0))
    head_spec = lambda w: pl.BlockSpec((nb, GDN_HEADS, r, w), lambda i, c: (i, 0, c, 0))
    head_shape = lambda w, dt_: jax.ShapeDtypeStruct((b, GDN_HEADS, t, w), dt_)
    u, w, qd, kd, qk, gl = pl.pallas_call(
        functools.partial(_gdn_prep_kernel, nb=nb, r=r),
        grid=(b // nb, nt),
        in_specs=[blk(0), blk(1), blk(2),
                  pl.BlockSpec((nb, r, 128), lambda i, c: (i, c, COL_GAB // 128)),
                  pl.BlockSpec((CONV_W, 3 * GDN_QK), lambda i, c: (0, 0)),
                  pl.BlockSpec((nb, 3, 8, GDN_QK), lambda i, c: (i, 0, 0, 0)),
                  vec, vec],
        out_specs=[head_spec(GDN_DV), head_spec(GDN_DK), head_spec(GDN_DK), head_spec(GDN_DK),
                   head_spec(CHUNK),
                   pl.BlockSpec((nb, GDN_HEADS, r // CHUNK, 1, 128), lambda i, c: (i, 0, c, 0, 0))],
        out_shape=[head_shape(GDN_DV, F32), head_shape(GDN_DK, BF16), head_shape(GDN_DK, BF16),
                   head_shape(GDN_DK, BF16), head_shape(CHUNK, BF16),
                   jax.ShapeDtypeStruct((b, GDN_HEADS, nc, 1, 128), F32)],
        scratch_shapes=[pltpu.VMEM((nb, 3, 8, GDN_QK), F32)],
        compiler_params=_cparams(("parallel", "arbitrary")),
        name="gdn_prep",
    )(p3, p3, p3, p3, conv_w, conv_state8, alog, dt)

    sb = min(b, 8)
    chunk_spec = lambda w: pl.BlockSpec((sb, GDN_HEADS, CHUNK, w), lambda i, c: (i, 0, c, 0))
    state_spec = pl.BlockSpec((sb, GDN_HEADS, GDN_DK, GDN_DV), lambda i, c: (i, 0, 0, 0))
    return pl.pallas_call(
        functools.partial(_gdn_scan_kernel, nb=sb),
        grid=(b // sb, nc),
        in_specs=[chunk_spec(GDN_DV), chunk_spec(GDN_DK), chunk_spec(GDN_DK), chunk_spec(GDN_DK),
                  chunk_spec(CHUNK),
                  pl.BlockSpec((sb, GDN_HEADS, 1, 1, 128), lambda i, c: (i, 0, c, 0, 0)),
                  state_spec],
        out_specs=[pl.BlockSpec((sb, CHUNK, GDN_VW), lambda i, c: (i, c, 0)), state_spec],
        out_shape=[jax.ShapeDtypeStruct((b, t, GDN_VW), F32),
                   jax.ShapeDtypeStruct((b, GDN_HEADS, GDN_DK, GDN_DV), F32)],
        scratch_shapes=[pltpu.VMEM((sb, GDN_HEADS, GDN_DK, GDN_DV), F32)],
        compiler_params=_cparams(("parallel", "arbitrary")),
        name="gdn_scan",
    )(u, w, qd, kd, qk, gl, s0)


def _hgrn_prep_kernel(q_ref, f_ref, lb_ref, att_ref, qe_ref, ke_ref, dec_ref, *, nb, r):
    n = nb * r
    nc = n // CHUNK
    row = lax.broadcasted_iota(jnp.int32, (1, CHUNK, CHUNK), 1)
    col = lax.broadcasted_iota(jnp.int32, (1, CHUNK, CHUNK), 2)
    uppers = _upper_half_masks((n, HGRN_DK))
    halves = [1 << lvl for lvl in range(len(uppers))]
    pairs = [(row // (2 * m) == col // (2 * m)) & ((row // m) % 2 == 1) & ((col // m) % 2 == 0) for m in halves]

    def bmm_nt(a, b):
        a3, b3 = (x.astype(BF16).reshape(nc, CHUNK, HGRN_DK) for x in (a, b))
        return jnp.einsum('cid,cjd->cij', a3, b3, preferred_element_type=F32)

    for h in range(HGRN_HEADS):
        sl = slice(h * HGRN_DK, (h + 1) * HGRN_DK)
        lb = lb_ref[:, sl]
        f = lb + (1.0 - lb) * _sigmoid(f_ref[:, :, sl].reshape(n, HGRN_DK))
        q = _silu(q_ref[:, :, sl].reshape(n, HGRN_DK)) * (HGRN_DK ** -0.5)
        k = 1.0 - f
        scans = _segment_scans(jnp.log2(f), uppers)
        cb, c_tot = scans[-1]

        att = jnp.where(row == col, bmm_nt(q, k), 0.0)
        for lvl in range(len(halves)):
            pre_m, tot_m = scans[lvl]
            att = att + jnp.where(pairs[lvl], bmm_nt(q * jnp.exp2(pre_m), k * jnp.exp2(tot_m - pre_m)), 0.0)

        att_ref[:, h] = att.astype(BF16).reshape(nb, r, CHUNK)
        qe_ref[:, h] = (q * jnp.exp2(cb)).astype(BF16).reshape(nb, r, HGRN_DK)
        ke_ref[:, h] = (k * jnp.exp2(c_tot - cb)).astype(BF16).reshape(nb, r, HGRN_DK)
        dec_ref[:, h] = jnp.exp2(c_tot).reshape(nc, CHUNK, HGRN_DK)[:, 0:1, :].reshape(nb, r // CHUNK, 1, HGRN_DK)


def _hgrn_scan_kernel(att_ref, qe_ref, ke_ref, dec_ref, v_ref, s0_ref, o_ref, sf_ref, st_ref, *, nb):
    c_idx = pl.program_id(1)

    @pl.when(c_idx == 0)
    def _():
        st_ref[...] = s0_ref[...]

    chains = [(b, h) for b in range(nb) for h in range(HGRN_HEADS)]
    sls = [slice(h * HGRN_DV, (h + 1) * HGRN_DV) for h in range(HGRN_HEADS)]
    st_old = [st_ref[b, h] for b, h in chains]
    vs = [v_ref[b, :, sls[h]].astype(BF16) for b, h in chains]
    o_st = [_dot_nt(qe_ref[b, h], st.astype(BF16)) for (b, h), st in zip(chains, st_old)]
    for (b, h), st, v, o1 in zip(chains, st_old, vs, o_st):
        o_ref[b, :, sls[h]] = o1 + jnp.dot(att_ref[b, h], v, preferred_element_type=F32)
        st_ref[b, h] = st * dec_ref[b, h, 0] + _dot_tn(v, ke_ref[b, h])

    @pl.when(c_idx == pl.num_programs(1) - 1)
    def _():
        sf_ref[...] = st_ref[...]


def _hgrn(p3, lb, s0t):
    b, t, _ = p3.shape
    r = min(HGRN_PREP_ROWS, t)
    nb = max(1, min(b, HGRN_PREP_ROWS // r))
    nc = t // CHUNK
    col0 = (COL_GDN + 4 * GDN_QK) // HGRN_KW
    blk = lambda j: pl.BlockSpec((nb, r, HGRN_KW), lambda i, c: (i, c, col0 + j))
    head_spec = lambda w: pl.BlockSpec((nb, HGRN_HEADS, r, w), lambda i, c: (i, 0, c, 0))
    head_shape = lambda w: jax.ShapeDtypeStruct((b, HGRN_HEADS, t, w), BF16)
    att, qe, ke, dec = pl.pallas_call(
        functools.partial(_hgrn_prep_kernel, nb=nb, r=r),
        grid=(b // nb, t // r),
        in_specs=[blk(0), blk(1), pl.BlockSpec((1, HGRN_KW), lambda i, c: (0, 0))],
        out_specs=[head_spec(CHUNK), head_spec(HGRN_DK), head_spec(HGRN_DK),
                   pl.BlockSpec((nb, HGRN_HEADS, r // CHUNK, 1, HGRN_DK), lambda i, c: (i, 0, c, 0, 0))],
        out_shape=[head_shape(CHUNK), head_shape(HGRN_DK), head_shape(HGRN_DK),
                   jax.ShapeDtypeStruct((b, HGRN_HEADS, nc, 1, HGRN_DK), F32)],
        compiler_params=_cparams(("parallel", "parallel")),
        name="hgrn_prep",
    )(p3, p3, lb)

    sb = min(b, 8)
    chunk_spec = lambda w: pl.BlockSpec((sb, HGRN_HEADS, CHUNK, w), lambda i, c: (i, 0, c, 0))
    state_spec = pl.BlockSpec((sb, HGRN_HEADS, HGRN_DV, HGRN_DK), lambda i, c: (i, 0, 0, 0))
    return pl.pallas_call(
        functools.partial(_hgrn_scan_kernel, nb=sb),
        grid=(b // sb, nc),
        in_specs=[chunk_spec(CHUNK), chunk_spec(HGRN_DK), chunk_spec(HGRN_DK),
                  pl.BlockSpec((sb, HGRN_HEADS, 1, 1, HGRN_DK), lambda i, c: (i, 0, c, 0, 0)),
                  pl.BlockSpec((sb, CHUNK, HGRN_KW), lambda i, c: (i, c, col0 + 2)),
                  state_spec],
        out_specs=[pl.BlockSpec((sb, CHUNK, HGRN_KW), lambda i, c: (i, c, 0)), state_spec],
        out_shape=[jax.ShapeDtypeStruct((b, t, HGRN_KW), F32),
                   jax.ShapeDtypeStruct((b, HGRN_HEADS, HGRN_DV, HGRN_DK), F32)],
        scratch_shapes=[pltpu.VMEM((sb, HGRN_HEADS, HGRN_DV, HGRN_DK), F32)],
        compiler_params=_cparams(("parallel", "arbitrary")),
        name="hgrn_scan",
    )(att, qe, ke, dec, p3, s0t)


def _merge_kernel(x_ref, oa_ref, ob_ref, zb_ref, oc_ref, zc_ref, g0_ref, g1_ref, g2_ref, ngb_ref, ngc_ref,
                  wa_ref, wb_ref, wc_ref, wo_ref, y_ref):
    def branch(o, w_ref, g_ref):
        gate = _sigmoid(g_ref[...].astype(F32))
        return gate * jnp.dot(o.astype(BF16), w_ref[...], preferred_element_type=F32)

    def normed(o_ref, z_ref, ng_ref):
        return jnp.concatenate(
            [_rms(o_ref[:, h * 128:(h + 1) * 128], ng_ref[...]) * _silu(z_ref[:, h * 128:(h + 1) * 128])
             for h in range(GDN_HEADS)], axis=-1)

    mixed = (branch(oa_ref[...], wa_ref, g0_ref) + branch(normed(ob_ref, zb_ref, ngb_ref), wb_ref, g1_ref)
             + branch(normed(oc_ref, zc_ref, ngc_ref), wc_ref, g2_ref))
    y_ref[...] = x_ref[...] + jnp.dot(mixed.astype(BF16), wo_ref[...], preferred_element_type=F32)


def _merge(x, oa, ob, oc, p, pg, ngb, ngc, wa, wb, wc, wo):
    n = x.shape[0]
    tm = _row_tile(n, 512)
    row = lambda w: pl.BlockSpec((tm, w), lambda i: (i, 0))
    gate = lambda j: pl.BlockSpec((tm, D_MODEL), lambda i: (i, j))
    zb_spec = pl.BlockSpec((tm, GDN_VW), lambda i: (i, COL_GDN // GDN_VW + 3))
    zc_spec = pl.BlockSpec((tm, HGRN_KW), lambda i: (i, (COL_GDN + 4 * GDN_QK) // HGRN_KW + 3))
    vec = pl.BlockSpec((1, 128), lambda i: (0, 0))
    wsp = lambda k: pl.BlockSpec((k, D_MODEL), lambda i: (0, 0))
    return pl.pallas_call(
        _merge_kernel,
        grid=(n // tm,),
        in_specs=[row(D_MODEL), row(MLA_VW), row(GDN_VW), zb_spec, row(HGRN_KW), zc_spec,
                  gate(0), gate(1), gate(2), vec, vec,
                  wsp(MLA_VW), wsp(GDN_VW), wsp(HGRN_KW), wsp(D_MODEL)],
        out_specs=row(D_MODEL),
        out_shape=jax.ShapeDtypeStruct((n, D_MODEL), F32),
        compiler_params=_cparams(("parallel",)),
        name="merge_out",
    )(x, oa, ob, p, oc, p, pg, pg, pg, ngb, ngc, wa, wb, wc, wo)


def _router_kernel(x_ref, g_ref, w_ref, b_ref, cw_ref, cwt_ref, cnt_ref):
    h = _rms(x_ref[...], g_ref[...])
    logits = jnp.dot(h, w_ref[...], preferred_element_type=F32, precision=lax.Precision.HIGHEST) + b_ref[...]
    lane = lax.broadcasted_iota(jnp.int32, logits.shape, 1)
    valid = lane < N_EXPERTS
    neg = -jnp.inf
    l1 = jnp.where(valid, logits, neg)
    m1 = jnp.max(l1, axis=-1, keepdims=True)
    i1 = jnp.min(jnp.where(l1 == m1, lane, 128), axis=-1, keepdims=True)
    l2 = jnp.where(lane == i1, neg, l1)
    m2 = jnp.max(l2, axis=-1, keepdims=True)
    i2 = jnp.min(jnp.where(l2 == m2, lane, 128), axis=-1, keepdims=True)
    e2 = jnp.exp(m2 - m1)
    den = 1.0 + e2
    cw = jnp.where(lane == i1, 1.0 / den, 0.0) + jnp.where(lane == i2, e2 / den, 0.0)
    cw_ref[...] = cw
    cwt_ref[...] = cw.T
    cnt_ref[0] = jnp.sum((cw > 0.0).astype(F32), axis=0, keepdims=True)


def _router(x, g, w, b, tm):
    n = x.shape[0]
    return pl.pallas_call(
        _router_kernel,
        grid=(n // tm,),
        in_specs=[pl.BlockSpec((tm, D_MODEL), lambda i: (i, 0)),
                  pl.BlockSpec((1, D_MODEL), lambda i: (0, 0)),
                  pl.BlockSpec((D_MODEL, 128), lambda i: (0, 0)),
                  pl.BlockSpec((1, 128), lambda i: (0, 0))],
        out_specs=[pl.BlockSpec((tm, 128), lambda i: (i, 0)),
                   pl.BlockSpec((128, tm), lambda i: (0, i)),
                   pl.BlockSpec((1, 1, 128), lambda i: (i, 0, 0))],
        out_shape=[jax.ShapeDtypeStruct((n, 128), F32),
                   jax.ShapeDtypeStruct((128, n), F32),
                   jax.ShapeDtypeStruct((n // tm, 1, 128), F32)],
        compiler_params=_cparams(("parallel",)),
        name="moe_router",
    )(x, g, w, b)


def _moe_kernel(cnt_ref, x_ref, g_ref, cw_ref, cwt_ref, wg_ref, wu_ref, wd_ref, y_ref,
                h_ref, rcol_ref, rrow_ref, *, t, br):
    i, e = pl.program_id(0), pl.program_id(1)

    @pl.when(e == 0)
    def _():
        x = x_ref[...]
        h_ref[...] = _rms(x, g_ref[...]).astype(BF16)
        y_ref[...] = x
        r = lax.broadcasted_iota(jnp.int32, (t, t), 0)
        c = lax.broadcasted_iota(jnp.int32, (t, t), 1)
        on = cw_ref[...] > 0.0
        rank = jnp.dot((c < r).astype(BF16), on.astype(BF16), preferred_element_type=F32)
        rcol_ref[...] = jnp.where(on, rank.astype(jnp.int32), -1)
        on_t = cwt_ref[...] > 0.0
        rank_t = jnp.dot(on_t.astype(BF16), (r < c).astype(BF16), preferred_element_type=F32)
        rrow_ref[...] = jnp.where(on_t, rank_t.astype(jnp.int32), -1)

    count = cnt_ref[i * N_EXPERTS + e]
    n_blocks = lax.div(count + (br - 1), br)
    sel = lax.broadcasted_iota(jnp.int32, (1, 128), 1) == e
    rank_c = jnp.sum(jnp.where(sel, rcol_ref[...], 0), axis=-1, keepdims=True)
    rank_r = rrow_ref[pl.ds(e, 1), :]
    w_r = cwt_ref[pl.ds(e, 1), :]

    def body(j, carry):
        base = j * br
        rows = lax.broadcasted_iota(jnp.int32, (br, t), 0) + base
        pick = rows == rank_r
        xc = jnp.dot(pick.astype(BF16), h_ref[...], preferred_element_type=F32).astype(BF16)
        a = jnp.dot(xc, wg_ref[0], preferred_element_type=F32)
        b = jnp.dot(xc, wu_ref[0], preferred_element_type=F32)
        yc = jnp.dot((_silu(a) * b).astype(BF16), wd_ref[0], preferred_element_type=F32)
        w_rows = jnp.sum(jnp.where(pick, w_r, 0.0), axis=-1, keepdims=True)
        cols = lax.broadcasted_iota(jnp.int32, (t, br), 1) + base
        y_ref[...] += jnp.dot((cols == rank_c).astype(BF16), (yc * w_rows).astype(BF16),
                              preferred_element_type=F32)
        return carry

    lax.fori_loop(0, n_blocks, body, 0)


def _moe(x, g, cw, cwt, counts, wg, wu, wd, t):
    n = x.shape[0]
    ne, _, ff = wg.shape
    br = MOE_BLOCK_ROWS
    grid_spec = pltpu.PrefetchScalarGridSpec(
        num_scalar_prefetch=1,
        grid=(n // t, ne),
        in_specs=[pl.BlockSpec((t, D_MODEL), lambda i, e, cnt: (i, 0)),
                  pl.BlockSpec((1, D_MODEL), lambda i, e, cnt: (0, 0)),
                  pl.BlockSpec((t, 128), lambda i, e, cnt: (i, 0)),
                  pl.BlockSpec((128, t), lambda i, e, cnt: (0, i)),
                  pl.BlockSpec((1, D_MODEL, ff), lambda i, e, cnt: (e, 0, 0)),
                  pl.BlockSpec((1, D_MODEL, ff), lambda i, e, cnt: (e, 0, 0)),
                  pl.BlockSpec((1, ff, D_MODEL), lambda i, e, cnt: (e, 0, 0))],
        out_specs=pl.BlockSpec((t, D_MODEL), lambda i, e, cnt: (i, 0)),
        scratch_shapes=[pltpu.VMEM((t, D_MODEL), BF16),
                        pltpu.VMEM((t, 128), jnp.int32),
                        pltpu.VMEM((128, t), jnp.int32)])
    return pl.pallas_call(
        functools.partial(_moe_kernel, t=t, br=br),
        grid_spec=grid_spec,
        out_shape=jax.ShapeDtypeStruct((n, D_MODEL), F32),
        compiler_params=_cparams(("parallel", "arbitrary")),
        name="moe",
    )(counts, x, g, cw, cwt, wg, wu, wd)


def _ffn_kernel(x_ref, g_ref, wg_ref, wu_ref, wd_ref, y_ref, h_ref):
    e = pl.program_id(1)

    @pl.when(e == 0)
    def _():
        x = x_ref[...]
        h_ref[...] = _rms(x, g_ref[...]).astype(BF16)
        y_ref[...] = x

    h = h_ref[...]
    a = jnp.dot(h, wg_ref[0], preferred_element_type=F32)
    b = jnp.dot(h, wu_ref[0], preferred_element_type=F32)
    y_ref[...] += jnp.dot((_silu(a) * b).astype(BF16), wd_ref[0], preferred_element_type=F32)


def _ffn(x, g, wg, wu, wd):
    n = x.shape[0]
    ne, _, ff = wg.shape
    tm = _row_tile(n, 512)
    return pl.pallas_call(
        _ffn_kernel,
        grid=(n // tm, ne),
        in_specs=[pl.BlockSpec((tm, D_MODEL), lambda i, e: (i, 0)),
                  pl.BlockSpec((1, D_MODEL), lambda i, e: (0, 0)),
                  pl.BlockSpec((1, D_MODEL, ff), lambda i, e: (e, 0, 0)),
                  pl.BlockSpec((1, D_MODEL, ff), lambda i, e: (e, 0, 0)),
                  pl.BlockSpec((1, ff, D_MODEL), lambda i, e: (e, 0, 0))],
        out_specs=pl.BlockSpec((tm, D_MODEL), lambda i, e: (i, 0)),
        out_shape=jax.ShapeDtypeStruct((n, D_MODEL), F32),
        scratch_shapes=[pltpu.VMEM((tm, D_MODEL), BF16)],
        compiler_params=_cparams(("parallel", "arbitrary")),
        name="ffn",
    )(x, g, wg, wu, wd)


def _pad_lanes(x, left, total):
    return jnp.pad(x, [(0, 0)] * (x.ndim - 1) + [(left, total - left - x.shape[-1])])


def _pack_w_in(w):
    cq, ckv, kr, gq, gk, gv, gz, ga, gb, hq, hf, hi, hg, gates = jnp.split(
        w, np.cumsum(SPLIT_SIZES)[:-1].tolist(), axis=-1)
    kr_blk = _pad_lanes(kr, ROPE_LANE0, 128)
    ab_blk = _pad_lanes(jnp.concatenate([ga, gb], axis=-1), 0, 256)
    return jnp.concatenate([cq, ckv, kr_blk, ab_blk, gq, gk, gv, gz, hq, hf, hi, hg, gates], axis=-1).astype(BF16)


def _rope_tables(n_pos):
    half = MLA_ROPE // 2
    inv = ROPE_THETA ** (-jnp.arange(half, dtype=F32) / half)
    ang = jnp.arange(n_pos, dtype=F32)[:, None] * inv[None, :]
    cos, sin = jnp.cos(ang), jnp.sin(ang)
    one = jnp.ones((n_pos, MLA_NOPE), F32)
    zero = jnp.zeros((n_pos, MLA_NOPE), F32)
    tail = jnp.zeros((n_pos, HEAD_PAD - MLA_QK_HEAD), F32)
    z16 = jnp.zeros((n_pos, half), F32)
    c = jnp.concatenate([one, cos, cos, tail], axis=-1)
    s1 = jnp.concatenate([zero, -sin, z16, tail], axis=-1)
    s2 = jnp.concatenate([zero, z16, sin, tail], axis=-1)
    return c, s1, s2


def _layer_weights(l, a):
    f = {}
    f['mixer_g'] = a['mixer_norm_g'][l][None]
    f['w_in'] = _pack_w_in(a['w_in'][l])
    f['gq'] = a['mla_q_norm_g'][l][None]
    f['gkv'] = a['mla_kv_norm_g'][l][None]
    wq = a['mla_w_q_up'][l].reshape(MLA_Q_LORA, MLA_HEADS, MLA_QK_HEAD)
    f['wq'] = _pad_lanes(wq, 0, HEAD_PAD).reshape(MLA_Q_LORA, QK_W).astype(BF16)
    wkv = a['mla_w_kv_up'][l].reshape(MLA_KV_LORA, MLA_HEADS, MLA_NOPE + MLA_V)
    f['wk'] = _pad_lanes(wkv[:, :, :MLA_NOPE], 0, HEAD_PAD).reshape(MLA_KV_LORA, QK_W).astype(BF16)
    f['wv'] = wkv[:, :, MLA_NOPE:].reshape(MLA_KV_LORA, MLA_VW).astype(BF16)
    f['wv_pad'] = _pad_lanes(wkv[:, :, MLA_NOPE:], 0, HEAD_PAD).reshape(MLA_KV_LORA, QK_W).astype(BF16)
    f['hq'] = _pad_lanes(a['mla_q_head_norm_g'][l][None], 0, HEAD_PAD)
    hk = a['mla_k_head_norm_g'][l][None]
    f['hk_nope'] = _pad_lanes(hk[:, :MLA_NOPE], 0, HEAD_PAD)
    f['hk_rope'] = _pad_lanes(hk[:, MLA_NOPE:], ROPE_LANE0, HEAD_PAD)
    f['wo_a'] = a['mla_w_o'][l].astype(BF16)
    f['conv_w'] = a['gdn_conv_w'][l]
    f['alog'] = _pad_lanes(a['gdn_a_log'][l][None], 0, 128)
    f['dt'] = _pad_lanes(a['gdn_dt_bias'][l][None], 0, 128)
    f['gdn_g'] = a['gdn_norm_g'][l][None]
    f['wo_b'] = a['gdn_w_o'][l].astype(BF16)
    f['hgrn_g'] = a['hgrn_norm_g'][l][None]
    f['wo_c'] = a['hgrn_w_o'][l].astype(BF16)
    f['w_out'] = a['w_out'][l].astype(BF16)
    f['ffn_g'] = a['ffn_norm_g'][l][None]
    if l % 2 == 0:
        wg, wu, wd = a['dense_w_gate'][l // 2], a['dense_w_up'][l // 2], a['dense_w_down'][l // 2]
        ff = wg.shape[1]
        half = ff // 2
        f['ffn'] = (jnp.moveaxis(wg.reshape(D_MODEL, 2, half), 1, 0).astype(BF16),
                    jnp.moveaxis(wu.reshape(D_MODEL, 2, half), 1, 0).astype(BF16),
                    wd.reshape(2, half, D_MODEL).astype(BF16))
        f['router'] = None
    else:
        f['ffn'] = (a['moe_w_gate'][l // 2].astype(BF16), a['moe_w_up'][l // 2].astype(BF16),
                    a['moe_w_down'][l // 2].astype(BF16))
        f['router'] = (_pad_lanes(a['moe_w_router'][l // 2], 0, 128),
                       _pad_lanes(a['moe_b_router'][l // 2][None], 0, 128))
    return f


def _trunk_layer(x, b, t, f, lb, tabs, past):
    n = b * t
    p, p_gate = _in_proj(x, f['mixer_g'], f['w_in'])
    p3 = p.reshape(b, t, P_MAIN)

    fresh = past['ckv'] is None
    q, ckv, kr = _mla_pre(p, b, t, f['gq'], f['gkv'], f['wq'], f['hq'], f['hk_rope'], tabs, transposed=fresh)
    if fresh:
        k_all, vt_all = _kv_up(ckv, kr, b, t, f['wk'], f['wv_pad'], f['hk_nope'], transposed=True)
        o_a = _attention_t(q, k_all.reshape(b, t, QK_W), vt_all)
    else:
        s = past['ckv'].shape[1] + t
        ckv_all = jnp.concatenate([past['ckv'], ckv.reshape(b, t, -1)], axis=1).reshape(b * s, -1)
        kr_past = _pad_lanes(past['kr'], ROPE_LANE0, HEAD_PAD)
        kr_all = jnp.concatenate([kr_past, kr.reshape(b, t, -1)], axis=1).reshape(b * s, -1)
        k_all, v_all = _kv_up(ckv_all, kr_all, b, s, f['wk'], f['wv'], f['hk_nope'], transposed=False)
        o_a = _attention(q.reshape(b, t, QK_W), k_all.reshape(b, s, QK_W), v_all.reshape(b, s, MLA_VW),
                         causal=False)

    conv8 = jnp.pad(past['conv'].reshape(b, CONV_W - 1, 3, GDN_QK).transpose(0, 2, 1, 3),
                    ((0, 0), (0, 0), (8 - (CONV_W - 1), 0), (0, 0)))
    o_b, gdn_s = _gdn(p3, f['conv_w'], conv8, past['gdn'], f['alog'], f['dt'])
    gdn_conv = p3[:, t - (CONV_W - 1):, COL_GDN:COL_GDN + 3 * GDN_QK]

    o_c, hgrn_st = _hgrn(p3, lb, jnp.swapaxes(past['hgrn'], -1, -2))
    hgrn_s = jnp.swapaxes(hgrn_st, -1, -2)

    x = _merge(x, o_a.reshape(n, -1), o_b.reshape(n, -1), o_c.reshape(n, -1), p, p_gate, f['gdn_g'],
               f['hgrn_g'], f['wo_a'], f['wo_b'], f['wo_c'], f['w_out'])

    wg, wu, wd = f['ffn']
    if f['router'] is None:
        x = _ffn(x, f['ffn_g'], wg, wu, wd)
    else:
        t_moe = _row_tile(n, MOE_TILE)
        cw, cwt, cnt = _router(x, f['ffn_g'], *f['router'], t_moe)
        counts = cnt[:, 0, :N_EXPERTS].astype(jnp.int32).reshape(-1)
        x = _moe(x, f['ffn_g'], cw, cwt, counts, wg, wu, wd, t_moe)

    new_ckv = ckv.reshape(b, t, MLA_KV_LORA)
    new_kr = kr.reshape(b, t, HEAD_PAD)[:, :, ROPE_LANE0:ROPE_LANE0 + MLA_ROPE]
    return x, (new_ckv, new_kr, gdn_s, gdn_conv, hgrn_s)


def kernel(x_prompt, x_sample, cache_mla_ckv, cache_mla_krope, state_gdn, state_gdn_conv, state_hgrn,
           mixer_norm_g, w_in, mla_q_norm_g, mla_w_q_up, mla_kv_norm_g, mla_w_kv_up,
           mla_q_head_norm_g, mla_k_head_norm_g, mla_w_o,
           gdn_conv_w, gdn_a_log, gdn_dt_bias, gdn_norm_g, gdn_w_o,
           hgrn_lb_logits, hgrn_norm_g, hgrn_w_o, w_out, ffn_norm_g,
           dense_w_gate, dense_w_up, dense_w_down,
           moe_w_router, moe_b_router, moe_w_gate, moe_w_up, moe_w_down):
    a = dict(mixer_norm_g=mixer_norm_g, w_in=w_in, mla_q_norm_g=mla_q_norm_g, mla_w_q_up=mla_w_q_up,
             mla_kv_norm_g=mla_kv_norm_g, mla_w_kv_up=mla_w_kv_up, mla_q_head_norm_g=mla_q_head_norm_g,
             mla_k_head_norm_g=mla_k_head_norm_g, mla_w_o=mla_w_o, gdn_conv_w=gdn_conv_w,
             gdn_a_log=gdn_a_log, gdn_dt_bias=gdn_dt_bias, gdn_norm_g=gdn_norm_g, gdn_w_o=gdn_w_o,
             hgrn_norm_g=hgrn_norm_g, hgrn_w_o=hgrn_w_o, w_out=w_out, ffn_norm_g=ffn_norm_g,
             dense_w_gate=dense_w_gate, dense_w_up=dense_w_up, dense_w_down=dense_w_down,
             moe_w_router=moe_w_router, moe_b_router=moe_b_router, moe_w_gate=moe_w_gate,
             moe_w_up=moe_w_up, moe_w_down=moe_w_down)
    depth = w_in.shape[0]
    lb_soft = jax.nn.softmax(hgrn_lb_logits.astype(F32), axis=0)
    hgrn_lb = jnp.cumsum(lb_soft, axis=0) - lb_soft[0]

    b_p, t_p = x_prompt.shape[:2]
    b_s, t_s = x_sample.shape[:2]
    past_len = cache_mla_ckv.shape[2]
    tab_all = _rope_tables(max(t_p, past_len + t_s))
    tabs_p = tuple(tb[:t_p] for tb in tab_all)
    tabs_s = tuple(tb[past_len:past_len + t_s] for tb in tab_all)

    xp = x_prompt.reshape(b_p * t_p, D_MODEL)
    xs = x_sample.reshape(b_s * t_s, D_MODEL)
    past_p = dict(ckv=None, kr=None,
                  gdn=jnp.zeros((b_p, GDN_HEADS, GDN_DK, GDN_DV), F32),
                  conv=jnp.zeros((b_p, CONV_W - 1, 3 * GDN_QK), F32),
                  hgrn=jnp.zeros((b_p, HGRN_HEADS, HGRN_DK, HGRN_DV), F32))
    st_p, st_s = [], []
    for l in range(depth):
        f = _layer_weights(l, a)
        lb = hgrn_lb[l][None]
        past_s = dict(ckv=cache_mla_ckv[l], kr=cache_mla_krope[l], gdn=state_gdn[l],
                      conv=state_gdn_conv[l], hgrn=state_hgrn[l])
        xp, sp = _trunk_layer(xp, b_p, t_p, f, lb, tabs_p, past_p)
        xs, ss = _trunk_layer(xs, b_s, t_s, f, lb, tabs_s, past_s)
        st_p.append(sp)
        st_s.append(ss)

    def stack(lst, i):
        return jnp.stack([s[i] for s in lst], axis=0)

    return (xp.reshape(b_p, t_p, D_MODEL), xs.reshape(b_s, t_s, D_MODEL),
            stack(st_p, 0), stack(st_p, 1), stack(st_p, 2), stack(st_p, 3), stack(st_p, 4),
            stack(st_s, 0), stack(st_s, 1), stack(st_s, 2), stack(st_s, 3), stack(st_s, 4))
```

```python
import functools

import jax
import jax.numpy as jnp
import numpy as np
from jax import lax
from jax.experimental import pallas as pl
from jax.experimental.pallas import tpu as pltpu

F32 = jnp.float32
BF16 = jnp.bfloat16

D_MODEL = 1024
CHUNK = 64
NORM_EPS = 1e-6

MLA_HEADS = 8
MLA_NOPE = 64
MLA_ROPE = 32
MLA_V = 64
MLA_Q_LORA = 384
MLA_KV_LORA = 256
MLA_QK_HEAD = MLA_NOPE + MLA_ROPE
MLA_VW = MLA_HEADS * MLA_V
ROPE_THETA = 10000.0
LOG2_E = 1.4426950408889634
HEAD_PAD = 128
QK_W = MLA_HEADS * HEAD_PAD
VT_ROWS = MLA_V + 16

GDN_HEADS = 4
GDN_DK = 128
GDN_DV = 128
GDN_QK = GDN_HEADS * GDN_DK
GDN_VW = GDN_HEADS * GDN_DV
CONV_W = 4
HGRN_PREP_ROWS = 512
GDN_PREP_ROWS = 1024

HGRN_HEADS = 4
HGRN_DK = 128
HGRN_DV = 128
HGRN_KW = HGRN_HEADS * HGRN_DK

N_BRANCH = 3
SPLIT_SIZES = (MLA_Q_LORA, MLA_KV_LORA, MLA_ROPE,
               GDN_QK, GDN_QK, GDN_VW, GDN_VW, GDN_HEADS, GDN_HEADS,
               HGRN_KW, HGRN_KW, HGRN_KW, HGRN_KW,
               N_BRANCH * D_MODEL)

N_EXPERTS = 8
FF_EXPERT = 1408
MOE_TILE = 1024
MOE_BLOCK_ROWS = 288

P_COLS = 8192
P_MAIN = P_COLS - N_BRANCH * D_MODEL
P_TN = 512
COL_CKV = MLA_Q_LORA
COL_KR = MLA_Q_LORA + MLA_KV_LORA
COL_GAB = COL_KR + 128
COL_GDN = 1024
ROPE_LANE0 = MLA_NOPE

VMEM_LIMIT = 56 * 1024 * 1024


def _cparams(sem):
    return pltpu.CompilerParams(dimension_semantics=sem, vmem_limit_bytes=VMEM_LIMIT)


def _row_tile(n, cap):
    for t in (2048, 1024, 512, 256, 128, 64, 32, 16, 8):
        if t <= cap and n % t == 0:
            return t
    raise ValueError(f"no row tile for {n}")


def _sigmoid(x):
    return 1.0 / (1.0 + jnp.exp(-x))


def _silu(x):
    return x * (0.5 * jnp.tanh(0.5 * x) + 0.5)


def _rms(x, g):
    ms = jnp.mean(x * x, axis=-1, keepdims=True)
    return x * lax.rsqrt(ms + NORM_EPS) * g


def _in_proj_kernel(x_ref, g_ref, w_ref, o_ref, gate_ref, h_ref):
    j = pl.program_id(1)

    @pl.when(j == 0)
    def _():
        h_ref[...] = _rms(x_ref[...], g_ref[...]).astype(BF16)

    @pl.when(j < P_MAIN // P_TN)
    def _():
        o_ref[...] = jnp.dot(h_ref[...], w_ref[...], preferred_element_type=F32)

    @pl.when(j >= P_MAIN // P_TN)
    def _():
        gate_ref[...] = jnp.dot(h_ref[...], w_ref[...], preferred_element_type=F32).astype(BF16)


def _in_proj(x, g, w):
    n = x.shape[0]
    tm = _row_tile(n, 2048)
    n_main = P_MAIN // P_TN
    return pl.pallas_call(
        _in_proj_kernel,
        grid=(n // tm, P_COLS // P_TN),
        in_specs=[pl.BlockSpec((tm, D_MODEL), lambda i, j: (i, 0)),
                  pl.BlockSpec((1, D_MODEL), lambda i, j: (0, 0)),
                  pl.BlockSpec((D_MODEL, P_TN), lambda i, j: (0, j))],
        out_specs=[pl.BlockSpec((tm, P_TN), lambda i, j: (i, jnp.minimum(j, n_main - 1))),
                   pl.BlockSpec((tm, P_TN), lambda i, j: (i, jnp.maximum(j - n_main, 0)))],
        out_shape=[jax.ShapeDtypeStruct((n, P_MAIN), F32),
                   jax.ShapeDtypeStruct((n, P_COLS - P_MAIN), BF16)],
        scratch_shapes=[pltpu.VMEM((tm, D_MODEL), BF16)],
        compiler_params=_cparams(("parallel", "arbitrary")),
        name="in_proj",
    )(x, g, w)


def _rope(x, c, s1, s2):
    return x * c + pltpu.roll(x, HEAD_PAD - 16, 1) * s1 + pltpu.roll(x, 16, 1) * s2


def _mla_pre_kernel(p_ref, gq_ref, gkv_ref, wq_ref, hq_ref, hk_ref, c_ref, s1_ref, s2_ref, *rest,
                    scale, transposed):
    if transposed:
        hqt_ref, ct_ref, s1t_ref, s2t_ref, q_ref, ckv_ref, kr_ref = rest
    else:
        q_ref, ckv_ref, kr_ref = rest
    c, s1, s2 = c_ref[...], s1_ref[...], s2_ref[...]
    ckv_ref[...] = _rms(p_ref[:, COL_CKV:COL_KR], gkv_ref[...])

    kr = p_ref[:, COL_KR:COL_KR + HEAD_PAD]
    kr_ms = jnp.sum(kr * kr, axis=-1, keepdims=True) * (1.0 / MLA_ROPE)
    kr_ref[...] = _rope(kr * lax.rsqrt(kr_ms + NORM_EPS) * hk_ref[...], c, s1, s2)

    cq = _rms(p_ref[:, 0:MLA_Q_LORA], gq_ref[...]).astype(BF16)
    q = jnp.dot(cq, wq_ref[...], preferred_element_type=F32)
    if transposed:
        row = lax.broadcasted_iota(jnp.int32, (HEAD_PAD, 1), 0)
        hqt, ct, s1t, s2t = hqt_ref[...], ct_ref[...], s1t_ref[...], s2t_ref[...]
        for h in range(MLA_HEADS):
            qt = q[:, h * HEAD_PAD:(h + 1) * HEAD_PAD].T
            sq = qt * qt
            ms_n = jnp.sum(sq[0:MLA_NOPE], axis=0, keepdims=True) * (1.0 / MLA_NOPE)
            ms_r = jnp.sum(sq[MLA_NOPE:MLA_QK_HEAD], axis=0, keepdims=True) * (1.0 / MLA_ROPE)
            inv = jnp.where(row < MLA_NOPE, lax.rsqrt(ms_n + NORM_EPS), lax.rsqrt(ms_r + NORM_EPS))
            x = qt * inv * hqt
            x = x * ct + pltpu.roll(x, HEAD_PAD - 16, 0) * s1t + pltpu.roll(x, 16, 0) * s2t
            q_ref[0, h] = x.astype(BF16)
    else:
        lane = lax.broadcasted_iota(jnp.int32, (1, HEAD_PAD), 1)
        is_nope = lane < MLA_NOPE
        hq = hq_ref[...]
        for h in range(MLA_HEADS):
            qh = q[:, h * HEAD_PAD:(h + 1) * HEAD_PAD]
            sq = qh * qh
            ms_n = jnp.sum(jnp.where(is_nope, sq, 0.0), axis=-1, keepdims=True) * (1.0 / MLA_NOPE)
            ms_r = jnp.sum(jnp.where(is_nope, 0.0, sq), axis=-1, keepdims=True) * (1.0 / MLA_ROPE)
            inv = jnp.where(is_nope, lax.rsqrt(ms_n + NORM_EPS), lax.rsqrt(ms_r + NORM_EPS))
            qh = _rope(qh * inv * hq, c, s1, s2) * scale
            q_ref[:, h * HEAD_PAD:(h + 1) * HEAD_PAD] = qh.astype(BF16)


def _mla_pre(p, b, t_seq, gq, gkv, wq, hq, hk, tabs, transposed):
    n = p.shape[0]
    tm = _row_tile(n, 512)
    c, s1, s2 = tabs
    if tm > t_seq:
        c, s1, s2 = (jnp.tile(t, (tm // t_seq, 1)) for t in (c, s1, s2))
    n_tab = c.shape[0] // tm
    tab_spec = pl.BlockSpec((tm, HEAD_PAD), lambda i: (i % n_tab, 0))
    vec = lambda w: pl.BlockSpec((1, w), lambda i: (0, 0))
    scale = MLA_QK_HEAD ** -0.5
    extra, extra_specs = (), []
    if transposed:
        scale *= LOG2_E
        q_spec = pl.BlockSpec((1, MLA_HEADS, HEAD_PAD, tm), lambda i: (i // n_tab, 0, 0, i % n_tab))
        q_shape = jax.ShapeDtypeStruct((b, MLA_HEADS, HEAD_PAD, t_seq), BF16)
        hqt = jnp.broadcast_to((hq[0] * scale)[:, None], (HEAD_PAD, tm))
        extra = (hqt, c.T, s1.T, s2.T)
        tab_t_spec = pl.BlockSpec((HEAD_PAD, tm), lambda i: (0, i % n_tab))
        extra_specs = [pl.BlockSpec((HEAD_PAD, tm), lambda i: (0, 0)), tab_t_spec, tab_t_spec, tab_t_spec]
    else:
        q_spec = pl.BlockSpec((tm, QK_W), lambda i: (i, 0))
        q_shape = jax.ShapeDtypeStruct((n, QK_W), BF16)
    return pl.pallas_call(
        functools.partial(_mla_pre_kernel, scale=scale, transposed=transposed),
        grid=(n // tm,),
        in_specs=[pl.BlockSpec((tm, 1024), lambda i: (i, 0)),
                  vec(MLA_Q_LORA), vec(MLA_KV_LORA),
                  pl.BlockSpec((MLA_Q_LORA, QK_W), lambda i: (0, 0)),
                  vec(HEAD_PAD), vec(HEAD_PAD), tab_spec, tab_spec, tab_spec, *extra_specs],
        out_specs=[q_spec,
                   pl.BlockSpec((tm, MLA_KV_LORA), lambda i: (i, 0)),
                   pl.BlockSpec((tm, HEAD_PAD), lambda i: (i, 0))],
        out_shape=[q_shape,
                   jax.ShapeDtypeStruct((n, MLA_KV_LORA), F32),
                   jax.ShapeDtypeStruct((n, HEAD_PAD), F32)],
        compiler_params=_cparams(("parallel",)),
        name="mla_pre",
    )(p, gq, gkv, wq, hq, hk, c, s1, s2, *extra)


def _kv_up_kernel(ckv_ref, kr_ref, wk_ref, wv_ref, hk_ref, k_ref, v_ref, *, transposed):
    c = ckv_ref[...].astype(BF16)
    k = jnp.dot(c, wk_ref[...], preferred_element_type=F32)
    kr = kr_ref[...]
    hk = hk_ref[...]
    for h in range(MLA_HEADS):
        kh = k[:, h * HEAD_PAD:(h + 1) * HEAD_PAD]
        ms = jnp.sum(kh * kh, axis=-1, keepdims=True) * (1.0 / MLA_NOPE)
        k_ref[:, h * HEAD_PAD:(h + 1) * HEAD_PAD] = (kh * lax.rsqrt(ms + NORM_EPS) * hk + kr).astype(BF16)
    v = jnp.dot(c, wv_ref[...], preferred_element_type=F32)
    if transposed:
        lane = lax.broadcasted_iota(jnp.int32, (1, HEAD_PAD), 1)
        for h in range(MLA_HEADS):
            vh = jnp.where(lane < MLA_V, v[:, h * HEAD_PAD:(h + 1) * HEAD_PAD], 1.0)
            v_ref[0, h] = vh.T[0:VT_ROWS].astype(BF16)
    else:
        v_ref[...] = v.astype(BF16)


def _kv_up(ckv, kr, b, s_len, wk, wv, hk, transposed):
    n = ckv.shape[0]
    tm = _row_tile(n, 512)
    if transposed:
        n_t = s_len // tm
        v_spec = pl.BlockSpec((1, MLA_HEADS, VT_ROWS, tm), lambda i: (i // n_t, 0, 0, i % n_t))
        v_shape = jax.ShapeDtypeStruct((b, MLA_HEADS, VT_ROWS, s_len), BF16)
    else:
        v_spec = pl.BlockSpec((tm, MLA_VW), lambda i: (i, 0))
        v_shape = jax.ShapeDtypeStruct((n, MLA_VW), BF16)
    return pl.pallas_call(
        functools.partial(_kv_up_kernel, transposed=transposed),
        grid=(n // tm,),
        in_specs=[pl.BlockSpec((tm, MLA_KV_LORA), lambda i: (i, 0)),
                  pl.BlockSpec((tm, HEAD_PAD), lambda i: (i, 0)),
                  pl.BlockSpec((MLA_KV_LORA, QK_W), lambda i: (0, 0)),
                  pl.BlockSpec((MLA_KV_LORA, wv.shape[1]), lambda i: (0, 0)),
                  pl.BlockSpec((1, HEAD_PAD), lambda i: (0, 0))],
        out_specs=[pl.BlockSpec((tm, QK_W), lambda i: (i, 0)), v_spec],
        out_shape=[jax.ShapeDtypeStruct((n, QK_W), BF16), v_shape],
        compiler_params=_cparams(("parallel",)),
        name="kv_up",
    )(ckv, kr, wk, wv, hk)


def _kv_up_rows_kernel(ckv_ref, kr_ref, wk_ref, wv_ref, hk_ref, *rest):
    k_ref, v_ref = rest[-2:]
    _kv_up_kernel(ckv_ref.at[0], kr_ref.at[0], wk_ref, wv_ref, hk_ref, k_ref.at[0], v_ref.at[0], transposed=False)


def _kv_up_rows(ckv, kr, s_total, row0, wk, wv, hk, into=None):
    b, s_in, _ = ckv.shape
    tm = _row_tile(s_in, 512)
    blk0 = row0 // tm
    in_specs = [pl.BlockSpec((1, tm, MLA_KV_LORA), lambda i, j: (i, j, 0)),
                pl.BlockSpec((1, tm, HEAD_PAD), lambda i, j: (i, j, 0)),
                pl.BlockSpec((MLA_KV_LORA, QK_W), lambda i, j: (0, 0)),
                pl.BlockSpec((MLA_KV_LORA, MLA_VW), lambda i, j: (0, 0)),
                pl.BlockSpec((1, HEAD_PAD), lambda i, j: (0, 0))]
    args = [ckv, kr, wk, wv, hk]
    aliases = {}
    if into is not None:
        in_specs += [pl.BlockSpec(memory_space=pl.ANY), pl.BlockSpec(memory_space=pl.ANY)]
        args += list(into)
        aliases = {5: 0, 6: 1}
    return pl.pallas_call(
        _kv_up_rows_kernel,
        grid=(b, s_in // tm),
        in_specs=in_specs,
        out_specs=[pl.BlockSpec((1, tm, QK_W), lambda i, j: (i, blk0 + j, 0)),
                   pl.BlockSpec((1, tm, MLA_VW), lambda i, j: (i, blk0 + j, 0))],
        out_shape=[jax.ShapeDtypeStruct((b, s_total, QK_W), BF16),
                   jax.ShapeDtypeStruct((b, s_total, MLA_VW), BF16)],
        input_output_aliases=aliases,
        compiler_params=_cparams(("parallel", "parallel")),
        name="kv_up_rows",
    )(*args)


def _attn_kernel(q_ref, k_ref, v_ref, o_ref, m_ref, l_ref, acc_ref, *, tq, tk, causal):
    m_ref[...] = jnp.full(m_ref.shape, -jnp.inf, F32)
    l_ref[...] = jnp.zeros(l_ref.shape, F32)
    acc_ref[...] = jnp.zeros(acc_ref.shape, F32)

    def block(start, masked):
        if masked:
            rq = lax.broadcasted_iota(jnp.int32, (tq, tk), 0) // CHUNK
            ck = lax.broadcasted_iota(jnp.int32, (tq, tk), 1) // CHUNK
            allowed = ck <= rq
        def scores(h):
            qh = q_ref[0, :, h * HEAD_PAD:(h + 1) * HEAD_PAD]
            kh = k_ref[0, pl.ds(start, tk), h * HEAD_PAD:(h + 1) * HEAD_PAD]
            return lax.dot_general(qh, kh, (((1,), (1,)), ((), ())), preferred_element_type=F32)

        all_scores = [scores(h) for h in range(MLA_HEADS)]
        for h in range(MLA_HEADS):
            vh = v_ref[0, pl.ds(start, tk), h * MLA_V:(h + 1) * MLA_V]
            s = all_scores[h]
            if masked:
                s = jnp.where(allowed, s, -jnp.inf)
            m_prev = m_ref[h]
            m_new = jnp.maximum(m_prev, jnp.max(s, axis=-1, keepdims=True))
            alpha = jnp.exp(m_prev - m_new)
            p = jnp.exp(s - m_new)
            l_ref[h] = alpha * l_ref[h] + jnp.sum(p, axis=-1, keepdims=True)
            pv = jnp.dot(p.astype(BF16), vh, preferred_element_type=F32)
            acc_ref[h] = alpha * acc_ref[h] + pv
            m_ref[h] = m_new

    if causal:
        qi = pl.program_id(1)

        def body(j, carry):
            block(pl.multiple_of(j * tk, tk), False)
            return carry

        lax.fori_loop(0, qi, body, 0)
        block(pl.multiple_of(qi * tk, tk), True)
    else:
        block(0, False)

    for h in range(MLA_HEADS):
        o_ref[0, :, h * MLA_V:(h + 1) * MLA_V] = acc_ref[h] / l_ref[h]


def _attention(q, k, v, causal):
    b, t, _ = q.shape
    s = k.shape[1]
    if causal:
        tq = tk = min(256, t)
    else:
        tq, tk = t, s
    return pl.pallas_call(
        functools.partial(_attn_kernel, tq=tq, tk=tk, causal=causal),
        grid=(b, t // tq),
        in_specs=[pl.BlockSpec((1, tq, QK_W), lambda i, j: (i, j, 0)),
                  pl.BlockSpec((1, s, QK_W), lambda i, j: (i, 0, 0)),
                  pl.BlockSpec((1, s, MLA_VW), lambda i, j: (i, 0, 0))],
        out_specs=pl.BlockSpec((1, tq, MLA_VW), lambda i, j: (i, j, 0)),
        out_shape=jax.ShapeDtypeStruct((b, t, MLA_VW), F32),
        scratch_shapes=[pltpu.VMEM((MLA_HEADS, tq, 1), F32),
                        pltpu.VMEM((MLA_HEADS, tq, 1), F32),
                        pltpu.VMEM((MLA_HEADS, tq, MLA_V), F32)],
        compiler_params=_cparams(("parallel", "arbitrary")),
        name="mla_attn",
    )(q, k, v)


def _attn_t_kernel(qt_ref, k_ref, vt_ref, o_ref, m_ref, acc_ref, *, tq, tk):
    m_ref[...] = jnp.full(m_ref.shape, -jnp.inf, F32)
    acc_ref[...] = jnp.zeros(acc_ref.shape, F32)

    def blocks(starts, masked):
        if masked:
            ck = lax.broadcasted_iota(jnp.int32, (tk, tq), 0) // CHUNK
            cq = lax.broadcasted_iota(jnp.int32, (tk, tq), 1) // CHUNK
            allowed = ck <= cq

        def scores(start, h):
            kh = k_ref[0, pl.ds(start, tk), h * HEAD_PAD:(h + 1) * HEAD_PAD]
            return jnp.dot(kh, qt_ref[0, h], preferred_element_type=F32)

        def update(start, h, s):
            if masked:
                s = jnp.where(allowed, s, -jnp.inf)
            m_prev = m_ref[h]
            m_new = jnp.maximum(m_prev, jnp.max(s, axis=0, keepdims=True))
            alpha = jnp.exp2(m_prev - m_new)
            p = jnp.exp2(s - m_new).astype(BF16)
            pv = jnp.dot(vt_ref[0, h, :, pl.ds(start, tk)], p, preferred_element_type=F32)
            acc_ref[h] = alpha * acc_ref[h] + pv
            m_ref[h] = m_new

        items = [(start, h) for start in starts for h in range(MLA_HEADS)]
        ahead = 5
        pending = [scores(*it) for it in items[:ahead]]
        for n, it in enumerate(items):
            s = pending.pop(0)
            if n + ahead < len(items):
                pending.append(scores(*items[n + ahead]))
            update(*it, s)

    qi = pl.program_id(1)

    def body(j, carry):
        first = pl.multiple_of(2 * j * tk, tk)
        blocks([first, pl.multiple_of(first + tk, tk)], False)
        return carry

    lax.fori_loop(0, qi // 2, body, 0)

    @pl.when(qi % 2 == 1)
    def _():
        blocks([pl.multiple_of((qi - 1) * tk, tk)], False)

    blocks([pl.multiple_of(qi * tk, tk)], True)

    ot = jnp.concatenate([acc_ref[h, 0:MLA_V] / acc_ref[h, MLA_V:MLA_V + 1] for h in range(MLA_HEADS)], axis=0)
    o_ref[0] = ot.T


def _attention_t(qt, k, vt):
    b, _, _, t = qt.shape
    s = k.shape[1]
    tq = tk = min(256, t)
    return pl.pallas_call(
        functools.partial(_attn_t_kernel, tq=tq, tk=tk),
        grid=(b, t // tq),
        in_specs=[pl.BlockSpec((1, MLA_HEADS, HEAD_PAD, tq), lambda i, j: (i, 0, 0, j)),
                  pl.BlockSpec((1, s, QK_W), lambda i, j: (i, 0, 0)),
                  pl.BlockSpec((1, MLA_HEADS, VT_ROWS, s), lambda i, j: (i, 0, 0, 0))],
        out_specs=pl.BlockSpec((1, tq, MLA_VW), lambda i, j: (i, j, 0)),
        out_shape=jax.ShapeDtypeStruct((b, t, MLA_VW), F32),
        scratch_shapes=[pltpu.VMEM((MLA_HEADS, 1, tq), F32),
                        pltpu.VMEM((MLA_HEADS, VT_ROWS, tq), F32)],
        compiler_params=_cparams(("parallel", "arbitrary")),
        name="mla_attn_t",
    )(qt, k, vt)


def _row_iota(shape):
    return lax.broadcasted_iota(jnp.int32, shape, 0)


def _upper_half_masks(shape):
    row = _row_iota(shape)
    masks = []
    m = 1
    while m < CHUNK:
        masks.append((row // m) % 2 == 1)
        m *= 2
    return masks


def _segment_scans(g, uppers):
    rows = g.shape[0]
    pre, tot = g, g
    out = [(pre, tot)]
    m = 1
    while m < CHUNK:
        upper = uppers[len(out) - 1]
        from_lower = pltpu.roll(tot, m, 0)
        from_upper = pltpu.roll(tot, rows - m, 0)
        pre = pre + jnp.where(upper, from_lower, 0.0)
        tot = tot + jnp.where(upper, from_lower, from_upper)
        out.append((pre, tot))
        m *= 2
    return out


def _dot_nt(a, b):
    return lax.dot_general(a, b, (((1,), (1,)), ((), ())), preferred_element_type=F32)


def _dot_tn(a, b):
    return lax.dot_general(a, b, (((0,), (0,)), ((), ())), preferred_element_type=F32)


def _gdn_prep_kernel(q_ref, k_ref, v_ref, ab_ref, cw_ref, cs_ref, alog_ref, dt_ref,
                     u_ref, w_ref, qd_ref, kd_ref, qk_ref, gl_ref, carry_ref, *, nb, r):
    @pl.when(pl.program_id(1) == 0)
    def _():
        carry_ref[...] = cs_ref[...]

    n = nb * r
    nc = n // CHUNK

    def conv(x_ref, j):
        w = cw_ref[:, j * GDN_QK:(j + 1) * GDN_QK]
        outs = []
        for b in range(nb):
            x = x_ref[b]
            xp = jnp.concatenate([carry_ref[b, j], x], axis=0)
            y = x * w[3:4]
            for d in range(1, CONV_W):
                y = y + xp[8 - d:8 - d + r] * w[3 - d:4 - d]
            carry_ref[b, j] = x[r - 8:]
            outs.append(_silu(y))
        return outs[0] if nb == 1 else jnp.concatenate(outs, axis=0)

    q_all, k_all, v_all = conv(q_ref, 0), conv(k_ref, 1), conv(v_ref, 2)

    ab = ab_ref[...].reshape(n, 128)
    x = ab + dt_ref[...]
    softplus = jnp.maximum(x, 0.0) + jnp.log(1.0 + jnp.exp(-jnp.abs(x)))
    g_blk = (-LOG2_E) * jnp.exp(alog_ref[...]) * softplus
    gam_blk = _segment_scans(g_blk, _upper_half_masks(g_blk.shape))[-1][0]
    gam_t = gam_blk.T
    gam3_blk = gam_blk.reshape(nc, CHUNK, 128)
    beta3_blk = _sigmoid(ab).reshape(nc, CHUNK, 128)

    row = lax.broadcasted_iota(jnp.int32, (1, CHUNK, CHUNK), 1)
    col = lax.broadcasted_iota(jnp.int32, (1, CHUNK, CHUNK), 2)
    eye = (row == col).astype(F32)

    def bmm(a, b):
        return jnp.einsum('cij,cjk->cik', a.astype(BF16), b.astype(BF16), preferred_element_type=F32)

    def bmm_nt(a, b):
        return jnp.einsum('cid,cjd->cij', a.astype(BF16), b.astype(BF16), preferred_element_type=F32)

    for h in range(GDN_HEADS):
        sl = slice(h * GDN_DK, (h + 1) * GDN_DK)
        q, k, v = q_all[:, sl], k_all[:, sl], v_all[:, sl]
        q = q * lax.rsqrt(jnp.sum(q * q, axis=-1, keepdims=True) + NORM_EPS) * (GDN_DK ** -0.5)
        k = k * lax.rsqrt(jnp.sum(k * k, axis=-1, keepdims=True) + NORM_EPS)
        q, k, v = (a.reshape(nc, CHUNK, GDN_DK) for a in (q, k, v))
        gam = gam3_blk[:, :, h:h + 1]
        beta = beta3_blk[:, :, GDN_HEADS + h:GDN_HEADS + h + 1]
        gam_row = jnp.stack([gam_t[h:h + 1, c * CHUNK:(c + 1) * CHUNK] for c in range(nc)], axis=0)
        decay = jnp.where(row >= col, jnp.exp2(jnp.minimum(gam - gam_row, 0.0)), 0.0)
        a = jnp.where(row > col, beta * bmm_nt(k, k) * decay, 0.0)
        t_inv = eye - a
        pw = a
        m = 1
        while 2 * m < CHUNK:
            pw = bmm(pw, pw)
            t_inv = t_inv + bmm(t_inv, pw)
            m *= 2
        e_gam = jnp.exp2(gam)
        gam_last = gam[:, CHUNK - 1:CHUNK, :]
        u_ref[:, h] = bmm(t_inv, v * beta).reshape(nb, r, GDN_DV)
        w_ref[:, h] = bmm(t_inv, k * (beta * e_gam)).astype(BF16).reshape(nb, r, GDN_DK)
        qd_ref[:, h] = (q * e_gam).astype(BF16).reshape(nb, r, GDN_DK)
        kd_ref[:, h] = (k * jnp.exp2(gam_last - gam)).astype(BF16).reshape(nb, r, GDN_DK)
        qk_ref[:, h] = (bmm_nt(q, k) * decay).astype(BF16).reshape(nb, r, CHUNK)
        gl_ref[:, h] = jnp.broadcast_to(jnp.exp2(gam_last), (nc, 1, 128)).reshape(nb, r // CHUNK, 1, 128)


def _gdn_scan_kernel(u_ref, w_ref, qd_ref, kd_ref, qk_ref, gl_ref, s0_ref, o_ref, sf_ref, s_ref, *, nb):
    c_idx = pl.program_id(1)

    @pl.when(c_idx == 0)
    def _():
        s_ref[...] = s0_ref[...]

    chains = [(b, h) for b in range(nb) for h in range(GDN_HEADS)]
    s_old = [s_ref[b, h] for b, h in chains]
    s_bf = [s.astype(BF16) for s in s_old]
    v_new = [u_ref[b, h] - jnp.dot(w_ref[b, h], sb, preferred_element_type=F32)
             for (b, h), sb in zip(chains, s_bf)]
    v_bf = [v.astype(BF16) for v in v_new]
    for (b, h), s, sb, vb in zip(chains, s_old, s_bf, v_bf):
        o = (jnp.dot(qd_ref[b, h], sb, preferred_element_type=F32)
             + jnp.dot(qk_ref[b, h], vb, preferred_element_type=F32))
        o_ref[b, :, h * GDN_DV:(h + 1) * GDN_DV] = o
        s_ref[b, h] = gl_ref[b, h, 0] * s + _dot_tn(kd_ref[b, h], vb)

    @pl.when(c_idx == pl.num_programs(1) - 1)
    def _():
        sf_ref[...] = s_ref[...]


def _gdn_prep(p3, conv_w, conv_state8, alog, dt):
    b, t, _ = p3.shape
    r = min(GDN_PREP_ROWS, t)
    nb = max(1, min(b, GDN_PREP_ROWS // r))
    nt = t // r
    nc = t // CHUNK
    blk = lambda j: pl.BlockSpec((nb, r, GDN_QK), lambda i, c: (i, c, COL_GDN // GDN_QK + j))
    vec = pl.BlockSpec((1, 128), lambda i, c: (0, 0))
    head_spec = lambda w: pl.BlockSpec((nb, GDN_HEADS, r, w), lambda i, c: (i, 0, c, 0))
    head_shape = lambda w, dt_: jax.ShapeDtypeStruct((b, GDN_HEADS, t, w), dt_)
    u, w, qd, kd, qk, gl = pl.pallas_call(
        functools.partial(_gdn_prep_kernel, nb=nb, r=r),
        grid=(b // nb, nt),
        in_specs=[blk(0), blk(1), blk(2),
                  pl.BlockSpec((nb, r, 128), lambda i, c: (i, c, COL_GAB // 128)),
                  pl.BlockSpec((CONV_W, 3 * GDN_QK), lambda i, c: (0, 0)),
                  pl.BlockSpec((nb, 3, 8, GDN_QK), lambda i, c: (i, 0, 0, 0)),
                  vec, vec],
        out_specs=[head_spec(GDN_DV), head_spec(GDN_DK), head_spec(GDN_DK), head_spec(GDN_DK),
                   head_spec(CHUNK),
                   pl.BlockSpec((nb, GDN_HEADS, r // CHUNK, 1, 128), lambda i, c: (i, 0, c, 0, 0))],
        out_shape=[head_shape(GDN_DV, F32), head_shape(GDN_DK, BF16), head_shape(GDN_DK, BF16),
                   head_shape(GDN_DK, BF16), head_shape(CHUNK, BF16),
                   jax.ShapeDtypeStruct((b, GDN_HEADS, nc, 1, 128), F32)],
        scratch_shapes=[pltpu.VMEM((nb, 3, 8, GDN_QK), F32)],
        compiler_params=_cparams(("parallel", "arbitrary")),
        name="gdn_prep",
    )(p3, p3, p3, p3, conv_w, conv_state8, alog, dt)
    return u, w, qd, kd, qk, gl


def _hgrn_prep_kernel(q_ref, f_ref, lb_ref, att_ref, qe_ref, ke_ref, dec_ref, *, nb, r):
    n = nb * r
    nc = n // CHUNK
    row = lax.broadcasted_iota(jnp.int32, (1, CHUNK, CHUNK), 1)
    col = lax.broadcasted_iota(jnp.int32, (1, CHUNK, CHUNK), 2)
    uppers = _upper_half_masks((n, HGRN_DK))
    halves = [1 << lvl for lvl in range(len(uppers))]
    pairs = [(row // (2 * m) == col // (2 * m)) & ((row // m) % 2 == 1) & ((col // m) % 2 == 0) for m in halves]

    def bmm_nt(a, b):
        a3, b3 = (x.astype(BF16).reshape(nc, CHUNK, HGRN_DK) for x in (a, b))
        return jnp.einsum('cid,cjd->cij', a3, b3, preferred_element_type=F32)

    for h in range(HGRN_HEADS):
        sl = slice(h * HGRN_DK, (h + 1) * HGRN_DK)
        lb = lb_ref[:, sl]
        f = lb + (1.0 - lb) * _sigmoid(f_ref[:, :, sl].reshape(n, HGRN_DK))
        q = _silu(q_ref[:, :, sl].reshape(n, HGRN_DK)) * (HGRN_DK ** -0.5)
        k = 1.0 - f
        scans = _segment_scans(jnp.log2(f), uppers)
        cb, c_tot = scans[-1]

        att = jnp.where(row == col, bmm_nt(q, k), 0.0)
        for lvl in range(len(halves)):
            pre_m, tot_m = scans[lvl]
            att = att + jnp.where(pairs[lvl], bmm_nt(q * jnp.exp2(pre_m), k * jnp.exp2(tot_m - pre_m)), 0.0)

        att_ref[:, h] = att.astype(BF16).reshape(nb, r, CHUNK)
        qe_ref[:, h] = (q * jnp.exp2(cb)).astype(BF16).reshape(nb, r, HGRN_DK)
        ke_ref[:, h] = (k * jnp.exp2(c_tot - cb)).astype(BF16).reshape(nb, r, HGRN_DK)
        dec_ref[:, h] = jnp.exp2(c_tot).reshape(nc, CHUNK, HGRN_DK)[:, 0:1, :].reshape(nb, r // CHUNK, 1, HGRN_DK)


def _hgrn_scan_kernel(att_ref, qe_ref, ke_ref, dec_ref, v_ref, s0_ref, o_ref, sf_ref, st_ref, *, nb):
    c_idx = pl.program_id(1)

    @pl.when(c_idx == 0)
    def _():
        st_ref[...] = s0_ref[...]

    chains = [(b, h) for b in range(nb) for h in range(HGRN_HEADS)]
    sls = [slice(h * HGRN_DV, (h + 1) * HGRN_DV) for h in range(HGRN_HEADS)]
    st_old = [st_ref[b, h] for b, h in chains]
    vs = [v_ref[b, :, sls[h]].astype(BF16) for b, h in chains]
    o_st = [_dot_nt(qe_ref[b, h], st.astype(BF16)) for (b, h), st in zip(chains, st_old)]
    for (b, h), st, v, o1 in zip(chains, st_old, vs, o_st):
        o_ref[b, :, sls[h]] = o1 + jnp.dot(att_ref[b, h], v, preferred_element_type=F32)
        st_ref[b, h] = st * dec_ref[b, h, 0] + _dot_tn(v, ke_ref[b, h])

    @pl.when(c_idx == pl.num_programs(1) - 1)
    def _():
        sf_ref[...] = st_ref[...]


def _hgrn_prep(p3, lb):
    b, t, _ = p3.shape
    r = min(HGRN_PREP_ROWS, t)
    nb = max(1, min(b, HGRN_PREP_ROWS // r))
    nc = t // CHUNK
    col0 = (COL_GDN + 4 * GDN_QK) // HGRN_KW
    blk = lambda j: pl.BlockSpec((nb, r, HGRN_KW), lambda i, c: (i, c, col0 + j))
    head_spec = lambda w: pl.BlockSpec((nb, HGRN_HEADS, r, w), lambda i, c: (i, 0, c, 0))
    head_shape = lambda w: jax.ShapeDtypeStruct((b, HGRN_HEADS, t, w), BF16)
    att, qe, ke, dec = pl.pallas_call(
        functools.partial(_hgrn_prep_kernel, nb=nb, r=r),
        grid=(b // nb, t // r),
        in_specs=[blk(0), blk(1), pl.BlockSpec((1, HGRN_KW), lambda i, c: (0, 0))],
        out_specs=[head_spec(CHUNK), head_spec(HGRN_DK), head_spec(HGRN_DK),
                   pl.BlockSpec((nb, HGRN_HEADS, r // CHUNK, 1, HGRN_DK), lambda i, c: (i, 0, c, 0, 0))],
        out_shape=[head_shape(CHUNK), head_shape(HGRN_DK), head_shape(HGRN_DK),
                   jax.ShapeDtypeStruct((b, HGRN_HEADS, nc, 1, HGRN_DK), F32)],
        compiler_params=_cparams(("parallel", "parallel")),
        name="hgrn_prep",
    )(p3, p3, lb)
    return att, qe, ke, dec


def _rec_scan_kernel(u_ref, w_ref, qd_ref, kd_ref, qk_ref, gl_ref, sg0_ref,
                     att_ref, qe_ref, ke_ref, dec_ref, v_ref, sh0_ref,
                     og_ref, sgf_ref, oh_ref, shf_ref, sg_ref, sh_ref, *, nb):
    _gdn_scan_kernel(u_ref, w_ref, qd_ref, kd_ref, qk_ref, gl_ref, sg0_ref, og_ref, sgf_ref, sg_ref, nb=nb)
    _hgrn_scan_kernel(att_ref, qe_ref, ke_ref, dec_ref, v_ref, sh0_ref, oh_ref, shf_ref, sh_ref, nb=nb)


def _rec_scan(gdn_parts, hgrn_parts, p3, s0_gdn, s0t_hgrn):
    b, t, _ = p3.shape
    nc = t // CHUNK
    sb = min(b, 8)
    col_v = (COL_GDN + 4 * GDN_QK) // HGRN_KW + 2
    chunk_spec = lambda w: pl.BlockSpec((sb, GDN_HEADS, CHUNK, w), lambda i, c: (i, 0, c, 0))
    per_chunk = pl.BlockSpec((sb, GDN_HEADS, 1, 1, 128), lambda i, c: (i, 0, c, 0, 0))
    state_spec = pl.BlockSpec((sb, GDN_HEADS, 128, 128), lambda i, c: (i, 0, 0, 0))
    out_spec = pl.BlockSpec((sb, CHUNK, GDN_VW), lambda i, c: (i, c, 0))
    state_shape = jax.ShapeDtypeStruct((b, GDN_HEADS, 128, 128), F32)
    return pl.pallas_call(
        functools.partial(_rec_scan_kernel, nb=sb),
        grid=(b // sb, nc),
        in_specs=[chunk_spec(GDN_DV), chunk_spec(GDN_DK), chunk_spec(GDN_DK), chunk_spec(GDN_DK),
                  chunk_spec(CHUNK), per_chunk, state_spec,
                  chunk_spec(CHUNK), chunk_spec(HGRN_DK), chunk_spec(HGRN_DK), per_chunk,
                  pl.BlockSpec((sb, CHUNK, HGRN_KW), lambda i, c: (i, c, col_v)), state_spec],
        out_specs=[out_spec, state_spec, out_spec, state_spec],
        out_shape=[jax.ShapeDtypeStruct((b, t, GDN_VW), F32), state_shape,
                   jax.ShapeDtypeStruct((b, t, HGRN_KW), F32), state_shape],
        scratch_shapes=[pltpu.VMEM((sb, GDN_HEADS, 128, 128), F32), pltpu.VMEM((sb, HGRN_HEADS, 128, 128), F32)],
        compiler_params=_cparams(("parallel", "arbitrary")),
        name="rec_scan",
    )(*gdn_parts, s0_gdn, *hgrn_parts, p3, s0t_hgrn)


def _merge_kernel(x_ref, oa_ref, ob_ref, zb_ref, oc_ref, zc_ref, g0_ref, g1_ref, g2_ref, ngb_ref, ngc_ref,
                  wa_ref, wb_ref, wc_ref, wo_ref, y_ref):
    def branch(o, w_ref, g_ref):
        gate = _sigmoid(g_ref[...].astype(F32))
        return gate * jnp.dot(o.astype(BF16), w_ref[...], preferred_element_type=F32)

    def normed(o_ref, z_ref, ng_ref):
        return jnp.concatenate(
            [_rms(o_ref[:, h * 128:(h + 1) * 128], ng_ref[...]) * _silu(z_ref[:, h * 128:(h + 1) * 128])
             for h in range(GDN_HEADS)], axis=-1)

    mixed = (branch(oa_ref[...], wa_ref, g0_ref) + branch(normed(ob_ref, zb_ref, ngb_ref), wb_ref, g1_ref)
             + branch(normed(oc_ref, zc_ref, ngc_ref), wc_ref, g2_ref))
    y_ref[...] = x_ref[...] + jnp.dot(mixed.astype(BF16), wo_ref[...], preferred_element_type=F32)


def _merge(x, oa, ob, oc, p, pg, ngb, ngc, wa, wb, wc, wo):
    n = x.shape[0]
    tm = _row_tile(n, 512)
    row = lambda w: pl.BlockSpec((tm, w), lambda i: (i, 0))
    gate = lambda j: pl.BlockSpec((tm, D_MODEL), lambda i: (i, j))
    zb_spec = pl.BlockSpec((tm, GDN_VW), lambda i: (i, COL_GDN // GDN_VW + 3))
    zc_spec = pl.BlockSpec((tm, HGRN_KW), lambda i: (i, (COL_GDN + 4 * GDN_QK) // HGRN_KW + 3))
    vec = pl.BlockSpec((1, 128), lambda i: (0, 0))
    wsp = lambda k: pl.BlockSpec((k, D_MODEL), lambda i: (0, 0))
    return pl.pallas_call(
        _merge_kernel,
        grid=(n // tm,),
        in_specs=[row(D_MODEL), row(MLA_VW), row(GDN_VW), zb_spec, row(HGRN_KW), zc_spec,
                  gate(0), gate(1), gate(2), vec, vec,
                  wsp(MLA_VW), wsp(GDN_VW), wsp(HGRN_KW), wsp(D_MODEL)],
        out_specs=row(D_MODEL),
        out_shape=jax.ShapeDtypeStruct((n, D_MODEL), F32),
        compiler_params=_cparams(("parallel",)),
        name="merge_out",
    )(x, oa, ob, p, oc, p, pg, pg, pg, ngb, ngc, wa, wb, wc, wo)


def _router_kernel(x_ref, g_ref, w_ref, b_ref, cw_ref, cwt_ref, cnt_ref):
    h = _rms(x_ref[...], g_ref[...])
    logits = jnp.dot(h, w_ref[...], preferred_element_type=F32, precision=lax.Precision.HIGHEST) + b_ref[...]
    lane = lax.broadcasted_iota(jnp.int32, logits.shape, 1)
    valid = lane < N_EXPERTS
    neg = -jnp.inf
    l1 = jnp.where(valid, logits, neg)
    m1 = jnp.max(l1, axis=-1, keepdims=True)
    i1 = jnp.min(jnp.where(l1 == m1, lane, 128), axis=-1, keepdims=True)
    l2 = jnp.where(lane == i1, neg, l1)
    m2 = jnp.max(l2, axis=-1, keepdims=True)
    i2 = jnp.min(jnp.where(l2 == m2, lane, 128), axis=-1, keepdims=True)
    e2 = jnp.exp(m2 - m1)
    den = 1.0 + e2
    cw = jnp.where(lane == i1, 1.0 / den, 0.0) + jnp.where(lane == i2, e2 / den, 0.0)
    cw_ref[...] = cw
    cwt_ref[...] = cw.T
    cnt_ref[0] = jnp.sum((cw > 0.0).astype(F32), axis=0, keepdims=True)


def _router(x, g, w, b, tm):
    n = x.shape[0]
    return pl.pallas_call(
        _router_kernel,
        grid=(n // tm,),
        in_specs=[pl.BlockSpec((tm, D_MODEL), lambda i: (i, 0)),
                  pl.BlockSpec((1, D_MODEL), lambda i: (0, 0)),
                  pl.BlockSpec((D_MODEL, 128), lambda i: (0, 0)),
                  pl.BlockSpec((1, 128), lambda i: (0, 0))],
        out_specs=[pl.BlockSpec((tm, 128), lambda i: (i, 0)),
                   pl.BlockSpec((128, tm), lambda i: (0, i)),
                   pl.BlockSpec((1, 1, 128), lambda i: (i, 0, 0))],
        out_shape=[jax.ShapeDtypeStruct((n, 128), F32),
                   jax.ShapeDtypeStruct((128, n), F32),
                   jax.ShapeDtypeStruct((n // tm, 1, 128), F32)],
        compiler_params=_cparams(("parallel",)),
        name="moe_router",
    )(x, g, w, b)


def _moe_kernel(cnt_ref, x_ref, g_ref, cw_ref, cwt_ref, wg_ref, wu_ref, wd_ref, y_ref,
                h_ref, rcol_ref, rrow_ref, *, t, br):
    i, e = pl.program_id(0), pl.program_id(1)

    @pl.when(e == 0)
    def _():
        x = x_ref[...]
        h_ref[...] = _rms(x, g_ref[...]).astype(BF16)
        y_ref[...] = x
        r = lax.broadcasted_iota(jnp.int32, (t, t), 0)
        c = lax.broadcasted_iota(jnp.int32, (t, t), 1)
        on = cw_ref[...] > 0.0
        rank = jnp.dot((c < r).astype(BF16), on.astype(BF16), preferred_element_type=F32)
        rcol_ref[...] = jnp.where(on, rank.astype(jnp.int32), -1)
        on_t = cwt_ref[...] > 0.0
        rank_t = jnp.dot(on_t.astype(BF16), (r < c).astype(BF16), preferred_element_type=F32)
        rrow_ref[...] = jnp.where(on_t, rank_t.astype(jnp.int32), -1)

    count = cnt_ref[i * N_EXPERTS + e]
    n_blocks = lax.div(count + (br - 1), br)
    sel = lax.broadcasted_iota(jnp.int32, (1, 128), 1) == e
    rank_c = jnp.sum(jnp.where(sel, rcol_ref[...], 0), axis=-1, keepdims=True)
    rank_r = rrow_ref[pl.ds(e, 1), :]
    w_r = cwt_ref[pl.ds(e, 1), :]

    def body(j, carry):
        base = j * br
        rows = lax.broadcasted_iota(jnp.int32, (br, t), 0) + base
        pick = rows == rank_r
        xc = jnp.dot(pick.astype(BF16), h_ref[...], preferred_element_type=F32).astype(BF16)
        a = jnp.dot(xc, wg_ref[0], preferred_element_type=F32)
        b = jnp.dot(xc, wu_ref[0], preferred_element_type=F32)
        yc = jnp.dot((_silu(a) * b).astype(BF16), wd_ref[0], preferred_element_type=F32)
        w_rows = jnp.sum(jnp.where(pick, w_r, 0.0), axis=-1, keepdims=True)
        cols = lax.broadcasted_iota(jnp.int32, (t, br), 1) + base
        y_ref[...] += jnp.dot((cols == rank_c).astype(BF16), (yc * w_rows).astype(BF16),
                              preferred_element_type=F32)
        return carry

    lax.fori_loop(0, n_blocks, body, 0)


def _moe(x, g, cw, cwt, counts, wg, wu, wd, t):
    n = x.shape[0]
    ne, _, ff = wg.shape
    br = MOE_BLOCK_ROWS
    grid_spec = pltpu.PrefetchScalarGridSpec(
        num_scalar_prefetch=1,
        grid=(n // t, ne),
        in_specs=[pl.BlockSpec((t, D_MODEL), lambda i, e, cnt: (i, 0)),
                  pl.BlockSpec((1, D_MODEL), lambda i, e, cnt: (0, 0)),
                  pl.BlockSpec((t, 128), lambda i, e, cnt: (i, 0)),
                  pl.BlockSpec((128, t), lambda i, e, cnt: (0, i)),
                  pl.BlockSpec((1, D_MODEL, ff), lambda i, e, cnt: (e, 0, 0)),
                  pl.BlockSpec((1, D_MODEL, ff), lambda i, e, cnt: (e, 0, 0)),
                  pl.BlockSpec((1, ff, D_MODEL), lambda i, e, cnt: (e, 0, 0))],
        out_specs=pl.BlockSpec((t, D_MODEL), lambda i, e, cnt: (i, 0)),
        scratch_shapes=[pltpu.VMEM((t, D_MODEL), BF16),
                        pltpu.VMEM((t, 128), jnp.int32),
                        pltpu.VMEM((128, t), jnp.int32)])
    return pl.pallas_call(
        functools.partial(_moe_kernel, t=t, br=br),
        grid_spec=grid_spec,
        out_shape=jax.ShapeDtypeStruct((n, D_MODEL), F32),
        compiler_params=_cparams(("parallel", "arbitrary")),
        name="moe",
    )(counts, x, g, cw, cwt, wg, wu, wd)


def _ffn_kernel(x_ref, g_ref, wg_ref, wu_ref, wd_ref, y_ref, h_ref):
    e = pl.program_id(1)

    @pl.when(e == 0)
    def _():
        x = x_ref[...]
        h_ref[...] = _rms(x, g_ref[...]).astype(BF16)
        y_ref[...] = x

    h = h_ref[...]
    a = jnp.dot(h, wg_ref[0], preferred_element_type=F32)
    b = jnp.dot(h, wu_ref[0], preferred_element_type=F32)
    y_ref[...] += jnp.dot((_silu(a) * b).astype(BF16), wd_ref[0], preferred_element_type=F32)


def _ffn(x, g, wg, wu, wd):
    n = x.shape[0]
    ne, _, ff = wg.shape
    tm = _row_tile(n, 512)
    return pl.pallas_call(
        _ffn_kernel,
        grid=(n // tm, ne),
        in_specs=[pl.BlockSpec((tm, D_MODEL), lambda i, e: (i, 0)),
                  pl.BlockSpec((1, D_MODEL), lambda i, e: (0, 0)),
                  pl.BlockSpec((1, D_MODEL, ff), lambda i, e: (e, 0, 0)),
                  pl.BlockSpec((1, D_MODEL, ff), lambda i, e: (e, 0, 0)),
                  pl.BlockSpec((1, ff, D_MODEL), lambda i, e: (e, 0, 0))],
        out_specs=pl.BlockSpec((tm, D_MODEL), lambda i, e: (i, 0)),
        out_shape=jax.ShapeDtypeStruct((n, D_MODEL), F32),
        scratch_shapes=[pltpu.VMEM((tm, D_MODEL), BF16)],
        compiler_params=_cparams(("parallel", "arbitrary")),
        name="ffn",
    )(x, g, wg, wu, wd)


def _pad_lanes(x, left, total):
    return jnp.pad(x, [(0, 0)] * (x.ndim - 1) + [(left, total - left - x.shape[-1])])


def _pack_w_in(w):
    cq, ckv, kr, gq, gk, gv, gz, ga, gb, hq, hf, hi, hg, gates = jnp.split(
        w, np.cumsum(SPLIT_SIZES)[:-1].tolist(), axis=-1)
    kr_blk = _pad_lanes(kr, ROPE_LANE0, 128)
    ab_blk = _pad_lanes(jnp.concatenate([ga, gb], axis=-1), 0, 256)
    return jnp.concatenate([cq, ckv, kr_blk, ab_blk, gq, gk, gv, gz, hq, hf, hi, hg, gates], axis=-1).astype(BF16)


def _rope_tables(n_pos):
    half = MLA_ROPE // 2
    inv = ROPE_THETA ** (-jnp.arange(half, dtype=F32) / half)
    ang = jnp.arange(n_pos, dtype=F32)[:, None] * inv[None, :]
    cos, sin = jnp.cos(ang), jnp.sin(ang)
    one = jnp.ones((n_pos, MLA_NOPE), F32)
    zero = jnp.zeros((n_pos, MLA_NOPE), F32)
    tail = jnp.zeros((n_pos, HEAD_PAD - MLA_QK_HEAD), F32)
    z16 = jnp.zeros((n_pos, half), F32)
    c = jnp.concatenate([one, cos, cos, tail], axis=-1)
    s1 = jnp.concatenate([zero, -sin, z16, tail], axis=-1)
    s2 = jnp.concatenate([zero, z16, sin, tail], axis=-1)
    return c, s1, s2


def _layer_weights(l, a):
    f = {}
    f['mixer_g'] = a['mixer_norm_g'][l][None]
    f['w_in'] = _pack_w_in(a['w_in'][l])
    f['gq'] = a['mla_q_norm_g'][l][None]
    f['gkv'] = a['mla_kv_norm_g'][l][None]
    wq = a['mla_w_q_up'][l].reshape(MLA_Q_LORA, MLA_HEADS, MLA_QK_HEAD)
    f['wq'] = _pad_lanes(wq, 0, HEAD_PAD).reshape(MLA_Q_LORA, QK_W).astype(BF16)
    wkv = a['mla_w_kv_up'][l].reshape(MLA_KV_LORA, MLA_HEADS, MLA_NOPE + MLA_V)
    f['wk'] = _pad_lanes(wkv[:, :, :MLA_NOPE], 0, HEAD_PAD).reshape(MLA_KV_LORA, QK_W).astype(BF16)
    f['wv'] = wkv[:, :, MLA_NOPE:].reshape(MLA_KV_LORA, MLA_VW).astype(BF16)
    f['wv_pad'] = _pad_lanes(wkv[:, :, MLA_NOPE:], 0, HEAD_PAD).reshape(MLA_KV_LORA, QK_W).astype(BF16)
    f['hq'] = _pad_lanes(a['mla_q_head_norm_g'][l][None], 0, HEAD_PAD)
    hk = a['mla_k_head_norm_g'][l][None]
    f['hk_nope'] = _pad_lanes(hk[:, :MLA_NOPE], 0, HEAD_PAD)
    f['hk_rope'] = _pad_lanes(hk[:, MLA_NOPE:], ROPE_LANE0, HEAD_PAD)
    f['wo_a'] = a['mla_w_o'][l].astype(BF16)
    f['conv_w'] = a['gdn_conv_w'][l]
    f['alog'] = _pad_lanes(a['gdn_a_log'][l][None], 0, 128)
    f['dt'] = _pad_lanes(a['gdn_dt_bias'][l][None], 0, 128)
    f['gdn_g'] = a['gdn_norm_g'][l][None]
    f['wo_b'] = a['gdn_w_o'][l].astype(BF16)
    f['hgrn_g'] = a['hgrn_norm_g'][l][None]
    f['wo_c'] = a['hgrn_w_o'][l].astype(BF16)
    f['w_out'] = a['w_out'][l].astype(BF16)
    f['ffn_g'] = a['ffn_norm_g'][l][None]
    if l % 2 == 0:
        wg, wu, wd = a['dense_w_gate'][l // 2], a['dense_w_up'][l // 2], a['dense_w_down'][l // 2]
        ff = wg.shape[1]
        half = ff // 2
        f['ffn'] = (jnp.moveaxis(wg.reshape(D_MODEL, 2, half), 1, 0).astype(BF16),
                    jnp.moveaxis(wu.reshape(D_MODEL, 2, half), 1, 0).astype(BF16),
                    wd.reshape(2, half, D_MODEL).astype(BF16))
        f['router'] = None
    else:
        f['ffn'] = (a['moe_w_gate'][l // 2].astype(BF16), a['moe_w_up'][l // 2].astype(BF16),
                    a['moe_w_down'][l // 2].astype(BF16))
        f['router'] = (_pad_lanes(a['moe_w_router'][l // 2], 0, 128),
                       _pad_lanes(a['moe_b_router'][l // 2][None], 0, 128))
    return f


def _trunk_layer(x, b, t, f, lb, tabs, past):
    n = b * t
    p, p_gate = _in_proj(x, f['mixer_g'], f['w_in'])
    p3 = p.reshape(b, t, P_MAIN)

    fresh = past['ckv'] is None
    q, ckv, kr = _mla_pre(p, b, t, f['gq'], f['gkv'], f['wq'], f['hq'], f['hk_rope'], tabs, transposed=fresh)
    if fresh:
        k_all, vt_all = _kv_up(ckv, kr, b, t, f['wk'], f['wv_pad'], f['hk_nope'], transposed=True)
        o_a = _attention_t(q, k_all.reshape(b, t, QK_W), vt_all)
    else:
        past_len = past['ckv'].shape[1]
        s = past_len + t
        kr_past = _pad_lanes(past['kr'], ROPE_LANE0, HEAD_PAD)
        kv = _kv_up_rows(past['ckv'], kr_past, s, 0, f['wk'], f['wv'], f['hk_nope'])
        k_all, v_all = _kv_up_rows(ckv.reshape(b, t, -1), kr.reshape(b, t, -1), s, past_len,
                                   f['wk'], f['wv'], f['hk_nope'], into=kv)
        o_a = _attention(q.reshape(b, t, QK_W), k_all, v_all, causal=False)

    conv8 = jnp.pad(past['conv'].reshape(b, CONV_W - 1, 3, GDN_QK).transpose(0, 2, 1, 3),
                    ((0, 0), (0, 0), (8 - (CONV_W - 1), 0), (0, 0)))
    gdn_parts = _gdn_prep(p3, f['conv_w'], conv8, f['alog'], f['dt'])
    gdn_conv = p3[:, t - (CONV_W - 1):, COL_GDN:COL_GDN + 3 * GDN_QK]

    hgrn_parts = _hgrn_prep(p3, lb)
    o_b, gdn_s, o_c, hgrn_st = _rec_scan(gdn_parts, hgrn_parts, p3, past['gdn'],
                                         jnp.swapaxes(past['hgrn'], -1, -2))
    hgrn_s = jnp.swapaxes(hgrn_st, -1, -2)

    x = _merge(x, o_a.reshape(n, -1), o_b.reshape(n, -1), o_c.reshape(n, -1), p, p_gate, f['gdn_g'],
               f['hgrn_g'], f['wo_a'], f['wo_b'], f['wo_c'], f['w_out'])

    wg, wu, wd = f['ffn']
    if f['router'] is None:
        x = _ffn(x, f['ffn_g'], wg, wu, wd)
    else:
        t_moe = _row_tile(n, MOE_TILE)
        cw, cwt, cnt = _router(x, f['ffn_g'], *f['router'], t_moe)
        counts = cnt[:, 0, :N_EXPERTS].astype(jnp.int32).reshape(-1)
        x = _moe(x, f['ffn_g'], cw, cwt, counts, wg, wu, wd, t_moe)

    new_ckv = ckv.reshape(b, t, MLA_KV_LORA)
    new_kr = kr.reshape(b, t, HEAD_PAD)[:, :, ROPE_LANE0:ROPE_LANE0 + MLA_ROPE]
    return x, (new_ckv, new_kr, gdn_s, gdn_conv, hgrn_s)


def kernel(x_prompt, x_sample, cache_mla_ckv, cache_mla_krope, state_gdn, state_gdn_conv, state_hgrn,
           mixer_norm_g, w_in, mla_q_norm_g, mla_w_q_up, mla_kv_norm_g, mla_w_kv_up,
           mla_q_head_norm_g, mla_k_head_norm_g, mla_w_o,
           gdn_conv_w, gdn_a_log, gdn_dt_bias, gdn_norm_g, gdn_w_o,
           hgrn_lb_logits, hgrn_norm_g, hgrn_w_o, w_out, ffn_norm_g,
           dense_w_gate, dense_w_up, dense_w_down,
           moe_w_router, moe_b_router, moe_w_gate, moe_w_up, moe_w_down):
    a = dict(mixer_norm_g=mixer_norm_g, w_in=w_in, mla_q_norm_g=mla_q_norm_g, mla_w_q_up=mla_w_q_up,
             mla_kv_norm_g=mla_kv_norm_g, mla_w_kv_up=mla_w_kv_up, mla_q_head_norm_g=mla_q_head_norm_g,
             mla_k_head_norm_g=mla_k_head_norm_g, mla_w_o=mla_w_o, gdn_conv_w=gdn_conv_w,
             gdn_a_log=gdn_a_log, gdn_dt_bias=gdn_dt_bias, gdn_norm_g=gdn_norm_g, gdn_w_o=gdn_w_o,
             hgrn_norm_g=hgrn_norm_g, hgrn_w_o=hgrn_w_o, w_out=w_out, ffn_norm_g=ffn_norm_g,
             dense_w_gate=dense_w_gate, dense_w_up=dense_w_up, dense_w_down=dense_w_down,
             moe_w_router=moe_w_router, moe_b_router=moe_b_router, moe_w_gate=moe_w_gate,
             moe_w_up=moe_w_up, moe_w_down=moe_w_down)
    depth = w_in.shape[0]
    lb_soft = jax.nn.softmax(hgrn_lb_logits.astype(F32), axis=0)
    hgrn_lb = jnp.cumsum(lb_soft, axis=0) - lb_soft[0]

    b_p, t_p = x_prompt.shape[:2]
    b_s, t_s = x_sample.shape[:2]
    past_len = cache_mla_ckv.shape[2]
    tab_all = _rope_tables(max(t_p, past_len + t_s))
    tabs_p = tuple(tb[:t_p] for tb in tab_all)
    tabs_s = tuple(tb[past_len:past_len + t_s] for tb in tab_all)

    xp = x_prompt.reshape(b_p * t_p, D_MODEL)
    xs = x_sample.reshape(b_s * t_s, D_MODEL)
    past_p = dict(ckv=None, kr=None,
                  gdn=jnp.zeros((b_p, GDN_HEADS, GDN_DK, GDN_DV), F32),
                  conv=jnp.zeros((b_p, CONV_W - 1, 3 * GDN_QK), F32),
                  hgrn=jnp.zeros((b_p, HGRN_HEADS, HGRN_DK, HGRN_DV), F32))
    st_p, st_s = [], []
    for l in range(depth):
        f = _layer_weights(l, a)
        lb = hgrn_lb[l][None]
        past_s = dict(ckv=cache_mla_ckv[l], kr=cache_mla_krope[l], gdn=state_gdn[l],
                      conv=state_gdn_conv[l], hgrn=state_hgrn[l])
        xp, sp = _trunk_layer(xp, b_p, t_p, f, lb, tabs_p, past_p)
        xs, ss = _trunk_layer(xs, b_s, t_s, f, lb, tabs_s, past_s)
        st_p.append(sp)
        st_s.append(ss)

    def stack(lst, i):
        return jnp.stack([s[i] for s in lst], axis=0)

    return (xp.reshape(b_p, t_p, D_MODEL), xs.reshape(b_s, t_s, D_MODEL),
            stack(st_p, 0), stack(st_p, 1), stack(st_p, 2), stack(st_p, 3), stack(st_p, 4),
            stack(st_s, 0), stack(st_s, 1), stack(st_s, 2), stack(st_s, 3), stack(st_s, 4))
```

```python
import functools

import jax
import jax.numpy as jnp
import numpy as np
from jax import lax
from jax.experimental import pallas as pl
from jax.experimental.pallas import tpu as pltpu

F32 = jnp.float32
BF16 = jnp.bfloat16

D_MODEL = 1024
CHUNK = 64
NORM_EPS = 1e-6

MLA_HEADS = 8
MLA_NOPE = 64
MLA_ROPE = 32
MLA_V = 64
MLA_Q_LORA = 384
MLA_KV_LORA = 256
MLA_QK_HEAD = MLA_NOPE + MLA_ROPE
MLA_VW = MLA_HEADS * MLA_V
ROPE_THETA = 10000.0
LOG2_E = 1.4426950408889634
HEAD_PAD = 128
QK_W = MLA_HEADS * HEAD_PAD
VT_ROWS = MLA_V + 16

GDN_HEADS = 4
GDN_DK = 128
GDN_DV = 128
GDN_QK = GDN_HEADS * GDN_DK
GDN_VW = GDN_HEADS * GDN_DV
CONV_W = 4
HGRN_PREP_ROWS = 512
GDN_PREP_ROWS = 1024

HGRN_HEADS = 4
HGRN_DK = 128
HGRN_DV = 128
HGRN_KW = HGRN_HEADS * HGRN_DK

N_BRANCH = 3
SPLIT_SIZES = (MLA_Q_LORA, MLA_KV_LORA, MLA_ROPE,
               GDN_QK, GDN_QK, GDN_VW, GDN_VW, GDN_HEADS, GDN_HEADS,
               HGRN_KW, HGRN_KW, HGRN_KW, HGRN_KW,
               N_BRANCH * D_MODEL)

N_EXPERTS = 8
FF_EXPERT = 1408
MOE_TILE = 1024
TOP_K = 2
MOE_BLOCK_ROWS = tuple(MOE_TILE * TOP_K // N_EXPERTS + d for d in (-32, -16, 0, 16, 32, 48))

P_COLS = 8192
P_MAIN = P_COLS - N_BRANCH * D_MODEL
P_TN = 512
COL_CKV = MLA_Q_LORA
COL_KR = MLA_Q_LORA + MLA_KV_LORA
COL_GAB = COL_KR + 128
COL_GDN = 1024
ROPE_LANE0 = MLA_NOPE

VMEM_LIMIT = 56 * 1024 * 1024


def _cparams(sem):
    return pltpu.CompilerParams(dimension_semantics=sem, vmem_limit_bytes=VMEM_LIMIT)


def _row_tile(n, cap):
    for t in (2048, 1024, 512, 256, 128, 64, 32, 16, 8):
        if t <= cap and n % t == 0:
            return t
    raise ValueError(f"no row tile for {n}")


def _sigmoid(x):
    return 1.0 / (1.0 + jnp.exp(-x))


def _silu(x):
    return x * (0.5 * jnp.tanh(0.5 * x) + 0.5)


def _rms(x, g):
    ms = jnp.mean(x * x, axis=-1, keepdims=True)
    return x * lax.rsqrt(ms + NORM_EPS) * g


def _in_proj_kernel(x_ref, g_ref, w_ref, o_ref, gate_ref, h_ref):
    j = pl.program_id(1)

    @pl.when(j == 0)
    def _():
        h_ref[...] = _rms(x_ref[...], g_ref[...]).astype(BF16)

    @pl.when(j < P_MAIN // P_TN)
    def _():
        o_ref[...] = jnp.dot(h_ref[...], w_ref[...], preferred_element_type=F32)

    @pl.when(j >= P_MAIN // P_TN)
    def _():
        gate_ref[...] = jnp.dot(h_ref[...], w_ref[...], preferred_element_type=F32).astype(BF16)


def _in_proj(x, g, w):
    n = x.shape[0]
    tm = _row_tile(n, 2048)
    n_main = P_MAIN // P_TN
    return pl.pallas_call(
        _in_proj_kernel,
        grid=(n // tm, P_COLS // P_TN),
        in_specs=[pl.BlockSpec((tm, D_MODEL), lambda i, j: (i, 0)),
                  pl.BlockSpec((1, D_MODEL), lambda i, j: (0, 0)),
                  pl.BlockSpec((D_MODEL, P_TN), lambda i, j: (0, j))],
        out_specs=[pl.BlockSpec((tm, P_TN), lambda i, j: (i, jnp.minimum(j, n_main - 1))),
                   pl.BlockSpec((tm, P_TN), lambda i, j: (i, jnp.maximum(j - n_main, 0)))],
        out_shape=[jax.ShapeDtypeStruct((n, P_MAIN), F32),
                   jax.ShapeDtypeStruct((n, P_COLS - P_MAIN), BF16)],
        scratch_shapes=[pltpu.VMEM((tm, D_MODEL), BF16)],
        compiler_params=_cparams(("parallel", "arbitrary")),
        name="in_proj",
    )(x, g, w)


def _rope(x, c, s1, s2):
    return x * c + pltpu.roll(x, HEAD_PAD - 16, 1) * s1 + pltpu.roll(x, 16, 1) * s2


def _mla_pre_kernel(p_ref, gq_ref, gkv_ref, wq_ref, hq_ref, hk_ref, c_ref, s1_ref, s2_ref, *rest,
                    scale, transposed):
    if transposed:
        hqt_ref, ct_ref, s1t_ref, s2t_ref, q_ref, ckv_ref, kr_ref = rest
    else:
        q_ref, ckv_ref, kr_ref = rest
    c, s1, s2 = c_ref[...], s1_ref[...], s2_ref[...]
    ckv_ref[...] = _rms(p_ref[:, COL_CKV:COL_KR], gkv_ref[...])

    kr = p_ref[:, COL_KR:COL_KR + HEAD_PAD]
    kr_ms = jnp.sum(kr * kr, axis=-1, keepdims=True) * (1.0 / MLA_ROPE)
    kr_ref[...] = _rope(kr * lax.rsqrt(kr_ms + NORM_EPS) * hk_ref[...], c, s1, s2)

    cq = _rms(p_ref[:, 0:MLA_Q_LORA], gq_ref[...]).astype(BF16)
    q = jnp.dot(cq, wq_ref[...], preferred_element_type=F32)
    if transposed:
        row = lax.broadcasted_iota(jnp.int32, (HEAD_PAD, 1), 0)
        hqt, ct, s1t, s2t = hqt_ref[...], ct_ref[...], s1t_ref[...], s2t_ref[...]
        for h in range(MLA_HEADS):
            qt = q[:, h * HEAD_PAD:(h + 1) * HEAD_PAD].T
            sq = qt * qt
            ms_n = jnp.sum(sq[0:MLA_NOPE], axis=0, keepdims=True) * (1.0 / MLA_NOPE)
            ms_r = jnp.sum(sq[MLA_NOPE:MLA_QK_HEAD], axis=0, keepdims=True) * (1.0 / MLA_ROPE)
            inv = jnp.where(row < MLA_NOPE, lax.rsqrt(ms_n + NORM_EPS), lax.rsqrt(ms_r + NORM_EPS))
            x = qt * inv * hqt
            x = x * ct + pltpu.roll(x, HEAD_PAD - 16, 0) * s1t + pltpu.roll(x, 16, 0) * s2t
            q_ref[0, h] = x.astype(BF16)
    else:
        lane = lax.broadcasted_iota(jnp.int32, (1, HEAD_PAD), 1)
        is_nope = lane < MLA_NOPE
        hq = hq_ref[...]
        for h in range(MLA_HEADS):
            qh = q[:, h * HEAD_PAD:(h + 1) * HEAD_PAD]
            sq = qh * qh
            ms_n = jnp.sum(jnp.where(is_nope, sq, 0.0), axis=-1, keepdims=True) * (1.0 / MLA_NOPE)
            ms_r = jnp.sum(jnp.where(is_nope, 0.0, sq), axis=-1, keepdims=True) * (1.0 / MLA_ROPE)
            inv = jnp.where(is_nope, lax.rsqrt(ms_n + NORM_EPS), lax.rsqrt(ms_r + NORM_EPS))
            qh = _rope(qh * inv * hq, c, s1, s2) * scale
            q_ref[:, h * HEAD_PAD:(h + 1) * HEAD_PAD] = qh.astype(BF16)


def _mla_pre(p, b, t_seq, gq, gkv, wq, hq, hk, tabs, transposed):
    n = p.shape[0]
    tm = _row_tile(n, 512)
    c, s1, s2 = tabs
    if tm > t_seq:
        c, s1, s2 = (jnp.tile(t, (tm // t_seq, 1)) for t in (c, s1, s2))
    n_tab = c.shape[0] // tm
    tab_spec = pl.BlockSpec((tm, HEAD_PAD), lambda i: (i % n_tab, 0))
    vec = lambda w: pl.BlockSpec((1, w), lambda i: (0, 0))
    scale = MLA_QK_HEAD ** -0.5
    extra, extra_specs = (), []
    if transposed:
        scale *= LOG2_E
        q_spec = pl.BlockSpec((1, MLA_HEADS, HEAD_PAD, tm), lambda i: (i // n_tab, 0, 0, i % n_tab))
        q_shape = jax.ShapeDtypeStruct((b, MLA_HEADS, HEAD_PAD, t_seq), BF16)
        hqt = jnp.broadcast_to((hq[0] * scale)[:, None], (HEAD_PAD, tm))
        extra = (hqt, c.T, s1.T, s2.T)
        tab_t_spec = pl.BlockSpec((HEAD_PAD, tm), lambda i: (0, i % n_tab))
        extra_specs = [pl.BlockSpec((HEAD_PAD, tm), lambda i: (0, 0)), tab_t_spec, tab_t_spec, tab_t_spec]
    else:
        q_spec = pl.BlockSpec((tm, QK_W), lambda i: (i, 0))
        q_shape = jax.ShapeDtypeStruct((n, QK_W), BF16)
    return pl.pallas_call(
        functools.partial(_mla_pre_kernel, scale=scale, transposed=transposed),
        grid=(n // tm,),
        in_specs=[pl.BlockSpec((tm, 1024), lambda i: (i, 0)),
                  vec(MLA_Q_LORA), vec(MLA_KV_LORA),
                  pl.BlockSpec((MLA_Q_LORA, QK_W), lambda i: (0, 0)),
                  vec(HEAD_PAD), vec(HEAD_PAD), tab_spec, tab_spec, tab_spec, *extra_specs],
        out_specs=[q_spec,
                   pl.BlockSpec((tm, MLA_KV_LORA), lambda i: (i, 0)),
                   pl.BlockSpec((tm, HEAD_PAD), lambda i: (i, 0))],
        out_shape=[q_shape,
                   jax.ShapeDtypeStruct((n, MLA_KV_LORA), F32),
                   jax.ShapeDtypeStruct((n, HEAD_PAD), F32)],
        compiler_params=_cparams(("parallel",)),
        name="mla_pre",
    )(p, gq, gkv, wq, hq, hk, c, s1, s2, *extra)


def _kv_up_kernel(ckv_ref, kr_ref, wk_ref, wv_ref, hk_ref, k_ref, v_ref, *, transposed):
    c = ckv_ref[...].astype(BF16)
    k = jnp.dot(c, wk_ref[...], preferred_element_type=F32)
    kr = kr_ref[...]
    hk = hk_ref[...]
    for h in range(MLA_HEADS):
        kh = k[:, h * HEAD_PAD:(h + 1) * HEAD_PAD]
        ms = jnp.sum(kh * kh, axis=-1, keepdims=True) * (1.0 / MLA_NOPE)
        k_ref[:, h * HEAD_PAD:(h + 1) * HEAD_PAD] = (kh * lax.rsqrt(ms + NORM_EPS) * hk + kr).astype(BF16)
    v = jnp.dot(c, wv_ref[...], preferred_element_type=F32)
    if transposed:
        lane = lax.broadcasted_iota(jnp.int32, (1, HEAD_PAD), 1)
        for h in range(MLA_HEADS):
            vh = jnp.where(lane < MLA_V, v[:, h * HEAD_PAD:(h + 1) * HEAD_PAD], 1.0)
            v_ref[0, h] = vh.T[0:VT_ROWS].astype(BF16)
    else:
        v_ref[...] = v.astype(BF16)


def _kv_up(ckv, kr, b, s_len, wk, wv, hk, transposed):
    n = ckv.shape[0]
    tm = _row_tile(n, 512)
    if transposed:
        n_t = s_len // tm
        v_spec = pl.BlockSpec((1, MLA_HEADS, VT_ROWS, tm), lambda i: (i // n_t, 0, 0, i % n_t))
        v_shape = jax.ShapeDtypeStruct((b, MLA_HEADS, VT_ROWS, s_len), BF16)
    else:
        v_spec = pl.BlockSpec((tm, MLA_VW), lambda i: (i, 0))
        v_shape = jax.ShapeDtypeStruct((n, MLA_VW), BF16)
    return pl.pallas_call(
        functools.partial(_kv_up_kernel, transposed=transposed),
        grid=(n // tm,),
        in_specs=[pl.BlockSpec((tm, MLA_KV_LORA), lambda i: (i, 0)),
                  pl.BlockSpec((tm, HEAD_PAD), lambda i: (i, 0)),
                  pl.BlockSpec((MLA_KV_LORA, QK_W), lambda i: (0, 0)),
                  pl.BlockSpec((MLA_KV_LORA, wv.shape[1]), lambda i: (0, 0)),
                  pl.BlockSpec((1, HEAD_PAD), lambda i: (0, 0))],
        out_specs=[pl.BlockSpec((tm, QK_W), lambda i: (i, 0)), v_spec],
        out_shape=[jax.ShapeDtypeStruct((n, QK_W), BF16), v_shape],
        compiler_params=_cparams(("parallel",)),
        name="kv_up",
    )(ckv, kr, wk, wv, hk)


def _kv_up_rows_kernel(ckv_ref, kr_ref, wk_ref, wv_ref, hk_ref, *rest):
    k_ref, v_ref = rest[-2:]
    _kv_up_kernel(ckv_ref.at[0], kr_ref.at[0], wk_ref, wv_ref, hk_ref, k_ref.at[0], v_ref.at[0], transposed=False)


def _kv_up_rows(ckv, kr, s_total, row0, wk, wv, hk, into=None):
    b, s_in, _ = ckv.shape
    tm = _row_tile(s_in, 512)
    blk0 = row0 // tm
    in_specs = [pl.BlockSpec((1, tm, MLA_KV_LORA), lambda i, j: (i, j, 0)),
                pl.BlockSpec((1, tm, HEAD_PAD), lambda i, j: (i, j, 0)),
                pl.BlockSpec((MLA_KV_LORA, QK_W), lambda i, j: (0, 0)),
                pl.BlockSpec((MLA_KV_LORA, MLA_VW), lambda i, j: (0, 0)),
                pl.BlockSpec((1, HEAD_PAD), lambda i, j: (0, 0))]
    args = [ckv, kr, wk, wv, hk]
    aliases = {}
    if into is not None:
        in_specs += [pl.BlockSpec(memory_space=pl.ANY), pl.BlockSpec(memory_space=pl.ANY)]
        args += list(into)
        aliases = {5: 0, 6: 1}
    return pl.pallas_call(
        _kv_up_rows_kernel,
        grid=(b, s_in // tm),
        in_specs=in_specs,
        out_specs=[pl.BlockSpec((1, tm, QK_W), lambda i, j: (i, blk0 + j, 0)),
                   pl.BlockSpec((1, tm, MLA_VW), lambda i, j: (i, blk0 + j, 0))],
        out_shape=[jax.ShapeDtypeStruct((b, s_total, QK_W), BF16),
                   jax.ShapeDtypeStruct((b, s_total, MLA_VW), BF16)],
        input_output_aliases=aliases,
        compiler_params=_cparams(("parallel", "parallel")),
        name="kv_up_rows",
    )(*args)


def _attn_kernel(q_ref, k_ref, v_ref, o_ref, m_ref, l_ref, acc_ref, *, tq, tk, causal):
    m_ref[...] = jnp.full(m_ref.shape, -jnp.inf, F32)
    l_ref[...] = jnp.zeros(l_ref.shape, F32)
    acc_ref[...] = jnp.zeros(acc_ref.shape, F32)

    def block(start, masked):
        if masked:
            rq = lax.broadcasted_iota(jnp.int32, (tq, tk), 0) // CHUNK
            ck = lax.broadcasted_iota(jnp.int32, (tq, tk), 1) // CHUNK
            allowed = ck <= rq
        def scores(h):
            qh = q_ref[0, :, h * HEAD_PAD:(h + 1) * HEAD_PAD]
            kh = k_ref[0, pl.ds(start, tk), h * HEAD_PAD:(h + 1) * HEAD_PAD]
            return lax.dot_general(qh, kh, (((1,), (1,)), ((), ())), preferred_element_type=F32)

        all_scores = [scores(h) for h in range(MLA_HEADS)]
        for h in range(MLA_HEADS):
            vh = v_ref[0, pl.ds(start, tk), h * MLA_V:(h + 1) * MLA_V]
            s = all_scores[h]
            if masked:
                s = jnp.where(allowed, s, -jnp.inf)
            m_prev = m_ref[h]
            m_new = jnp.maximum(m_prev, jnp.max(s, axis=-1, keepdims=True))
            alpha = jnp.exp(m_prev - m_new)
            p = jnp.exp(s - m_new)
            l_ref[h] = alpha * l_ref[h] + jnp.sum(p, axis=-1, keepdims=True)
            pv = jnp.dot(p.astype(BF16), vh, preferred_element_type=F32)
            acc_ref[h] = alpha * acc_ref[h] + pv
            m_ref[h] = m_new

    if causal:
        qi = pl.program_id(1)

        def body(j, carry):
            block(pl.multiple_of(j * tk, tk), False)
            return carry

        lax.fori_loop(0, qi, body, 0)
        block(pl.multiple_of(qi * tk, tk), True)
    else:
        block(0, False)

    for h in range(MLA_HEADS):
        o_ref[0, :, h * MLA_V:(h + 1) * MLA_V] = acc_ref[h] / l_ref[h]


def _attention(q, k, v, causal):
    b, t, _ = q.shape
    s = k.shape[1]
    if causal:
        tq = tk = min(256, t)
    else:
        tq, tk = t, s
    return pl.pallas_call(
        functools.partial(_attn_kernel, tq=tq, tk=tk, causal=causal),
        grid=(b, t // tq),
        in_specs=[pl.BlockSpec((1, tq, QK_W), lambda i, j: (i, j, 0)),
                  pl.BlockSpec((1, s, QK_W), lambda i, j: (i, 0, 0)),
                  pl.BlockSpec((1, s, MLA_VW), lambda i, j: (i, 0, 0))],
        out_specs=pl.BlockSpec((1, tq, MLA_VW), lambda i, j: (i, j, 0)),
        out_shape=jax.ShapeDtypeStruct((b, t, MLA_VW), F32),
        scratch_shapes=[pltpu.VMEM((MLA_HEADS, tq, 1), F32),
                        pltpu.VMEM((MLA_HEADS, tq, 1), F32),
                        pltpu.VMEM((MLA_HEADS, tq, MLA_V), F32)],
        compiler_params=_cparams(("parallel", "arbitrary")),
        name="mla_attn",
    )(q, k, v)


def _attn_t_kernel(qt_ref, k_ref, vt_ref, o_ref, m_ref, acc_ref, *, tq, tk):
    m_ref[...] = jnp.full(m_ref.shape, -jnp.inf, F32)
    acc_ref[...] = jnp.zeros(acc_ref.shape, F32)

    def blocks(starts, masked):
        if masked:
            ck = lax.broadcasted_iota(jnp.int32, (tk, tq), 0) // CHUNK
            cq = lax.broadcasted_iota(jnp.int32, (tk, tq), 1) // CHUNK
            allowed = ck <= cq

        def scores(start, h):
            kh = k_ref[0, pl.ds(start, tk), h * HEAD_PAD:(h + 1) * HEAD_PAD]
            return jnp.dot(kh, qt_ref[0, h], preferred_element_type=F32)

        def update(start, h, s):
            if masked:
                s = jnp.where(allowed, s, -jnp.inf)
            m_prev = m_ref[h]
            m_new = jnp.maximum(m_prev, jnp.max(s, axis=0, keepdims=True))
            alpha = jnp.exp2(m_prev - m_new)
            p = jnp.exp2(s - m_new).astype(BF16)
            pv = jnp.dot(vt_ref[0, h, :, pl.ds(start, tk)], p, preferred_element_type=F32)
            acc_ref[h] = alpha * acc_ref[h] + pv
            m_ref[h] = m_new

        items = [(start, h) for start in starts for h in range(MLA_HEADS)]
        ahead = 5
        pending = [scores(*it) for it in items[:ahead]]
        for n, it in enumerate(items):
            s = pending.pop(0)
            if n + ahead < len(items):
                pending.append(scores(*items[n + ahead]))
            update(*it, s)

    qi = pl.program_id(1)

    def body(j, carry):
        first = pl.multiple_of(2 * j * tk, tk)
        blocks([first, pl.multiple_of(first + tk, tk)], False)
        return carry

    lax.fori_loop(0, qi // 2, body, 0)

    @pl.when(qi % 2 == 1)
    def _():
        blocks([pl.multiple_of((qi - 1) * tk, tk)], False)

    blocks([pl.multiple_of(qi * tk, tk)], True)

    ot = jnp.concatenate([acc_ref[h, 0:MLA_V] / acc_ref[h, MLA_V:MLA_V + 1] for h in range(MLA_HEADS)], axis=0)
    o_ref[0] = ot.T


def _attention_t(qt, k, vt):
    b, _, _, t = qt.shape
    s = k.shape[1]
    tq = tk = min(256, t)
    return pl.pallas_call(
        functools.partial(_attn_t_kernel, tq=tq, tk=tk),
        grid=(b, t // tq),
        in_specs=[pl.BlockSpec((1, MLA_HEADS, HEAD_PAD, tq), lambda i, j: (i, 0, 0, j)),
                  pl.BlockSpec((1, s, QK_W), lambda i, j: (i, 0, 0)),
                  pl.BlockSpec((1, MLA_HEADS, VT_ROWS, s), lambda i, j: (i, 0, 0, 0))],
        out_specs=pl.BlockSpec((1, tq, MLA_VW), lambda i, j: (i, j, 0)),
        out_shape=jax.ShapeDtypeStruct((b, t, MLA_VW), F32),
        scratch_shapes=[pltpu.VMEM((MLA_HEADS, 1, tq), F32),
                        pltpu.VMEM((MLA_HEADS, VT_ROWS, tq), F32)],
        compiler_params=_cparams(("parallel", "arbitrary")),
        name="mla_attn_t",
    )(qt, k, vt)


def _row_iota(shape):
    return lax.broadcasted_iota(jnp.int32, shape, 0)


def _upper_half_masks(shape):
    row = _row_iota(shape)
    masks = []
    m = 1
    while m < CHUNK:
        masks.append((row // m) % 2 == 1)
        m *= 2
    return masks


def _segment_scans(g, uppers):
    rows = g.shape[0]
    pre, tot = g, g
    out = [(pre, tot)]
    m = 1
    while m < CHUNK:
        upper = uppers[len(out) - 1]
        from_lower = pltpu.roll(tot, m, 0)
        from_upper = pltpu.roll(tot, rows - m, 0)
        pre = pre + jnp.where(upper, from_lower, 0.0)
        tot = tot + jnp.where(upper, from_lower, from_upper)
        out.append((pre, tot))
        m *= 2
    return out


def _dot_nt(a, b):
    return lax.dot_general(a, b, (((1,), (1,)), ((), ())), preferred_element_type=F32)


def _dot_tn(a, b):
    return lax.dot_general(a, b, (((0,), (0,)), ((), ())), preferred_element_type=F32)


def _gdn_prep_kernel(q_ref, k_ref, v_ref, ab_ref, cw_ref, cs_ref, alog_ref, dt_ref,
                     u_ref, w_ref, qd_ref, kd_ref, qk_ref, gl_ref, carry_ref, *, nb, r):
    @pl.when(pl.program_id(1) == 0)
    def _():
        carry_ref[...] = cs_ref[...]

    n = nb * r
    nc = n // CHUNK

    def conv(x_ref, j):
        w = cw_ref[:, j * GDN_QK:(j + 1) * GDN_QK]
        outs = []
        for b in range(nb):
            x = x_ref[b]
            xp = jnp.concatenate([carry_ref[b, j], x], axis=0)
            y = x * w[3:4]
            for d in range(1, CONV_W):
                y = y + xp[8 - d:8 - d + r] * w[3 - d:4 - d]
            carry_ref[b, j] = x[r - 8:]
            outs.append(_silu(y))
        return outs[0] if nb == 1 else jnp.concatenate(outs, axis=0)

    q_all, k_all, v_all = conv(q_ref, 0), conv(k_ref, 1), conv(v_ref, 2)

    ab = ab_ref[...].reshape(n, 128)
    x = ab + dt_ref[...]
    softplus = jnp.maximum(x, 0.0) + jnp.log(1.0 + jnp.exp(-jnp.abs(x)))
    g_blk = (-LOG2_E) * jnp.exp(alog_ref[...]) * softplus
    gam_blk = _segment_scans(g_blk, _upper_half_masks(g_blk.shape))[-1][0]
    gam_t = gam_blk.T
    gam3_blk = gam_blk.reshape(nc, CHUNK, 128)
    beta3_blk = _sigmoid(ab).reshape(nc, CHUNK, 128)

    row = lax.broadcasted_iota(jnp.int32, (1, CHUNK, CHUNK), 1)
    col = lax.broadcasted_iota(jnp.int32, (1, CHUNK, CHUNK), 2)
    eye = (row == col).astype(F32)

    def bmm(a, b):
        return jnp.einsum('cij,cjk->cik', a.astype(BF16), b.astype(BF16), preferred_element_type=F32)

    def bmm_nt(a, b):
        return jnp.einsum('cid,cjd->cij', a.astype(BF16), b.astype(BF16), preferred_element_type=F32)

    for h in range(GDN_HEADS):
        sl = slice(h * GDN_DK, (h + 1) * GDN_DK)
        q, k, v = q_all[:, sl], k_all[:, sl], v_all[:, sl]
        q = q * lax.rsqrt(jnp.sum(q * q, axis=-1, keepdims=True) + NORM_EPS) * (GDN_DK ** -0.5)
        k = k * lax.rsqrt(jnp.sum(k * k, axis=-1, keepdims=True) + NORM_EPS)
        q, k, v = (a.reshape(nc, CHUNK, GDN_DK) for a in (q, k, v))
        gam = gam3_blk[:, :, h:h + 1]
        beta = beta3_blk[:, :, GDN_HEADS + h:GDN_HEADS + h + 1]
        gam_row = jnp.stack([gam_t[h:h + 1, c * CHUNK:(c + 1) * CHUNK] for c in range(nc)], axis=0)
        decay = jnp.where(row >= col, jnp.exp2(jnp.minimum(gam - gam_row, 0.0)), 0.0)
        a = jnp.where(row > col, beta * bmm_nt(k, k) * decay, 0.0)
        t_inv = eye - a
        pw = a
        m = 1
        while 2 * m < CHUNK:
            pw = bmm(pw, pw)
            t_inv = t_inv + bmm(t_inv, pw)
            m *= 2
        e_gam = jnp.exp2(gam)
        gam_last = gam[:, CHUNK - 1:CHUNK, :]
        u_ref[:, h] = bmm(t_inv, v * beta).reshape(nb, r, GDN_DV)
        w_ref[:, h] = bmm(t_inv, k * (beta * e_gam)).astype(BF16).reshape(nb, r, GDN_DK)
        qd_ref[:, h] = (q * e_gam).astype(BF16).reshape(nb, r, GDN_DK)
        kd_ref[:, h] = (k * jnp.exp2(gam_last - gam)).astype(BF16).reshape(nb, r, GDN_DK)
        qk_ref[:, h] = (bmm_nt(q, k) * decay).astype(BF16).reshape(nb, r, CHUNK)
        gl_ref[:, h] = jnp.broadcast_to(jnp.exp2(gam_last), (nc, 1, 128)).reshape(nb, r // CHUNK, 1, 128)


def _gdn_scan_kernel(u_ref, w_ref, qd_ref, kd_ref, qk_ref, gl_ref, s0_ref, o_ref, sf_ref, s_ref, *, nb):
    c_idx = pl.program_id(1)

    @pl.when(c_idx == 0)
    def _():
        s_ref[...] = s0_ref[...]

    chains = [(b, h) for b in range(nb) for h in range(GDN_HEADS)]
    s_old = [s_ref[b, h] for b, h in chains]
    s_bf = [s.astype(BF16) for s in s_old]
    v_new = [u_ref[b, h] - jnp.dot(w_ref[b, h], sb, preferred_element_type=F32)
             for (b, h), sb in zip(chains, s_bf)]
    v_bf = [v.astype(BF16) for v in v_new]
    for (b, h), s, sb, vb in zip(chains, s_old, s_bf, v_bf):
        o = (jnp.dot(qd_ref[b, h], sb, preferred_element_type=F32)
             + jnp.dot(qk_ref[b, h], vb, preferred_element_type=F32))
        o_ref[b, :, h * GDN_DV:(h + 1) * GDN_DV] = o
        s_ref[b, h] = gl_ref[b, h, 0] * s + _dot_tn(kd_ref[b, h], vb)

    @pl.when(c_idx == pl.num_programs(1) - 1)
    def _():
        sf_ref[...] = s_ref[...]


def _gdn_prep(p3, conv_w, conv_state8, alog, dt):
    b, t, _ = p3.shape
    r = min(GDN_PREP_ROWS, t)
    nb = max(1, min(b, GDN_PREP_ROWS // r))
    nt = t // r
    nc = t // CHUNK
    blk = lambda j: pl.BlockSpec((nb, r, GDN_QK), lambda i, c: (i, c, COL_GDN // GDN_QK + j))
    vec = pl.BlockSpec((1, 128), lambda i, c: (0, 0))
    head_spec = lambda w: pl.BlockSpec((nb, GDN_HEADS, r, w), lambda i, c: (i, 0, c, 0))
    head_shape = lambda w, dt_: jax.ShapeDtypeStruct((b, GDN_HEADS, t, w), dt_)
    u, w, qd, kd, qk, gl = pl.pallas_call(
        functools.partial(_gdn_prep_kernel, nb=nb, r=r),
        grid=(b // nb, nt),
        in_specs=[blk(0), blk(1), blk(2),
                  pl.BlockSpec((nb, r, 128), lambda i, c: (i, c, COL_GAB // 128)),
                  pl.BlockSpec((CONV_W, 3 * GDN_QK), lambda i, c: (0, 0)),
                  pl.BlockSpec((nb, 3, 8, GDN_QK), lambda i, c: (i, 0, 0, 0)),
                  vec, vec],
        out_specs=[head_spec(GDN_DV), head_spec(GDN_DK), head_spec(GDN_DK), head_spec(GDN_DK),
                   head_spec(CHUNK),
                   pl.BlockSpec((nb, GDN_HEADS, r // CHUNK, 1, 128), lambda i, c: (i, 0, c, 0, 0))],
        out_shape=[head_shape(GDN_DV, F32), head_shape(GDN_DK, BF16), head_shape(GDN_DK, BF16),
                   head_shape(GDN_DK, BF16), head_shape(CHUNK, BF16),
                   jax.ShapeDtypeStruct((b, GDN_HEADS, nc, 1, 128), F32)],
        scratch_shapes=[pltpu.VMEM((nb, 3, 8, GDN_QK), F32)],
        compiler_params=_cparams(("parallel", "arbitrary")),
        name="gdn_prep",
    )(p3, p3, p3, p3, conv_w, conv_state8, alog, dt)
    return u, w, qd, kd, qk, gl


def _hgrn_prep_kernel(q_ref, f_ref, lb_ref, att_ref, qe_ref, ke_ref, dec_ref, *, nb, r):
    n = nb * r
    nc = n // CHUNK
    row = lax.broadcasted_iota(jnp.int32, (1, CHUNK, CHUNK), 1)
    col = lax.broadcasted_iota(jnp.int32, (1, CHUNK, CHUNK), 2)
    uppers = _upper_half_masks((n, HGRN_DK))
    halves = [1 << lvl for lvl in range(len(uppers))]
    pairs = [(row // (2 * m) == col // (2 * m)) & ((row // m) % 2 == 1) & ((col // m) % 2 == 0) for m in halves]

    def bmm_nt(a, b):
        a3, b3 = (x.astype(BF16).reshape(nc, CHUNK, HGRN_DK) for x in (a, b))
        return jnp.einsum('cid,cjd->cij', a3, b3, preferred_element_type=F32)

    for h in range(HGRN_HEADS):
        sl = slice(h * HGRN_DK, (h + 1) * HGRN_DK)
        lb = lb_ref[:, sl]
        f = lb + (1.0 - lb) * _sigmoid(f_ref[:, :, sl].reshape(n, HGRN_DK))
        q = _silu(q_ref[:, :, sl].reshape(n, HGRN_DK)) * (HGRN_DK ** -0.5)
        k = 1.0 - f
        scans = _segment_scans(jnp.log2(f), uppers)
        cb, c_tot = scans[-1]

        att = jnp.where(row == col, bmm_nt(q, k), 0.0)
        for lvl in range(len(halves)):
            pre_m, tot_m = scans[lvl]
            att = att + jnp.where(pairs[lvl], bmm_nt(q * jnp.exp2(pre_m), k * jnp.exp2(tot_m - pre_m)), 0.0)

        att_ref[:, h] = att.astype(BF16).reshape(nb, r, CHUNK)
        qe_ref[:, h] = (q * jnp.exp2(cb)).astype(BF16).reshape(nb, r, HGRN_DK)
        ke_ref[:, h] = (k * jnp.exp2(c_tot - cb)).astype(BF16).reshape(nb, r, HGRN_DK)
        dec_ref[:, h] = jnp.exp2(c_tot).reshape(nc, CHUNK, HGRN_DK)[:, 0:1, :].reshape(nb, r // CHUNK, 1, HGRN_DK)


def _hgrn_scan_kernel(att_ref, qe_ref, ke_ref, dec_ref, v_ref, s0_ref, o_ref, sf_ref, st_ref, *, nb):
    c_idx = pl.program_id(1)

    @pl.when(c_idx == 0)
    def _():
        st_ref[...] = s0_ref[...]

    chains = [(b, h) for b in range(nb) for h in range(HGRN_HEADS)]
    sls = [slice(h * HGRN_DV, (h + 1) * HGRN_DV) for h in range(HGRN_HEADS)]
    st_old = [st_ref[b, h] for b, h in chains]
    vs = [v_ref[b, :, sls[h]].astype(BF16) for b, h in chains]
    o_st = [_dot_nt(qe_ref[b, h], st.astype(BF16)) for (b, h), st in zip(chains, st_old)]
    for (b, h), st, v, o1 in zip(chains, st_old, vs, o_st):
        o_ref[b, :, sls[h]] = o1 + jnp.dot(att_ref[b, h], v, preferred_element_type=F32)
        st_ref[b, h] = st * dec_ref[b, h, 0] + _dot_tn(v, ke_ref[b, h])

    @pl.when(c_idx == pl.num_programs(1) - 1)
    def _():
        sf_ref[...] = st_ref[...]


def _hgrn_prep(p3, lb):
    b, t, _ = p3.shape
    r = min(HGRN_PREP_ROWS, t)
    nb = max(1, min(b, HGRN_PREP_ROWS // r))
    nc = t // CHUNK
    col0 = (COL_GDN + 4 * GDN_QK) // HGRN_KW
    blk = lambda j: pl.BlockSpec((nb, r, HGRN_KW), lambda i, c: (i, c, col0 + j))
    head_spec = lambda w: pl.BlockSpec((nb, HGRN_HEADS, r, w), lambda i, c: (i, 0, c, 0))
    head_shape = lambda w: jax.ShapeDtypeStruct((b, HGRN_HEADS, t, w), BF16)
    att, qe, ke, dec = pl.pallas_call(
        functools.partial(_hgrn_prep_kernel, nb=nb, r=r),
        grid=(b // nb, t // r),
        in_specs=[blk(0), blk(1), pl.BlockSpec((1, HGRN_KW), lambda i, c: (0, 0))],
        out_specs=[head_spec(CHUNK), head_spec(HGRN_DK), head_spec(HGRN_DK),
                   pl.BlockSpec((nb, HGRN_HEADS, r // CHUNK, 1, HGRN_DK), lambda i, c: (i, 0, c, 0, 0))],
        out_shape=[head_shape(CHUNK), head_shape(HGRN_DK), head_shape(HGRN_DK),
                   jax.ShapeDtypeStruct((b, HGRN_HEADS, nc, 1, HGRN_DK), F32)],
        compiler_params=_cparams(("parallel", "parallel")),
        name="hgrn_prep",
    )(p3, p3, lb)
    return att, qe, ke, dec


def _rec_scan_kernel(u_ref, w_ref, qd_ref, kd_ref, qk_ref, gl_ref, sg0_ref,
                     att_ref, qe_ref, ke_ref, dec_ref, v_ref, sh0_ref,
                     og_ref, sgf_ref, oh_ref, shf_ref, sg_ref, sh_ref, *, nb):
    _gdn_scan_kernel(u_ref, w_ref, qd_ref, kd_ref, qk_ref, gl_ref, sg0_ref, og_ref, sgf_ref, sg_ref, nb=nb)
    _hgrn_scan_kernel(att_ref, qe_ref, ke_ref, dec_ref, v_ref, sh0_ref, oh_ref, shf_ref, sh_ref, nb=nb)


def _rec_scan(gdn_parts, hgrn_parts, p3, s0_gdn, s0t_hgrn):
    b, t, _ = p3.shape
    nc = t // CHUNK
    sb = min(b, 8)
    col_v = (COL_GDN + 4 * GDN_QK) // HGRN_KW + 2
    chunk_spec = lambda w: pl.BlockSpec((sb, GDN_HEADS, CHUNK, w), lambda i, c: (i, 0, c, 0))
    per_chunk = pl.BlockSpec((sb, GDN_HEADS, 1, 1, 128), lambda i, c: (i, 0, c, 0, 0))
    state_spec = pl.BlockSpec((sb, GDN_HEADS, 128, 128), lambda i, c: (i, 0, 0, 0))
    out_spec = pl.BlockSpec((sb, CHUNK, GDN_VW), lambda i, c: (i, c, 0))
    state_shape = jax.ShapeDtypeStruct((b, GDN_HEADS, 128, 128), F32)
    return pl.pallas_call(
        functools.partial(_rec_scan_kernel, nb=sb),
        grid=(b // sb, nc),
        in_specs=[chunk_spec(GDN_DV), chunk_spec(GDN_DK), chunk_spec(GDN_DK), chunk_spec(GDN_DK),
                  chunk_spec(CHUNK), per_chunk, state_spec,
                  chunk_spec(CHUNK), chunk_spec(HGRN_DK), chunk_spec(HGRN_DK), per_chunk,
                  pl.BlockSpec((sb, CHUNK, HGRN_KW), lambda i, c: (i, c, col_v)), state_spec],
        out_specs=[out_spec, state_spec, out_spec, state_spec],
        out_shape=[jax.ShapeDtypeStruct((b, t, GDN_VW), F32), state_shape,
                   jax.ShapeDtypeStruct((b, t, HGRN_KW), F32), state_shape],
        scratch_shapes=[pltpu.VMEM((sb, GDN_HEADS, 128, 128), F32), pltpu.VMEM((sb, HGRN_HEADS, 128, 128), F32)],
        compiler_params=_cparams(("parallel", "arbitrary")),
        name="rec_scan",
    )(*gdn_parts, s0_gdn, *hgrn_parts, p3, s0t_hgrn)


def _merge_kernel(x_ref, oa_ref, ob_ref, zb_ref, oc_ref, zc_ref, g0_ref, g1_ref, g2_ref, ngb_ref, ngc_ref,
                  wa_ref, wb_ref, wc_ref, wo_ref, y_ref):
    def branch(o, w_ref, g_ref):
        gate = _sigmoid(g_ref[...].astype(F32))
        return gate * jnp.dot(o.astype(BF16), w_ref[...], preferred_element_type=F32)

    def normed(o_ref, z_ref, ng_ref):
        return jnp.concatenate(
            [_rms(o_ref[:, h * 128:(h + 1) * 128], ng_ref[...]) * _silu(z_ref[:, h * 128:(h + 1) * 128])
             for h in range(GDN_HEADS)], axis=-1)

    mixed = (branch(oa_ref[...], wa_ref, g0_ref) + branch(normed(ob_ref, zb_ref, ngb_ref), wb_ref, g1_ref)
             + branch(normed(oc_ref, zc_ref, ngc_ref), wc_ref, g2_ref))
    y_ref[...] = x_ref[...] + jnp.dot(mixed.astype(BF16), wo_ref[...], preferred_element_type=F32)


def _merge(x, oa, ob, oc, p, pg, ngb, ngc, wa, wb, wc, wo):
    n = x.shape[0]
    tm = _row_tile(n, 512)
    row = lambda w: pl.BlockSpec((tm, w), lambda i: (i, 0))
    gate = lambda j: pl.BlockSpec((tm, D_MODEL), lambda i: (i, j))
    zb_spec = pl.BlockSpec((tm, GDN_VW), lambda i: (i, COL_GDN // GDN_VW + 3))
    zc_spec = pl.BlockSpec((tm, HGRN_KW), lambda i: (i, (COL_GDN + 4 * GDN_QK) // HGRN_KW + 3))
    vec = pl.BlockSpec((1, 128), lambda i: (0, 0))
    wsp = lambda k: pl.BlockSpec((k, D_MODEL), lambda i: (0, 0))
    return pl.pallas_call(
        _merge_kernel,
        grid=(n // tm,),
        in_specs=[row(D_MODEL), row(MLA_VW), row(GDN_VW), zb_spec, row(HGRN_KW), zc_spec,
                  gate(0), gate(1), gate(2), vec, vec,
                  wsp(MLA_VW), wsp(GDN_VW), wsp(HGRN_KW), wsp(D_MODEL)],
        out_specs=row(D_MODEL),
        out_shape=jax.ShapeDtypeStruct((n, D_MODEL), F32),
        compiler_params=_cparams(("parallel",)),
        name="merge_out",
    )(x, oa, ob, p, oc, p, pg, pg, pg, ngb, ngc, wa, wb, wc, wo)


def _router_kernel(x_ref, g_ref, w_ref, b_ref, cw_ref, cwt_ref, cnt_ref):
    h = _rms(x_ref[...], g_ref[...])
    logits = jnp.dot(h, w_ref[...], preferred_element_type=F32, precision=lax.Precision.HIGHEST) + b_ref[...]
    lane = lax.broadcasted_iota(jnp.int32, logits.shape, 1)
    valid = lane < N_EXPERTS
    neg = -jnp.inf
    l1 = jnp.where(valid, logits, neg)
    m1 = jnp.max(l1, axis=-1, keepdims=True)
    i1 = jnp.min(jnp.where(l1 == m1, lane, 128), axis=-1, keepdims=True)
    l2 = jnp.where(lane == i1, neg, l1)
    m2 = jnp.max(l2, axis=-1, keepdims=True)
    i2 = jnp.min(jnp.where(l2 == m2, lane, 128), axis=-1, keepdims=True)
    e2 = jnp.exp(m2 - m1)
    den = 1.0 + e2
    cw = jnp.where(lane == i1, 1.0 / den, 0.0) + jnp.where(lane == i2, e2 / den, 0.0)
    cw_ref[...] = cw
    cwt_ref[...] = cw.T
    cnt_ref[0] = jnp.sum((cw > 0.0).astype(F32), axis=0, keepdims=True)


def _router(x, g, w, b, tm):
    n = x.shape[0]
    return pl.pallas_call(
        _router_kernel,
        grid=(n // tm,),
        in_specs=[pl.BlockSpec((tm, D_MODEL), lambda i: (i, 0)),
                  pl.BlockSpec((1, D_MODEL), lambda i: (0, 0)),
                  pl.BlockSpec((D_MODEL, 128), lambda i: (0, 0)),
                  pl.BlockSpec((1, 128), lambda i: (0, 0))],
        out_specs=[pl.BlockSpec((tm, 128), lambda i: (i, 0)),
                   pl.BlockSpec((128, tm), lambda i: (0, i)),
                   pl.BlockSpec((1, 1, 128), lambda i: (i, 0, 0))],
        out_shape=[jax.ShapeDtypeStruct((n, 128), F32),
                   jax.ShapeDtypeStruct((128, n), F32),
                   jax.ShapeDtypeStruct((n // tm, 1, 128), F32)],
        compiler_params=_cparams(("parallel",)),
        name="moe_router",
    )(x, g, w, b)


def _moe_kernel(cnt_ref, x_ref, g_ref, cw_ref, cwt_ref, wg_ref, wu_ref, wd_ref, y_ref,
                h_ref, rcol_ref, rrow_ref, *, t, br):
    i, e = pl.program_id(0), pl.program_id(1)

    @pl.when(e == 0)
    def _():
        x = x_ref[...]
        h_ref[...] = _rms(x, g_ref[...]).astype(BF16)
        y_ref[...] = x
        r = lax.broadcasted_iota(jnp.int32, (t, t), 0)
        c = lax.broadcasted_iota(jnp.int32, (t, t), 1)
        on = cw_ref[...] > 0.0
        rank = jnp.dot((c < r).astype(BF16), on.astype(BF16), preferred_element_type=F32)
        rcol_ref[...] = jnp.where(on, rank.astype(jnp.int32), -1)
        on_t = cwt_ref[...] > 0.0
        rank_t = jnp.dot(on_t.astype(BF16), (r < c).astype(BF16), preferred_element_type=F32)
        rrow_ref[...] = jnp.where(on_t, rank_t.astype(jnp.int32), -1)

    count = cnt_ref[i * N_EXPERTS + e]
    sel = lax.broadcasted_iota(jnp.int32, (1, 128), 1) == e
    rank_c = jnp.sum(jnp.where(sel, rcol_ref[...], 0), axis=-1, keepdims=True)
    rank_r = rrow_ref[pl.ds(e, 1), :]
    w_r = cwt_ref[pl.ds(e, 1), :]

    def expert_block(base, rows_n):
        rows = lax.broadcasted_iota(jnp.int32, (rows_n, t), 0) + base
        pick = rows == rank_r
        xc = jnp.dot(pick.astype(BF16), h_ref[...], preferred_element_type=F32).astype(BF16)
        a = jnp.dot(xc, wg_ref[0], preferred_element_type=F32)
        b = jnp.dot(xc, wu_ref[0], preferred_element_type=F32)
        yc = jnp.dot((_silu(a) * b).astype(BF16), wd_ref[0], preferred_element_type=F32)
        w_rows = jnp.sum(jnp.where(pick, w_r, 0.0), axis=-1, keepdims=True)
        cols = lax.broadcasted_iota(jnp.int32, (t, rows_n), 1) + base
        y_ref[...] += jnp.dot((cols == rank_c).astype(BF16), (yc * w_rows).astype(BF16),
                              preferred_element_type=F32)

    lo = 0
    for size in br:
        @pl.when((count > lo) & (count <= size))
        def _(size=size):
            expert_block(0, size)
        lo = size

    @pl.when(count > br[-1])
    def _():
        def body(j, carry):
            expert_block(j * br[-1], br[-1])
            return carry

        lax.fori_loop(0, lax.div(count + (br[-1] - 1), br[-1]), body, 0)


def _moe(x, g, cw, cwt, counts, wg, wu, wd, t):
    n = x.shape[0]
    ne, _, ff = wg.shape
    br = MOE_BLOCK_ROWS
    grid_spec = pltpu.PrefetchScalarGridSpec(
        num_scalar_prefetch=1,
        grid=(n // t, ne),
        in_specs=[pl.BlockSpec((t, D_MODEL), lambda i, e, cnt: (i, 0)),
                  pl.BlockSpec((1, D_MODEL), lambda i, e, cnt: (0, 0)),
                  pl.BlockSpec((t, 128), lambda i, e, cnt: (i, 0)),
                  pl.BlockSpec((128, t), lambda i, e, cnt: (0, i)),
                  pl.BlockSpec((1, D_MODEL, ff), lambda i, e, cnt: (e, 0, 0)),
                  pl.BlockSpec((1, D_MODEL, ff), lambda i, e, cnt: (e, 0, 0)),
                  pl.BlockSpec((1, ff, D_MODEL), lambda i, e, cnt: (e, 0, 0))],
        out_specs=pl.BlockSpec((t, D_MODEL), lambda i, e, cnt: (i, 0)),
        scratch_shapes=[pltpu.VMEM((t, D_MODEL), BF16),
                        pltpu.VMEM((t, 128), jnp.int32),
                        pltpu.VMEM((128, t), jnp.int32)])
    return pl.pallas_call(
        functools.partial(_moe_kernel, t=t, br=br),
        grid_spec=grid_spec,
        out_shape=jax.ShapeDtypeStruct((n, D_MODEL), F32),
        compiler_params=_cparams(("parallel", "arbitrary")),
        name="moe",
    )(counts, x, g, cw, cwt, wg, wu, wd)


def _ffn_kernel(x_ref, g_ref, wg_ref, wu_ref, wd_ref, y_ref, h_ref):
    e = pl.program_id(1)

    @pl.when(e == 0)
    def _():
        x = x_ref[...]
        h_ref[...] = _rms(x, g_ref[...]).astype(BF16)
        y_ref[...] = x

    h = h_ref[...]
    a = jnp.dot(h, wg_ref[0], preferred_element_type=F32)
    b = jnp.dot(h, wu_ref[0], preferred_element_type=F32)
    y_ref[...] += jnp.dot((_silu(a) * b).astype(BF16), wd_ref[0], preferred_element_type=F32)


def _ffn(x, g, wg, wu, wd):
    n = x.shape[0]
    ne, _, ff = wg.shape
    tm = _row_tile(n, 512)
    return pl.pallas_call(
        _ffn_kernel,
        grid=(n // tm, ne),
        in_specs=[pl.BlockSpec((tm, D_MODEL), lambda i, e: (i, 0)),
                  pl.BlockSpec((1, D_MODEL), lambda i, e: (0, 0)),
                  pl.BlockSpec((1, D_MODEL, ff), lambda i, e: (e, 0, 0)),
                  pl.BlockSpec((1, D_MODEL, ff), lambda i, e: (e, 0, 0)),
                  pl.BlockSpec((1, ff, D_MODEL), lambda i, e: (e, 0, 0))],
        out_specs=pl.BlockSpec((tm, D_MODEL), lambda i, e: (i, 0)),
        out_shape=jax.ShapeDtypeStruct((n, D_MODEL), F32),
        scratch_shapes=[pltpu.VMEM((tm, D_MODEL), BF16)],
        compiler_params=_cparams(("parallel", "arbitrary")),
        name="ffn",
    )(x, g, wg, wu, wd)


def _pad_lanes(x, left, total):
    return jnp.pad(x, [(0, 0)] * (x.ndim - 1) + [(left, total - left - x.shape[-1])])


def _pack_w_in(w):
    cq, ckv, kr, gq, gk, gv, gz, ga, gb, hq, hf, hi, hg, gates = jnp.split(
        w, np.cumsum(SPLIT_SIZES)[:-1].tolist(), axis=-1)
    kr_blk = _pad_lanes(kr, ROPE_LANE0, 128)
    ab_blk = _pad_lanes(jnp.concatenate([ga, gb], axis=-1), 0, 256)
    return jnp.concatenate([cq, ckv, kr_blk, ab_blk, gq, gk, gv, gz, hq, hf, hi, hg, gates], axis=-1).astype(BF16)


def _rope_tables(n_pos):
    half = MLA_ROPE // 2
    inv = ROPE_THETA ** (-jnp.arange(half, dtype=F32) / half)
    ang = jnp.arange(n_pos, dtype=F32)[:, None] * inv[None, :]
    cos, sin = jnp.cos(ang), jnp.sin(ang)
    one = jnp.ones((n_pos, MLA_NOPE), F32)
    zero = jnp.zeros((n_pos, MLA_NOPE), F32)
    tail = jnp.zeros((n_pos, HEAD_PAD - MLA_QK_HEAD), F32)
    z16 = jnp.zeros((n_pos, half), F32)
    c = jnp.concatenate([one, cos, cos, tail], axis=-1)
    s1 = jnp.concatenate([zero, -sin, z16, tail], axis=-1)
    s2 = jnp.concatenate([zero, z16, sin, tail], axis=-1)
    return c, s1, s2


def _layer_weights(l, a):
    f = {}
    f['mixer_g'] = a['mixer_norm_g'][l][None]
    f['w_in'] = _pack_w_in(a['w_in'][l])
    f['gq'] = a['mla_q_norm_g'][l][None]
    f['gkv'] = a['mla_kv_norm_g'][l][None]
    wq = a['mla_w_q_up'][l].reshape(MLA_Q_LORA, MLA_HEADS, MLA_QK_HEAD)
    f['wq'] = _pad_lanes(wq, 0, HEAD_PAD).reshape(MLA_Q_LORA, QK_W).astype(BF16)
    wkv = a['mla_w_kv_up'][l].reshape(MLA_KV_LORA, MLA_HEADS, MLA_NOPE + MLA_V)
    f['wk'] = _pad_lanes(wkv[:, :, :MLA_NOPE], 0, HEAD_PAD).reshape(MLA_KV_LORA, QK_W).astype(BF16)
    f['wv'] = wkv[:, :, MLA_NOPE:].reshape(MLA_KV_LORA, MLA_VW).astype(BF16)
    f['wv_pad'] = _pad_lanes(wkv[:, :, MLA_NOPE:], 0, HEAD_PAD).reshape(MLA_KV_LORA, QK_W).astype(BF16)
    f['hq'] = _pad_lanes(a['mla_q_head_norm_g'][l][None], 0, HEAD_PAD)
    hk = a['mla_k_head_norm_g'][l][None]
    f['hk_nope'] = _pad_lanes(hk[:, :MLA_NOPE], 0, HEAD_PAD)
    f['hk_rope'] = _pad_lanes(hk[:, MLA_NOPE:], ROPE_LANE0, HEAD_PAD)
    f['wo_a'] = a['mla_w_o'][l].astype(BF16)
    f['conv_w'] = a['gdn_conv_w'][l]
    f['alog'] = _pad_lanes(a['gdn_a_log'][l][None], 0, 128)
    f['dt'] = _pad_lanes(a['gdn_dt_bias'][l][None], 0, 128)
    f['gdn_g'] = a['gdn_norm_g'][l][None]
    f['wo_b'] = a['gdn_w_o'][l].astype(BF16)
    f['hgrn_g'] = a['hgrn_norm_g'][l][None]
    f['wo_c'] = a['hgrn_w_o'][l].astype(BF16)
    f['w_out'] = a['w_out'][l].astype(BF16)
    f['ffn_g'] = a['ffn_norm_g'][l][None]
    if l % 2 == 0:
        wg, wu, wd = a['dense_w_gate'][l // 2], a['dense_w_up'][l // 2], a['dense_w_down'][l // 2]
        ff = wg.shape[1]
        half = ff // 2
        f['ffn'] = (jnp.moveaxis(wg.reshape(D_MODEL, 2, half), 1, 0).astype(BF16),
                    jnp.moveaxis(wu.reshape(D_MODEL, 2, half), 1, 0).astype(BF16),
                    wd.reshape(2, half, D_MODEL).astype(BF16))
        f['router'] = None
    else:
        f['ffn'] = (a['moe_w_gate'][l // 2].astype(BF16), a['moe_w_up'][l // 2].astype(BF16),
                    a['moe_w_down'][l // 2].astype(BF16))
        f['router'] = (_pad_lanes(a['moe_w_router'][l // 2], 0, 128),
                       _pad_lanes(a['moe_b_router'][l // 2][None], 0, 128))
    return f


def _trunk_layer(x, b, t, f, lb, tabs, past):
    n = b * t
    p, p_gate = _in_proj(x, f['mixer_g'], f['w_in'])
    p3 = p.reshape(b, t, P_MAIN)

    fresh = past['ckv'] is None
    q, ckv, kr = _mla_pre(p, b, t, f['gq'], f['gkv'], f['wq'], f['hq'], f['hk_rope'], tabs, transposed=fresh)
    if fresh:
        k_all, vt_all = _kv_up(ckv, kr, b, t, f['wk'], f['wv_pad'], f['hk_nope'], transposed=True)
        o_a = _attention_t(q, k_all.reshape(b, t, QK_W), vt_all)
    else:
        past_len = past['ckv'].shape[1]
        s = past_len + t
        kr_past = _pad_lanes(past['kr'], ROPE_LANE0, HEAD_PAD)
        kv = _kv_up_rows(past['ckv'], kr_past, s, 0, f['wk'], f['wv'], f['hk_nope'])
        k_all, v_all = _kv_up_rows(ckv.reshape(b, t, -1), kr.reshape(b, t, -1), s, past_len,
                                   f['wk'], f['wv'], f['hk_nope'], into=kv)
        o_a = _attention(q.reshape(b, t, QK_W), k_all, v_all, causal=False)

    conv8 = jnp.pad(past['conv'].reshape(b, CONV_W - 1, 3, GDN_QK).transpose(0, 2, 1, 3),
                    ((0, 0), (0, 0), (8 - (CONV_W - 1), 0), (0, 0)))
    gdn_parts = _gdn_prep(p3, f['conv_w'], conv8, f['alog'], f['dt'])
    gdn_conv = p3[:, t - (CONV_W - 1):, COL_GDN:COL_GDN + 3 * GDN_QK]

    hgrn_parts = _hgrn_prep(p3, lb)
    o_b, gdn_s, o_c, hgrn_st = _rec_scan(gdn_parts, hgrn_parts, p3, past['gdn'],
                                         jnp.swapaxes(past['hgrn'], -1, -2))
    hgrn_s = jnp.swapaxes(hgrn_st, -1, -2)

    x = _merge(x, o_a.reshape(n, -1), o_b.reshape(n, -1), o_c.reshape(n, -1), p, p_gate, f['gdn_g'],
               f['hgrn_g'], f['wo_a'], f['wo_b'], f['wo_c'], f['w_out'])

    wg, wu, wd = f['ffn']
    if f['router'] is None:
        x = _ffn(x, f['ffn_g'], wg, wu, wd)
    else:
        t_moe = _row_tile(n, MOE_TILE)
        cw, cwt, cnt = _router(x, f['ffn_g'], *f['router'], t_moe)
        counts = cnt[:, 0, :N_EXPERTS].astype(jnp.int32).reshape(-1)
        x = _moe(x, f['ffn_g'], cw, cwt, counts, wg, wu, wd, t_moe)

    new_ckv = ckv.reshape(b, t, MLA_KV_LORA)
    new_kr = kr.reshape(b, t, HEAD_PAD)[:, :, ROPE_LANE0:ROPE_LANE0 + MLA_ROPE]
    return x, (new_ckv, new_kr, gdn_s, gdn_conv, hgrn_s)


def kernel(x_prompt, x_sample, cache_mla_ckv, cache_mla_krope, state_gdn, state_gdn_conv, state_hgrn,
           mixer_norm_g, w_in, mla_q_norm_g, mla_w_q_up, mla_kv_norm_g, mla_w_kv_up,
           mla_q_head_norm_g, mla_k_head_norm_g, mla_w_o,
           gdn_conv_w, gdn_a_log, gdn_dt_bias, gdn_norm_g, gdn_w_o,
           hgrn_lb_logits, hgrn_norm_g, hgrn_w_o, w_out, ffn_norm_g,
           dense_w_gate, dense_w_up, dense_w_down,
           moe_w_router, moe_b_router, moe_w_gate, moe_w_up, moe_w_down):
    a = dict(mixer_norm_g=mixer_norm_g, w_in=w_in, mla_q_norm_g=mla_q_norm_g, mla_w_q_up=mla_w_q_up,
             mla_kv_norm_g=mla_kv_norm_g, mla_w_kv_up=mla_w_kv_up, mla_q_head_norm_g=mla_q_head_norm_g,
             mla_k_head_norm_g=mla_k_head_norm_g, mla_w_o=mla_w_o, gdn_conv_w=gdn_conv_w,
             gdn_a_log=gdn_a_log, gdn_dt_bias=gdn_dt_bias, gdn_norm_g=gdn_norm_g, gdn_w_o=gdn_w_o,
             hgrn_norm_g=hgrn_norm_g, hgrn_w_o=hgrn_w_o, w_out=w_out, ffn_norm_g=ffn_norm_g,
             dense_w_gate=dense_w_gate, dense_w_up=dense_w_up, dense_w_down=dense_w_down,
             moe_w_router=moe_w_router, moe_b_router=moe_b_router, moe_w_gate=moe_w_gate,
             moe_w_up=moe_w_up, moe_w_down=moe_w_down)
    depth = w_in.shape[0]
    lb_soft = jax.nn.softmax(hgrn_lb_logits.astype(F32), axis=0)
    hgrn_lb = jnp.cumsum(lb_soft, axis=0) - lb_soft[0]

    b_p, t_p = x_prompt.shape[:2]
    b_s, t_s = x_sample.shape[:2]
    past_len = cache_mla_ckv.shape[2]
    tab_all = _rope_tables(max(t_p, past_len + t_s))
    tabs_p = tuple(tb[:t_p] for tb in tab_all)
    tabs_s = tuple(tb[past_len:past_len + t_s] for tb in tab_all)

    xp = x_prompt.reshape(b_p * t_p, D_MODEL)
    xs = x_sample.reshape(b_s * t_s, D_MODEL)
    past_p = dict(ckv=None, kr=None,
                  gdn=jnp.zeros((b_p, GDN_HEADS, GDN_DK, GDN_DV), F32),
                  conv=jnp.zeros((b_p, CONV_W - 1, 3 * GDN_QK), F32),
                  hgrn=jnp.zeros((b_p, HGRN_HEADS, HGRN_DK, HGRN_DV), F32))
    st_p, st_s = [], []
    for l in range(depth):
        f = _layer_weights(l, a)
        lb = hgrn_lb[l][None]
        past_s = dict(ckv=cache_mla_ckv[l], kr=cache_mla_krope[l], gdn=state_gdn[l],
                      conv=state_gdn_conv[l], hgrn=state_hgrn[l])
        xp, sp = _trunk_layer(xp, b_p, t_p, f, lb, tabs_p, past_p)
        xs, ss = _trunk_layer(xs, b_s, t_s, f, lb, tabs_s, past_s)
        st_p.append(sp)
        st_s.append(ss)

    def stack(lst, i):
        return jnp.stack([s[i] for s in lst], axis=0)

    return (xp.reshape(b_p, t_p, D_MODEL), xs.reshape(b_s, t_s, D_MODEL),
            stack(st_p, 0), stack(st_p, 1), stack(st_p, 2), stack(st_p, 3), stack(st_p, 4),
            stack(st_s, 0), stack(st_s, 1), stack(st_s, 2), stack(st_s, 3), stack(st_s, 4))
```

```python
import functools

import jax
import jax.numpy as jnp
import numpy as np
from jax import lax
from jax.experimental import pallas as pl
from jax.experimental.pallas import tpu as pltpu

F32 = jnp.float32
BF16 = jnp.bfloat16

D_MODEL = 1024
CHUNK = 64
NORM_EPS = 1e-6

MLA_HEADS = 8
MLA_NOPE = 64
MLA_ROPE = 32
MLA_V = 64
MLA_Q_LORA = 384
MLA_KV_LORA = 256
MLA_QK_HEAD = MLA_NOPE + MLA_ROPE
MLA_VW = MLA_HEADS * MLA_V
ROPE_THETA = 10000.0
LOG2_E = 1.4426950408889634
HEAD_PAD = 128
QK_W = MLA_HEADS * HEAD_PAD
VT_ROWS = MLA_V + 16

GDN_HEADS = 4
GDN_DK = 128
GDN_DV = 128
GDN_QK = GDN_HEADS * GDN_DK
GDN_VW = GDN_HEADS * GDN_DV
CONV_W = 4
HGRN_PREP_ROWS = 512
GDN_PREP_ROWS = 1024

HGRN_HEADS = 4
HGRN_DK = 128
HGRN_DV = 128
HGRN_KW = HGRN_HEADS * HGRN_DK

N_BRANCH = 3
SPLIT_SIZES = (MLA_Q_LORA, MLA_KV_LORA, MLA_ROPE,
               GDN_QK, GDN_QK, GDN_VW, GDN_VW, GDN_HEADS, GDN_HEADS,
               HGRN_KW, HGRN_KW, HGRN_KW, HGRN_KW,
               N_BRANCH * D_MODEL)

N_EXPERTS = 8
FF_EXPERT = 1408
MOE_TILE = 1024
TOP_K = 2
MOE_BLOCK_ROWS = tuple(MOE_TILE * TOP_K // N_EXPERTS + d for d in (-32, -16, 0, 16, 32, 48))

P_COLS = 8192
P_MAIN = P_COLS - N_BRANCH * D_MODEL
P_TN = 512
COL_CKV = MLA_Q_LORA
COL_KR = MLA_Q_LORA + MLA_KV_LORA
COL_GAB = COL_KR + 128
COL_GDN = 1024
ROPE_LANE0 = MLA_NOPE

VMEM_LIMIT = 56 * 1024 * 1024


def _cparams(sem):
    return pltpu.CompilerParams(dimension_semantics=sem, vmem_limit_bytes=VMEM_LIMIT)


def _row_tile(n, cap):
    for t in (2048, 1024, 512, 256, 128, 64, 32, 16, 8):
        if t <= cap and n % t == 0:
            return t
    raise ValueError(f"no row tile for {n}")


def _sigmoid(x):
    return 1.0 / (1.0 + jnp.exp(-x))


def _silu(x):
    return x * (0.5 * jnp.tanh(0.5 * x) + 0.5)


def _rms(x, g):
    ms = jnp.mean(x * x, axis=-1, keepdims=True)
    return x * lax.rsqrt(ms + NORM_EPS) * g


def _in_proj_kernel(x_ref, g_ref, w_ref, o_ref, gate_ref, h_ref):
    j = pl.program_id(1)

    @pl.when(j == 0)
    def _():
        h_ref[...] = _rms(x_ref[...], g_ref[...]).astype(BF16)

    @pl.when(j < P_MAIN // P_TN)
    def _():
        o_ref[...] = jnp.dot(h_ref[...], w_ref[...], preferred_element_type=F32)

    @pl.when(j >= P_MAIN // P_TN)
    def _():
        gate_ref[...] = jnp.dot(h_ref[...], w_ref[...], preferred_element_type=F32).astype(BF16)


def _in_proj(x, g, w):
    n = x.shape[0]
    tm = _row_tile(n, 2048)
    n_main = P_MAIN // P_TN
    return pl.pallas_call(
        _in_proj_kernel,
        grid=(n // tm, P_COLS // P_TN),
        in_specs=[pl.BlockSpec((tm, D_MODEL), lambda i, j: (i, 0)),
                  pl.BlockSpec((1, D_MODEL), lambda i, j: (0, 0)),
                  pl.BlockSpec((D_MODEL, P_TN), lambda i, j: (0, j))],
        out_specs=[pl.BlockSpec((tm, P_TN), lambda i, j: (i, jnp.minimum(j, n_main - 1))),
                   pl.BlockSpec((tm, P_TN), lambda i, j: (i, jnp.maximum(j - n_main, 0)))],
        out_shape=[jax.ShapeDtypeStruct((n, P_MAIN), F32),
                   jax.ShapeDtypeStruct((n, P_COLS - P_MAIN), BF16)],
        scratch_shapes=[pltpu.VMEM((tm, D_MODEL), BF16)],
        compiler_params=_cparams(("parallel", "arbitrary")),
        name="in_proj",
    )(x, g, w)


def _rope(x, c, s1, s2):
    return x * c + pltpu.roll(x, HEAD_PAD - 16, 1) * s1 + pltpu.roll(x, 16, 1) * s2


def _mla_pre_kernel(p_ref, gq_ref, gkv_ref, wq_ref, hq_ref, hk_ref, c_ref, s1_ref, s2_ref, *rest,
                    scale, transposed):
    if transposed:
        hqt_ref, ct_ref, s1t_ref, s2t_ref, q_ref, ckv_ref, kr_ref = rest
    else:
        q_ref, ckv_ref, kr_ref = rest
    c, s1, s2 = c_ref[...], s1_ref[...], s2_ref[...]
    ckv_ref[...] = _rms(p_ref[:, COL_CKV:COL_KR], gkv_ref[...])

    kr = p_ref[:, COL_KR:COL_KR + HEAD_PAD]
    kr_ms = jnp.sum(kr * kr, axis=-1, keepdims=True) * (1.0 / MLA_ROPE)
    kr_ref[...] = _rope(kr * lax.rsqrt(kr_ms + NORM_EPS) * hk_ref[...], c, s1, s2)

    cq = _rms(p_ref[:, 0:MLA_Q_LORA], gq_ref[...]).astype(BF16)
    q = jnp.dot(cq, wq_ref[...], preferred_element_type=F32)
    if transposed:
        row = lax.broadcasted_iota(jnp.int32, (HEAD_PAD, 1), 0)
        hqt, ct, s1t, s2t = hqt_ref[...], ct_ref[...], s1t_ref[...], s2t_ref[...]
        for h in range(MLA_HEADS):
            qt = q[:, h * HEAD_PAD:(h + 1) * HEAD_PAD].T
            sq = qt * qt
            ms_n = jnp.sum(sq[0:MLA_NOPE], axis=0, keepdims=True) * (1.0 / MLA_NOPE)
            ms_r = jnp.sum(sq[MLA_NOPE:MLA_QK_HEAD], axis=0, keepdims=True) * (1.0 / MLA_ROPE)
            inv = jnp.where(row < MLA_NOPE, lax.rsqrt(ms_n + NORM_EPS), lax.rsqrt(ms_r + NORM_EPS))
            x = qt * inv * hqt
            x = x * ct + pltpu.roll(x, HEAD_PAD - 16, 0) * s1t + pltpu.roll(x, 16, 0) * s2t
            q_ref[0, h] = x.astype(BF16)
    else:
        lane = lax.broadcasted_iota(jnp.int32, (1, HEAD_PAD), 1)
        is_nope = lane < MLA_NOPE
        hq = hq_ref[...]
        for h in range(MLA_HEADS):
            qh = q[:, h * HEAD_PAD:(h + 1) * HEAD_PAD]
            sq = qh * qh
            ms_n = jnp.sum(jnp.where(is_nope, sq, 0.0), axis=-1, keepdims=True) * (1.0 / MLA_NOPE)
            ms_r = jnp.sum(jnp.where(is_nope, 0.0, sq), axis=-1, keepdims=True) * (1.0 / MLA_ROPE)
            inv = jnp.where(is_nope, lax.rsqrt(ms_n + NORM_EPS), lax.rsqrt(ms_r + NORM_EPS))
            qh = _rope(qh * inv * hq, c, s1, s2) * scale
            q_ref[:, h * HEAD_PAD:(h + 1) * HEAD_PAD] = qh.astype(BF16)


def _mla_pre(p, b, t_seq, gq, gkv, wq, hq, hk, tabs, transposed):
    n = p.shape[0]
    tm = _row_tile(n, 512)
    c, s1, s2 = tabs
    if tm > t_seq:
        c, s1, s2 = (jnp.tile(t, (tm // t_seq, 1)) for t in (c, s1, s2))
    n_tab = c.shape[0] // tm
    tab_spec = pl.BlockSpec((tm, HEAD_PAD), lambda i: (i % n_tab, 0))
    vec = lambda w: pl.BlockSpec((1, w), lambda i: (0, 0))
    scale = MLA_QK_HEAD ** -0.5
    extra, extra_specs = (), []
    if transposed:
        scale *= LOG2_E
        q_spec = pl.BlockSpec((1, MLA_HEADS, HEAD_PAD, tm), lambda i: (i // n_tab, 0, 0, i % n_tab))
        q_shape = jax.ShapeDtypeStruct((b, MLA_HEADS, HEAD_PAD, t_seq), BF16)
        hqt = jnp.broadcast_to((hq[0] * scale)[:, None], (HEAD_PAD, tm))
        extra = (hqt, c.T, s1.T, s2.T)
        tab_t_spec = pl.BlockSpec((HEAD_PAD, tm), lambda i: (0, i % n_tab))
        extra_specs = [pl.BlockSpec((HEAD_PAD, tm), lambda i: (0, 0)), tab_t_spec, tab_t_spec, tab_t_spec]
    else:
        q_spec = pl.BlockSpec((tm, QK_W), lambda i: (i, 0))
        q_shape = jax.ShapeDtypeStruct((n, QK_W), BF16)
    return pl.pallas_call(
        functools.partial(_mla_pre_kernel, scale=scale, transposed=transposed),
        grid=(n // tm,),
        in_specs=[pl.BlockSpec((tm, 1024), lambda i: (i, 0)),
                  vec(MLA_Q_LORA), vec(MLA_KV_LORA),
                  pl.BlockSpec((MLA_Q_LORA, QK_W), lambda i: (0, 0)),
                  vec(HEAD_PAD), vec(HEAD_PAD), tab_spec, tab_spec, tab_spec, *extra_specs],
        out_specs=[q_spec,
                   pl.BlockSpec((tm, MLA_KV_LORA), lambda i: (i, 0)),
                   pl.BlockSpec((tm, HEAD_PAD), lambda i: (i, 0))],
        out_shape=[q_shape,
                   jax.ShapeDtypeStruct((n, MLA_KV_LORA), F32),
                   jax.ShapeDtypeStruct((n, HEAD_PAD), F32)],
        compiler_params=_cparams(("parallel",)),
        name="mla_pre",
    )(p, gq, gkv, wq, hq, hk, c, s1, s2, *extra)


def _kv_up_kernel(ckv_ref, kr_ref, wk_ref, wv_ref, hk_ref, k_ref, v_ref):
    c = ckv_ref[...].astype(BF16)
    k = jnp.dot(c, wk_ref[...], preferred_element_type=F32)
    kr = kr_ref[...]
    hk = hk_ref[...]
    for h in range(MLA_HEADS):
        kh = k[:, h * HEAD_PAD:(h + 1) * HEAD_PAD]
        ms = jnp.sum(kh * kh, axis=-1, keepdims=True) * (1.0 / MLA_NOPE)
        k_ref[:, h * HEAD_PAD:(h + 1) * HEAD_PAD] = (kh * lax.rsqrt(ms + NORM_EPS) * hk + kr).astype(BF16)
    v = jnp.dot(c, wv_ref[...], preferred_element_type=F32)
    lane = lax.broadcasted_iota(jnp.int32, (1, HEAD_PAD), 1)
    for h in range(MLA_HEADS):
        vh = jnp.where(lane < MLA_V, v[:, h * HEAD_PAD:(h + 1) * HEAD_PAD], 1.0)
        v_ref[0, h] = vh.T[0:VT_ROWS].astype(BF16)


def _kv_up(ckv, kr, b, s_len, wk, wv, hk):
    n = ckv.shape[0]
    tm = _row_tile(n, 512)
    n_t = s_len // tm
    return pl.pallas_call(
        _kv_up_kernel,
        grid=(n // tm,),
        in_specs=[pl.BlockSpec((tm, MLA_KV_LORA), lambda i: (i, 0)),
                  pl.BlockSpec((tm, HEAD_PAD), lambda i: (i, 0)),
                  pl.BlockSpec((MLA_KV_LORA, QK_W), lambda i: (0, 0)),
                  pl.BlockSpec((MLA_KV_LORA, QK_W), lambda i: (0, 0)),
                  pl.BlockSpec((1, HEAD_PAD), lambda i: (0, 0))],
        out_specs=[pl.BlockSpec((tm, QK_W), lambda i: (i, 0)),
                   pl.BlockSpec((1, MLA_HEADS, VT_ROWS, tm), lambda i: (i // n_t, 0, 0, i % n_t))],
        out_shape=[jax.ShapeDtypeStruct((n, QK_W), BF16),
                   jax.ShapeDtypeStruct((b, MLA_HEADS, VT_ROWS, s_len), BF16)],
        compiler_params=_cparams(("parallel",)),
        name="kv_up",
    )(ckv, kr, wk, wv, hk)


def _kv_hist_kernel(ckv_ref, kr_ref, wkt_ref, wv_ref, hkt_ref, *rest):
    kt_ref, v_ref = rest[-2:]
    c = ckv_ref[0].astype(BF16)
    kt = _dot_nt(wkt_ref[...], c)
    krt = kr_ref[0].T
    hkt = hkt_ref[...]
    for h in range(MLA_HEADS):
        kh = kt[h * HEAD_PAD:(h + 1) * HEAD_PAD]
        ms = jnp.sum(kh * kh, axis=0, keepdims=True) * (1.0 / MLA_NOPE)
        kt_ref[0, h * HEAD_PAD:(h + 1) * HEAD_PAD, :] = (kh * lax.rsqrt(ms + NORM_EPS) * hkt + krt).astype(BF16)
    v_ref[0] = jnp.dot(c, wv_ref[...], preferred_element_type=F32).astype(BF16)


def _kv_hist(ckv, kr, s_total, row0, wkt, wv, hk, into=None):
    b, s_in, _ = ckv.shape
    tm = _row_tile(s_in, 512)
    blk0 = row0 // tm
    hkt = jnp.broadcast_to(hk[0][:, None], (HEAD_PAD, tm))
    in_specs = [pl.BlockSpec((1, tm, MLA_KV_LORA), lambda i, j: (i, j, 0)),
                pl.BlockSpec((1, tm, HEAD_PAD), lambda i, j: (i, j, 0)),
                pl.BlockSpec((QK_W, MLA_KV_LORA), lambda i, j: (0, 0)),
                pl.BlockSpec((MLA_KV_LORA, MLA_VW), lambda i, j: (0, 0)),
                pl.BlockSpec((HEAD_PAD, tm), lambda i, j: (0, 0))]
    args = [ckv, kr, wkt, wv, hkt]
    aliases = {}
    if into is not None:
        in_specs += [pl.BlockSpec(memory_space=pl.ANY), pl.BlockSpec(memory_space=pl.ANY)]
        args += list(into)
        aliases = {5: 0, 6: 1}
    return pl.pallas_call(
        _kv_hist_kernel,
        grid=(b, s_in // tm),
        in_specs=in_specs,
        out_specs=[pl.BlockSpec((1, QK_W, tm), lambda i, j: (i, 0, blk0 + j)),
                   pl.BlockSpec((1, tm, MLA_VW), lambda i, j: (i, blk0 + j, 0))],
        out_shape=[jax.ShapeDtypeStruct((b, QK_W, s_total), BF16),
                   jax.ShapeDtypeStruct((b, s_total, MLA_VW), BF16)],
        input_output_aliases=aliases,
        compiler_params=_cparams(("parallel", "parallel")),
        name="kv_hist",
    )(*args)


def _attn_hist_kernel(q_ref, kt_ref, v_ref, o_ref, *, q_pos0, s_valid):
    t, s_len = q_ref.shape[1], kt_ref.shape[2]
    k_pos = lax.broadcasted_iota(jnp.int32, (t, s_len), 1)
    q_pos = lax.broadcasted_iota(jnp.int32, (t, s_len), 0) + q_pos0
    allowed = (k_pos < s_valid) & (k_pos // CHUNK <= q_pos // CHUNK)
    scores = [jnp.dot(q_ref[0, :, h * HEAD_PAD:(h + 1) * HEAD_PAD], kt_ref[0, h * HEAD_PAD:(h + 1) * HEAD_PAD, :],
                      preferred_element_type=F32) for h in range(MLA_HEADS)]
    for h in range(MLA_HEADS):
        s = jnp.where(allowed, scores[h], -jnp.inf)
        p = jnp.exp(s - jnp.max(s, axis=-1, keepdims=True))
        pv = jnp.dot(p.astype(BF16), v_ref[0, :, h * MLA_V:(h + 1) * MLA_V], preferred_element_type=F32)
        o_ref[0, :, h * MLA_V:(h + 1) * MLA_V] = pv / jnp.sum(p, axis=-1, keepdims=True)


def _attention_hist(q, kt, v, q_pos0, s_valid):
    b, t, _ = q.shape
    s = kt.shape[2]
    return pl.pallas_call(
        functools.partial(_attn_hist_kernel, q_pos0=q_pos0, s_valid=s_valid),
        grid=(b,),
        in_specs=[pl.BlockSpec((1, t, QK_W), lambda i: (i, 0, 0)),
                  pl.BlockSpec((1, QK_W, s), lambda i: (i, 0, 0)),
                  pl.BlockSpec((1, s, MLA_VW), lambda i: (i, 0, 0))],
        out_specs=pl.BlockSpec((1, t, MLA_VW), lambda i: (i, 0, 0)),
        out_shape=jax.ShapeDtypeStruct((b, t, MLA_VW), F32),
        compiler_params=_cparams(("parallel",)),
        name="mla_attn_hist",
    )(q, kt, v)


def _attn_t_kernel(qt_ref, k_ref, vt_ref, o_ref, m_ref, acc_ref, *, tq, tk):
    m_ref[...] = jnp.full(m_ref.shape, -jnp.inf, F32)
    acc_ref[...] = jnp.zeros(acc_ref.shape, F32)

    def blocks(starts, masked):
        if masked:
            ck = lax.broadcasted_iota(jnp.int32, (tk, tq), 0) // CHUNK
            cq = lax.broadcasted_iota(jnp.int32, (tk, tq), 1) // CHUNK
            allowed = ck <= cq

        def scores(start, h):
            kh = k_ref[0, pl.ds(start, tk), h * HEAD_PAD:(h + 1) * HEAD_PAD]
            return jnp.dot(kh, qt_ref[0, h], preferred_element_type=F32)

        def update(start, h, s):
            if masked:
                s = jnp.where(allowed, s, -jnp.inf)
            m_prev = m_ref[h]
            m_new = jnp.maximum(m_prev, jnp.max(s, axis=0, keepdims=True))
            alpha = jnp.exp2(m_prev - m_new)
            p = jnp.exp2(s - m_new).astype(BF16)
            pv = jnp.dot(vt_ref[0, h, :, pl.ds(start, tk)], p, preferred_element_type=F32)
            acc_ref[h] = alpha * acc_ref[h] + pv
            m_ref[h] = m_new

        items = [(start, h) for start in starts for h in range(MLA_HEADS)]
        ahead = 5
        pending = [scores(*it) for it in items[:ahead]]
        for n, it in enumerate(items):
            s = pending.pop(0)
            if n + ahead < len(items):
                pending.append(scores(*items[n + ahead]))
            update(*it, s)

    qi = pl.program_id(1)

    def body(j, carry):
        first = pl.multiple_of(2 * j * tk, tk)
        blocks([first, pl.multiple_of(first + tk, tk)], False)
        return carry

    lax.fori_loop(0, qi // 2, body, 0)

    @pl.when(qi % 2 == 1)
    def _():
        blocks([pl.multiple_of((qi - 1) * tk, tk)], False)

    blocks([pl.multiple_of(qi * tk, tk)], True)

    ot = jnp.concatenate([acc_ref[h, 0:MLA_V] / acc_ref[h, MLA_V:MLA_V + 1] for h in range(MLA_HEADS)], axis=0)
    o_ref[0] = ot.T


def _attention_t(qt, k, vt):
    b, _, _, t = qt.shape
    s = k.shape[1]
    tq = tk = min(256, t)
    return pl.pallas_call(
        functools.partial(_attn_t_kernel, tq=tq, tk=tk),
        grid=(b, t // tq),
        in_specs=[pl.BlockSpec((1, MLA_HEADS, HEAD_PAD, tq), lambda i, j: (i, 0, 0, j)),
                  pl.BlockSpec((1, s, QK_W), lambda i, j: (i, 0, 0)),
                  pl.BlockSpec((1, MLA_HEADS, VT_ROWS, s), lambda i, j: (i, 0, 0, 0))],
        out_specs=pl.BlockSpec((1, tq, MLA_VW), lambda i, j: (i, j, 0)),
        out_shape=jax.ShapeDtypeStruct((b, t, MLA_VW), F32),
        scratch_shapes=[pltpu.VMEM((MLA_HEADS, 1, tq), F32),
                        pltpu.VMEM((MLA_HEADS, VT_ROWS, tq), F32)],
        compiler_params=_cparams(("parallel", "arbitrary")),
        name="mla_attn_t",
    )(qt, k, vt)


def _row_iota(shape):
    return lax.broadcasted_iota(jnp.int32, shape, 0)


def _upper_half_masks(shape):
    row = _row_iota(shape)
    masks = []
    m = 1
    while m < CHUNK:
        masks.append((row // m) % 2 == 1)
        m *= 2
    return masks


def _segment_scans(g, uppers):
    rows = g.shape[0]
    pre, tot = g, g
    out = [(pre, tot)]
    m = 1
    while m < CHUNK:
        upper = uppers[len(out) - 1]
        from_lower = pltpu.roll(tot, m, 0)
        from_upper = pltpu.roll(tot, rows - m, 0)
        pre = pre + jnp.where(upper, from_lower, 0.0)
        tot = tot + jnp.where(upper, from_lower, from_upper)
        out.append((pre, tot))
        m *= 2
    return out


def _dot_nt(a, b):
    return lax.dot_general(a, b, (((1,), (1,)), ((), ())), preferred_element_type=F32)


def _dot_tn(a, b):
    return lax.dot_general(a, b, (((0,), (0,)), ((), ())), preferred_element_type=F32)


def _gdn_prep_kernel(q_ref, k_ref, v_ref, ab_ref, cw_ref, cs_ref, alog_ref, dt_ref,
                     u_ref, w_ref, qd_ref, kd_ref, qk_ref, gl_ref, carry_ref, *, nb, r):
    @pl.when(pl.program_id(1) == 0)
    def _():
        carry_ref[...] = cs_ref[...]

    n = nb * r
    nc = n // CHUNK

    def conv(x_ref, j):
        w = cw_ref[:, j * GDN_QK:(j + 1) * GDN_QK]
        outs = []
        for b in range(nb):
            x = x_ref[b]
            xp = jnp.concatenate([carry_ref[b, j], x], axis=0)
            y = x * w[3:4]
            for d in range(1, CONV_W):
                y = y + xp[8 - d:8 - d + r] * w[3 - d:4 - d]
            carry_ref[b, j] = x[r - 8:]
            outs.append(_silu(y))
        return outs[0] if nb == 1 else jnp.concatenate(outs, axis=0)

    q_all, k_all, v_all = conv(q_ref, 0), conv(k_ref, 1), conv(v_ref, 2)

    ab = ab_ref[...].reshape(n, 128)
    x = ab + dt_ref[...]
    softplus = jnp.maximum(x, 0.0) + jnp.log(1.0 + jnp.exp(-jnp.abs(x)))
    g_blk = (-LOG2_E) * jnp.exp(alog_ref[...]) * softplus
    gam_blk = _segment_scans(g_blk, _upper_half_masks(g_blk.shape))[-1][0]
    gam_t = gam_blk.T
    gam3_blk = gam_blk.reshape(nc, CHUNK, 128)
    beta3_blk = _sigmoid(ab).reshape(nc, CHUNK, 128)

    row = lax.broadcasted_iota(jnp.int32, (1, CHUNK, CHUNK), 1)
    col = lax.broadcasted_iota(jnp.int32, (1, CHUNK, CHUNK), 2)
    eye = (row == col).astype(F32)

    def bmm(a, b):
        return jnp.einsum('cij,cjk->cik', a.astype(BF16), b.astype(BF16), preferred_element_type=F32)

    def bmm_nt(a, b):
        return jnp.einsum('cid,cjd->cij', a.astype(BF16), b.astype(BF16), preferred_element_type=F32)

    for h in range(GDN_HEADS):
        sl = slice(h * GDN_DK, (h + 1) * GDN_DK)
        q, k, v = q_all[:, sl], k_all[:, sl], v_all[:, sl]
        q = q * lax.rsqrt(jnp.sum(q * q, axis=-1, keepdims=True) + NORM_EPS) * (GDN_DK ** -0.5)
        k = k * lax.rsqrt(jnp.sum(k * k, axis=-1, keepdims=True) + NORM_EPS)
        q, k, v = (a.reshape(nc, CHUNK, GDN_DK) for a in (q, k, v))
        gam = gam3_blk[:, :, h:h + 1]
        beta = beta3_blk[:, :, GDN_HEADS + h:GDN_HEADS + h + 1]
        gam_row = jnp.stack([gam_t[h:h + 1, c * CHUNK:(c + 1) * CHUNK] for c in range(nc)], axis=0)
        decay = jnp.where(row >= col, jnp.exp2(jnp.minimum(gam - gam_row, 0.0)), 0.0)
        a = jnp.where(row > col, beta * bmm_nt(k, k) * decay, 0.0)
        t_inv = eye - a
        pw = a
        m = 1
        while 2 * m < CHUNK:
            pw = bmm(pw, pw)
            t_inv = t_inv + bmm(t_inv, pw)
            m *= 2
        e_gam = jnp.exp2(gam)
        gam_last = gam[:, CHUNK - 1:CHUNK, :]
        u_ref[:, h] = bmm(t_inv, v * beta).reshape(nb, r, GDN_DV)
        w_ref[:, h] = bmm(t_inv, k * (beta * e_gam)).astype(BF16).reshape(nb, r, GDN_DK)
        qd_ref[:, h] = (q * e_gam).astype(BF16).reshape(nb, r, GDN_DK)
        kd_ref[:, h] = (k * jnp.exp2(gam_last - gam)).astype(BF16).reshape(nb, r, GDN_DK)
        qk_ref[:, h] = (bmm_nt(q, k) * decay).astype(BF16).reshape(nb, r, CHUNK)
        gl_ref[:, h] = jnp.broadcast_to(jnp.exp2(gam_last), (nc, 1, 128)).reshape(nb, r // CHUNK, 1, 128)


def _gdn_scan_kernel(u_ref, w_ref, qd_ref, kd_ref, qk_ref, gl_ref, s0_ref, o_ref, sf_ref, s_ref, *, nb):
    c_idx = pl.program_id(1)

    @pl.when(c_idx == 0)
    def _():
        s_ref[...] = s0_ref[...]

    chains = [(b, h) for b in range(nb) for h in range(GDN_HEADS)]
    s_old = [s_ref[b, h] for b, h in chains]
    s_bf = [s.astype(BF16) for s in s_old]
    v_new = [u_ref[b, h] - jnp.dot(w_ref[b, h], sb, preferred_element_type=F32)
             for (b, h), sb in zip(chains, s_bf)]
    v_bf = [v.astype(BF16) for v in v_new]
    for (b, h), s, sb, vb in zip(chains, s_old, s_bf, v_bf):
        o = (jnp.dot(qd_ref[b, h], sb, preferred_element_type=F32)
             + jnp.dot(qk_ref[b, h], vb, preferred_element_type=F32))
        o_ref[b, :, h * GDN_DV:(h + 1) * GDN_DV] = o
        s_ref[b, h] = gl_ref[b, h, 0] * s + _dot_tn(kd_ref[b, h], vb)

    @pl.when(c_idx == pl.num_programs(1) - 1)
    def _():
        sf_ref[...] = s_ref[...]


def _gdn_prep(p3, conv_w, conv_state8, alog, dt):
    b, t, _ = p3.shape
    r = min(GDN_PREP_ROWS, t)
    nb = max(1, min(b, GDN_PREP_ROWS // r))
    nt = t // r
    nc = t // CHUNK
    blk = lambda j: pl.BlockSpec((nb, r, GDN_QK), lambda i, c: (i, c, COL_GDN // GDN_QK + j))
    vec = pl.BlockSpec((1, 128), lambda i, c: (0, 0))
    head_spec = lambda w: pl.BlockSpec((nb, GDN_HEADS, r, w), lambda i, c: (i, 0, c, 0))
    head_shape = lambda w, dt_: jax.ShapeDtypeStruct((b, GDN_HEADS, t, w), dt_)
    u, w, qd, kd, qk, gl = pl.pallas_call(
        functools.partial(_gdn_prep_kernel, nb=nb, r=r),
        grid=(b // nb, nt),
        in_specs=[blk(0), blk(1), blk(2),
                  pl.BlockSpec((nb, r, 128), lambda i, c: (i, c, COL_GAB // 128)),
                  pl.BlockSpec((CONV_W, 3 * GDN_QK), lambda i, c: (0, 0)),
                  pl.BlockSpec((nb, 3, 8, GDN_QK), lambda i, c: (i, 0, 0, 0)),
                  vec, vec],
        out_specs=[head_spec(GDN_DV), head_spec(GDN_DK), head_spec(GDN_DK), head_spec(GDN_DK),
                   head_spec(CHUNK),
                   pl.BlockSpec((nb, GDN_HEADS, r // CHUNK, 1, 128), lambda i, c: (i, 0, c, 0, 0))],
        out_shape=[head_shape(GDN_DV, F32), head_shape(GDN_DK, BF16), head_shape(GDN_DK, BF16),
                   head_shape(GDN_DK, BF16), head_shape(CHUNK, BF16),
                   jax.ShapeDtypeStruct((b, GDN_HEADS, nc, 1, 128), F32)],
        scratch_shapes=[pltpu.VMEM((nb, 3, 8, GDN_QK), F32)],
        compiler_params=_cparams(("parallel", "arbitrary")),
        name="gdn_prep",
    )(p3, p3, p3, p3, conv_w, conv_state8, alog, dt)
    return u, w, qd, kd, qk, gl


def _hgrn_prep_kernel(q_ref, f_ref, lb_ref, att_ref, qe_ref, ke_ref, dec_ref, *, nb, r):
    n = nb * r
    nc = n // CHUNK
    row = lax.broadcasted_iota(jnp.int32, (1, CHUNK, CHUNK), 1)
    col = lax.broadcasted_iota(jnp.int32, (1, CHUNK, CHUNK), 2)
    uppers = _upper_half_masks((n, HGRN_DK))
    halves = [1 << lvl for lvl in range(len(uppers))]
    pairs = [(row // (2 * m) == col // (2 * m)) & ((row // m) % 2 == 1) & ((col // m) % 2 == 0) for m in halves]

    def bmm_nt(a, b):
        a3, b3 = (x.astype(BF16).reshape(nc, CHUNK, HGRN_DK) for x in (a, b))
        return jnp.einsum('cid,cjd->cij', a3, b3, preferred_element_type=F32)

    for h in range(HGRN_HEADS):
        sl = slice(h * HGRN_DK, (h + 1) * HGRN_DK)
        lb = lb_ref[:, sl]
        f = lb + (1.0 - lb) * _sigmoid(f_ref[:, :, sl].reshape(n, HGRN_DK))
        q = _silu(q_ref[:, :, sl].reshape(n, HGRN_DK)) * (HGRN_DK ** -0.5)
        k = 1.0 - f
        scans = _segment_scans(jnp.log2(f), uppers)
        cb, c_tot = scans[-1]

        att = jnp.where(row == col, bmm_nt(q, k), 0.0)
        for lvl in range(len(halves)):
            pre_m, tot_m = scans[lvl]
            att = att + jnp.where(pairs[lvl], bmm_nt(q * jnp.exp2(pre_m), k * jnp.exp2(tot_m - pre_m)), 0.0)

        att_ref[:, h] = att.astype(BF16).reshape(nb, r, CHUNK)
        qe_ref[:, h] = (q * jnp.exp2(cb)).astype(BF16).reshape(nb, r, HGRN_DK)
        ke_ref[:, h] = (k * jnp.exp2(c_tot - cb)).astype(BF16).reshape(nb, r, HGRN_DK)
        dec_ref[:, h] = jnp.exp2(c_tot).reshape(nc, CHUNK, HGRN_DK)[:, 0:1, :].reshape(nb, r // CHUNK, 1, HGRN_DK)


def _hgrn_scan_kernel(att_ref, qe_ref, ke_ref, dec_ref, v_ref, s0_ref, o_ref, sf_ref, st_ref, *, nb):
    c_idx = pl.program_id(1)

    @pl.when(c_idx == 0)
    def _():
        st_ref[...] = s0_ref[...]

    chains = [(b, h) for b in range(nb) for h in range(HGRN_HEADS)]
    sls = [slice(h * HGRN_DV, (h + 1) * HGRN_DV) for h in range(HGRN_HEADS)]
    st_old = [st_ref[b, h] for b, h in chains]
    vs = [v_ref[b, :, sls[h]].astype(BF16) for b, h in chains]
    o_st = [_dot_nt(qe_ref[b, h], st.astype(BF16)) for (b, h), st in zip(chains, st_old)]
    for (b, h), st, v, o1 in zip(chains, st_old, vs, o_st):
        o_ref[b, :, sls[h]] = o1 + jnp.dot(att_ref[b, h], v, preferred_element_type=F32)
        st_ref[b, h] = st * dec_ref[b, h, 0] + _dot_tn(v, ke_ref[b, h])

    @pl.when(c_idx == pl.num_programs(1) - 1)
    def _():
        sf_ref[...] = st_ref[...]


def _hgrn_prep(p3, lb):
    b, t, _ = p3.shape
    r = min(HGRN_PREP_ROWS, t)
    nb = max(1, min(b, HGRN_PREP_ROWS // r))
    nc = t // CHUNK
    col0 = (COL_GDN + 4 * GDN_QK) // HGRN_KW
    blk = lambda j: pl.BlockSpec((nb, r, HGRN_KW), lambda i, c: (i, c, col0 + j))
    head_spec = lambda w: pl.BlockSpec((nb, HGRN_HEADS, r, w), lambda i, c: (i, 0, c, 0))
    head_shape = lambda w: jax.ShapeDtypeStruct((b, HGRN_HEADS, t, w), BF16)
    att, qe, ke, dec = pl.pallas_call(
        functools.partial(_hgrn_prep_kernel, nb=nb, r=r),
        grid=(b // nb, t // r),
        in_specs=[blk(0), blk(1), pl.BlockSpec((1, HGRN_KW), lambda i, c: (0, 0))],
        out_specs=[head_spec(CHUNK), head_spec(HGRN_DK), head_spec(HGRN_DK),
                   pl.BlockSpec((nb, HGRN_HEADS, r // CHUNK, 1, HGRN_DK), lambda i, c: (i, 0, c, 0, 0))],
        out_shape=[head_shape(CHUNK), head_shape(HGRN_DK), head_shape(HGRN_DK),
                   jax.ShapeDtypeStruct((b, HGRN_HEADS, nc, 1, HGRN_DK), F32)],
        compiler_params=_cparams(("parallel", "parallel")),
        name="hgrn_prep",
    )(p3, p3, lb)
    return att, qe, ke, dec


def _rec_scan_kernel(u_ref, w_ref, qd_ref, kd_ref, qk_ref, gl_ref, sg0_ref,
                     att_ref, qe_ref, ke_ref, dec_ref, v_ref, sh0_ref,
                     og_ref, sgf_ref, oh_ref, shf_ref, sg_ref, sh_ref, *, nb):
    _gdn_scan_kernel(u_ref, w_ref, qd_ref, kd_ref, qk_ref, gl_ref, sg0_ref, og_ref, sgf_ref, sg_ref, nb=nb)
    _hgrn_scan_kernel(att_ref, qe_ref, ke_ref, dec_ref, v_ref, sh0_ref, oh_ref, shf_ref, sh_ref, nb=nb)


def _rec_scan(gdn_parts, hgrn_parts, p3, s0_gdn, s0t_hgrn):
    b, t, _ = p3.shape
    nc = t // CHUNK
    sb = min(b, 8)
    col_v = (COL_GDN + 4 * GDN_QK) // HGRN_KW + 2
    chunk_spec = lambda w: pl.BlockSpec((sb, GDN_HEADS, CHUNK, w), lambda i, c: (i, 0, c, 0))
    per_chunk = pl.BlockSpec((sb, GDN_HEADS, 1, 1, 128), lambda i, c: (i, 0, c, 0, 0))
    state_spec = pl.BlockSpec((sb, GDN_HEADS, 128, 128), lambda i, c: (i, 0, 0, 0))
    out_spec = pl.BlockSpec((sb, CHUNK, GDN_VW), lambda i, c: (i, c, 0))
    state_shape = jax.ShapeDtypeStruct((b, GDN_HEADS, 128, 128), F32)
    return pl.pallas_call(
        functools.partial(_rec_scan_kernel, nb=sb),
        grid=(b // sb, nc),
        in_specs=[chunk_spec(GDN_DV), chunk_spec(GDN_DK), chunk_spec(GDN_DK), chunk_spec(GDN_DK),
                  chunk_spec(CHUNK), per_chunk, state_spec,
                  chunk_spec(CHUNK), chunk_spec(HGRN_DK), chunk_spec(HGRN_DK), per_chunk,
                  pl.BlockSpec((sb, CHUNK, HGRN_KW), lambda i, c: (i, c, col_v)), state_spec],
        out_specs=[out_spec, state_spec, out_spec, state_spec],
        out_shape=[jax.ShapeDtypeStruct((b, t, GDN_VW), F32), state_shape,
                   jax.ShapeDtypeStruct((b, t, HGRN_KW), F32), state_shape],
        scratch_shapes=[pltpu.VMEM((sb, GDN_HEADS, 128, 128), F32), pltpu.VMEM((sb, HGRN_HEADS, 128, 128), F32)],
        compiler_params=_cparams(("parallel", "arbitrary")),
        name="rec_scan",
    )(*gdn_parts, s0_gdn, *hgrn_parts, p3, s0t_hgrn)


def _merge_kernel(x_ref, oa_ref, ob_ref, zb_ref, oc_ref, zc_ref, g0_ref, g1_ref, g2_ref, ngb_ref, ngc_ref,
                  wa_ref, wb_ref, wc_ref, wo_ref, y_ref):
    def branch(o, w_ref, g_ref):
        gate = _sigmoid(g_ref[...].astype(F32))
        return gate * jnp.dot(o.astype(BF16), w_ref[...], preferred_element_type=F32)

    def normed(o_ref, z_ref, ng_ref):
        return jnp.concatenate(
            [_rms(o_ref[:, h * 128:(h + 1) * 128], ng_ref[...]) * _silu(z_ref[:, h * 128:(h + 1) * 128])
             for h in range(GDN_HEADS)], axis=-1)

    mixed = (branch(oa_ref[...], wa_ref, g0_ref) + branch(normed(ob_ref, zb_ref, ngb_ref), wb_ref, g1_ref)
             + branch(normed(oc_ref, zc_ref, ngc_ref), wc_ref, g2_ref))
    y_ref[...] = x_ref[...] + jnp.dot(mixed.astype(BF16), wo_ref[...], preferred_element_type=F32)


def _merge(x, oa, ob, oc, p, pg, ngb, ngc, wa, wb, wc, wo):
    n = x.shape[0]
    tm = _row_tile(n, 512)
    row = lambda w: pl.BlockSpec((tm, w), lambda i: (i, 0))
    gate = lambda j: pl.BlockSpec((tm, D_MODEL), lambda i: (i, j))
    zb_spec = pl.BlockSpec((tm, GDN_VW), lambda i: (i, COL_GDN // GDN_VW + 3))
    zc_spec = pl.BlockSpec((tm, HGRN_KW), lambda i: (i, (COL_GDN + 4 * GDN_QK) // HGRN_KW + 3))
    vec = pl.BlockSpec((1, 128), lambda i: (0, 0))
    wsp = lambda k: pl.BlockSpec((k, D_MODEL), lambda i: (0, 0))
    return pl.pallas_call(
        _merge_kernel,
        grid=(n // tm,),
        in_specs=[row(D_MODEL), row(MLA_VW), row(GDN_VW), zb_spec, row(HGRN_KW), zc_spec,
                  gate(0), gate(1), gate(2), vec, vec,
                  wsp(MLA_VW), wsp(GDN_VW), wsp(HGRN_KW), wsp(D_MODEL)],
        out_specs=row(D_MODEL),
        out_shape=jax.ShapeDtypeStruct((n, D_MODEL), F32),
        compiler_params=_cparams(("parallel",)),
        name="merge_out",
    )(x, oa, ob, p, oc, p, pg, pg, pg, ngb, ngc, wa, wb, wc, wo)


def _router_kernel(x_ref, g_ref, w_ref, b_ref, cw_ref, cwt_ref, cnt_ref):
    h = _rms(x_ref[...], g_ref[...])
    logits = jnp.dot(h, w_ref[...], preferred_element_type=F32, precision=lax.Precision.HIGHEST) + b_ref[...]
    lane = lax.broadcasted_iota(jnp.int32, logits.shape, 1)
    valid = lane < N_EXPERTS
    neg = -jnp.inf
    l1 = jnp.where(valid, logits, neg)
    m1 = jnp.max(l1, axis=-1, keepdims=True)
    i1 = jnp.min(jnp.where(l1 == m1, lane, 128), axis=-1, keepdims=True)
    l2 = jnp.where(lane == i1, neg, l1)
    m2 = jnp.max(l2, axis=-1, keepdims=True)
    i2 = jnp.min(jnp.where(l2 == m2, lane, 128), axis=-1, keepdims=True)
    e2 = jnp.exp(m2 - m1)
    den = 1.0 + e2
    cw = jnp.where(lane == i1, 1.0 / den, 0.0) + jnp.where(lane == i2, e2 / den, 0.0)
    cw_ref[...] = cw
    cwt_ref[...] = cw.T
    cnt_ref[0] = jnp.sum((cw > 0.0).astype(F32), axis=0, keepdims=True)


def _router(x, g, w, b, tm):
    n = x.shape[0]
    return pl.pallas_call(
        _router_kernel,
        grid=(n // tm,),
        in_specs=[pl.BlockSpec((tm, D_MODEL), lambda i: (i, 0)),
                  pl.BlockSpec((1, D_MODEL), lambda i: (0, 0)),
                  pl.BlockSpec((D_MODEL, 128), lambda i: (0, 0)),
                  pl.BlockSpec((1, 128), lambda i: (0, 0))],
        out_specs=[pl.BlockSpec((tm, 128), lambda i: (i, 0)),
                   pl.BlockSpec((128, tm), lambda i: (0, i)),
                   pl.BlockSpec((1, 1, 128), lambda i: (i, 0, 0))],
        out_shape=[jax.ShapeDtypeStruct((n, 128), F32),
                   jax.ShapeDtypeStruct((128, n), F32),
                   jax.ShapeDtypeStruct((n // tm, 1, 128), F32)],
        compiler_params=_cparams(("parallel",)),
        name="moe_router",
    )(x, g, w, b)


def _moe_kernel(cnt_ref, x_ref, g_ref, cw_ref, cwt_ref, wg_ref, wu_ref, wd_ref, y_ref,
                h_ref, rcol_ref, rrow_ref, *, t, br):
    i, e = pl.program_id(0), pl.program_id(1)

    @pl.when(e == 0)
    def _():
        x = x_ref[...]
        h_ref[...] = _rms(x, g_ref[...]).astype(BF16)
        y_ref[...] = x
        r = lax.broadcasted_iota(jnp.int32, (t, t), 0)
        c = lax.broadcasted_iota(jnp.int32, (t, t), 1)
        on = cw_ref[...] > 0.0
        rank = jnp.dot((c < r).astype(BF16), on.astype(BF16), preferred_element_type=F32)
        rcol_ref[...] = jnp.where(on, rank.astype(jnp.int32), -1)
        on_t = cwt_ref[...] > 0.0
        rank_t = jnp.dot(on_t.astype(BF16), (r < c).astype(BF16), preferred_element_type=F32)
        rrow_ref[...] = jnp.where(on_t, rank_t.astype(jnp.int32), -1)

    count = cnt_ref[i * N_EXPERTS + e]
    sel = lax.broadcasted_iota(jnp.int32, (1, 128), 1) == e
    rank_c = jnp.sum(jnp.where(sel, rcol_ref[...], 0), axis=-1, keepdims=True)
    rank_r = rrow_ref[pl.ds(e, 1), :]
    w_r = cwt_ref[pl.ds(e, 1), :]

    def expert_block(base, rows_n):
        rows = lax.broadcasted_iota(jnp.int32, (rows_n, t), 0) + base
        pick = rows == rank_r
        xc = jnp.dot(pick.astype(BF16), h_ref[...], preferred_element_type=F32).astype(BF16)
        a = jnp.dot(xc, wg_ref[0], preferred_element_type=F32)
        b = jnp.dot(xc, wu_ref[0], preferred_element_type=F32)
        yc = jnp.dot((_silu(a) * b).astype(BF16), wd_ref[0], preferred_element_type=F32)
        w_rows = jnp.sum(jnp.where(pick, w_r, 0.0), axis=-1, keepdims=True)
        cols = lax.broadcasted_iota(jnp.int32, (t, rows_n), 1) + base
        y_ref[...] += jnp.dot((cols == rank_c).astype(BF16), (yc * w_rows).astype(BF16),
                              preferred_element_type=F32)

    lo = 0
    for size in br:
        @pl.when((count > lo) & (count <= size))
        def _(size=size):
            expert_block(0, size)
        lo = size

    @pl.when(count > br[-1])
    def _():
        def body(j, carry):
            expert_block(j * br[-1], br[-1])
            return carry

        lax.fori_loop(0, lax.div(count + (br[-1] - 1), br[-1]), body, 0)


def _moe(x, g, cw, cwt, counts, wg, wu, wd, t):
    n = x.shape[0]
    ne, _, ff = wg.shape
    br = MOE_BLOCK_ROWS
    grid_spec = pltpu.PrefetchScalarGridSpec(
        num_scalar_prefetch=1,
        grid=(n // t, ne),
        in_specs=[pl.BlockSpec((t, D_MODEL), lambda i, e, cnt: (i, 0)),
                  pl.BlockSpec((1, D_MODEL), lambda i, e, cnt: (0, 0)),
                  pl.BlockSpec((t, 128), lambda i, e, cnt: (i, 0)),
                  pl.BlockSpec((128, t), lambda i, e, cnt: (0, i)),
                  pl.BlockSpec((1, D_MODEL, ff), lambda i, e, cnt: (e, 0, 0)),
                  pl.BlockSpec((1, D_MODEL, ff), lambda i, e, cnt: (e, 0, 0)),
                  pl.BlockSpec((1, ff, D_MODEL), lambda i, e, cnt: (e, 0, 0))],
        out_specs=pl.BlockSpec((t, D_MODEL), lambda i, e, cnt: (i, 0)),
        scratch_shapes=[pltpu.VMEM((t, D_MODEL), BF16),
                        pltpu.VMEM((t, 128), jnp.int32),
                        pltpu.VMEM((128, t), jnp.int32)])
    return pl.pallas_call(
        functools.partial(_moe_kernel, t=t, br=br),
        grid_spec=grid_spec,
        out_shape=jax.ShapeDtypeStruct((n, D_MODEL), F32),
        compiler_params=_cparams(("parallel", "arbitrary")),
        name="moe",
    )(counts, x, g, cw, cwt, wg, wu, wd)


def _ffn_kernel(x_ref, g_ref, wg_ref, wu_ref, wd_ref, y_ref, h_ref):
    e = pl.program_id(1)

    @pl.when(e == 0)
    def _():
        x = x_ref[...]
        h_ref[...] = _rms(x, g_ref[...]).astype(BF16)
        y_ref[...] = x

    h = h_ref[...]
    a = jnp.dot(h, wg_ref[0], preferred_element_type=F32)
    b = jnp.dot(h, wu_ref[0], preferred_element_type=F32)
    y_ref[...] += jnp.dot((_silu(a) * b).astype(BF16), wd_ref[0], preferred_element_type=F32)


def _ffn(x, g, wg, wu, wd):
    n = x.shape[0]
    ne, _, ff = wg.shape
    tm = _row_tile(n, 512)
    return pl.pallas_call(
        _ffn_kernel,
        grid=(n // tm, ne),
        in_specs=[pl.BlockSpec((tm, D_MODEL), lambda i, e: (i, 0)),
                  pl.BlockSpec((1, D_MODEL), lambda i, e: (0, 0)),
                  pl.BlockSpec((1, D_MODEL, ff), lambda i, e: (e, 0, 0)),
                  pl.BlockSpec((1, D_MODEL, ff), lambda i, e: (e, 0, 0)),
                  pl.BlockSpec((1, ff, D_MODEL), lambda i, e: (e, 0, 0))],
        out_specs=pl.BlockSpec((tm, D_MODEL), lambda i, e: (i, 0)),
        out_shape=jax.ShapeDtypeStruct((n, D_MODEL), F32),
        scratch_shapes=[pltpu.VMEM((tm, D_MODEL), BF16)],
        compiler_params=_cparams(("parallel", "arbitrary")),
        name="ffn",
    )(x, g, wg, wu, wd)


def _pad_lanes(x, left, total):
    return jnp.pad(x, [(0, 0)] * (x.ndim - 1) + [(left, total - left - x.shape[-1])])


def _pack_w_in(w):
    cq, ckv, kr, gq, gk, gv, gz, ga, gb, hq, hf, hi, hg, gates = jnp.split(
        w, np.cumsum(SPLIT_SIZES)[:-1].tolist(), axis=-1)
    kr_blk = _pad_lanes(kr, ROPE_LANE0, 128)
    ab_blk = _pad_lanes(jnp.concatenate([ga, gb], axis=-1), 0, 256)
    return jnp.concatenate([cq, ckv, kr_blk, ab_blk, gq, gk, gv, gz, hq, hf, hi, hg, gates], axis=-1).astype(BF16)


def _rope_tables(n_pos):
    half = MLA_ROPE // 2
    inv = ROPE_THETA ** (-jnp.arange(half, dtype=F32) / half)
    ang = jnp.arange(n_pos, dtype=F32)[:, None] * inv[None, :]
    cos, sin = jnp.cos(ang), jnp.sin(ang)
    one = jnp.ones((n_pos, MLA_NOPE), F32)
    zero = jnp.zeros((n_pos, MLA_NOPE), F32)
    tail = jnp.zeros((n_pos, HEAD_PAD - MLA_QK_HEAD), F32)
    z16 = jnp.zeros((n_pos, half), F32)
    c = jnp.concatenate([one, cos, cos, tail], axis=-1)
    s1 = jnp.concatenate([zero, -sin, z16, tail], axis=-1)
    s2 = jnp.concatenate([zero, z16, sin, tail], axis=-1)
    return c, s1, s2


def _layer_weights(l, a):
    f = {}
    f['mixer_g'] = a['mixer_norm_g'][l][None]
    f['w_in'] = _pack_w_in(a['w_in'][l])
    f['gq'] = a['mla_q_norm_g'][l][None]
    f['gkv'] = a['mla_kv_norm_g'][l][None]
    wq = a['mla_w_q_up'][l].reshape(MLA_Q_LORA, MLA_HEADS, MLA_QK_HEAD)
    f['wq'] = _pad_lanes(wq, 0, HEAD_PAD).reshape(MLA_Q_LORA, QK_W).astype(BF16)
    wkv = a['mla_w_kv_up'][l].reshape(MLA_KV_LORA, MLA_HEADS, MLA_NOPE + MLA_V)
    f['wk'] = _pad_lanes(wkv[:, :, :MLA_NOPE], 0, HEAD_PAD).reshape(MLA_KV_LORA, QK_W).astype(BF16)
    f['wkt'] = f['wk'].T
    f['wv'] = wkv[:, :, MLA_NOPE:].reshape(MLA_KV_LORA, MLA_VW).astype(BF16)
    f['wv_pad'] = _pad_lanes(wkv[:, :, MLA_NOPE:], 0, HEAD_PAD).reshape(MLA_KV_LORA, QK_W).astype(BF16)
    f['hq'] = _pad_lanes(a['mla_q_head_norm_g'][l][None], 0, HEAD_PAD)
    hk = a['mla_k_head_norm_g'][l][None]
    f['hk_nope'] = _pad_lanes(hk[:, :MLA_NOPE], 0, HEAD_PAD)
    f['hk_rope'] = _pad_lanes(hk[:, MLA_NOPE:], ROPE_LANE0, HEAD_PAD)
    f['wo_a'] = a['mla_w_o'][l].astype(BF16)
    f['conv_w'] = a['gdn_conv_w'][l]
    f['alog'] = _pad_lanes(a['gdn_a_log'][l][None], 0, 128)
    f['dt'] = _pad_lanes(a['gdn_dt_bias'][l][None], 0, 128)
    f['gdn_g'] = a['gdn_norm_g'][l][None]
    f['wo_b'] = a['gdn_w_o'][l].astype(BF16)
    f['hgrn_g'] = a['hgrn_norm_g'][l][None]
    f['wo_c'] = a['hgrn_w_o'][l].astype(BF16)
    f['w_out'] = a['w_out'][l].astype(BF16)
    f['ffn_g'] = a['ffn_norm_g'][l][None]
    if l % 2 == 0:
        wg, wu, wd = a['dense_w_gate'][l // 2], a['dense_w_up'][l // 2], a['dense_w_down'][l // 2]
        ff = wg.shape[1]
        half = ff // 2
        f['ffn'] = (jnp.moveaxis(wg.reshape(D_MODEL, 2, half), 1, 0).astype(BF16),
                    jnp.moveaxis(wu.reshape(D_MODEL, 2, half), 1, 0).astype(BF16),
                    wd.reshape(2, half, D_MODEL).astype(BF16))
        f['router'] = None
    else:
        f['ffn'] = (a['moe_w_gate'][l // 2].astype(BF16), a['moe_w_up'][l // 2].astype(BF16),
                    a['moe_w_down'][l // 2].astype(BF16))
        f['router'] = (_pad_lanes(a['moe_w_router'][l // 2], 0, 128),
                       _pad_lanes(a['moe_b_router'][l // 2][None], 0, 128))
    return f


def _trunk_layer(x, b, t, f, lb, tabs, past):
    n = b * t
    p, p_gate = _in_proj(x, f['mixer_g'], f['w_in'])
    p3 = p.reshape(b, t, P_MAIN)

    fresh = past['ckv'] is None
    q, ckv, kr = _mla_pre(p, b, t, f['gq'], f['gkv'], f['wq'], f['hq'], f['hk_rope'], tabs, transposed=fresh)
    if fresh:
        k_all, vt_all = _kv_up(ckv, kr, b, t, f['wk'], f['wv_pad'], f['hk_nope'])
        o_a = _attention_t(q, k_all.reshape(b, t, QK_W), vt_all)
    else:
        past_len = past['ckv'].shape[1]
        t_pad = -(-t // 128) * 128
        pad_rows = lambda a: jnp.pad(a.reshape(b, t, -1), ((0, 0), (0, t_pad - t), (0, 0)))
        kr_past = _pad_lanes(past['kr'], ROPE_LANE0, HEAD_PAD)
        kv = _kv_hist(past['ckv'], kr_past, past_len + t_pad, 0, f['wkt'], f['wv'], f['hk_nope'])
        kt_all, v_all = _kv_hist(pad_rows(ckv), pad_rows(kr), past_len + t_pad, past_len,
                                 f['wkt'], f['wv'], f['hk_nope'], into=kv)
        o_a = _attention_hist(q.reshape(b, t, QK_W), kt_all, v_all, past_len, past_len + t)

    conv8 = jnp.pad(past['conv'].reshape(b, CONV_W - 1, 3, GDN_QK).transpose(0, 2, 1, 3),
                    ((0, 0), (0, 0), (8 - (CONV_W - 1), 0), (0, 0)))
    gdn_parts = _gdn_prep(p3, f['conv_w'], conv8, f['alog'], f['dt'])
    gdn_conv = p3[:, t - (CONV_W - 1):, COL_GDN:COL_GDN + 3 * GDN_QK]

    hgrn_parts = _hgrn_prep(p3, lb)
    o_b, gdn_s, o_c, hgrn_st = _rec_scan(gdn_parts, hgrn_parts, p3, past['gdn'],
                                         jnp.swapaxes(past['hgrn'], -1, -2))
    hgrn_s = jnp.swapaxes(hgrn_st, -1, -2)

    x = _merge(x, o_a.reshape(n, -1), o_b.reshape(n, -1), o_c.reshape(n, -1), p, p_gate, f['gdn_g'],
               f['hgrn_g'], f['wo_a'], f['wo_b'], f['wo_c'], f['w_out'])

    wg, wu, wd = f['ffn']
    if f['router'] is None:
        x = _ffn(x, f['ffn_g'], wg, wu, wd)
    else:
        t_moe = _row_tile(n, MOE_TILE)
        cw, cwt, cnt = _router(x, f['ffn_g'], *f['router'], t_moe)
        counts = cnt[:, 0, :N_EXPERTS].astype(jnp.int32).reshape(-1)
        x = _moe(x, f['ffn_g'], cw, cwt, counts, wg, wu, wd, t_moe)

    new_ckv = ckv.reshape(b, t, MLA_KV_LORA)
    new_kr = kr.reshape(b, t, HEAD_PAD)[:, :, ROPE_LANE0:ROPE_LANE0 + MLA_ROPE]
    return x, (new_ckv, new_kr, gdn_s, gdn_conv, hgrn_s)


def kernel(x_prompt, x_sample, cache_mla_ckv, cache_mla_krope, state_gdn, state_gdn_conv, state_hgrn,
           mixer_norm_g, w_in, mla_q_norm_g, mla_w_q_up, mla_kv_norm_g, mla_w_kv_up,
           mla_q_head_norm_g, mla_k_head_norm_g, mla_w_o,
           gdn_conv_w, gdn_a_log, gdn_dt_bias, gdn_norm_g, gdn_w_o,
           hgrn_lb_logits, hgrn_norm_g, hgrn_w_o, w_out, ffn_norm_g,
           dense_w_gate, dense_w_up, dense_w_down,
           moe_w_router, moe_b_router, moe_w_gate, moe_w_up, moe_w_down):
    a = dict(mixer_norm_g=mixer_norm_g, w_in=w_in, mla_q_norm_g=mla_q_norm_g, mla_w_q_up=mla_w_q_up,
             mla_kv_norm_g=mla_kv_norm_g, mla_w_kv_up=mla_w_kv_up, mla_q_head_norm_g=mla_q_head_norm_g,
             mla_k_head_norm_g=mla_k_head_norm_g, mla_w_o=mla_w_o, gdn_conv_w=gdn_conv_w,
             gdn_a_log=gdn_a_log, gdn_dt_bias=gdn_dt_bias, gdn_norm_g=gdn_norm_g, gdn_w_o=gdn_w_o,
             hgrn_norm_g=hgrn_norm_g, hgrn_w_o=hgrn_w_o, w_out=w_out, ffn_norm_g=ffn_norm_g,
             dense_w_gate=dense_w_gate, dense_w_up=dense_w_up, dense_w_down=dense_w_down,
             moe_w_router=moe_w_router, moe_b_router=moe_b_router, moe_w_gate=moe_w_gate,
             moe_w_up=moe_w_up, moe_w_down=moe_w_down)
    depth = w_in.shape[0]
    lb_soft = jax.nn.softmax(hgrn_lb_logits.astype(F32), axis=0)
    hgrn_lb = jnp.cumsum(lb_soft, axis=0) - lb_soft[0]

    b_p, t_p = x_prompt.shape[:2]
    b_s, t_s = x_sample.shape[:2]
    past_len = cache_mla_ckv.shape[2]
    tab_all = _rope_tables(max(t_p, past_len + t_s))
    tabs_p = tuple(tb[:t_p] for tb in tab_all)
    tabs_s = tuple(tb[past_len:past_len + t_s] for tb in tab_all)

    xp = x_prompt.reshape(b_p * t_p, D_MODEL)
    xs = x_sample.reshape(b_s * t_s, D_MODEL)
    past_p = dict(ckv=None, kr=None,
                  gdn=jnp.zeros((b_p, GDN_HEADS, GDN_DK, GDN_DV), F32),
                  conv=jnp.zeros((b_p, CONV_W - 1, 3 * GDN_QK), F32),
                  hgrn=jnp.zeros((b_p, HGRN_HEADS, HGRN_DK, HGRN_DV), F32))
    st_p, st_s = [], []
    for l in range(depth):
        f = _layer_weights(l, a)
        lb = hgrn_lb[l][None]
        past_s = dict(ckv=cache_mla_ckv[l], kr=cache_mla_krope[l], gdn=state_gdn[l],
                      conv=state_gdn_conv[l], hgrn=state_hgrn[l])
        xp, sp = _trunk_layer(xp, b_p, t_p, f, lb, tabs_p, past_p)
        xs, ss = _trunk_layer(xs, b_s, t_s, f, lb, tabs_s, past_s)
        st_p.append(sp)
        st_s.append(ss)

    def stack(lst, i):
        return jnp.stack([s[i] for s in lst], axis=0)

    return (xp.reshape(b_p, t_p, D_MODEL), xs.reshape(b_s, t_s, D_MODEL),
            stack(st_p, 0), stack(st_p, 1), stack(st_p, 2), stack(st_p, 3), stack(st_p, 4),
            stack(st_s, 0), stack(st_s, 1), stack(st_s, 2), stack(st_s, 3), stack(st_s, 4))
```

```python
import functools

import jax
import jax.numpy as jnp
import numpy as np
from jax import lax
from jax.experimental import pallas as pl
from jax.experimental.pallas import tpu as pltpu

F32 = jnp.float32
BF16 = jnp.bfloat16

D_MODEL = 1024
CHUNK = 64
NORM_EPS = 1e-6

MLA_HEADS = 8
MLA_NOPE = 64
MLA_ROPE = 32
MLA_V = 64
MLA_Q_LORA = 384
MLA_KV_LORA = 256
MLA_QK_HEAD = MLA_NOPE + MLA_ROPE
MLA_VW = MLA_HEADS * MLA_V
ROPE_THETA = 10000.0
LOG2_E = 1.4426950408889634
HEAD_PAD = 128
QK_W = MLA_HEADS * HEAD_PAD
VT_ROWS = MLA_V + 16

GDN_HEADS = 4
GDN_DK = 128
GDN_DV = 128
GDN_QK = GDN_HEADS * GDN_DK
GDN_VW = GDN_HEADS * GDN_DV
CONV_W = 4
HGRN_PREP_ROWS = 512
GDN_PREP_ROWS = 1024

HGRN_HEADS = 4
HGRN_DK = 128
HGRN_DV = 128
HGRN_KW = HGRN_HEADS * HGRN_DK

N_BRANCH = 3
SPLIT_SIZES = (MLA_Q_LORA, MLA_KV_LORA, MLA_ROPE,
               GDN_QK, GDN_QK, GDN_VW, GDN_VW, GDN_HEADS, GDN_HEADS,
               HGRN_KW, HGRN_KW, HGRN_KW, HGRN_KW,
               N_BRANCH * D_MODEL)

N_EXPERTS = 8
FF_EXPERT = 1408
MOE_TILE = 1024
TOP_K = 2
MOE_BLOCK_ROWS = tuple(MOE_TILE * TOP_K // N_EXPERTS + d for d in (-32, -16, 0, 16, 32, 48))

P_COLS = 8192
P_MAIN = P_COLS - N_BRANCH * D_MODEL
P_TN = 512
COL_CKV = MLA_Q_LORA
COL_KR = MLA_Q_LORA + MLA_KV_LORA
COL_GAB = COL_KR + 128
COL_GDN = 1024
ROPE_LANE0 = MLA_NOPE

VMEM_LIMIT = 56 * 1024 * 1024


def _cparams(sem):
    return pltpu.CompilerParams(dimension_semantics=sem, vmem_limit_bytes=VMEM_LIMIT)


def _row_tile(n, cap):
    for t in (2048, 1024, 512, 256, 128, 64, 32, 16, 8):
        if t <= cap and n % t == 0:
            return t
    raise ValueError(f"no row tile for {n}")


def _sigmoid(x):
    return 1.0 / (1.0 + jnp.exp(-x))


def _silu(x):
    return x * (0.5 * jnp.tanh(0.5 * x) + 0.5)


def _rms(x, g):
    ms = jnp.mean(x * x, axis=-1, keepdims=True)
    return x * lax.rsqrt(ms + NORM_EPS) * g


def _in_proj_kernel(x_ref, g_ref, w_ref, o_ref, gate_ref, h_ref):
    j = pl.program_id(1)

    @pl.when(j == 0)
    def _():
        h_ref[...] = _rms(x_ref[...], g_ref[...]).astype(BF16)

    @pl.when(j < P_MAIN // P_TN)
    def _():
        o_ref[...] = jnp.dot(h_ref[...], w_ref[...], preferred_element_type=F32)

    @pl.when(j >= P_MAIN // P_TN)
    def _():
        gate_ref[...] = jnp.dot(h_ref[...], w_ref[...], preferred_element_type=F32).astype(BF16)


def _in_proj(x, g, w):
    n = x.shape[0]
    tm = _row_tile(n, 2048)
    n_main = P_MAIN // P_TN
    return pl.pallas_call(
        _in_proj_kernel,
        grid=(n // tm, P_COLS // P_TN),
        in_specs=[pl.BlockSpec((tm, D_MODEL), lambda i, j: (i, 0)),
                  pl.BlockSpec((1, D_MODEL), lambda i, j: (0, 0)),
                  pl.BlockSpec((D_MODEL, P_TN), lambda i, j: (0, j))],
        out_specs=[pl.BlockSpec((tm, P_TN), lambda i, j: (i, jnp.minimum(j, n_main - 1))),
                   pl.BlockSpec((tm, P_TN), lambda i, j: (i, jnp.maximum(j - n_main, 0)))],
        out_shape=[jax.ShapeDtypeStruct((n, P_MAIN), F32),
                   jax.ShapeDtypeStruct((n, P_COLS - P_MAIN), BF16)],
        scratch_shapes=[pltpu.VMEM((tm, D_MODEL), BF16)],
        compiler_params=_cparams(("parallel", "arbitrary")),
        name="in_proj",
    )(x, g, w)


def _rope(x, c, s1, s2):
    return x * c + pltpu.roll(x, HEAD_PAD - 16, 1) * s1 + pltpu.roll(x, 16, 1) * s2


def _mla_pre_kernel(p_ref, gq_ref, gkv_ref, wq_ref, hq_ref, hk_ref, c_ref, s1_ref, s2_ref, *rest,
                    scale, transposed):
    if transposed:
        hqt_ref, ct_ref, s1t_ref, s2t_ref, q_ref, ckv_ref, kr_ref = rest
    else:
        q_ref, ckv_ref, kr_ref = rest
    c, s1, s2 = c_ref[...], s1_ref[...], s2_ref[...]
    ckv_ref[...] = _rms(p_ref[:, COL_CKV:COL_KR], gkv_ref[...])

    kr = p_ref[:, COL_KR:COL_KR + HEAD_PAD]
    kr_ms = jnp.sum(kr * kr, axis=-1, keepdims=True) * (1.0 / MLA_ROPE)
    kr_ref[...] = _rope(kr * lax.rsqrt(kr_ms + NORM_EPS) * hk_ref[...], c, s1, s2)

    cq = _rms(p_ref[:, 0:MLA_Q_LORA], gq_ref[...]).astype(BF16)
    q = jnp.dot(cq, wq_ref[...], preferred_element_type=F32)
    if transposed:
        row = lax.broadcasted_iota(jnp.int32, (HEAD_PAD, 1), 0)
        hqt, ct, s1t, s2t = hqt_ref[...], ct_ref[...], s1t_ref[...], s2t_ref[...]
        for h in range(MLA_HEADS):
            qt = q[:, h * HEAD_PAD:(h + 1) * HEAD_PAD].T
            sq = qt * qt
            ms_n = jnp.sum(sq[0:MLA_NOPE], axis=0, keepdims=True) * (1.0 / MLA_NOPE)
            ms_r = jnp.sum(sq[MLA_NOPE:MLA_QK_HEAD], axis=0, keepdims=True) * (1.0 / MLA_ROPE)
            inv = jnp.where(row < MLA_NOPE, lax.rsqrt(ms_n + NORM_EPS), lax.rsqrt(ms_r + NORM_EPS))
            x = qt * inv * hqt
            x = x * ct + pltpu.roll(x, HEAD_PAD - 16, 0) * s1t + pltpu.roll(x, 16, 0) * s2t
            q_ref[0, h] = x.astype(BF16)
    else:
        lane = lax.broadcasted_iota(jnp.int32, (1, HEAD_PAD), 1)
        is_nope = lane < MLA_NOPE
        hq = hq_ref[...]
        for h in range(MLA_HEADS):
            qh = q[:, h * HEAD_PAD:(h + 1) * HEAD_PAD]
            sq = qh * qh
            ms_n = jnp.sum(jnp.where(is_nope, sq, 0.0), axis=-1, keepdims=True) * (1.0 / MLA_NOPE)
            ms_r = jnp.sum(jnp.where(is_nope, 0.0, sq), axis=-1, keepdims=True) * (1.0 / MLA_ROPE)
            inv = jnp.where(is_nope, lax.rsqrt(ms_n + NORM_EPS), lax.rsqrt(ms_r + NORM_EPS))
            qh = _rope(qh * inv * hq, c, s1, s2) * scale
            q_ref[:, h * HEAD_PAD:(h + 1) * HEAD_PAD] = qh.astype(BF16)


def _mla_pre(p, b, t_seq, gq, gkv, wq, hq, hk, tabs, transposed):
    n = p.shape[0]
    tm = _row_tile(n, 512)
    c, s1, s2 = tabs
    if tm > t_seq:
        c, s1, s2 = (jnp.tile(t, (tm // t_seq, 1)) for t in (c, s1, s2))
    n_tab = c.shape[0] // tm
    tab_spec = pl.BlockSpec((tm, HEAD_PAD), lambda i: (i % n_tab, 0))
    vec = lambda w: pl.BlockSpec((1, w), lambda i: (0, 0))
    scale = MLA_QK_HEAD ** -0.5
    extra, extra_specs = (), []
    if transposed:
        scale *= LOG2_E
        q_spec = pl.BlockSpec((1, MLA_HEADS, HEAD_PAD, tm), lambda i: (i // n_tab, 0, 0, i % n_tab))
        q_shape = jax.ShapeDtypeStruct((b, MLA_HEADS, HEAD_PAD, t_seq), BF16)
        hqt = jnp.broadcast_to((hq[0] * scale)[:, None], (HEAD_PAD, tm))
        extra = (hqt, c.T, s1.T, s2.T)
        tab_t_spec = pl.BlockSpec((HEAD_PAD, tm), lambda i: (0, i % n_tab))
        extra_specs = [pl.BlockSpec((HEAD_PAD, tm), lambda i: (0, 0)), tab_t_spec, tab_t_spec, tab_t_spec]
    else:
        q_spec = pl.BlockSpec((tm, QK_W), lambda i: (i, 0))
        q_shape = jax.ShapeDtypeStruct((n, QK_W), BF16)
    return pl.pallas_call(
        functools.partial(_mla_pre_kernel, scale=scale, transposed=transposed),
        grid=(n // tm,),
        in_specs=[pl.BlockSpec((tm, 1024), lambda i: (i, 0)),
                  vec(MLA_Q_LORA), vec(MLA_KV_LORA),
                  pl.BlockSpec((MLA_Q_LORA, QK_W), lambda i: (0, 0)),
                  vec(HEAD_PAD), vec(HEAD_PAD), tab_spec, tab_spec, tab_spec, *extra_specs],
        out_specs=[q_spec,
                   pl.BlockSpec((tm, MLA_KV_LORA), lambda i: (i, 0)),
                   pl.BlockSpec((tm, HEAD_PAD), lambda i: (i, 0))],
        out_shape=[q_shape,
                   jax.ShapeDtypeStruct((n, MLA_KV_LORA), F32),
                   jax.ShapeDtypeStruct((n, HEAD_PAD), F32)],
        compiler_params=_cparams(("parallel",)),
        name="mla_pre",
    )(p, gq, gkv, wq, hq, hk, c, s1, s2, *extra)


def _kv_up_kernel(ckv_ref, kr_ref, wk_ref, wv_ref, hk_ref, k_ref, v_ref):
    c = ckv_ref[...].astype(BF16)
    k = jnp.dot(c, wk_ref[...], preferred_element_type=F32)
    kr = kr_ref[...]
    hk = hk_ref[...]
    for h in range(MLA_HEADS):
        kh = k[:, h * HEAD_PAD:(h + 1) * HEAD_PAD]
        ms = jnp.sum(kh * kh, axis=-1, keepdims=True) * (1.0 / MLA_NOPE)
        k_ref[:, h * HEAD_PAD:(h + 1) * HEAD_PAD] = (kh * lax.rsqrt(ms + NORM_EPS) * hk + kr).astype(BF16)
    v = jnp.dot(c, wv_ref[...], preferred_element_type=F32)
    lane = lax.broadcasted_iota(jnp.int32, (1, HEAD_PAD), 1)
    for h in range(MLA_HEADS):
        vh = jnp.where(lane < MLA_V, v[:, h * HEAD_PAD:(h + 1) * HEAD_PAD], 1.0)
        v_ref[0, h] = vh.T[0:VT_ROWS].astype(BF16)


def _kv_up(ckv, kr, b, s_len, wk, wv, hk):
    n = ckv.shape[0]
    tm = _row_tile(n, 512)
    n_t = s_len // tm
    return pl.pallas_call(
        _kv_up_kernel,
        grid=(n // tm,),
        in_specs=[pl.BlockSpec((tm, MLA_KV_LORA), lambda i: (i, 0)),
                  pl.BlockSpec((tm, HEAD_PAD), lambda i: (i, 0)),
                  pl.BlockSpec((MLA_KV_LORA, QK_W), lambda i: (0, 0)),
                  pl.BlockSpec((MLA_KV_LORA, QK_W), lambda i: (0, 0)),
                  pl.BlockSpec((1, HEAD_PAD), lambda i: (0, 0))],
        out_specs=[pl.BlockSpec((tm, QK_W), lambda i: (i, 0)),
                   pl.BlockSpec((1, MLA_HEADS, VT_ROWS, tm), lambda i: (i // n_t, 0, 0, i % n_t))],
        out_shape=[jax.ShapeDtypeStruct((n, QK_W), BF16),
                   jax.ShapeDtypeStruct((b, MLA_HEADS, VT_ROWS, s_len), BF16)],
        compiler_params=_cparams(("parallel",)),
        name="kv_up",
    )(ckv, kr, wk, wv, hk)


def _kv_hist_kernel(ckv_ref, kr_ref, wkt_ref, wv_ref, hkt_ref, *rest):
    kt_ref, v_ref = rest[-2:]
    c = ckv_ref[0].astype(BF16)
    kt = _dot_nt(wkt_ref[...], c)
    kr = kr_ref[0]
    if kr.shape[1] == MLA_ROPE:
        rows = kr.shape[0]
        kr = jnp.concatenate([jnp.zeros((rows, ROPE_LANE0), F32), kr,
                              jnp.zeros((rows, HEAD_PAD - ROPE_LANE0 - MLA_ROPE), F32)], axis=1)
    krt = kr.T
    hkt = hkt_ref[...]
    for h in range(MLA_HEADS):
        kh = kt[h * HEAD_PAD:(h + 1) * HEAD_PAD]
        ms = jnp.sum(kh * kh, axis=0, keepdims=True) * (1.0 / MLA_NOPE)
        kt_ref[0, h * HEAD_PAD:(h + 1) * HEAD_PAD, :] = (kh * lax.rsqrt(ms + NORM_EPS) * hkt + krt).astype(BF16)
    v_ref[0] = jnp.dot(c, wv_ref[...], preferred_element_type=F32).astype(BF16)


def _kv_hist(ckv, kr, s_total, row0, wkt, wv, hk, into=None):
    b, s_in, _ = ckv.shape
    tm = _row_tile(s_in, 1024)
    blk0 = row0 // tm
    hkt = jnp.broadcast_to(hk[0][:, None], (HEAD_PAD, tm))
    in_specs = [pl.BlockSpec((1, tm, MLA_KV_LORA), lambda i, j: (i, j, 0)),
                pl.BlockSpec((1, tm, kr.shape[-1]), lambda i, j: (i, j, 0)),
                pl.BlockSpec((QK_W, MLA_KV_LORA), lambda i, j: (0, 0)),
                pl.BlockSpec((MLA_KV_LORA, MLA_VW), lambda i, j: (0, 0)),
                pl.BlockSpec((HEAD_PAD, tm), lambda i, j: (0, 0))]
    args = [ckv, kr, wkt, wv, hkt]
    aliases = {}
    if into is not None:
        in_specs += [pl.BlockSpec(memory_space=pl.ANY), pl.BlockSpec(memory_space=pl.ANY)]
        args += list(into)
        aliases = {5: 0, 6: 1}
    return pl.pallas_call(
        _kv_hist_kernel,
        grid=(b, s_in // tm),
        in_specs=in_specs,
        out_specs=[pl.BlockSpec((1, QK_W, tm), lambda i, j: (i, 0, blk0 + j)),
                   pl.BlockSpec((1, tm, MLA_VW), lambda i, j: (i, blk0 + j, 0))],
        out_shape=[jax.ShapeDtypeStruct((b, QK_W, s_total), BF16),
                   jax.ShapeDtypeStruct((b, s_total, MLA_VW), BF16)],
        input_output_aliases=aliases,
        compiler_params=_cparams(("parallel", "parallel")),
        name="kv_hist",
    )(*args)


def _attn_hist_kernel(q_ref, kt_ref, v_ref, o_ref, *, q_pos0, s_valid):
    t, s_len = q_ref.shape[1], kt_ref.shape[2]
    k_pos = lax.broadcasted_iota(jnp.int32, (t, s_len), 1)
    q_pos = lax.broadcasted_iota(jnp.int32, (t, s_len), 0) + q_pos0
    allowed = (k_pos < s_valid) & (k_pos // CHUNK <= q_pos // CHUNK)
    scores = [jnp.dot(q_ref[0, :, h * HEAD_PAD:(h + 1) * HEAD_PAD], kt_ref[0, h * HEAD_PAD:(h + 1) * HEAD_PAD, :],
                      preferred_element_type=F32) for h in range(MLA_HEADS)]
    for h in range(MLA_HEADS):
        s = jnp.where(allowed, scores[h], -jnp.inf)
        p = jnp.exp(s - jnp.max(s, axis=-1, keepdims=True))
        pv = jnp.dot(p.astype(BF16), v_ref[0, :, h * MLA_V:(h + 1) * MLA_V], preferred_element_type=F32)
        o_ref[0, :, h * MLA_V:(h + 1) * MLA_V] = pv / jnp.sum(p, axis=-1, keepdims=True)


def _attention_hist(q, kt, v, q_pos0, s_valid):
    b, t, _ = q.shape
    s = kt.shape[2]
    return pl.pallas_call(
        functools.partial(_attn_hist_kernel, q_pos0=q_pos0, s_valid=s_valid),
        grid=(b,),
        in_specs=[pl.BlockSpec((1, t, QK_W), lambda i: (i, 0, 0)),
                  pl.BlockSpec((1, QK_W, s), lambda i: (i, 0, 0)),
                  pl.BlockSpec((1, s, MLA_VW), lambda i: (i, 0, 0))],
        out_specs=pl.BlockSpec((1, t, MLA_VW), lambda i: (i, 0, 0)),
        out_shape=jax.ShapeDtypeStruct((b, t, MLA_VW), F32),
        compiler_params=_cparams(("parallel",)),
        name="mla_attn_hist",
    )(q, kt, v)


def _attn_t_kernel(qt_ref, k_ref, vt_ref, o_ref, m_ref, acc_ref, *, tq, tk):
    m_ref[...] = jnp.full(m_ref.shape, -jnp.inf, F32)
    acc_ref[...] = jnp.zeros(acc_ref.shape, F32)

    def blocks(starts, masked):
        if masked:
            ck = lax.broadcasted_iota(jnp.int32, (tk, tq), 0) // CHUNK
            cq = lax.broadcasted_iota(jnp.int32, (tk, tq), 1) // CHUNK
            allowed = ck <= cq

        def scores(start, h):
            kh = k_ref[0, pl.ds(start, tk), h * HEAD_PAD:(h + 1) * HEAD_PAD]
            return jnp.dot(kh, qt_ref[0, h], preferred_element_type=F32)

        def update(start, h, s):
            if masked:
                s = jnp.where(allowed, s, -jnp.inf)
            m_prev = m_ref[h]
            m_new = jnp.maximum(m_prev, jnp.max(s, axis=0, keepdims=True))
            alpha = jnp.exp2(m_prev - m_new)
            p = jnp.exp2(s - m_new).astype(BF16)
            pv = jnp.dot(vt_ref[0, h, :, pl.ds(start, tk)], p, preferred_element_type=F32)
            acc_ref[h] = alpha * acc_ref[h] + pv
            m_ref[h] = m_new

        items = [(start, h) for start in starts for h in range(MLA_HEADS)]
        ahead = 5
        pending = [scores(*it) for it in items[:ahead]]
        for n, it in enumerate(items):
            s = pending.pop(0)
            if n + ahead < len(items):
                pending.append(scores(*items[n + ahead]))
            update(*it, s)

    qi = pl.program_id(1)

    def body(j, carry):
        first = pl.multiple_of(2 * j * tk, tk)
        blocks([first, pl.multiple_of(first + tk, tk)], False)
        return carry

    lax.fori_loop(0, qi // 2, body, 0)

    @pl.when(qi % 2 == 1)
    def _():
        blocks([pl.multiple_of((qi - 1) * tk, tk)], False)

    blocks([pl.multiple_of(qi * tk, tk)], True)

    ot = jnp.concatenate([acc_ref[h, 0:MLA_V] / acc_ref[h, MLA_V:MLA_V + 1] for h in range(MLA_HEADS)], axis=0)
    o_ref[0] = ot.T


def _attention_t(qt, k, vt):
    b, _, _, t = qt.shape
    s = k.shape[1]
    tq = tk = min(256, t)
    return pl.pallas_call(
        functools.partial(_attn_t_kernel, tq=tq, tk=tk),
        grid=(b, t // tq),
        in_specs=[pl.BlockSpec((1, MLA_HEADS, HEAD_PAD, tq), lambda i, j: (i, 0, 0, j)),
                  pl.BlockSpec((1, s, QK_W), lambda i, j: (i, 0, 0)),
                  pl.BlockSpec((1, MLA_HEADS, VT_ROWS, s), lambda i, j: (i, 0, 0, 0))],
        out_specs=pl.BlockSpec((1, tq, MLA_VW), lambda i, j: (i, j, 0)),
        out_shape=jax.ShapeDtypeStruct((b, t, MLA_VW), F32),
        scratch_shapes=[pltpu.VMEM((MLA_HEADS, 1, tq), F32),
                        pltpu.VMEM((MLA_HEADS, VT_ROWS, tq), F32)],
        compiler_params=_cparams(("parallel", "arbitrary")),
        name="mla_attn_t",
    )(qt, k, vt)


def _row_iota(shape):
    return lax.broadcasted_iota(jnp.int32, shape, 0)


def _upper_half_masks(shape):
    row = _row_iota(shape)
    masks = []
    m = 1
    while m < CHUNK:
        masks.append((row // m) % 2 == 1)
        m *= 2
    return masks


def _segment_scans(g, uppers):
    rows = g.shape[0]
    pre, tot = g, g
    out = [(pre, tot)]
    m = 1
    while m < CHUNK:
        upper = uppers[len(out) - 1]
        from_lower = pltpu.roll(tot, m, 0)
        from_upper = pltpu.roll(tot, rows - m, 0)
        pre = pre + jnp.where(upper, from_lower, 0.0)
        tot = tot + jnp.where(upper, from_lower, from_upper)
        out.append((pre, tot))
        m *= 2
    return out


def _dot_nt(a, b):
    return lax.dot_general(a, b, (((1,), (1,)), ((), ())), preferred_element_type=F32)


def _dot_tn(a, b):
    return lax.dot_general(a, b, (((0,), (0,)), ((), ())), preferred_element_type=F32)


def _gdn_prep_kernel(q_ref, k_ref, v_ref, ab_ref, cw_ref, cs_ref, alog_ref, dt_ref,
                     u_ref, w_ref, qd_ref, kd_ref, qk_ref, gl_ref, carry_ref, *, nb, r):
    @pl.when(pl.program_id(1) == 0)
    def _():
        carry_ref[...] = cs_ref[...]

    n = nb * r
    nc = n // CHUNK

    def conv(x_ref, j):
        w = cw_ref[:, j * GDN_QK:(j + 1) * GDN_QK]
        outs = []
        for b in range(nb):
            x = x_ref[b]
            xp = jnp.concatenate([carry_ref[b, j], x], axis=0)
            y = x * w[3:4]
            for d in range(1, CONV_W):
                y = y + xp[8 - d:8 - d + r] * w[3 - d:4 - d]
            carry_ref[b, j] = x[r - 8:]
            outs.append(_silu(y))
        return outs[0] if nb == 1 else jnp.concatenate(outs, axis=0)

    q_all, k_all, v_all = conv(q_ref, 0), conv(k_ref, 1), conv(v_ref, 2)

    ab = ab_ref[...].reshape(n, 128)
    x = ab + dt_ref[...]
    softplus = jnp.maximum(x, 0.0) + jnp.log(1.0 + jnp.exp(-jnp.abs(x)))
    g_blk = (-LOG2_E) * jnp.exp(alog_ref[...]) * softplus
    gam_blk = _segment_scans(g_blk, _upper_half_masks(g_blk.shape))[-1][0]
    gam_t = gam_blk.T
    gam3_blk = gam_blk.reshape(nc, CHUNK, 128)
    beta3_blk = _sigmoid(ab).reshape(nc, CHUNK, 128)

    row = lax.broadcasted_iota(jnp.int32, (1, CHUNK, CHUNK), 1)
    col = lax.broadcasted_iota(jnp.int32, (1, CHUNK, CHUNK), 2)
    eye = (row == col).astype(F32)

    def bmm(a, b):
        return jnp.einsum('cij,cjk->cik', a.astype(BF16), b.astype(BF16), preferred_element_type=F32)

    def bmm_nt(a, b):
        return jnp.einsum('cid,cjd->cij', a.astype(BF16), b.astype(BF16), preferred_element_type=F32)

    for h in range(GDN_HEADS):
        sl = slice(h * GDN_DK, (h + 1) * GDN_DK)
        q, k, v = q_all[:, sl], k_all[:, sl], v_all[:, sl]
        q = q * lax.rsqrt(jnp.sum(q * q, axis=-1, keepdims=True) + NORM_EPS) * (GDN_DK ** -0.5)
        k = k * lax.rsqrt(jnp.sum(k * k, axis=-1, keepdims=True) + NORM_EPS)
        q, k, v = (a.reshape(nc, CHUNK, GDN_DK) for a in (q, k, v))
        gam = gam3_blk[:, :, h:h + 1]
        beta = beta3_blk[:, :, GDN_HEADS + h:GDN_HEADS + h + 1]
        gam_row = jnp.stack([gam_t[h:h + 1, c * CHUNK:(c + 1) * CHUNK] for c in range(nc)], axis=0)
        decay = jnp.where(row >= col, jnp.exp2(jnp.minimum(gam - gam_row, 0.0)), 0.0)
        a = jnp.where(row > col, beta * bmm_nt(k, k) * decay, 0.0)
        t_inv = eye - a
        pw = a
        m = 1
        while 2 * m < CHUNK:
            pw = bmm(pw, pw)
            t_inv = t_inv + bmm(t_inv, pw)
            m *= 2
        e_gam = jnp.exp2(gam)
        gam_last = gam[:, CHUNK - 1:CHUNK, :]
        u_ref[:, h] = bmm(t_inv, v * beta).reshape(nb, r, GDN_DV)
        w_ref[:, h] = bmm(t_inv, k * (beta * e_gam)).astype(BF16).reshape(nb, r, GDN_DK)
        qd_ref[:, h] = (q * e_gam).astype(BF16).reshape(nb, r, GDN_DK)
        kd_ref[:, h] = (k * jnp.exp2(gam_last - gam)).astype(BF16).reshape(nb, r, GDN_DK)
        qk_ref[:, h] = (bmm_nt(q, k) * decay).astype(BF16).reshape(nb, r, CHUNK)
        gl_ref[:, h] = jnp.broadcast_to(jnp.exp2(gam_last), (nc, 1, 128)).reshape(nb, r // CHUNK, 1, 128)


def _gdn_scan_kernel(u_ref, w_ref, qd_ref, kd_ref, qk_ref, gl_ref, s0_ref, o_ref, sf_ref, s_ref, *, nb):
    c_idx = pl.program_id(1)

    @pl.when(c_idx == 0)
    def _():
        s_ref[...] = s0_ref[...]

    chains = [(b, h) for b in range(nb) for h in range(GDN_HEADS)]
    s_old = [s_ref[b, h] for b, h in chains]
    s_bf = [s.astype(BF16) for s in s_old]
    v_new = [u_ref[b, h] - jnp.dot(w_ref[b, h], sb, preferred_element_type=F32)
             for (b, h), sb in zip(chains, s_bf)]
    v_bf = [v.astype(BF16) for v in v_new]
    for (b, h), s, sb, vb in zip(chains, s_old, s_bf, v_bf):
        o = (jnp.dot(qd_ref[b, h], sb, preferred_element_type=F32)
             + jnp.dot(qk_ref[b, h], vb, preferred_element_type=F32))
        o_ref[b, :, h * GDN_DV:(h + 1) * GDN_DV] = o
        s_ref[b, h] = gl_ref[b, h, 0] * s + _dot_tn(kd_ref[b, h], vb)

    @pl.when(c_idx == pl.num_programs(1) - 1)
    def _():
        sf_ref[...] = s_ref[...]


def _gdn_prep(p3, conv_w, conv_state8, alog, dt):
    b, t, _ = p3.shape
    r = min(GDN_PREP_ROWS, t)
    nb = max(1, min(b, GDN_PREP_ROWS // r))
    nt = t // r
    nc = t // CHUNK
    blk = lambda j: pl.BlockSpec((nb, r, GDN_QK), lambda i, c: (i, c, COL_GDN // GDN_QK + j))
    vec = pl.BlockSpec((1, 128), lambda i, c: (0, 0))
    head_spec = lambda w: pl.BlockSpec((nb, GDN_HEADS, r, w), lambda i, c: (i, 0, c, 0))
    head_shape = lambda w, dt_: jax.ShapeDtypeStruct((b, GDN_HEADS, t, w), dt_)
    u, w, qd, kd, qk, gl = pl.pallas_call(
        functools.partial(_gdn_prep_kernel, nb=nb, r=r),
        grid=(b // nb, nt),
        in_specs=[blk(0), blk(1), blk(2),
                  pl.BlockSpec((nb, r, 128), lambda i, c: (i, c, COL_GAB // 128)),
                  pl.BlockSpec((CONV_W, 3 * GDN_QK), lambda i, c: (0, 0)),
                  pl.BlockSpec((nb, 3, 8, GDN_QK), lambda i, c: (i, 0, 0, 0)),
                  vec, vec],
        out_specs=[head_spec(GDN_DV), head_spec(GDN_DK), head_spec(GDN_DK), head_spec(GDN_DK),
                   head_spec(CHUNK),
                   pl.BlockSpec((nb, GDN_HEADS, r // CHUNK, 1, 128), lambda i, c: (i, 0, c, 0, 0))],
        out_shape=[head_shape(GDN_DV, F32), head_shape(GDN_DK, BF16), head_shape(GDN_DK, BF16),
                   head_shape(GDN_DK, BF16), head_shape(CHUNK, BF16),
                   jax.ShapeDtypeStruct((b, GDN_HEADS, nc, 1, 128), F32)],
        scratch_shapes=[pltpu.VMEM((nb, 3, 8, GDN_QK), F32)],
        compiler_params=_cparams(("parallel", "arbitrary")),
        name="gdn_prep",
    )(p3, p3, p3, p3, conv_w, conv_state8, alog, dt)
    return u, w, qd, kd, qk, gl


def _hgrn_prep_kernel(q_ref, f_ref, lb_ref, att_ref, qe_ref, ke_ref, dec_ref, *, nb, r):
    n = nb * r
    nc = n // CHUNK
    row = lax.broadcasted_iota(jnp.int32, (1, CHUNK, CHUNK), 1)
    col = lax.broadcasted_iota(jnp.int32, (1, CHUNK, CHUNK), 2)
    uppers = _upper_half_masks((n, HGRN_DK))
    halves = [1 << lvl for lvl in range(len(uppers))]
    pairs = [(row // (2 * m) == col // (2 * m)) & ((row // m) % 2 == 1) & ((col // m) % 2 == 0) for m in halves]

    def bmm_nt(a, b):
        a3, b3 = (x.astype(BF16).reshape(nc, CHUNK, HGRN_DK) for x in (a, b))
        return jnp.einsum('cid,cjd->cij', a3, b3, preferred_element_type=F32)

    for h in range(HGRN_HEADS):
        sl = slice(h * HGRN_DK, (h + 1) * HGRN_DK)
        lb = lb_ref[:, sl]
        f = lb + (1.0 - lb) * _sigmoid(f_ref[:, :, sl].reshape(n, HGRN_DK))
        q = _silu(q_ref[:, :, sl].reshape(n, HGRN_DK)) * (HGRN_DK ** -0.5)
        k = 1.0 - f
        scans = _segment_scans(jnp.log2(f), uppers)
        cb, c_tot = scans[-1]

        att = jnp.where(row == col, bmm_nt(q, k), 0.0)
        for lvl in range(len(halves)):
            pre_m, tot_m = scans[lvl]
            att = att + jnp.where(pairs[lvl], bmm_nt(q * jnp.exp2(pre_m), k * jnp.exp2(tot_m - pre_m)), 0.0)

        att_ref[:, h] = att.astype(BF16).reshape(nb, r, CHUNK)
        qe_ref[:, h] = (q * jnp.exp2(cb)).astype(BF16).reshape(nb, r, HGRN_DK)
        ke_ref[:, h] = (k * jnp.exp2(c_tot - cb)).astype(BF16).reshape(nb, r, HGRN_DK)
        dec_ref[:, h] = jnp.exp2(c_tot).reshape(nc, CHUNK, HGRN_DK)[:, 0:1, :].reshape(nb, r // CHUNK, 1, HGRN_DK)


def _hgrn_scan_kernel(att_ref, qe_ref, ke_ref, dec_ref, v_ref, s0_ref, o_ref, sf_ref, st_ref, *, nb):
    c_idx = pl.program_id(1)

    @pl.when(c_idx == 0)
    def _():
        st_ref[...] = s0_ref[...]

    chains = [(b, h) for b in range(nb) for h in range(HGRN_HEADS)]
    sls = [slice(h * HGRN_DV, (h + 1) * HGRN_DV) for h in range(HGRN_HEADS)]
    st_old = [st_ref[b, h] for b, h in chains]
    vs = [v_ref[b, :, sls[h]].astype(BF16) for b, h in chains]
    o_st = [_dot_nt(qe_ref[b, h], st.astype(BF16)) for (b, h), st in zip(chains, st_old)]
    for (b, h), st, v, o1 in zip(chains, st_old, vs, o_st):
        o_ref[b, :, sls[h]] = o1 + jnp.dot(att_ref[b, h], v, preferred_element_type=F32)
        st_ref[b, h] = st * dec_ref[b, h, 0] + _dot_tn(v, ke_ref[b, h])

    @pl.when(c_idx == pl.num_programs(1) - 1)
    def _():
        sf_ref[...] = st_ref[...]


def _hgrn_prep(p3, lb):
    b, t, _ = p3.shape
    r = min(HGRN_PREP_ROWS, t)
    nb = max(1, min(b, HGRN_PREP_ROWS // r))
    nc = t // CHUNK
    col0 = (COL_GDN + 4 * GDN_QK) // HGRN_KW
    blk = lambda j: pl.BlockSpec((nb, r, HGRN_KW), lambda i, c: (i, c, col0 + j))
    head_spec = lambda w: pl.BlockSpec((nb, HGRN_HEADS, r, w), lambda i, c: (i, 0, c, 0))
    head_shape = lambda w: jax.ShapeDtypeStruct((b, HGRN_HEADS, t, w), BF16)
    att, qe, ke, dec = pl.pallas_call(
        functools.partial(_hgrn_prep_kernel, nb=nb, r=r),
        grid=(b // nb, t // r),
        in_specs=[blk(0), blk(1), pl.BlockSpec((1, HGRN_KW), lambda i, c: (0, 0))],
        out_specs=[head_spec(CHUNK), head_spec(HGRN_DK), head_spec(HGRN_DK),
                   pl.BlockSpec((nb, HGRN_HEADS, r // CHUNK, 1, HGRN_DK), lambda i, c: (i, 0, c, 0, 0))],
        out_shape=[head_shape(CHUNK), head_shape(HGRN_DK), head_shape(HGRN_DK),
                   jax.ShapeDtypeStruct((b, HGRN_HEADS, nc, 1, HGRN_DK), F32)],
        compiler_params=_cparams(("parallel", "parallel")),
        name="hgrn_prep",
    )(p3, p3, lb)
    return att, qe, ke, dec


def _rec_scan_kernel(u_ref, w_ref, qd_ref, kd_ref, qk_ref, gl_ref, sg0_ref,
                     att_ref, qe_ref, ke_ref, dec_ref, v_ref, sh0_ref,
                     og_ref, sgf_ref, oh_ref, shf_ref, sg_ref, sh_ref, *, nb):
    _gdn_scan_kernel(u_ref, w_ref, qd_ref, kd_ref, qk_ref, gl_ref, sg0_ref, og_ref, sgf_ref, sg_ref, nb=nb)
    _hgrn_scan_kernel(att_ref, qe_ref, ke_ref, dec_ref, v_ref, sh0_ref, oh_ref, shf_ref, sh_ref, nb=nb)


def _rec_scan(gdn_parts, hgrn_parts, p3, s0_gdn, s0t_hgrn):
    b, t, _ = p3.shape
    nc = t // CHUNK
    sb = min(b, 8)
    col_v = (COL_GDN + 4 * GDN_QK) // HGRN_KW + 2
    chunk_spec = lambda w: pl.BlockSpec((sb, GDN_HEADS, CHUNK, w), lambda i, c: (i, 0, c, 0))
    per_chunk = pl.BlockSpec((sb, GDN_HEADS, 1, 1, 128), lambda i, c: (i, 0, c, 0, 0))
    state_spec = pl.BlockSpec((sb, GDN_HEADS, 128, 128), lambda i, c: (i, 0, 0, 0))
    out_spec = pl.BlockSpec((sb, CHUNK, GDN_VW), lambda i, c: (i, c, 0))
    state_shape = jax.ShapeDtypeStruct((b, GDN_HEADS, 128, 128), F32)
    return pl.pallas_call(
        functools.partial(_rec_scan_kernel, nb=sb),
        grid=(b // sb, nc),
        in_specs=[chunk_spec(GDN_DV), chunk_spec(GDN_DK), chunk_spec(GDN_DK), chunk_spec(GDN_DK),
                  chunk_spec(CHUNK), per_chunk, state_spec,
                  chunk_spec(CHUNK), chunk_spec(HGRN_DK), chunk_spec(HGRN_DK), per_chunk,
                  pl.BlockSpec((sb, CHUNK, HGRN_KW), lambda i, c: (i, c, col_v)), state_spec],
        out_specs=[out_spec, state_spec, out_spec, state_spec],
        out_shape=[jax.ShapeDtypeStruct((b, t, GDN_VW), F32), state_shape,
                   jax.ShapeDtypeStruct((b, t, HGRN_KW), F32), state_shape],
        scratch_shapes=[pltpu.VMEM((sb, GDN_HEADS, 128, 128), F32), pltpu.VMEM((sb, HGRN_HEADS, 128, 128), F32)],
        compiler_params=_cparams(("parallel", "arbitrary")),
        name="rec_scan",
    )(*gdn_parts, s0_gdn, *hgrn_parts, p3, s0t_hgrn)


def _merge_kernel(x_ref, oa_ref, ob_ref, zb_ref, oc_ref, zc_ref, g0_ref, g1_ref, g2_ref, ngb_ref, ngc_ref,
                  wa_ref, wb_ref, wc_ref, wo_ref, y_ref):
    def branch(o, w_ref, g_ref):
        gate = _sigmoid(g_ref[...].astype(F32))
        return gate * jnp.dot(o.astype(BF16), w_ref[...], preferred_element_type=F32)

    def normed(o_ref, z_ref, ng_ref):
        return jnp.concatenate(
            [_rms(o_ref[:, h * 128:(h + 1) * 128], ng_ref[...]) * _silu(z_ref[:, h * 128:(h + 1) * 128])
             for h in range(GDN_HEADS)], axis=-1)

    mixed = (branch(oa_ref[...], wa_ref, g0_ref) + branch(normed(ob_ref, zb_ref, ngb_ref), wb_ref, g1_ref)
             + branch(normed(oc_ref, zc_ref, ngc_ref), wc_ref, g2_ref))
    y_ref[...] = x_ref[...] + jnp.dot(mixed.astype(BF16), wo_ref[...], preferred_element_type=F32)


def _merge(x, oa, ob, oc, p, pg, ngb, ngc, wa, wb, wc, wo):
    n = x.shape[0]
    tm = _row_tile(n, 512)
    row = lambda w: pl.BlockSpec((tm, w), lambda i: (i, 0))
    gate = lambda j: pl.BlockSpec((tm, D_MODEL), lambda i: (i, j))
    zb_spec = pl.BlockSpec((tm, GDN_VW), lambda i: (i, COL_GDN // GDN_VW + 3))
    zc_spec = pl.BlockSpec((tm, HGRN_KW), lambda i: (i, (COL_GDN + 4 * GDN_QK) // HGRN_KW + 3))
    vec = pl.BlockSpec((1, 128), lambda i: (0, 0))
    wsp = lambda k: pl.BlockSpec((k, D_MODEL), lambda i: (0, 0))
    return pl.pallas_call(
        _merge_kernel,
        grid=(n // tm,),
        in_specs=[row(D_MODEL), row(MLA_VW), row(GDN_VW), zb_spec, row(HGRN_KW), zc_spec,
                  gate(0), gate(1), gate(2), vec, vec,
                  wsp(MLA_VW), wsp(GDN_VW), wsp(HGRN_KW), wsp(D_MODEL)],
        out_specs=row(D_MODEL),
        out_shape=jax.ShapeDtypeStruct((n, D_MODEL), F32),
        compiler_params=_cparams(("parallel",)),
        name="merge_out",
    )(x, oa, ob, p, oc, p, pg, pg, pg, ngb, ngc, wa, wb, wc, wo)


def _router_kernel(x_ref, g_ref, w_ref, b_ref, cw_ref, cwt_ref, cnt_ref):
    h = _rms(x_ref[...], g_ref[...])
    logits = jnp.dot(h, w_ref[...], preferred_element_type=F32, precision=lax.Precision.HIGHEST) + b_ref[...]
    lane = lax.broadcasted_iota(jnp.int32, logits.shape, 1)
    valid = lane < N_EXPERTS
    neg = -jnp.inf
    l1 = jnp.where(valid, logits, neg)
    m1 = jnp.max(l1, axis=-1, keepdims=True)
    i1 = jnp.min(jnp.where(l1 == m1, lane, 128), axis=-1, keepdims=True)
    l2 = jnp.where(lane == i1, neg, l1)
    m2 = jnp.max(l2, axis=-1, keepdims=True)
    i2 = jnp.min(jnp.where(l2 == m2, lane, 128), axis=-1, keepdims=True)
    e2 = jnp.exp(m2 - m1)
    den = 1.0 + e2
    cw = jnp.where(lane == i1, 1.0 / den, 0.0) + jnp.where(lane == i2, e2 / den, 0.0)
    cw_ref[...] = cw
    cwt_ref[...] = cw.T
    cnt_ref[0] = jnp.sum((cw > 0.0).astype(F32), axis=0, keepdims=True)


def _router(x, g, w, b, tm):
    n = x.shape[0]
    return pl.pallas_call(
        _router_kernel,
        grid=(n // tm,),
        in_specs=[pl.BlockSpec((tm, D_MODEL), lambda i: (i, 0)),
                  pl.BlockSpec((1, D_MODEL), lambda i: (0, 0)),
                  pl.BlockSpec((D_MODEL, 128), lambda i: (0, 0)),
                  pl.BlockSpec((1, 128), lambda i: (0, 0))],
        out_specs=[pl.BlockSpec((tm, 128), lambda i: (i, 0)),
                   pl.BlockSpec((128, tm), lambda i: (0, i)),
                   pl.BlockSpec((1, 1, 128), lambda i: (i, 0, 0))],
        out_shape=[jax.ShapeDtypeStruct((n, 128), F32),
                   jax.ShapeDtypeStruct((128, n), F32),
                   jax.ShapeDtypeStruct((n // tm, 1, 128), F32)],
        compiler_params=_cparams(("parallel",)),
        name="moe_router",
    )(x, g, w, b)


def _moe_kernel(cnt_ref, x_ref, g_ref, cw_ref, cwt_ref, wg_ref, wu_ref, wd_ref, y_ref,
                h_ref, rcol_ref, rrow_ref, *, t, br):
    i, e = pl.program_id(0), pl.program_id(1)

    @pl.when(e == 0)
    def _():
        x = x_ref[...]
        h_ref[...] = _rms(x, g_ref[...]).astype(BF16)
        y_ref[...] = x
        r = lax.broadcasted_iota(jnp.int32, (t, t), 0)
        c = lax.broadcasted_iota(jnp.int32, (t, t), 1)
        on = cw_ref[...] > 0.0
        rank = jnp.dot((c < r).astype(BF16), on.astype(BF16), preferred_element_type=F32)
        rcol_ref[...] = jnp.where(on, rank.astype(jnp.int32), -1)
        on_t = cwt_ref[...] > 0.0
        rank_t = jnp.dot(on_t.astype(BF16), (r < c).astype(BF16), preferred_element_type=F32)
        rrow_ref[...] = jnp.where(on_t, rank_t.astype(jnp.int32), -1)

    count = cnt_ref[i * N_EXPERTS + e]
    sel = lax.broadcasted_iota(jnp.int32, (1, 128), 1) == e
    rank_c = jnp.sum(jnp.where(sel, rcol_ref[...], 0), axis=-1, keepdims=True)
    rank_r = rrow_ref[pl.ds(e, 1), :]
    w_r = cwt_ref[pl.ds(e, 1), :]

    def expert_block(base, rows_n):
        rows = lax.broadcasted_iota(jnp.int32, (rows_n, t), 0) + base
        pick = rows == rank_r
        xc = jnp.dot(pick.astype(BF16), h_ref[...], preferred_element_type=F32).astype(BF16)
        a = jnp.dot(xc, wg_ref[0], preferred_element_type=F32)
        b = jnp.dot(xc, wu_ref[0], preferred_element_type=F32)
        yc = jnp.dot((_silu(a) * b).astype(BF16), wd_ref[0], preferred_element_type=F32)
        w_rows = jnp.sum(jnp.where(pick, w_r, 0.0), axis=-1, keepdims=True)
        cols = lax.broadcasted_iota(jnp.int32, (t, rows_n), 1) + base
        y_ref[...] += jnp.dot((cols == rank_c).astype(BF16), (yc * w_rows).astype(BF16),
                              preferred_element_type=F32)

    lo = 0
    for size in br:
        @pl.when((count > lo) & (count <= size))
        def _(size=size):
            expert_block(0, size)
        lo = size

    @pl.when(count > br[-1])
    def _():
        def body(j, carry):
            expert_block(j * br[-1], br[-1])
            return carry

        lax.fori_loop(0, lax.div(count + (br[-1] - 1), br[-1]), body, 0)


def _moe(x, g, cw, cwt, counts, wg, wu, wd, t):
    n = x.shape[0]
    ne, _, ff = wg.shape
    br = MOE_BLOCK_ROWS
    grid_spec = pltpu.PrefetchScalarGridSpec(
        num_scalar_prefetch=1,
        grid=(n // t, ne),
        in_specs=[pl.BlockSpec((t, D_MODEL), lambda i, e, cnt: (i, 0)),
                  pl.BlockSpec((1, D_MODEL), lambda i, e, cnt: (0, 0)),
                  pl.BlockSpec((t, 128), lambda i, e, cnt: (i, 0)),
                  pl.BlockSpec((128, t), lambda i, e, cnt: (0, i)),
                  pl.BlockSpec((1, D_MODEL, ff), lambda i, e, cnt: (e, 0, 0)),
                  pl.BlockSpec((1, D_MODEL, ff), lambda i, e, cnt: (e, 0, 0)),
                  pl.BlockSpec((1, ff, D_MODEL), lambda i, e, cnt: (e, 0, 0))],
        out_specs=pl.BlockSpec((t, D_MODEL), lambda i, e, cnt: (i, 0)),
        scratch_shapes=[pltpu.VMEM((t, D_MODEL), BF16),
                        pltpu.VMEM((t, 128), jnp.int32),
                        pltpu.VMEM((128, t), jnp.int32)])
    return pl.pallas_call(
        functools.partial(_moe_kernel, t=t, br=br),
        grid_spec=grid_spec,
        out_shape=jax.ShapeDtypeStruct((n, D_MODEL), F32),
        compiler_params=_cparams(("parallel", "arbitrary")),
        name="moe",
    )(counts, x, g, cw, cwt, wg, wu, wd)


def _ffn_kernel(x_ref, g_ref, wg_ref, wu_ref, wd_ref, y_ref, h_ref):
    e = pl.program_id(1)

    @pl.when(e == 0)
    def _():
        x = x_ref[...]
        h_ref[...] = _rms(x, g_ref[...]).astype(BF16)
        y_ref[...] = x

    h = h_ref[...]
    a = jnp.dot(h, wg_ref[0], preferred_element_type=F32)
    b = jnp.dot(h, wu_ref[0], preferred_element_type=F32)
    y_ref[...] += jnp.dot((_silu(a) * b).astype(BF16), wd_ref[0], preferred_element_type=F32)


def _ffn(x, g, wg, wu, wd):
    n = x.shape[0]
    ne, _, ff = wg.shape
    tm = _row_tile(n, 512)
    return pl.pallas_call(
        _ffn_kernel,
        grid=(n // tm, ne),
        in_specs=[pl.BlockSpec((tm, D_MODEL), lambda i, e: (i, 0)),
                  pl.BlockSpec((1, D_MODEL), lambda i, e: (0, 0)),
                  pl.BlockSpec((1, D_MODEL, ff), lambda i, e: (e, 0, 0)),
                  pl.BlockSpec((1, D_MODEL, ff), lambda i, e: (e, 0, 0)),
                  pl.BlockSpec((1, ff, D_MODEL), lambda i, e: (e, 0, 0))],
        out_specs=pl.BlockSpec((tm, D_MODEL), lambda i, e: (i, 0)),
        out_shape=jax.ShapeDtypeStruct((n, D_MODEL), F32),
        scratch_shapes=[pltpu.VMEM((tm, D_MODEL), BF16)],
        compiler_params=_cparams(("parallel", "arbitrary")),
        name="ffn",
    )(x, g, wg, wu, wd)


def _pad_lanes(x, left, total):
    return jnp.pad(x, [(0, 0)] * (x.ndim - 1) + [(left, total - left - x.shape[-1])])


def _pack_w_in(w):
    cq, ckv, kr, gq, gk, gv, gz, ga, gb, hq, hf, hi, hg, gates = jnp.split(
        w, np.cumsum(SPLIT_SIZES)[:-1].tolist(), axis=-1)
    kr_blk = _pad_lanes(kr, ROPE_LANE0, 128)
    ab_blk = _pad_lanes(jnp.concatenate([ga, gb], axis=-1), 0, 256)
    return jnp.concatenate([cq, ckv, kr_blk, ab_blk, gq, gk, gv, gz, hq, hf, hi, hg, gates], axis=-1).astype(BF16)


def _rope_tables(n_pos):
    half = MLA_ROPE // 2
    inv = ROPE_THETA ** (-jnp.arange(half, dtype=F32) / half)
    ang = jnp.arange(n_pos, dtype=F32)[:, None] * inv[None, :]
    cos, sin = jnp.cos(ang), jnp.sin(ang)
    one = jnp.ones((n_pos, MLA_NOPE), F32)
    zero = jnp.zeros((n_pos, MLA_NOPE), F32)
    tail = jnp.zeros((n_pos, HEAD_PAD - MLA_QK_HEAD), F32)
    z16 = jnp.zeros((n_pos, half), F32)
    c = jnp.concatenate([one, cos, cos, tail], axis=-1)
    s1 = jnp.concatenate([zero, -sin, z16, tail], axis=-1)
    s2 = jnp.concatenate([zero, z16, sin, tail], axis=-1)
    return c, s1, s2


def _layer_weights(l, a):
    f = {}
    f['mixer_g'] = a['mixer_norm_g'][l][None]
    f['w_in'] = _pack_w_in(a['w_in'][l])
    f['gq'] = a['mla_q_norm_g'][l][None]
    f['gkv'] = a['mla_kv_norm_g'][l][None]
    wq = a['mla_w_q_up'][l].reshape(MLA_Q_LORA, MLA_HEADS, MLA_QK_HEAD)
    f['wq'] = _pad_lanes(wq, 0, HEAD_PAD).reshape(MLA_Q_LORA, QK_W).astype(BF16)
    wkv = a['mla_w_kv_up'][l].reshape(MLA_KV_LORA, MLA_HEADS, MLA_NOPE + MLA_V)
    f['wk'] = _pad_lanes(wkv[:, :, :MLA_NOPE], 0, HEAD_PAD).reshape(MLA_KV_LORA, QK_W).astype(BF16)
    f['wkt'] = f['wk'].T
    f['wv'] = wkv[:, :, MLA_NOPE:].reshape(MLA_KV_LORA, MLA_VW).astype(BF16)
    f['wv_pad'] = _pad_lanes(wkv[:, :, MLA_NOPE:], 0, HEAD_PAD).reshape(MLA_KV_LORA, QK_W).astype(BF16)
    f['hq'] = _pad_lanes(a['mla_q_head_norm_g'][l][None], 0, HEAD_PAD)
    hk = a['mla_k_head_norm_g'][l][None]
    f['hk_nope'] = _pad_lanes(hk[:, :MLA_NOPE], 0, HEAD_PAD)
    f['hk_rope'] = _pad_lanes(hk[:, MLA_NOPE:], ROPE_LANE0, HEAD_PAD)
    f['wo_a'] = a['mla_w_o'][l].astype(BF16)
    f['conv_w'] = a['gdn_conv_w'][l]
    f['alog'] = _pad_lanes(a['gdn_a_log'][l][None], 0, 128)
    f['dt'] = _pad_lanes(a['gdn_dt_bias'][l][None], 0, 128)
    f['gdn_g'] = a['gdn_norm_g'][l][None]
    f['wo_b'] = a['gdn_w_o'][l].astype(BF16)
    f['hgrn_g'] = a['hgrn_norm_g'][l][None]
    f['wo_c'] = a['hgrn_w_o'][l].astype(BF16)
    f['w_out'] = a['w_out'][l].astype(BF16)
    f['ffn_g'] = a['ffn_norm_g'][l][None]
    if l % 2 == 0:
        wg, wu, wd = a['dense_w_gate'][l // 2], a['dense_w_up'][l // 2], a['dense_w_down'][l // 2]
        ff = wg.shape[1]
        half = ff // 2
        f['ffn'] = (jnp.moveaxis(wg.reshape(D_MODEL, 2, half), 1, 0).astype(BF16),
                    jnp.moveaxis(wu.reshape(D_MODEL, 2, half), 1, 0).astype(BF16),
                    wd.reshape(2, half, D_MODEL).astype(BF16))
        f['router'] = None
    else:
        f['ffn'] = (a['moe_w_gate'][l // 2].astype(BF16), a['moe_w_up'][l // 2].astype(BF16),
                    a['moe_w_down'][l // 2].astype(BF16))
        f['router'] = (_pad_lanes(a['moe_w_router'][l // 2], 0, 128),
                       _pad_lanes(a['moe_b_router'][l // 2][None], 0, 128))
    return f


def _trunk_layer(x, b, t, f, lb, tabs, past):
    n = b * t
    p, p_gate = _in_proj(x, f['mixer_g'], f['w_in'])
    p3 = p.reshape(b, t, P_MAIN)

    fresh = past['ckv'] is None
    q, ckv, kr = _mla_pre(p, b, t, f['gq'], f['gkv'], f['wq'], f['hq'], f['hk_rope'], tabs, transposed=fresh)
    if fresh:
        k_all, vt_all = _kv_up(ckv, kr, b, t, f['wk'], f['wv_pad'], f['hk_nope'])
        o_a = _attention_t(q, k_all.reshape(b, t, QK_W), vt_all)
    else:
        past_len = past['ckv'].shape[1]
        t_pad = -(-t // 128) * 128
        pad_rows = lambda a: jnp.pad(a.reshape(b, t, -1), ((0, 0), (0, t_pad - t), (0, 0)))
        kv = _kv_hist(past['ckv'], past['kr'], past_len + t_pad, 0, f['wkt'], f['wv'], f['hk_nope'])
        kt_all, v_all = _kv_hist(pad_rows(ckv), pad_rows(kr), past_len + t_pad, past_len,
                                 f['wkt'], f['wv'], f['hk_nope'], into=kv)
        o_a = _attention_hist(q.reshape(b, t, QK_W), kt_all, v_all, past_len, past_len + t)

    conv8 = jnp.pad(past['conv'].reshape(b, CONV_W - 1, 3, GDN_QK).transpose(0, 2, 1, 3),
                    ((0, 0), (0, 0), (8 - (CONV_W - 1), 0), (0, 0)))
    gdn_parts = _gdn_prep(p3, f['conv_w'], conv8, f['alog'], f['dt'])
    gdn_conv = p3[:, t - (CONV_W - 1):, COL_GDN:COL_GDN + 3 * GDN_QK]

    hgrn_parts = _hgrn_prep(p3, lb)
    o_b, gdn_s, o_c, hgrn_st = _rec_scan(gdn_parts, hgrn_parts, p3, past['gdn'],
                                         jnp.swapaxes(past['hgrn'], -1, -2))
    hgrn_s = jnp.swapaxes(hgrn_st, -1, -2)

    x = _merge(x, o_a.reshape(n, -1), o_b.reshape(n, -1), o_c.reshape(n, -1), p, p_gate, f['gdn_g'],
               f['hgrn_g'], f['wo_a'], f['wo_b'], f['wo_c'], f['w_out'])

    wg, wu, wd = f['ffn']
    if f['router'] is None:
        x = _ffn(x, f['ffn_g'], wg, wu, wd)
    else:
        t_moe = _row_tile(n, MOE_TILE)
        cw, cwt, cnt = _router(x, f['ffn_g'], *f['router'], t_moe)
        counts = cnt[:, 0, :N_EXPERTS].astype(jnp.int32).reshape(-1)
        x = _moe(x, f['ffn_g'], cw, cwt, counts, wg, wu, wd, t_moe)

    new_ckv = ckv.reshape(b, t, MLA_KV_LORA)
    new_kr = kr.reshape(b, t, HEAD_PAD)[:, :, ROPE_LANE0:ROPE_LANE0 + MLA_ROPE]
    return x, (new_ckv, new_kr, gdn_s, gdn_conv, hgrn_s)


def kernel(x_prompt, x_sample, cache_mla_ckv, cache_mla_krope, state_gdn, state_gdn_conv, state_hgrn,
           mixer_norm_g, w_in, mla_q_norm_g, mla_w_q_up, mla_kv_norm_g, mla_w_kv_up,
           mla_q_head_norm_g, mla_k_head_norm_g, mla_w_o,
           gdn_conv_w, gdn_a_log, gdn_dt_bias, gdn_norm_g, gdn_w_o,
           hgrn_lb_logits, hgrn_norm_g, hgrn_w_o, w_out, ffn_norm_g,
           dense_w_gate, dense_w_up, dense_w_down,
           moe_w_router, moe_b_router, moe_w_gate, moe_w_up, moe_w_down):
    a = dict(mixer_norm_g=mixer_norm_g, w_in=w_in, mla_q_norm_g=mla_q_norm_g, mla_w_q_up=mla_w_q_up,
             mla_kv_norm_g=mla_kv_norm_g, mla_w_kv_up=mla_w_kv_up, mla_q_head_norm_g=mla_q_head_norm_g,
             mla_k_head_norm_g=mla_k_head_norm_g, mla_w_o=mla_w_o, gdn_conv_w=gdn_conv_w,
             gdn_a_log=gdn_a_log, gdn_dt_bias=gdn_dt_bias, gdn_norm_g=gdn_norm_g, gdn_w_o=gdn_w_o,
             hgrn_norm_g=hgrn_norm_g, hgrn_w_o=hgrn_w_o, w_out=w_out, ffn_norm_g=ffn_norm_g,
             dense_w_gate=dense_w_gate, dense_w_up=dense_w_up, dense_w_down=dense_w_down,
             moe_w_router=moe_w_router, moe_b_router=moe_b_router, moe_w_gate=moe_w_gate,
             moe_w_up=moe_w_up, moe_w_down=moe_w_down)
    depth = w_in.shape[0]
    lb_soft = jax.nn.softmax(hgrn_lb_logits.astype(F32), axis=0)
    hgrn_lb = jnp.cumsum(lb_soft, axis=0) - lb_soft[0]

    b_p, t_p = x_prompt.shape[:2]
    b_s, t_s = x_sample.shape[:2]
    past_len = cache_mla_ckv.shape[2]
    tab_all = _rope_tables(max(t_p, past_len + t_s))
    tabs_p = tuple(tb[:t_p] for tb in tab_all)
    tabs_s = tuple(tb[past_len:past_len + t_s] for tb in tab_all)

    xp = x_prompt.reshape(b_p * t_p, D_MODEL)
    xs = x_sample.reshape(b_s * t_s, D_MODEL)
    past_p = dict(ckv=None, kr=None,
                  gdn=jnp.zeros((b_p, GDN_HEADS, GDN_DK, GDN_DV), F32),
                  conv=jnp.zeros((b_p, CONV_W - 1, 3 * GDN_QK), F32),
                  hgrn=jnp.zeros((b_p, HGRN_HEADS, HGRN_DK, HGRN_DV), F32))
    st_p, st_s = [], []
    for l in range(depth):
        f = _layer_weights(l, a)
        lb = hgrn_lb[l][None]
        past_s = dict(ckv=cache_mla_ckv[l], kr=cache_mla_krope[l], gdn=state_gdn[l],
                      conv=state_gdn_conv[l], hgrn=state_hgrn[l])
        xp, sp = _trunk_layer(xp, b_p, t_p, f, lb, tabs_p, past_p)
        xs, ss = _trunk_layer(xs, b_s, t_s, f, lb, tabs_s, past_s)
        st_p.append(sp)
        st_s.append(ss)

    def stack(lst, i):
        return jnp.stack([s[i] for s in lst], axis=0)

    return (xp.reshape(b_p, t_p, D_MODEL), xs.reshape(b_s, t_s, D_MODEL),
            stack(st_p, 0), stack(st_p, 1), stack(st_p, 2), stack(st_p, 3), stack(st_p, 4),
            stack(st_s, 0), stack(st_s, 1), stack(st_s, 2), stack(st_s, 3), stack(st_s, 4))
```

```python
import functools

import jax
import jax.numpy as jnp
import numpy as np
from jax import lax
from jax.experimental import pallas as pl
from jax.experimental.pallas import tpu as pltpu

F32 = jnp.float32
BF16 = jnp.bfloat16

D_MODEL = 1024
CHUNK = 64
NORM_EPS = 1e-6

MLA_HEADS = 8
MLA_NOPE = 64
MLA_ROPE = 32
MLA_V = 64
MLA_Q_LORA = 384
MLA_KV_LORA = 256
MLA_QK_HEAD = MLA_NOPE + MLA_ROPE
MLA_VW = MLA_HEADS * MLA_V
ROPE_THETA = 10000.0
LOG2_E = 1.4426950408889634
HEAD_PAD = 128
QK_W = MLA_HEADS * HEAD_PAD
VT_ROWS = MLA_V + 16

GDN_HEADS = 4
GDN_DK = 128
GDN_DV = 128
GDN_QK = GDN_HEADS * GDN_DK
GDN_VW = GDN_HEADS * GDN_DV
CONV_W = 4
HGRN_PREP_ROWS = 512
GDN_PREP_ROWS = 1024

HGRN_HEADS = 4
HGRN_DK = 128
HGRN_DV = 128
HGRN_KW = HGRN_HEADS * HGRN_DK

N_BRANCH = 3
SPLIT_SIZES = (MLA_Q_LORA, MLA_KV_LORA, MLA_ROPE,
               GDN_QK, GDN_QK, GDN_VW, GDN_VW, GDN_HEADS, GDN_HEADS,
               HGRN_KW, HGRN_KW, HGRN_KW, HGRN_KW,
               N_BRANCH * D_MODEL)

N_EXPERTS = 8
FF_EXPERT = 1408
MOE_TILE = 1024
TOP_K = 2
MOE_BLOCK_ROWS = tuple(MOE_TILE * TOP_K // N_EXPERTS + d for d in (-32, -16, 0, 16, 32, 48))

P_COLS = 8192
P_MAIN = P_COLS - N_BRANCH * D_MODEL
P_TN = 512
COL_CKV = MLA_Q_LORA
COL_KR = MLA_Q_LORA + MLA_KV_LORA
COL_GAB = COL_KR + 128
COL_GDN = 1024
ROPE_LANE0 = MLA_NOPE

VMEM_LIMIT = 56 * 1024 * 1024


def _cparams(sem):
    return pltpu.CompilerParams(dimension_semantics=sem, vmem_limit_bytes=VMEM_LIMIT)


def _row_tile(n, cap):
    for t in (2048, 1024, 512, 256, 128, 64, 32, 16, 8):
        if t <= cap and n % t == 0:
            return t
    raise ValueError(f"no row tile for {n}")


def _sigmoid(x):
    return 1.0 / (1.0 + jnp.exp(-x))


def _silu(x):
    return x * (0.5 * jnp.tanh(0.5 * x) + 0.5)


def _rms(x, g):
    ms = jnp.mean(x * x, axis=-1, keepdims=True)
    return x * lax.rsqrt(ms + NORM_EPS) * g


def _in_proj_kernel(x_ref, g_ref, w_ref, o_ref, gate_ref, h_ref):
    j = pl.program_id(1)

    @pl.when(j == 0)
    def _():
        h_ref[...] = _rms(x_ref[...], g_ref[...]).astype(BF16)

    @pl.when(j < P_MAIN // P_TN)
    def _():
        o_ref[...] = jnp.dot(h_ref[...], w_ref[...], preferred_element_type=F32)

    @pl.when(j >= P_MAIN // P_TN)
    def _():
        gate_ref[...] = jnp.dot(h_ref[...], w_ref[...], preferred_element_type=F32).astype(BF16)


def _in_proj(x, g, w):
    n = x.shape[0]
    tm = _row_tile(n, 2048)
    n_main = P_MAIN // P_TN
    return pl.pallas_call(
        _in_proj_kernel,
        grid=(n // tm, P_COLS // P_TN),
        in_specs=[pl.BlockSpec((tm, D_MODEL), lambda i, j: (i, 0)),
                  pl.BlockSpec((1, D_MODEL), lambda i, j: (0, 0)),
                  pl.BlockSpec((D_MODEL, P_TN), lambda i, j: (0, j))],
        out_specs=[pl.BlockSpec((tm, P_TN), lambda i, j: (i, jnp.minimum(j, n_main - 1))),
                   pl.BlockSpec((tm, P_TN), lambda i, j: (i, jnp.maximum(j - n_main, 0)))],
        out_shape=[jax.ShapeDtypeStruct((n, P_MAIN), F32),
                   jax.ShapeDtypeStruct((n, P_COLS - P_MAIN), BF16)],
        scratch_shapes=[pltpu.VMEM((tm, D_MODEL), BF16)],
        compiler_params=_cparams(("parallel", "arbitrary")),
        name="in_proj",
    )(x, g, w)


def _rope(x, c, s1, s2):
    return x * c + pltpu.roll(x, HEAD_PAD - 16, 1) * s1 + pltpu.roll(x, 16, 1) * s2


def _mla_pre_kernel(p_ref, gq_ref, gkv_ref, wq_ref, hq_ref, hk_ref, c_ref, s1_ref, s2_ref, *rest,
                    scale, transposed):
    q_ref, ckv_ref, kr_ref = rest[-3:]
    if transposed:
        hqt_ref, ct_ref, s1t_ref, s2t_ref = rest[:4]
    c, s1, s2 = c_ref[...], s1_ref[...], s2_ref[...]
    ckv_ref[...] = _rms(p_ref[:, COL_CKV:COL_KR], gkv_ref[...])

    kr = p_ref[:, COL_KR:COL_KR + HEAD_PAD]
    kr_ms = jnp.sum(kr * kr, axis=-1, keepdims=True) * (1.0 / MLA_ROPE)
    kr_ref[...] = _rope(kr * lax.rsqrt(kr_ms + NORM_EPS) * hk_ref[...], c, s1, s2)

    cq = _rms(p_ref[:, 0:MLA_Q_LORA], gq_ref[...]).astype(BF16)
    q = jnp.dot(cq, wq_ref[...], preferred_element_type=F32)
    if transposed:
        row = lax.broadcasted_iota(jnp.int32, (HEAD_PAD, 1), 0)
        hqt, ct, s1t, s2t = hqt_ref[...], ct_ref[...], s1t_ref[...], s2t_ref[...]
        for h in range(MLA_HEADS):
            qt = q[:, h * HEAD_PAD:(h + 1) * HEAD_PAD].T
            sq = qt * qt
            ms_n = jnp.sum(sq[0:MLA_NOPE], axis=0, keepdims=True) * (1.0 / MLA_NOPE)
            ms_r = jnp.sum(sq[MLA_NOPE:MLA_QK_HEAD], axis=0, keepdims=True) * (1.0 / MLA_ROPE)
            inv = jnp.where(row < MLA_NOPE, lax.rsqrt(ms_n + NORM_EPS), lax.rsqrt(ms_r + NORM_EPS))
            x = qt * inv * hqt
            x = x * ct + pltpu.roll(x, HEAD_PAD - 16, 0) * s1t + pltpu.roll(x, 16, 0) * s2t
            q_ref[0, h] = x.astype(BF16)
    else:
        lane = lax.broadcasted_iota(jnp.int32, (1, HEAD_PAD), 1)
        is_nope = lane < MLA_NOPE
        hq = hq_ref[...]
        for h in range(MLA_HEADS):
            qh = q[:, h * HEAD_PAD:(h + 1) * HEAD_PAD]
            sq = qh * qh
            ms_n = jnp.sum(jnp.where(is_nope, sq, 0.0), axis=-1, keepdims=True) * (1.0 / MLA_NOPE)
            ms_r = jnp.sum(jnp.where(is_nope, 0.0, sq), axis=-1, keepdims=True) * (1.0 / MLA_ROPE)
            inv = jnp.where(is_nope, lax.rsqrt(ms_n + NORM_EPS), lax.rsqrt(ms_r + NORM_EPS))
            qh = _rope(qh * inv * hq, c, s1, s2) * scale
            q_ref[:, h * HEAD_PAD:(h + 1) * HEAD_PAD] = qh.astype(BF16)


def _mla_pre(p, b, t_seq, gq, gkv, wq, hq, hk, tabs, transposed, layer, depth, ckv_buf):
    n = p.shape[0]
    tm = _row_tile(n, 512)
    c, s1, s2 = tabs
    if tm > t_seq:
        c, s1, s2 = (jnp.tile(t, (tm // t_seq, 1)) for t in (c, s1, s2))
    n_tab = c.shape[0] // tm
    tab_spec = pl.BlockSpec((tm, HEAD_PAD), lambda i: (i % n_tab, 0))
    vec = lambda w: pl.BlockSpec((1, w), lambda i: (0, 0))
    scale = MLA_QK_HEAD ** -0.5
    extra, extra_specs = (), []
    if transposed:
        scale *= LOG2_E
        q_spec = pl.BlockSpec((1, MLA_HEADS, HEAD_PAD, tm), lambda i: (i // n_tab, 0, 0, i % n_tab))
        q_shape = jax.ShapeDtypeStruct((b, MLA_HEADS, HEAD_PAD, t_seq), BF16)
        hqt = jnp.broadcast_to((hq[0] * scale)[:, None], (HEAD_PAD, tm))
        extra = (hqt, c.T, s1.T, s2.T)
        tab_t_spec = pl.BlockSpec((HEAD_PAD, tm), lambda i: (0, i % n_tab))
        extra_specs = [pl.BlockSpec((HEAD_PAD, tm), lambda i: (0, 0)), tab_t_spec, tab_t_spec, tab_t_spec]
    else:
        q_spec = pl.BlockSpec((tm, QK_W), lambda i: (i, 0))
        q_shape = jax.ShapeDtypeStruct((n, QK_W), BF16)
    aliases = {}
    if ckv_buf is not None:
        aliases = {9 + len(extra): 1}
        extra = (*extra, ckv_buf)
        extra_specs = [*extra_specs, pl.BlockSpec(memory_space=pl.ANY)]
    return pl.pallas_call(
        functools.partial(_mla_pre_kernel, scale=scale, transposed=transposed),
        grid=(n // tm,),
        in_specs=[pl.BlockSpec((tm, 1024), lambda i: (i, 0)),
                  vec(MLA_Q_LORA), vec(MLA_KV_LORA),
                  pl.BlockSpec((MLA_Q_LORA, QK_W), lambda i: (0, 0)),
                  vec(HEAD_PAD), vec(HEAD_PAD), tab_spec, tab_spec, tab_spec, *extra_specs],
        out_specs=[q_spec,
                   pl.BlockSpec((None, tm, MLA_KV_LORA), lambda i: (layer, i, 0)),
                   pl.BlockSpec((tm, HEAD_PAD), lambda i: (i, 0))],
        out_shape=[q_shape,
                   jax.ShapeDtypeStruct((depth, n, MLA_KV_LORA), F32),
                   jax.ShapeDtypeStruct((n, HEAD_PAD), F32)],
        input_output_aliases=aliases,
        compiler_params=_cparams(("parallel",)),
        name="mla_pre",
    )(p, gq, gkv, wq, hq, hk, c, s1, s2, *extra)


def _kv_up_kernel(ckv_ref, kr_ref, wk_ref, wv_ref, hk_ref, k_ref, v_ref):
    c = ckv_ref[...].astype(BF16)
    k = jnp.dot(c, wk_ref[...], preferred_element_type=F32)
    kr = kr_ref[...]
    hk = hk_ref[...]
    for h in range(MLA_HEADS):
        kh = k[:, h * HEAD_PAD:(h + 1) * HEAD_PAD]
        ms = jnp.sum(kh * kh, axis=-1, keepdims=True) * (1.0 / MLA_NOPE)
        k_ref[:, h * HEAD_PAD:(h + 1) * HEAD_PAD] = (kh * lax.rsqrt(ms + NORM_EPS) * hk + kr).astype(BF16)
    v = jnp.dot(c, wv_ref[...], preferred_element_type=F32)
    lane = lax.broadcasted_iota(jnp.int32, (1, HEAD_PAD), 1)
    for h in range(MLA_HEADS):
        vh = jnp.where(lane < MLA_V, v[:, h * HEAD_PAD:(h + 1) * HEAD_PAD], 1.0)
        v_ref[0, h] = vh.T[0:VT_ROWS].astype(BF16)


def _kv_up(ckv_buf, layer, kr, b, s_len, wk, wv, hk):
    n = ckv_buf.shape[1]
    tm = _row_tile(n, 512)
    n_t = s_len // tm
    return pl.pallas_call(
        _kv_up_kernel,
        grid=(n // tm,),
        in_specs=[pl.BlockSpec((None, tm, MLA_KV_LORA), lambda i: (layer, i, 0)),
                  pl.BlockSpec((tm, HEAD_PAD), lambda i: (i, 0)),
                  pl.BlockSpec((MLA_KV_LORA, QK_W), lambda i: (0, 0)),
                  pl.BlockSpec((MLA_KV_LORA, QK_W), lambda i: (0, 0)),
                  pl.BlockSpec((1, HEAD_PAD), lambda i: (0, 0))],
        out_specs=[pl.BlockSpec((tm, QK_W), lambda i: (i, 0)),
                   pl.BlockSpec((1, MLA_HEADS, VT_ROWS, tm), lambda i: (i // n_t, 0, 0, i % n_t))],
        out_shape=[jax.ShapeDtypeStruct((n, QK_W), BF16),
                   jax.ShapeDtypeStruct((b, MLA_HEADS, VT_ROWS, s_len), BF16)],
        compiler_params=_cparams(("parallel",)),
        name="kv_up",
    )(ckv_buf, kr, wk, wv, hk)


def _kv_hist_kernel(ckv_ref, kr_ref, wkt_ref, wv_ref, hkt_ref, *rest):
    kt_ref, v_ref = rest[-2:]
    c = ckv_ref[0].astype(BF16)
    kt = _dot_nt(wkt_ref[...], c)
    kr = kr_ref[0]
    if kr.shape[1] == MLA_ROPE:
        rows = kr.shape[0]
        kr = jnp.concatenate([jnp.zeros((rows, ROPE_LANE0), F32), kr,
                              jnp.zeros((rows, HEAD_PAD - ROPE_LANE0 - MLA_ROPE), F32)], axis=1)
    krt = kr.T
    hkt = hkt_ref[...]
    for h in range(MLA_HEADS):
        kh = kt[h * HEAD_PAD:(h + 1) * HEAD_PAD]
        ms = jnp.sum(kh * kh, axis=0, keepdims=True) * (1.0 / MLA_NOPE)
        kt_ref[0, h * HEAD_PAD:(h + 1) * HEAD_PAD, :] = (kh * lax.rsqrt(ms + NORM_EPS) * hkt + krt).astype(BF16)
    v_ref[0] = jnp.dot(c, wv_ref[...], preferred_element_type=F32).astype(BF16)


def _kv_hist(ckv, kr, s_total, row0, wkt, wv, hk, into=None, layer=None):
    b, s_in, _ = ckv.shape[-3:]
    tm = _row_tile(s_in, 1024)
    blk0 = row0 // tm
    hkt = jnp.broadcast_to(hk[0][:, None], (HEAD_PAD, tm))
    if layer is None:
        rows_spec = lambda w: pl.BlockSpec((1, tm, w), lambda i, j: (i, j, 0))
    else:
        rows_spec = lambda w: pl.BlockSpec((None, 1, tm, w), lambda i, j: (layer, i, j, 0))
    in_specs = [rows_spec(MLA_KV_LORA), rows_spec(kr.shape[-1]),
                pl.BlockSpec((QK_W, MLA_KV_LORA), lambda i, j: (0, 0)),
                pl.BlockSpec((MLA_KV_LORA, MLA_VW), lambda i, j: (0, 0)),
                pl.BlockSpec((HEAD_PAD, tm), lambda i, j: (0, 0))]
    args = [ckv, kr, wkt, wv, hkt]
    aliases = {}
    if into is not None:
        in_specs += [pl.BlockSpec(memory_space=pl.ANY), pl.BlockSpec(memory_space=pl.ANY)]
        args += list(into)
        aliases = {5: 0, 6: 1}
    return pl.pallas_call(
        _kv_hist_kernel,
        grid=(b, s_in // tm),
        in_specs=in_specs,
        out_specs=[pl.BlockSpec((1, QK_W, tm), lambda i, j: (i, 0, blk0 + j)),
                   pl.BlockSpec((1, tm, MLA_VW), lambda i, j: (i, blk0 + j, 0))],
        out_shape=[jax.ShapeDtypeStruct((b, QK_W, s_total), BF16),
                   jax.ShapeDtypeStruct((b, s_total, MLA_VW), BF16)],
        input_output_aliases=aliases,
        compiler_params=_cparams(("parallel", "parallel")),
        name="kv_hist",
    )(*args)


def _attn_hist_kernel(q_ref, kt_ref, v_ref, o_ref, *, q_pos0, s_valid):
    t, s_len = q_ref.shape[1], kt_ref.shape[2]
    k_pos = lax.broadcasted_iota(jnp.int32, (t, s_len), 1)
    q_pos = lax.broadcasted_iota(jnp.int32, (t, s_len), 0) + q_pos0
    allowed = (k_pos < s_valid) & (k_pos // CHUNK <= q_pos // CHUNK)
    scores = [jnp.dot(q_ref[0, :, h * HEAD_PAD:(h + 1) * HEAD_PAD], kt_ref[0, h * HEAD_PAD:(h + 1) * HEAD_PAD, :],
                      preferred_element_type=F32) for h in range(MLA_HEADS)]
    for h in range(MLA_HEADS):
        s = jnp.where(allowed, scores[h], -jnp.inf)
        p = jnp.exp(s - jnp.max(s, axis=-1, keepdims=True))
        pv = jnp.dot(p.astype(BF16), v_ref[0, :, h * MLA_V:(h + 1) * MLA_V], preferred_element_type=F32)
        o_ref[0, :, h * MLA_V:(h + 1) * MLA_V] = pv / jnp.sum(p, axis=-1, keepdims=True)


def _attention_hist(q, kt, v, q_pos0, s_valid):
    b, t, _ = q.shape
    s = kt.shape[2]
    return pl.pallas_call(
        functools.partial(_attn_hist_kernel, q_pos0=q_pos0, s_valid=s_valid),
        grid=(b,),
        in_specs=[pl.BlockSpec((1, t, QK_W), lambda i: (i, 0, 0)),
                  pl.BlockSpec((1, QK_W, s), lambda i: (i, 0, 0)),
                  pl.BlockSpec((1, s, MLA_VW), lambda i: (i, 0, 0))],
        out_specs=pl.BlockSpec((1, t, MLA_VW), lambda i: (i, 0, 0)),
        out_shape=jax.ShapeDtypeStruct((b, t, MLA_VW), F32),
        compiler_params=_cparams(("parallel",)),
        name="mla_attn_hist",
    )(q, kt, v)


def _attn_t_kernel(qt_ref, k_ref, vt_ref, o_ref, m_ref, acc_ref, *, tq, tk):
    m_ref[...] = jnp.full(m_ref.shape, -jnp.inf, F32)
    acc_ref[...] = jnp.zeros(acc_ref.shape, F32)

    def blocks(starts, masked):
        if masked:
            ck = lax.broadcasted_iota(jnp.int32, (tk, tq), 0) // CHUNK
            cq = lax.broadcasted_iota(jnp.int32, (tk, tq), 1) // CHUNK
            allowed = ck <= cq

        def scores(start, h):
            kh = k_ref[0, pl.ds(start, tk), h * HEAD_PAD:(h + 1) * HEAD_PAD]
            return jnp.dot(kh, qt_ref[0, h], preferred_element_type=F32)

        def update(start, h, s):
            if masked:
                s = jnp.where(allowed, s, -jnp.inf)
            m_prev = m_ref[h]
            m_new = jnp.maximum(m_prev, jnp.max(s, axis=0, keepdims=True))
            alpha = jnp.exp2(m_prev - m_new)
            p = jnp.exp2(s - m_new).astype(BF16)
            pv = jnp.dot(vt_ref[0, h, :, pl.ds(start, tk)], p, preferred_element_type=F32)
            acc_ref[h] = alpha * acc_ref[h] + pv
            m_ref[h] = m_new

        items = [(start, h) for start in starts for h in range(MLA_HEADS)]
        ahead = 5
        pending = [scores(*it) for it in items[:ahead]]
        for n, it in enumerate(items):
            s = pending.pop(0)
            if n + ahead < len(items):
                pending.append(scores(*items[n + ahead]))
            update(*it, s)

    qi = pl.program_id(1)

    def body(j, carry):
        first = pl.multiple_of(2 * j * tk, tk)
        blocks([first, pl.multiple_of(first + tk, tk)], False)
        return carry

    lax.fori_loop(0, qi // 2, body, 0)

    @pl.when(qi % 2 == 1)
    def _():
        blocks([pl.multiple_of((qi - 1) * tk, tk)], False)

    blocks([pl.multiple_of(qi * tk, tk)], True)

    ot = jnp.concatenate([acc_ref[h, 0:MLA_V] / acc_ref[h, MLA_V:MLA_V + 1] for h in range(MLA_HEADS)], axis=0)
    o_ref[0] = ot.T


def _attention_t(qt, k, vt):
    b, _, _, t = qt.shape
    s = k.shape[1]
    tq = tk = min(256, t)
    return pl.pallas_call(
        functools.partial(_attn_t_kernel, tq=tq, tk=tk),
        grid=(b, t // tq),
        in_specs=[pl.BlockSpec((1, MLA_HEADS, HEAD_PAD, tq), lambda i, j: (i, 0, 0, j)),
                  pl.BlockSpec((1, s, QK_W), lambda i, j: (i, 0, 0)),
                  pl.BlockSpec((1, MLA_HEADS, VT_ROWS, s), lambda i, j: (i, 0, 0, 0))],
        out_specs=pl.BlockSpec((1, tq, MLA_VW), lambda i, j: (i, j, 0)),
        out_shape=jax.ShapeDtypeStruct((b, t, MLA_VW), F32),
        scratch_shapes=[pltpu.VMEM((MLA_HEADS, 1, tq), F32),
                        pltpu.VMEM((MLA_HEADS, VT_ROWS, tq), F32)],
        compiler_params=_cparams(("parallel", "arbitrary")),
        name="mla_attn_t",
    )(qt, k, vt)


def _row_iota(shape):
    return lax.broadcasted_iota(jnp.int32, shape, 0)


def _upper_half_masks(shape):
    row = _row_iota(shape)
    masks = []
    m = 1
    while m < CHUNK:
        masks.append((row // m) % 2 == 1)
        m *= 2
    return masks


def _segment_scans(g, uppers):
    rows = g.shape[0]
    pre, tot = g, g
    out = [(pre, tot)]
    m = 1
    while m < CHUNK:
        upper = uppers[len(out) - 1]
        from_lower = pltpu.roll(tot, m, 0)
        from_upper = pltpu.roll(tot, rows - m, 0)
        pre = pre + jnp.where(upper, from_lower, 0.0)
        tot = tot + jnp.where(upper, from_lower, from_upper)
        out.append((pre, tot))
        m *= 2
    return out


def _dot_nt(a, b):
    return lax.dot_general(a, b, (((1,), (1,)), ((), ())), preferred_element_type=F32)


def _dot_tn(a, b):
    return lax.dot_general(a, b, (((0,), (0,)), ((), ())), preferred_element_type=F32)


def _gdn_prep_kernel(q_ref, k_ref, v_ref, ab_ref, cw_ref, cs_ref, alog_ref, dt_ref,
                     u_ref, w_ref, qd_ref, kd_ref, qk_ref, gl_ref, carry_ref, *, nb, r):
    @pl.when(pl.program_id(1) == 0)
    def _():
        carry_ref[...] = cs_ref[...]

    n = nb * r
    nc = n // CHUNK

    def conv(x_ref, j):
        w = cw_ref[:, j * GDN_QK:(j + 1) * GDN_QK]
        outs = []
        for b in range(nb):
            x = x_ref[b]
            xp = jnp.concatenate([carry_ref[b, j], x], axis=0)
            y = x * w[3:4]
            for d in range(1, CONV_W):
                y = y + xp[8 - d:8 - d + r] * w[3 - d:4 - d]
            carry_ref[b, j] = x[r - 8:]
            outs.append(_silu(y))
        return outs[0] if nb == 1 else jnp.concatenate(outs, axis=0)

    q_all, k_all, v_all = conv(q_ref, 0), conv(k_ref, 1), conv(v_ref, 2)

    ab = ab_ref[...].reshape(n, 128)
    x = ab + dt_ref[...]
    softplus = jnp.maximum(x, 0.0) + jnp.log(1.0 + jnp.exp(-jnp.abs(x)))
    g_blk = (-LOG2_E) * jnp.exp(alog_ref[...]) * softplus
    gam_blk = _segment_scans(g_blk, _upper_half_masks(g_blk.shape))[-1][0]
    gam_t = gam_blk.T
    gam3_blk = gam_blk.reshape(nc, CHUNK, 128)
    beta3_blk = _sigmoid(ab).reshape(nc, CHUNK, 128)

    row = lax.broadcasted_iota(jnp.int32, (1, CHUNK, CHUNK), 1)
    col = lax.broadcasted_iota(jnp.int32, (1, CHUNK, CHUNK), 2)
    eye = (row == col).astype(F32)

    def bmm(a, b):
        return jnp.einsum('cij,cjk->cik', a.astype(BF16), b.astype(BF16), preferred_element_type=F32)

    def bmm_nt(a, b):
        return jnp.einsum('cid,cjd->cij', a.astype(BF16), b.astype(BF16), preferred_element_type=F32)

    for h in range(GDN_HEADS):
        sl = slice(h * GDN_DK, (h + 1) * GDN_DK)
        q, k, v = q_all[:, sl], k_all[:, sl], v_all[:, sl]
        q = q * lax.rsqrt(jnp.sum(q * q, axis=-1, keepdims=True) + NORM_EPS) * (GDN_DK ** -0.5)
        k = k * lax.rsqrt(jnp.sum(k * k, axis=-1, keepdims=True) + NORM_EPS)
        q, k, v = (a.reshape(nc, CHUNK, GDN_DK) for a in (q, k, v))
        gam = gam3_blk[:, :, h:h + 1]
        beta = beta3_blk[:, :, GDN_HEADS + h:GDN_HEADS + h + 1]
        gam_row = jnp.stack([gam_t[h:h + 1, c * CHUNK:(c + 1) * CHUNK] for c in range(nc)], axis=0)
        decay = jnp.where(row >= col, jnp.exp2(jnp.minimum(gam - gam_row, 0.0)), 0.0)
        a = jnp.where(row > col, beta * bmm_nt(k, k) * decay, 0.0)
        t_inv = eye - a
        pw = a
        m = 1
        while 2 * m < CHUNK:
            pw = bmm(pw, pw)
            t_inv = t_inv + bmm(t_inv, pw)
            m *= 2
        e_gam = jnp.exp2(gam)
        gam_last = gam[:, CHUNK - 1:CHUNK, :]
        u_ref[:, h] = bmm(t_inv, v * beta).reshape(nb, r, GDN_DV)
        w_ref[:, h] = bmm(t_inv, k * (beta * e_gam)).astype(BF16).reshape(nb, r, GDN_DK)
        qd_ref[:, h] = (q * e_gam).astype(BF16).reshape(nb, r, GDN_DK)
        kd_ref[:, h] = (k * jnp.exp2(gam_last - gam)).astype(BF16).reshape(nb, r, GDN_DK)
        qk_ref[:, h] = (bmm_nt(q, k) * decay).astype(BF16).reshape(nb, r, CHUNK)
        gl_ref[:, h] = jnp.broadcast_to(jnp.exp2(gam_last), (nc, 1, 128)).reshape(nb, r // CHUNK, 1, 128)


def _gdn_scan_kernel(u_ref, w_ref, qd_ref, kd_ref, qk_ref, gl_ref, s0_ref, o_ref, sf_ref, s_ref, *, nb):
    c_idx = pl.program_id(1)

    @pl.when(c_idx == 0)
    def _():
        s_ref[...] = s0_ref[...]

    chains = [(b, h) for b in range(nb) for h in range(GDN_HEADS)]
    s_old = [s_ref[b, h] for b, h in chains]
    s_bf = [s.astype(BF16) for s in s_old]
    v_new = [u_ref[b, h] - jnp.dot(w_ref[b, h], sb, preferred_element_type=F32)
             for (b, h), sb in zip(chains, s_bf)]
    v_bf = [v.astype(BF16) for v in v_new]
    for (b, h), s, sb, vb in zip(chains, s_old, s_bf, v_bf):
        o = (jnp.dot(qd_ref[b, h], sb, preferred_element_type=F32)
             + jnp.dot(qk_ref[b, h], vb, preferred_element_type=F32))
        o_ref[b, :, h * GDN_DV:(h + 1) * GDN_DV] = o
        s_ref[b, h] = gl_ref[b, h, 0] * s + _dot_tn(kd_ref[b, h], vb)

    @pl.when(c_idx == pl.num_programs(1) - 1)
    def _():
        sf_ref[...] = s_ref[...]


def _gdn_prep(p3, conv_w, conv_state8, alog, dt):
    b, t, _ = p3.shape
    r = min(GDN_PREP_ROWS, t)
    nb = max(1, min(b, GDN_PREP_ROWS // r))
    nt = t // r
    nc = t // CHUNK
    blk = lambda j: pl.BlockSpec((nb, r, GDN_QK), lambda i, c: (i, c, COL_GDN // GDN_QK + j))
    vec = pl.BlockSpec((1, 128), lambda i, c: (0, 0))
    head_spec = lambda w: pl.BlockSpec((nb, GDN_HEADS, r, w), lambda i, c: (i, 0, c, 0))
    head_shape = lambda w, dt_: jax.ShapeDtypeStruct((b, GDN_HEADS, t, w), dt_)
    u, w, qd, kd, qk, gl = pl.pallas_call(
        functools.partial(_gdn_prep_kernel, nb=nb, r=r),
        grid=(b // nb, nt),
        in_specs=[blk(0), blk(1), blk(2),
                  pl.BlockSpec((nb, r, 128), lambda i, c: (i, c, COL_GAB // 128)),
                  pl.BlockSpec((CONV_W, 3 * GDN_QK), lambda i, c: (0, 0)),
                  pl.BlockSpec((nb, 3, 8, GDN_QK), lambda i, c: (i, 0, 0, 0)),
                  vec, vec],
        out_specs=[head_spec(GDN_DV), head_spec(GDN_DK), head_spec(GDN_DK), head_spec(GDN_DK),
                   head_spec(CHUNK),
                   pl.BlockSpec((nb, GDN_HEADS, r // CHUNK, 1, 128), lambda i, c: (i, 0, c, 0, 0))],
        out_shape=[head_shape(GDN_DV, F32), head_shape(GDN_DK, BF16), head_shape(GDN_DK, BF16),
                   head_shape(GDN_DK, BF16), head_shape(CHUNK, BF16),
                   jax.ShapeDtypeStruct((b, GDN_HEADS, nc, 1, 128), F32)],
        scratch_shapes=[pltpu.VMEM((nb, 3, 8, GDN_QK), F32)],
        compiler_params=_cparams(("parallel", "arbitrary")),
        name="gdn_prep",
    )(p3, p3, p3, p3, conv_w, conv_state8, alog, dt)
    return u, w, qd, kd, qk, gl


def _hgrn_prep_kernel(q_ref, f_ref, lb_ref, att_ref, qe_ref, ke_ref, dec_ref, *, nb, r):
    n = nb * r
    nc = n // CHUNK
    row = lax.broadcasted_iota(jnp.int32, (1, CHUNK, CHUNK), 1)
    col = lax.broadcasted_iota(jnp.int32, (1, CHUNK, CHUNK), 2)
    uppers = _upper_half_masks((n, HGRN_DK))
    halves = [1 << lvl for lvl in range(len(uppers))]
    pairs = [(row // (2 * m) == col // (2 * m)) & ((row // m) % 2 == 1) & ((col // m) % 2 == 0) for m in halves]

    def bmm_nt(a, b):
        a3, b3 = (x.astype(BF16).reshape(nc, CHUNK, HGRN_DK) for x in (a, b))
        return jnp.einsum('cid,cjd->cij', a3, b3, preferred_element_type=F32)

    for h in range(HGRN_HEADS):
        sl = slice(h * HGRN_DK, (h + 1) * HGRN_DK)
        lb = lb_ref[:, sl]
        f = lb + (1.0 - lb) * _sigmoid(f_ref[:, :, sl].reshape(n, HGRN_DK))
        q = _silu(q_ref[:, :, sl].reshape(n, HGRN_DK)) * (HGRN_DK ** -0.5)
        k = 1.0 - f
        scans = _segment_scans(jnp.log2(f), uppers)
        cb, c_tot = scans[-1]

        att = jnp.where(row == col, bmm_nt(q, k), 0.0)
        for lvl in range(len(halves)):
            pre_m, tot_m = scans[lvl]
            att = att + jnp.where(pairs[lvl], bmm_nt(q * jnp.exp2(pre_m), k * jnp.exp2(tot_m - pre_m)), 0.0)

        att_ref[:, h] = att.astype(BF16).reshape(nb, r, CHUNK)
        qe_ref[:, h] = (q * jnp.exp2(cb)).astype(BF16).reshape(nb, r, HGRN_DK)
        ke_ref[:, h] = (k * jnp.exp2(c_tot - cb)).astype(BF16).reshape(nb, r, HGRN_DK)
        dec_ref[:, h] = jnp.exp2(c_tot).reshape(nc, CHUNK, HGRN_DK)[:, 0:1, :].reshape(nb, r // CHUNK, 1, HGRN_DK)


def _hgrn_scan_kernel(att_ref, qe_ref, ke_ref, dec_ref, v_ref, s0_ref, o_ref, sf_ref, st_ref, *, nb):
    c_idx = pl.program_id(1)

    @pl.when(c_idx == 0)
    def _():
        st_ref[...] = s0_ref[...]

    chains = [(b, h) for b in range(nb) for h in range(HGRN_HEADS)]
    sls = [slice(h * HGRN_DV, (h + 1) * HGRN_DV) for h in range(HGRN_HEADS)]
    st_old = [st_ref[b, h] for b, h in chains]
    vs = [v_ref[b, :, sls[h]].astype(BF16) for b, h in chains]
    o_st = [_dot_nt(qe_ref[b, h], st.astype(BF16)) for (b, h), st in zip(chains, st_old)]
    for (b, h), st, v, o1 in zip(chains, st_old, vs, o_st):
        o_ref[b, :, sls[h]] = o1 + jnp.dot(att_ref[b, h], v, preferred_element_type=F32)
        st_ref[b, h] = st * dec_ref[b, h, 0] + _dot_tn(v, ke_ref[b, h])

    @pl.when(c_idx == pl.num_programs(1) - 1)
    def _():
        sf_ref[...] = st_ref[...]


def _hgrn_prep(p3, lb):
    b, t, _ = p3.shape
    r = min(HGRN_PREP_ROWS, t)
    nb = max(1, min(b, HGRN_PREP_ROWS // r))
    nc = t // CHUNK
    col0 = (COL_GDN + 4 * GDN_QK) // HGRN_KW
    blk = lambda j: pl.BlockSpec((nb, r, HGRN_KW), lambda i, c: (i, c, col0 + j))
    head_spec = lambda w: pl.BlockSpec((nb, HGRN_HEADS, r, w), lambda i, c: (i, 0, c, 0))
    head_shape = lambda w: jax.ShapeDtypeStruct((b, HGRN_HEADS, t, w), BF16)
    att, qe, ke, dec = pl.pallas_call(
        functools.partial(_hgrn_prep_kernel, nb=nb, r=r),
        grid=(b // nb, t // r),
        in_specs=[blk(0), blk(1), pl.BlockSpec((1, HGRN_KW), lambda i, c: (0, 0))],
        out_specs=[head_spec(CHUNK), head_spec(HGRN_DK), head_spec(HGRN_DK),
                   pl.BlockSpec((nb, HGRN_HEADS, r // CHUNK, 1, HGRN_DK), lambda i, c: (i, 0, c, 0, 0))],
        out_shape=[head_shape(CHUNK), head_shape(HGRN_DK), head_shape(HGRN_DK),
                   jax.ShapeDtypeStruct((b, HGRN_HEADS, nc, 1, HGRN_DK), F32)],
        compiler_params=_cparams(("parallel", "parallel")),
        name="hgrn_prep",
    )(p3, p3, lb)
    return att, qe, ke, dec


def _rec_scan_kernel(u_ref, w_ref, qd_ref, kd_ref, qk_ref, gl_ref, sg0_ref,
                     att_ref, qe_ref, ke_ref, dec_ref, v_ref, sh0_ref,
                     og_ref, sgf_ref, oh_ref, shf_ref, sg_ref, sh_ref, *, nb):
    _gdn_scan_kernel(u_ref, w_ref, qd_ref, kd_ref, qk_ref, gl_ref, sg0_ref, og_ref, sgf_ref, sg_ref, nb=nb)
    _hgrn_scan_kernel(att_ref, qe_ref, ke_ref, dec_ref, v_ref, sh0_ref, oh_ref, shf_ref, sh_ref, nb=nb)


def _rec_scan(gdn_parts, hgrn_parts, p3, s0_gdn, s0t_hgrn):
    b, t, _ = p3.shape
    nc = t // CHUNK
    sb = min(b, 8)
    col_v = (COL_GDN + 4 * GDN_QK) // HGRN_KW + 2
    chunk_spec = lambda w: pl.BlockSpec((sb, GDN_HEADS, CHUNK, w), lambda i, c: (i, 0, c, 0))
    per_chunk = pl.BlockSpec((sb, GDN_HEADS, 1, 1, 128), lambda i, c: (i, 0, c, 0, 0))
    state_spec = pl.BlockSpec((sb, GDN_HEADS, 128, 128), lambda i, c: (i, 0, 0, 0))
    out_spec = pl.BlockSpec((sb, CHUNK, GDN_VW), lambda i, c: (i, c, 0))
    state_shape = jax.ShapeDtypeStruct((b, GDN_HEADS, 128, 128), F32)
    return pl.pallas_call(
        functools.partial(_rec_scan_kernel, nb=sb),
        grid=(b // sb, nc),
        in_specs=[chunk_spec(GDN_DV), chunk_spec(GDN_DK), chunk_spec(GDN_DK), chunk_spec(GDN_DK),
                  chunk_spec(CHUNK), per_chunk, state_spec,
                  chunk_spec(CHUNK), chunk_spec(HGRN_DK), chunk_spec(HGRN_DK), per_chunk,
                  pl.BlockSpec((sb, CHUNK, HGRN_KW), lambda i, c: (i, c, col_v)), state_spec],
        out_specs=[out_spec, state_spec, out_spec, state_spec],
        out_shape=[jax.ShapeDtypeStruct((b, t, GDN_VW), F32), state_shape,
                   jax.ShapeDtypeStruct((b, t, HGRN_KW), F32), state_shape],
        scratch_shapes=[pltpu.VMEM((sb, GDN_HEADS, 128, 128), F32), pltpu.VMEM((sb, HGRN_HEADS, 128, 128), F32)],
        compiler_params=_cparams(("parallel", "arbitrary")),
        name="rec_scan",
    )(*gdn_parts, s0_gdn, *hgrn_parts, p3, s0t_hgrn)


def _merge_kernel(x_ref, oa_ref, ob_ref, zb_ref, oc_ref, zc_ref, g0_ref, g1_ref, g2_ref, ngb_ref, ngc_ref,
                  wa_ref, wb_ref, wc_ref, wo_ref, y_ref):
    def branch(o, w_ref, g_ref):
        gate = _sigmoid(g_ref[...].astype(F32))
        return gate * jnp.dot(o.astype(BF16), w_ref[...], preferred_element_type=F32)

    def normed(o_ref, z_ref, ng_ref):
        return jnp.concatenate(
            [_rms(o_ref[:, h * 128:(h + 1) * 128], ng_ref[...]) * _silu(z_ref[:, h * 128:(h + 1) * 128])
             for h in range(GDN_HEADS)], axis=-1)

    mixed = (branch(oa_ref[...], wa_ref, g0_ref) + branch(normed(ob_ref, zb_ref, ngb_ref), wb_ref, g1_ref)
             + branch(normed(oc_ref, zc_ref, ngc_ref), wc_ref, g2_ref))
    y_ref[...] = x_ref[...] + jnp.dot(mixed.astype(BF16), wo_ref[...], preferred_element_type=F32)


def _merge(x, oa, ob, oc, p, pg, ngb, ngc, wa, wb, wc, wo):
    n = x.shape[0]
    tm = _row_tile(n, 512)
    row = lambda w: pl.BlockSpec((tm, w), lambda i: (i, 0))
    gate = lambda j: pl.BlockSpec((tm, D_MODEL), lambda i: (i, j))
    zb_spec = pl.BlockSpec((tm, GDN_VW), lambda i: (i, COL_GDN // GDN_VW + 3))
    zc_spec = pl.BlockSpec((tm, HGRN_KW), lambda i: (i, (COL_GDN + 4 * GDN_QK) // HGRN_KW + 3))
    vec = pl.BlockSpec((1, 128), lambda i: (0, 0))
    wsp = lambda k: pl.BlockSpec((k, D_MODEL), lambda i: (0, 0))
    return pl.pallas_call(
        _merge_kernel,
        grid=(n // tm,),
        in_specs=[row(D_MODEL), row(MLA_VW), row(GDN_VW), zb_spec, row(HGRN_KW), zc_spec,
                  gate(0), gate(1), gate(2), vec, vec,
                  wsp(MLA_VW), wsp(GDN_VW), wsp(HGRN_KW), wsp(D_MODEL)],
        out_specs=row(D_MODEL),
        out_shape=jax.ShapeDtypeStruct((n, D_MODEL), F32),
        compiler_params=_cparams(("parallel",)),
        name="merge_out",
    )(x, oa, ob, p, oc, p, pg, pg, pg, ngb, ngc, wa, wb, wc, wo)


def _router_kernel(x_ref, g_ref, w_ref, b_ref, cw_ref, cwt_ref, cnt_ref):
    h = _rms(x_ref[...], g_ref[...])
    logits = jnp.dot(h, w_ref[...], preferred_element_type=F32, precision=lax.Precision.HIGHEST) + b_ref[...]
    lane = lax.broadcasted_iota(jnp.int32, logits.shape, 1)
    valid = lane < N_EXPERTS
    neg = -jnp.inf
    l1 = jnp.where(valid, logits, neg)
    m1 = jnp.max(l1, axis=-1, keepdims=True)
    i1 = jnp.min(jnp.where(l1 == m1, lane, 128), axis=-1, keepdims=True)
    l2 = jnp.where(lane == i1, neg, l1)
    m2 = jnp.max(l2, axis=-1, keepdims=True)
    i2 = jnp.min(jnp.where(l2 == m2, lane, 128), axis=-1, keepdims=True)
    e2 = jnp.exp(m2 - m1)
    den = 1.0 + e2
    cw = jnp.where(lane == i1, 1.0 / den, 0.0) + jnp.where(lane == i2, e2 / den, 0.0)
    cw_ref[...] = cw
    cwt_ref[...] = cw.T
    cnt_ref[0] = jnp.sum((cw > 0.0).astype(F32), axis=0, keepdims=True)


def _router(x, g, w, b, tm):
    n = x.shape[0]
    return pl.pallas_call(
        _router_kernel,
        grid=(n // tm,),
        in_specs=[pl.BlockSpec((tm, D_MODEL), lambda i: (i, 0)),
                  pl.BlockSpec((1, D_MODEL), lambda i: (0, 0)),
                  pl.BlockSpec((D_MODEL, 128), lambda i: (0, 0)),
                  pl.BlockSpec((1, 128), lambda i: (0, 0))],
        out_specs=[pl.BlockSpec((tm, 128), lambda i: (i, 0)),
                   pl.BlockSpec((128, tm), lambda i: (0, i)),
                   pl.BlockSpec((1, 1, 128), lambda i: (i, 0, 0))],
        out_shape=[jax.ShapeDtypeStruct((n, 128), F32),
                   jax.ShapeDtypeStruct((128, n), F32),
                   jax.ShapeDtypeStruct((n // tm, 1, 128), F32)],
        compiler_params=_cparams(("parallel",)),
        name="moe_router",
    )(x, g, w, b)


def _moe_kernel(cnt_ref, x_ref, g_ref, cw_ref, cwt_ref, wg_ref, wu_ref, wd_ref, y_ref,
                h_ref, rcol_ref, rrow_ref, *, t, br):
    i, e = pl.program_id(0), pl.program_id(1)

    @pl.when(e == 0)
    def _():
        x = x_ref[...]
        h_ref[...] = _rms(x, g_ref[...]).astype(BF16)
        y_ref[...] = x
        r = lax.broadcasted_iota(jnp.int32, (t, t), 0)
        c = lax.broadcasted_iota(jnp.int32, (t, t), 1)
        on = cw_ref[...] > 0.0
        rank = jnp.dot((c < r).astype(BF16), on.astype(BF16), preferred_element_type=F32)
        rcol_ref[...] = jnp.where(on, rank.astype(jnp.int32), -1)
        on_t = cwt_ref[...] > 0.0
        rank_t = jnp.dot(on_t.astype(BF16), (r < c).astype(BF16), preferred_element_type=F32)
        rrow_ref[...] = jnp.where(on_t, rank_t.astype(jnp.int32), -1)

    count = cnt_ref[i * N_EXPERTS + e]
    sel = lax.broadcasted_iota(jnp.int32, (1, 128), 1) == e
    rank_c = jnp.sum(jnp.where(sel, rcol_ref[...], 0), axis=-1, keepdims=True)
    rank_r = rrow_ref[pl.ds(e, 1), :]
    w_r = cwt_ref[pl.ds(e, 1), :]

    def expert_block(base, rows_n):
        rows = lax.broadcasted_iota(jnp.int32, (rows_n, t), 0) + base
        pick = rows == rank_r
        xc = jnp.dot(pick.astype(BF16), h_ref[...], preferred_element_type=F32).astype(BF16)
        a = jnp.dot(xc, wg_ref[0], preferred_element_type=F32)
        b = jnp.dot(xc, wu_ref[0], preferred_element_type=F32)
        yc = jnp.dot((_silu(a) * b).astype(BF16), wd_ref[0], preferred_element_type=F32)
        w_rows = jnp.sum(jnp.where(pick, w_r, 0.0), axis=-1, keepdims=True)
        cols = lax.broadcasted_iota(jnp.int32, (t, rows_n), 1) + base
        y_ref[...] += jnp.dot((cols == rank_c).astype(BF16), (yc * w_rows).astype(BF16),
                              preferred_element_type=F32)

    lo = 0
    for size in br:
        @pl.when((count > lo) & (count <= size))
        def _(size=size):
            expert_block(0, size)
        lo = size

    @pl.when(count > br[-1])
    def _():
        def body(j, carry):
            expert_block(j * br[-1], br[-1])
            return carry

        lax.fori_loop(0, lax.div(count + (br[-1] - 1), br[-1]), body, 0)


def _moe(x, g, cw, cwt, counts, wg, wu, wd, t):
    n = x.shape[0]
    ne, _, ff = wg.shape
    br = MOE_BLOCK_ROWS
    grid_spec = pltpu.PrefetchScalarGridSpec(
        num_scalar_prefetch=1,
        grid=(n // t, ne),
        in_specs=[pl.BlockSpec((t, D_MODEL), lambda i, e, cnt: (i, 0)),
                  pl.BlockSpec((1, D_MODEL), lambda i, e, cnt: (0, 0)),
                  pl.BlockSpec((t, 128), lambda i, e, cnt: (i, 0)),
                  pl.BlockSpec((128, t), lambda i, e, cnt: (0, i)),
                  pl.BlockSpec((1, D_MODEL, ff), lambda i, e, cnt: (e, 0, 0)),
                  pl.BlockSpec((1, D_MODEL, ff), lambda i, e, cnt: (e, 0, 0)),
                  pl.BlockSpec((1, ff, D_MODEL), lambda i, e, cnt: (e, 0, 0))],
        out_specs=pl.BlockSpec((t, D_MODEL), lambda i, e, cnt: (i, 0)),
        scratch_shapes=[pltpu.VMEM((t, D_MODEL), BF16),
                        pltpu.VMEM((t, 128), jnp.int32),
                        pltpu.VMEM((128, t), jnp.int32)])
    return pl.pallas_call(
        functools.partial(_moe_kernel, t=t, br=br),
        grid_spec=grid_spec,
        out_shape=jax.ShapeDtypeStruct((n, D_MODEL), F32),
        compiler_params=_cparams(("parallel", "arbitrary")),
        name="moe",
    )(counts, x, g, cw, cwt, wg, wu, wd)


def _ffn_kernel(x_ref, g_ref, wg_ref, wu_ref, wd_ref, y_ref, h_ref):
    e = pl.program_id(1)

    @pl.when(e == 0)
    def _():
        x = x_ref[...]
        h_ref[...] = _rms(x, g_ref[...]).astype(BF16)
        y_ref[...] = x

    h = h_ref[...]
    a = jnp.dot(h, wg_ref[0], preferred_element_type=F32)
    b = jnp.dot(h, wu_ref[0], preferred_element_type=F32)
    y_ref[...] += jnp.dot((_silu(a) * b).astype(BF16), wd_ref[0], preferred_element_type=F32)


def _ffn(x, g, wg, wu, wd):
    n = x.shape[0]
    ne, _, ff = wg.shape
    tm = _row_tile(n, 512)
    return pl.pallas_call(
        _ffn_kernel,
        grid=(n // tm, ne),
        in_specs=[pl.BlockSpec((tm, D_MODEL), lambda i, e: (i, 0)),
                  pl.BlockSpec((1, D_MODEL), lambda i, e: (0, 0)),
                  pl.BlockSpec((1, D_MODEL, ff), lambda i, e: (e, 0, 0)),
                  pl.BlockSpec((1, D_MODEL, ff), lambda i, e: (e, 0, 0)),
                  pl.BlockSpec((1, ff, D_MODEL), lambda i, e: (e, 0, 0))],
        out_specs=pl.BlockSpec((tm, D_MODEL), lambda i, e: (i, 0)),
        out_shape=jax.ShapeDtypeStruct((n, D_MODEL), F32),
        scratch_shapes=[pltpu.VMEM((tm, D_MODEL), BF16)],
        compiler_params=_cparams(("parallel", "arbitrary")),
        name="ffn",
    )(x, g, wg, wu, wd)


def _pad_lanes(x, left, total):
    return jnp.pad(x, [(0, 0)] * (x.ndim - 1) + [(left, total - left - x.shape[-1])])


def _pack_w_in(w):
    cq, ckv, kr, gq, gk, gv, gz, ga, gb, hq, hf, hi, hg, gates = jnp.split(
        w, np.cumsum(SPLIT_SIZES)[:-1].tolist(), axis=-1)
    kr_blk = _pad_lanes(kr, ROPE_LANE0, 128)
    ab_blk = _pad_lanes(jnp.concatenate([ga, gb], axis=-1), 0, 256)
    return jnp.concatenate([cq, ckv, kr_blk, ab_blk, gq, gk, gv, gz, hq, hf, hi, hg, gates], axis=-1).astype(BF16)


def _rope_tables(n_pos):
    half = MLA_ROPE // 2
    inv = ROPE_THETA ** (-jnp.arange(half, dtype=F32) / half)
    ang = jnp.arange(n_pos, dtype=F32)[:, None] * inv[None, :]
    cos, sin = jnp.cos(ang), jnp.sin(ang)
    one = jnp.ones((n_pos, MLA_NOPE), F32)
    zero = jnp.zeros((n_pos, MLA_NOPE), F32)
    tail = jnp.zeros((n_pos, HEAD_PAD - MLA_QK_HEAD), F32)
    z16 = jnp.zeros((n_pos, half), F32)
    c = jnp.concatenate([one, cos, cos, tail], axis=-1)
    s1 = jnp.concatenate([zero, -sin, z16, tail], axis=-1)
    s2 = jnp.concatenate([zero, z16, sin, tail], axis=-1)
    return c, s1, s2


def _layer_weights(l, a):
    f = {}
    f['mixer_g'] = a['mixer_norm_g'][l][None]
    f['w_in'] = _pack_w_in(a['w_in'][l])
    f['gq'] = a['mla_q_norm_g'][l][None]
    f['gkv'] = a['mla_kv_norm_g'][l][None]
    wq = a['mla_w_q_up'][l].reshape(MLA_Q_LORA, MLA_HEADS, MLA_QK_HEAD)
    f['wq'] = _pad_lanes(wq, 0, HEAD_PAD).reshape(MLA_Q_LORA, QK_W).astype(BF16)
    wkv = a['mla_w_kv_up'][l].reshape(MLA_KV_LORA, MLA_HEADS, MLA_NOPE + MLA_V)
    f['wk'] = _pad_lanes(wkv[:, :, :MLA_NOPE], 0, HEAD_PAD).reshape(MLA_KV_LORA, QK_W).astype(BF16)
    f['wkt'] = f['wk'].T
    f['wv'] = wkv[:, :, MLA_NOPE:].reshape(MLA_KV_LORA, MLA_VW).astype(BF16)
    f['wv_pad'] = _pad_lanes(wkv[:, :, MLA_NOPE:], 0, HEAD_PAD).reshape(MLA_KV_LORA, QK_W).astype(BF16)
    f['hq'] = _pad_lanes(a['mla_q_head_norm_g'][l][None], 0, HEAD_PAD)
    hk = a['mla_k_head_norm_g'][l][None]
    f['hk_nope'] = _pad_lanes(hk[:, :MLA_NOPE], 0, HEAD_PAD)
    f['hk_rope'] = _pad_lanes(hk[:, MLA_NOPE:], ROPE_LANE0, HEAD_PAD)
    f['wo_a'] = a['mla_w_o'][l].astype(BF16)
    f['conv_w'] = a['gdn_conv_w'][l]
    f['alog'] = _pad_lanes(a['gdn_a_log'][l][None], 0, 128)
    f['dt'] = _pad_lanes(a['gdn_dt_bias'][l][None], 0, 128)
    f['gdn_g'] = a['gdn_norm_g'][l][None]
    f['wo_b'] = a['gdn_w_o'][l].astype(BF16)
    f['hgrn_g'] = a['hgrn_norm_g'][l][None]
    f['wo_c'] = a['hgrn_w_o'][l].astype(BF16)
    f['w_out'] = a['w_out'][l].astype(BF16)
    f['ffn_g'] = a['ffn_norm_g'][l][None]
    if l % 2 == 0:
        wg, wu, wd = a['dense_w_gate'][l // 2], a['dense_w_up'][l // 2], a['dense_w_down'][l // 2]
        ff = wg.shape[1]
        half = ff // 2
        f['ffn'] = (jnp.moveaxis(wg.reshape(D_MODEL, 2, half), 1, 0).astype(BF16),
                    jnp.moveaxis(wu.reshape(D_MODEL, 2, half), 1, 0).astype(BF16),
                    wd.reshape(2, half, D_MODEL).astype(BF16))
        f['router'] = None
    else:
        f['ffn'] = (a['moe_w_gate'][l // 2].astype(BF16), a['moe_w_up'][l // 2].astype(BF16),
                    a['moe_w_down'][l // 2].astype(BF16))
        f['router'] = (_pad_lanes(a['moe_w_router'][l // 2], 0, 128),
                       _pad_lanes(a['moe_b_router'][l // 2][None], 0, 128))
    return f


def _trunk_layer(x, b, t, f, lb, tabs, past, layer, depth, ckv_buf):
    n = b * t
    p, p_gate = _in_proj(x, f['mixer_g'], f['w_in'])
    p3 = p.reshape(b, t, P_MAIN)

    fresh = past['ckv'] is None
    q, ckv_buf, kr = _mla_pre(p, b, t, f['gq'], f['gkv'], f['wq'], f['hq'], f['hk_rope'], tabs, fresh,
                              layer, depth, ckv_buf)
    if fresh:
        k_all, vt_all = _kv_up(ckv_buf, layer, kr, b, t, f['wk'], f['wv_pad'], f['hk_nope'])
        o_a = _attention_t(q, k_all.reshape(b, t, QK_W), vt_all)
    else:
        past_len = past['ckv'].shape[2]
        t_pad = -(-t // 128) * 128
        pad_rows = lambda a: jnp.pad(a.reshape(b, t, -1), ((0, 0), (0, t_pad - t), (0, 0)))
        kv = _kv_hist(past['ckv'], past['kr'], past_len + t_pad, 0, f['wkt'], f['wv'], f['hk_nope'],
                      layer=past['layer'])
        kt_all, v_all = _kv_hist(pad_rows(ckv_buf[layer]), pad_rows(kr), past_len + t_pad, past_len,
                                 f['wkt'], f['wv'], f['hk_nope'], into=kv)
        o_a = _attention_hist(q.reshape(b, t, QK_W), kt_all, v_all, past_len, past_len + t)

    conv8 = jnp.pad(past['conv'].reshape(b, CONV_W - 1, 3, GDN_QK).transpose(0, 2, 1, 3),
                    ((0, 0), (0, 0), (8 - (CONV_W - 1), 0), (0, 0)))
    gdn_parts = _gdn_prep(p3, f['conv_w'], conv8, f['alog'], f['dt'])
    gdn_conv = p3[:, t - (CONV_W - 1):, COL_GDN:COL_GDN + 3 * GDN_QK]

    hgrn_parts = _hgrn_prep(p3, lb)
    o_b, gdn_s, o_c, hgrn_st = _rec_scan(gdn_parts, hgrn_parts, p3, past['gdn'],
                                         jnp.swapaxes(past['hgrn'], -1, -2))
    hgrn_s = jnp.swapaxes(hgrn_st, -1, -2)

    x = _merge(x, o_a.reshape(n, -1), o_b.reshape(n, -1), o_c.reshape(n, -1), p, p_gate, f['gdn_g'],
               f['hgrn_g'], f['wo_a'], f['wo_b'], f['wo_c'], f['w_out'])

    wg, wu, wd = f['ffn']
    if f['router'] is None:
        x = _ffn(x, f['ffn_g'], wg, wu, wd)
    else:
        t_moe = _row_tile(n, MOE_TILE)
        cw, cwt, cnt = _router(x, f['ffn_g'], *f['router'], t_moe)
        counts = cnt[:, 0, :N_EXPERTS].astype(jnp.int32).reshape(-1)
        x = _moe(x, f['ffn_g'], cw, cwt, counts, wg, wu, wd, t_moe)

    new_kr = kr.reshape(b, t, HEAD_PAD)[:, :, ROPE_LANE0:ROPE_LANE0 + MLA_ROPE]
    return x, ckv_buf, (new_kr, gdn_s, gdn_conv, hgrn_s)


def kernel(x_prompt, x_sample, cache_mla_ckv, cache_mla_krope, state_gdn, state_gdn_conv, state_hgrn,
           mixer_norm_g, w_in, mla_q_norm_g, mla_w_q_up, mla_kv_norm_g, mla_w_kv_up,
           mla_q_head_norm_g, mla_k_head_norm_g, mla_w_o,
           gdn_conv_w, gdn_a_log, gdn_dt_bias, gdn_norm_g, gdn_w_o,
           hgrn_lb_logits, hgrn_norm_g, hgrn_w_o, w_out, ffn_norm_g,
           dense_w_gate, dense_w_up, dense_w_down,
           moe_w_router, moe_b_router, moe_w_gate, moe_w_up, moe_w_down):
    a = dict(mixer_norm_g=mixer_norm_g, w_in=w_in, mla_q_norm_g=mla_q_norm_g, mla_w_q_up=mla_w_q_up,
             mla_kv_norm_g=mla_kv_norm_g, mla_w_kv_up=mla_w_kv_up, mla_q_head_norm_g=mla_q_head_norm_g,
             mla_k_head_norm_g=mla_k_head_norm_g, mla_w_o=mla_w_o, gdn_conv_w=gdn_conv_w,
             gdn_a_log=gdn_a_log, gdn_dt_bias=gdn_dt_bias, gdn_norm_g=gdn_norm_g, gdn_w_o=gdn_w_o,
             hgrn_norm_g=hgrn_norm_g, hgrn_w_o=hgrn_w_o, w_out=w_out, ffn_norm_g=ffn_norm_g,
             dense_w_gate=dense_w_gate, dense_w_up=dense_w_up, dense_w_down=dense_w_down,
             moe_w_router=moe_w_router, moe_b_router=moe_b_router, moe_w_gate=moe_w_gate,
             moe_w_up=moe_w_up, moe_w_down=moe_w_down)
    depth = w_in.shape[0]
    lb_soft = jax.nn.softmax(hgrn_lb_logits.astype(F32), axis=0)
    hgrn_lb = jnp.cumsum(lb_soft, axis=0) - lb_soft[0]

    b_p, t_p = x_prompt.shape[:2]
    b_s, t_s = x_sample.shape[:2]
    past_len = cache_mla_ckv.shape[2]
    tab_all = _rope_tables(max(t_p, past_len + t_s))
    tabs_p = tuple(tb[:t_p] for tb in tab_all)
    tabs_s = tuple(tb[past_len:past_len + t_s] for tb in tab_all)

    xp = x_prompt.reshape(b_p * t_p, D_MODEL)
    xs = x_sample.reshape(b_s * t_s, D_MODEL)
    past_p = dict(ckv=None, kr=None,
                  gdn=jnp.zeros((b_p, GDN_HEADS, GDN_DK, GDN_DV), F32),
                  conv=jnp.zeros((b_p, CONV_W - 1, 3 * GDN_QK), F32),
                  hgrn=jnp.zeros((b_p, HGRN_HEADS, HGRN_DK, HGRN_DV), F32))
    st_p, st_s = [], []
    ckv_p = ckv_s = None
    for l in range(depth):
        f = _layer_weights(l, a)
        lb = hgrn_lb[l][None]
        past_s = dict(ckv=cache_mla_ckv, kr=cache_mla_krope, layer=l, gdn=state_gdn[l],
                      conv=state_gdn_conv[l], hgrn=state_hgrn[l])
        xp, ckv_p, sp = _trunk_layer(xp, b_p, t_p, f, lb, tabs_p, past_p, l, depth, ckv_p)
        xs, ckv_s, ss = _trunk_layer(xs, b_s, t_s, f, lb, tabs_s, past_s, l, depth, ckv_s)
        st_p.append(sp)
        st_s.append(ss)

    def stack(lst, i):
        return jnp.stack([s[i] for s in lst], axis=0)

    return (xp.reshape(b_p, t_p, D_MODEL), xs.reshape(b_s, t_s, D_MODEL),
            ckv_p.reshape(depth, b_p, t_p, MLA_KV_LORA),
            stack(st_p, 0), stack(st_p, 1), stack(st_p, 2), stack(st_p, 3),
            ckv_s.reshape(depth, b_s, t_s, MLA_KV_LORA),
            stack(st_s, 0), stack(st_s, 1), stack(st_s, 2), stack(st_s, 3))
```

```python
import functools

import jax
import jax.numpy as jnp
import numpy as np
from jax import lax
from jax.experimental import pallas as pl
from jax.experimental.pallas import tpu as pltpu

F32 = jnp.float32
BF16 = jnp.bfloat16

D_MODEL = 1024
CHUNK = 64
NORM_EPS = 1e-6

MLA_HEADS = 8
MLA_NOPE = 64
MLA_ROPE = 32
MLA_V = 64
MLA_Q_LORA = 384
MLA_KV_LORA = 256
MLA_QK_HEAD = MLA_NOPE + MLA_ROPE
MLA_VW = MLA_HEADS * MLA_V
ROPE_THETA = 10000.0
LOG2_E = 1.4426950408889634
HEAD_PAD = 128
QK_W = MLA_HEADS * HEAD_PAD
VT_ROWS = MLA_V + 16

GDN_HEADS = 4
GDN_DK = 128
GDN_DV = 128
GDN_QK = GDN_HEADS * GDN_DK
GDN_VW = GDN_HEADS * GDN_DV
CONV_W = 4
HGRN_PREP_ROWS = 512
GDN_PREP_ROWS = 1024

HGRN_HEADS = 4
HGRN_DK = 128
HGRN_DV = 128
HGRN_KW = HGRN_HEADS * HGRN_DK

N_BRANCH = 3
SPLIT_SIZES = (MLA_Q_LORA, MLA_KV_LORA, MLA_ROPE,
               GDN_QK, GDN_QK, GDN_VW, GDN_VW, GDN_HEADS, GDN_HEADS,
               HGRN_KW, HGRN_KW, HGRN_KW, HGRN_KW,
               N_BRANCH * D_MODEL)

N_EXPERTS = 8
FF_EXPERT = 1408
MOE_TILE = 1024
TOP_K = 2
MOE_BLOCK_ROWS = tuple(MOE_TILE * TOP_K // N_EXPERTS + d for d in (-32, -16, 0, 16, 32, 48))

P_COLS = 8192
P_MAIN = P_COLS - N_BRANCH * D_MODEL
P_TN = 512
COL_CKV = MLA_Q_LORA
COL_KR = MLA_Q_LORA + MLA_KV_LORA
COL_GAB = COL_KR + 128
COL_GDN = 1024
ROPE_LANE0 = MLA_NOPE

VMEM_LIMIT = 56 * 1024 * 1024


def _cparams(sem):
    return pltpu.CompilerParams(dimension_semantics=sem, vmem_limit_bytes=VMEM_LIMIT)


def _row_tile(n, cap):
    for t in (2048, 1024, 512, 256, 128, 64, 32, 16, 8):
        if t <= cap and n % t == 0:
            return t
    raise ValueError(f"no row tile for {n}")


def _sigmoid(x):
    return 1.0 / (1.0 + jnp.exp(-x))


def _silu(x):
    return x * (0.5 * jnp.tanh(0.5 * x) + 0.5)


def _rms(x, g):
    ms = jnp.mean(x * x, axis=-1, keepdims=True)
    return x * lax.rsqrt(ms + NORM_EPS) * g


def _in_proj_kernel(x_ref, g_ref, w_ref, o_ref, gate_ref, h_ref):
    j = pl.program_id(1)

    @pl.when(j == 0)
    def _():
        h_ref[...] = _rms(x_ref[...], g_ref[...]).astype(BF16)

    @pl.when(j < P_MAIN // P_TN)
    def _():
        o_ref[...] = jnp.dot(h_ref[...], w_ref[...], preferred_element_type=F32)

    @pl.when(j >= P_MAIN // P_TN)
    def _():
        gate_ref[...] = jnp.dot(h_ref[...], w_ref[...], preferred_element_type=F32).astype(BF16)


def _in_proj(x, g, w):
    n = x.shape[0]
    tm = _row_tile(n, 2048)
    n_main = P_MAIN // P_TN
    return pl.pallas_call(
        _in_proj_kernel,
        grid=(n // tm, P_COLS // P_TN),
        in_specs=[pl.BlockSpec((tm, D_MODEL), lambda i, j: (i, 0)),
                  pl.BlockSpec((1, D_MODEL), lambda i, j: (0, 0)),
                  pl.BlockSpec((D_MODEL, P_TN), lambda i, j: (0, j))],
        out_specs=[pl.BlockSpec((tm, P_TN), lambda i, j: (i, jnp.minimum(j, n_main - 1))),
                   pl.BlockSpec((tm, P_TN), lambda i, j: (i, jnp.maximum(j - n_main, 0)))],
        out_shape=[jax.ShapeDtypeStruct((n, P_MAIN), F32),
                   jax.ShapeDtypeStruct((n, P_COLS - P_MAIN), BF16)],
        scratch_shapes=[pltpu.VMEM((tm, D_MODEL), BF16)],
        compiler_params=_cparams(("parallel", "arbitrary")),
        name="in_proj",
    )(x, g, w)


def _rope(x, c, s1, s2):
    return x * c + pltpu.roll(x, HEAD_PAD - 16, 1) * s1 + pltpu.roll(x, 16, 1) * s2


def _mla_pre_kernel(p_ref, gq_ref, gkv_ref, wq_ref, hq_ref, hk_ref, c_ref, s1_ref, s2_ref, *rest,
                    scale, transposed):
    q_ref, ckv_ref, kr_ref = rest[-3:]
    if transposed:
        hqt_ref, ct_ref, s1t_ref, s2t_ref = rest[:4]
    c, s1, s2 = c_ref[...], s1_ref[...], s2_ref[...]
    ckv_ref[...] = _rms(p_ref[:, COL_CKV:COL_KR], gkv_ref[...])

    kr = p_ref[:, COL_KR:COL_KR + HEAD_PAD]
    kr_ms = jnp.sum(kr * kr, axis=-1, keepdims=True) * (1.0 / MLA_ROPE)
    kr_ref[...] = _rope(kr * lax.rsqrt(kr_ms + NORM_EPS) * hk_ref[...], c, s1, s2)

    cq = _rms(p_ref[:, 0:MLA_Q_LORA], gq_ref[...]).astype(BF16)
    q = jnp.dot(cq, wq_ref[...], preferred_element_type=F32)
    if transposed:
        row = lax.broadcasted_iota(jnp.int32, (HEAD_PAD, 1), 0)
        hqt, ct, s1t, s2t = hqt_ref[...], ct_ref[...], s1t_ref[...], s2t_ref[...]
        for h in range(MLA_HEADS):
            qt = q[:, h * HEAD_PAD:(h + 1) * HEAD_PAD].T
            sq = qt * qt
            ms_n = jnp.sum(sq[0:MLA_NOPE], axis=0, keepdims=True) * (1.0 / MLA_NOPE)
            ms_r = jnp.sum(sq[MLA_NOPE:MLA_QK_HEAD], axis=0, keepdims=True) * (1.0 / MLA_ROPE)
            inv = jnp.where(row < MLA_NOPE, lax.rsqrt(ms_n + NORM_EPS), lax.rsqrt(ms_r + NORM_EPS))
            x = qt * inv * hqt
            x = x * ct + pltpu.roll(x, HEAD_PAD - 16, 0) * s1t + pltpu.roll(x, 16, 0) * s2t
            q_ref[0, h] = x.astype(BF16)
    else:
        lane = lax.broadcasted_iota(jnp.int32, (1, HEAD_PAD), 1)
        is_nope = lane < MLA_NOPE
        hq = hq_ref[...]
        for h in range(MLA_HEADS):
            qh = q[:, h * HEAD_PAD:(h + 1) * HEAD_PAD]
            sq = qh * qh
            ms_n = jnp.sum(jnp.where(is_nope, sq, 0.0), axis=-1, keepdims=True) * (1.0 / MLA_NOPE)
            ms_r = jnp.sum(jnp.where(is_nope, 0.0, sq), axis=-1, keepdims=True) * (1.0 / MLA_ROPE)
            inv = jnp.where(is_nope, lax.rsqrt(ms_n + NORM_EPS), lax.rsqrt(ms_r + NORM_EPS))
            qh = _rope(qh * inv * hq, c, s1, s2) * scale
            q_ref[:, h * HEAD_PAD:(h + 1) * HEAD_PAD] = qh.astype(BF16)


def _mla_pre(p, b, t_seq, gq, gkv, wq, hq, hk, tabs, transposed, layer, depth, ckv_buf):
    n = p.shape[0]
    tm = _row_tile(n, 512)
    c, s1, s2 = tabs
    if tm > t_seq:
        c, s1, s2 = (jnp.tile(t, (tm // t_seq, 1)) for t in (c, s1, s2))
    n_tab = c.shape[0] // tm
    tab_spec = pl.BlockSpec((tm, HEAD_PAD), lambda i: (i % n_tab, 0))
    vec = lambda w: pl.BlockSpec((1, w), lambda i: (0, 0))
    scale = MLA_QK_HEAD ** -0.5
    extra, extra_specs = (), []
    if transposed:
        scale *= LOG2_E
        q_spec = pl.BlockSpec((1, MLA_HEADS, HEAD_PAD, tm), lambda i: (i // n_tab, 0, 0, i % n_tab))
        q_shape = jax.ShapeDtypeStruct((b, MLA_HEADS, HEAD_PAD, t_seq), BF16)
        hqt = jnp.broadcast_to((hq[0] * scale)[:, None], (HEAD_PAD, tm))
        extra = (hqt, c.T, s1.T, s2.T)
        tab_t_spec = pl.BlockSpec((HEAD_PAD, tm), lambda i: (0, i % n_tab))
        extra_specs = [pl.BlockSpec((HEAD_PAD, tm), lambda i: (0, 0)), tab_t_spec, tab_t_spec, tab_t_spec]
    else:
        q_spec = pl.BlockSpec((tm, QK_W), lambda i: (i, 0))
        q_shape = jax.ShapeDtypeStruct((n, QK_W), BF16)
    aliases = {}
    if ckv_buf is not None:
        aliases = {9 + len(extra): 1}
        extra = (*extra, ckv_buf)
        extra_specs = [*extra_specs, pl.BlockSpec(memory_space=pl.ANY)]
    return pl.pallas_call(
        functools.partial(_mla_pre_kernel, scale=scale, transposed=transposed),
        grid=(n // tm,),
        in_specs=[pl.BlockSpec((tm, 1024), lambda i: (i, 0)),
                  vec(MLA_Q_LORA), vec(MLA_KV_LORA),
                  pl.BlockSpec((MLA_Q_LORA, QK_W), lambda i: (0, 0)),
                  vec(HEAD_PAD), vec(HEAD_PAD), tab_spec, tab_spec, tab_spec, *extra_specs],
        out_specs=[q_spec,
                   pl.BlockSpec((None, tm, MLA_KV_LORA), lambda i: (layer, i, 0)),
                   pl.BlockSpec((tm, HEAD_PAD), lambda i: (i, 0))],
        out_shape=[q_shape,
                   jax.ShapeDtypeStruct((depth, n, MLA_KV_LORA), F32),
                   jax.ShapeDtypeStruct((n, HEAD_PAD), F32)],
        input_output_aliases=aliases,
        compiler_params=_cparams(("parallel",)),
        name="mla_pre",
    )(p, gq, gkv, wq, hq, hk, c, s1, s2, *extra)


def _kv_up_kernel(ckv_ref, kr_ref, wk_ref, wv_ref, hk_ref, k_ref, v_ref):
    c = ckv_ref[...].astype(BF16)
    k = jnp.dot(c, wk_ref[...], preferred_element_type=F32)
    kr = kr_ref[...]
    hk = hk_ref[...]
    for h in range(MLA_HEADS):
        kh = k[:, h * HEAD_PAD:(h + 1) * HEAD_PAD]
        ms = jnp.sum(kh * kh, axis=-1, keepdims=True) * (1.0 / MLA_NOPE)
        k_ref[:, h * HEAD_PAD:(h + 1) * HEAD_PAD] = (kh * lax.rsqrt(ms + NORM_EPS) * hk + kr).astype(BF16)
    v = jnp.dot(c, wv_ref[...], preferred_element_type=F32)
    lane = lax.broadcasted_iota(jnp.int32, (1, HEAD_PAD), 1)
    for h in range(MLA_HEADS):
        vh = jnp.where(lane < MLA_V, v[:, h * HEAD_PAD:(h + 1) * HEAD_PAD], 1.0)
        v_ref[0, h] = vh.T[0:VT_ROWS].astype(BF16)


def _kv_up(ckv_buf, layer, kr, b, s_len, wk, wv, hk):
    n = ckv_buf.shape[1]
    tm = _row_tile(n, 512)
    n_t = s_len // tm
    return pl.pallas_call(
        _kv_up_kernel,
        grid=(n // tm,),
        in_specs=[pl.BlockSpec((None, tm, MLA_KV_LORA), lambda i: (layer, i, 0)),
                  pl.BlockSpec((tm, HEAD_PAD), lambda i: (i, 0)),
                  pl.BlockSpec((MLA_KV_LORA, QK_W), lambda i: (0, 0)),
                  pl.BlockSpec((MLA_KV_LORA, QK_W), lambda i: (0, 0)),
                  pl.BlockSpec((1, HEAD_PAD), lambda i: (0, 0))],
        out_specs=[pl.BlockSpec((tm, QK_W), lambda i: (i, 0)),
                   pl.BlockSpec((1, MLA_HEADS, VT_ROWS, tm), lambda i: (i // n_t, 0, 0, i % n_t))],
        out_shape=[jax.ShapeDtypeStruct((n, QK_W), BF16),
                   jax.ShapeDtypeStruct((b, MLA_HEADS, VT_ROWS, s_len), BF16)],
        compiler_params=_cparams(("parallel",)),
        name="kv_up",
    )(ckv_buf, kr, wk, wv, hk)


def _kv_hist_kernel(ckv_ref, kr_ref, wkt_ref, wv_ref, hkt_ref, *rest):
    kt_ref, v_ref = rest[-2:]
    c = ckv_ref[0].astype(BF16)
    kt = _dot_nt(wkt_ref[...], c)
    kr = kr_ref[0]
    if kr.shape[1] == MLA_ROPE:
        rows = kr.shape[0]
        kr = jnp.concatenate([jnp.zeros((rows, ROPE_LANE0), F32), kr,
                              jnp.zeros((rows, HEAD_PAD - ROPE_LANE0 - MLA_ROPE), F32)], axis=1)
    krt = kr.T
    hkt = hkt_ref[...]
    for h in range(MLA_HEADS):
        kh = kt[h * HEAD_PAD:(h + 1) * HEAD_PAD]
        ms = jnp.sum(kh * kh, axis=0, keepdims=True) * (1.0 / MLA_NOPE)
        kt_ref[0, h * HEAD_PAD:(h + 1) * HEAD_PAD, :] = (kh * lax.rsqrt(ms + NORM_EPS) * hkt + krt).astype(BF16)
    v_ref[0] = jnp.dot(c, wv_ref[...], preferred_element_type=F32).astype(BF16)


def _kv_hist(ckv, kr, s_total, row0, wkt, wv, hk, into=None, layer=None):
    b, s_in, _ = ckv.shape[-3:]
    tm = _row_tile(s_in, 1024)
    blk0 = row0 // tm
    hkt = jnp.broadcast_to(hk[0][:, None], (HEAD_PAD, tm))
    if layer is None:
        rows_spec = lambda w: pl.BlockSpec((1, tm, w), lambda i, j: (i, j, 0))
    else:
        rows_spec = lambda w: pl.BlockSpec((None, 1, tm, w), lambda i, j: (layer, i, j, 0))
    in_specs = [rows_spec(MLA_KV_LORA), rows_spec(kr.shape[-1]),
                pl.BlockSpec((QK_W, MLA_KV_LORA), lambda i, j: (0, 0)),
                pl.BlockSpec((MLA_KV_LORA, MLA_VW), lambda i, j: (0, 0)),
                pl.BlockSpec((HEAD_PAD, tm), lambda i, j: (0, 0))]
    args = [ckv, kr, wkt, wv, hkt]
    aliases = {}
    if into is not None:
        in_specs += [pl.BlockSpec(memory_space=pl.ANY), pl.BlockSpec(memory_space=pl.ANY)]
        args += list(into)
        aliases = {5: 0, 6: 1}
    return pl.pallas_call(
        _kv_hist_kernel,
        grid=(b, s_in // tm),
        in_specs=in_specs,
        out_specs=[pl.BlockSpec((1, QK_W, tm), lambda i, j: (i, 0, blk0 + j)),
                   pl.BlockSpec((1, tm, MLA_VW), lambda i, j: (i, blk0 + j, 0))],
        out_shape=[jax.ShapeDtypeStruct((b, QK_W, s_total), BF16),
                   jax.ShapeDtypeStruct((b, s_total, MLA_VW), BF16)],
        input_output_aliases=aliases,
        compiler_params=_cparams(("parallel", "parallel")),
        name="kv_hist",
    )(*args)


def _attn_hist_kernel(q_ref, kt_ref, v_ref, o_ref, *, q_pos0, s_valid):
    t, s_len = q_ref.shape[1], kt_ref.shape[2]
    k_pos = lax.broadcasted_iota(jnp.int32, (t, s_len), 1)
    q_pos = lax.broadcasted_iota(jnp.int32, (t, s_len), 0) + q_pos0
    allowed = (k_pos < s_valid) & (k_pos // CHUNK <= q_pos // CHUNK)
    scores = [jnp.dot(q_ref[0, :, h * HEAD_PAD:(h + 1) * HEAD_PAD], kt_ref[0, h * HEAD_PAD:(h + 1) * HEAD_PAD, :],
                      preferred_element_type=F32) for h in range(MLA_HEADS)]
    for h in range(MLA_HEADS):
        s = jnp.where(allowed, scores[h], -jnp.inf)
        p = jnp.exp(s - jnp.max(s, axis=-1, keepdims=True))
        pv = jnp.dot(p.astype(BF16), v_ref[0, :, h * MLA_V:(h + 1) * MLA_V], preferred_element_type=F32)
        o_ref[0, :, h * MLA_V:(h + 1) * MLA_V] = pv / jnp.sum(p, axis=-1, keepdims=True)


def _attention_hist(q, kt, v, q_pos0, s_valid):
    b, t, _ = q.shape
    s = kt.shape[2]
    return pl.pallas_call(
        functools.partial(_attn_hist_kernel, q_pos0=q_pos0, s_valid=s_valid),
        grid=(b,),
        in_specs=[pl.BlockSpec((1, t, QK_W), lambda i: (i, 0, 0)),
                  pl.BlockSpec((1, QK_W, s), lambda i: (i, 0, 0)),
                  pl.BlockSpec((1, s, MLA_VW), lambda i: (i, 0, 0))],
        out_specs=pl.BlockSpec((1, t, MLA_VW), lambda i: (i, 0, 0)),
        out_shape=jax.ShapeDtypeStruct((b, t, MLA_VW), F32),
        compiler_params=_cparams(("parallel",)),
        name="mla_attn_hist",
    )(q, kt, v)


def _attn_t_kernel(qt_ref, k_ref, vt_ref, o_ref, m_ref, acc_ref, *, tq, tk):
    m_ref[...] = jnp.full(m_ref.shape, -jnp.inf, F32)
    acc_ref[...] = jnp.zeros(acc_ref.shape, F32)

    def blocks(starts, masked):
        if masked:
            ck = lax.broadcasted_iota(jnp.int32, (tk, tq), 0) // CHUNK
            cq = lax.broadcasted_iota(jnp.int32, (tk, tq), 1) // CHUNK
            allowed = ck <= cq

        def scores(start, h):
            kh = k_ref[0, pl.ds(start, tk), h * HEAD_PAD:(h + 1) * HEAD_PAD]
            return jnp.dot(kh, qt_ref[0, h], preferred_element_type=F32)

        def update(start, h, s):
            if masked:
                s = jnp.where(allowed, s, -jnp.inf)
            m_prev = m_ref[h]
            m_new = jnp.maximum(m_prev, jnp.max(s, axis=0, keepdims=True))
            alpha = jnp.exp2(m_prev - m_new)
            p = jnp.exp2(s - m_new).astype(BF16)
            pv = jnp.dot(vt_ref[0, h, :, pl.ds(start, tk)], p, preferred_element_type=F32)
            acc_ref[h] = alpha * acc_ref[h] + pv
            m_ref[h] = m_new

        items = [(start, h) for start in starts for h in range(MLA_HEADS)]
        ahead = 5
        pending = [scores(*it) for it in items[:ahead]]
        for n, it in enumerate(items):
            s = pending.pop(0)
            if n + ahead < len(items):
                pending.append(scores(*items[n + ahead]))
            update(*it, s)

    qi = pl.program_id(1)

    def body(j, carry):
        first = pl.multiple_of(2 * j * tk, tk)
        blocks([first, pl.multiple_of(first + tk, tk)], False)
        return carry

    lax.fori_loop(0, qi // 2, body, 0)

    @pl.when(qi % 2 == 1)
    def _():
        blocks([pl.multiple_of((qi - 1) * tk, tk)], False)

    blocks([pl.multiple_of(qi * tk, tk)], True)

    ot = jnp.concatenate([acc_ref[h, 0:MLA_V] / acc_ref[h, MLA_V:MLA_V + 1] for h in range(MLA_HEADS)], axis=0)
    o_ref[0] = ot.T


def _attention_t(qt, k, vt):
    b, _, _, t = qt.shape
    s = k.shape[1]
    tq = tk = min(256, t)
    return pl.pallas_call(
        functools.partial(_attn_t_kernel, tq=tq, tk=tk),
        grid=(b, t // tq),
        in_specs=[pl.BlockSpec((1, MLA_HEADS, HEAD_PAD, tq), lambda i, j: (i, 0, 0, j)),
                  pl.BlockSpec((1, s, QK_W), lambda i, j: (i, 0, 0)),
                  pl.BlockSpec((1, MLA_HEADS, VT_ROWS, s), lambda i, j: (i, 0, 0, 0))],
        out_specs=pl.BlockSpec((1, tq, MLA_VW), lambda i, j: (i, j, 0)),
        out_shape=jax.ShapeDtypeStruct((b, t, MLA_VW), F32),
        scratch_shapes=[pltpu.VMEM((MLA_HEADS, 1, tq), F32),
                        pltpu.VMEM((MLA_HEADS, VT_ROWS, tq), F32)],
        compiler_params=_cparams(("parallel", "arbitrary")),
        name="mla_attn_t",
    )(qt, k, vt)


def _row_iota(shape):
    return lax.broadcasted_iota(jnp.int32, shape, 0)


def _upper_half_masks(shape):
    row = _row_iota(shape)
    masks = []
    m = 1
    while m < CHUNK:
        masks.append((row // m) % 2 == 1)
        m *= 2
    return masks


def _segment_scans(g, uppers):
    rows = g.shape[0]
    pre, tot = g, g
    out = [(pre, tot)]
    m = 1
    while m < CHUNK:
        upper = uppers[len(out) - 1]
        from_lower = pltpu.roll(tot, m, 0)
        from_upper = pltpu.roll(tot, rows - m, 0)
        pre = pre + jnp.where(upper, from_lower, 0.0)
        tot = tot + jnp.where(upper, from_lower, from_upper)
        out.append((pre, tot))
        m *= 2
    return out


def _dot_nt(a, b):
    return lax.dot_general(a, b, (((1,), (1,)), ((), ())), preferred_element_type=F32)


def _dot_tn(a, b):
    return lax.dot_general(a, b, (((0,), (0,)), ((), ())), preferred_element_type=F32)


def _gdn_prep_kernel(q_ref, k_ref, v_ref, ab_ref, cw_ref, cs_ref, alog_ref, dt_ref,
                     u_ref, w_ref, qd_ref, kd_ref, qk_ref, gl_ref, carry_ref, *, nb, r):
    @pl.when(pl.program_id(1) == 0)
    def _():
        carry_ref[...] = cs_ref[...]

    n = nb * r
    nc = n // CHUNK

    def conv(x_ref, j):
        w = cw_ref[:, j * GDN_QK:(j + 1) * GDN_QK]
        outs = []
        for b in range(nb):
            x = x_ref[b]
            xp = jnp.concatenate([carry_ref[b, j], x], axis=0)
            y = x * w[3:4]
            for d in range(1, CONV_W):
                y = y + xp[8 - d:8 - d + r] * w[3 - d:4 - d]
            carry_ref[b, j] = x[r - 8:]
            outs.append(_silu(y))
        return outs[0] if nb == 1 else jnp.concatenate(outs, axis=0)

    q_all, k_all, v_all = conv(q_ref, 0), conv(k_ref, 1), conv(v_ref, 2)

    ab = ab_ref[...].reshape(n, 128)
    x = ab + dt_ref[...]
    softplus = jnp.maximum(x, 0.0) + jnp.log(1.0 + jnp.exp(-jnp.abs(x)))
    g_blk = (-LOG2_E) * jnp.exp(alog_ref[...]) * softplus
    gam_blk = _segment_scans(g_blk, _upper_half_masks(g_blk.shape))[-1][0]
    gam_t = gam_blk.T
    gam3_blk = gam_blk.reshape(nc, CHUNK, 128)
    beta3_blk = _sigmoid(ab).reshape(nc, CHUNK, 128)

    row = lax.broadcasted_iota(jnp.int32, (1, CHUNK, CHUNK), 1)
    col = lax.broadcasted_iota(jnp.int32, (1, CHUNK, CHUNK), 2)
    eye = (row == col).astype(F32)

    def bmm(a, b):
        return jnp.einsum('cij,cjk->cik', a.astype(BF16), b.astype(BF16), preferred_element_type=F32)

    def bmm_nt(a, b):
        return jnp.einsum('cid,cjd->cij', a.astype(BF16), b.astype(BF16), preferred_element_type=F32)

    for h in range(GDN_HEADS):
        sl = slice(h * GDN_DK, (h + 1) * GDN_DK)
        q, k, v = q_all[:, sl], k_all[:, sl], v_all[:, sl]
        q = q * lax.rsqrt(jnp.sum(q * q, axis=-1, keepdims=True) + NORM_EPS) * (GDN_DK ** -0.5)
        k = k * lax.rsqrt(jnp.sum(k * k, axis=-1, keepdims=True) + NORM_EPS)
        q, k, v = (a.reshape(nc, CHUNK, GDN_DK) for a in (q, k, v))
        gam = gam3_blk[:, :, h:h + 1]
        beta = beta3_blk[:, :, GDN_HEADS + h:GDN_HEADS + h + 1]
        gam_row = jnp.stack([gam_t[h:h + 1, c * CHUNK:(c + 1) * CHUNK] for c in range(nc)], axis=0)
        decay = jnp.where(row >= col, jnp.exp2(jnp.minimum(gam - gam_row, 0.0)), 0.0)
        a = jnp.where(row > col, beta * bmm_nt(k, k) * decay, 0.0)
        t_inv = eye - a
        pw = a
        m = 1
        while 2 * m < CHUNK:
            pw = bmm(pw, pw)
            t_inv = t_inv + bmm(t_inv, pw)
            m *= 2
        e_gam = jnp.exp2(gam)
        gam_last = gam[:, CHUNK - 1:CHUNK, :]
        u_ref[:, h] = bmm(t_inv, v * beta).reshape(nb, r, GDN_DV)
        w_ref[:, h] = bmm(t_inv, k * (beta * e_gam)).astype(BF16).reshape(nb, r, GDN_DK)
        qd_ref[:, h] = (q * e_gam).astype(BF16).reshape(nb, r, GDN_DK)
        kd_ref[:, h] = (k * jnp.exp2(gam_last - gam)).astype(BF16).reshape(nb, r, GDN_DK)
        qk_ref[:, h] = (bmm_nt(q, k) * decay).astype(BF16).reshape(nb, r, CHUNK)
        gl_ref[:, h] = jnp.broadcast_to(jnp.exp2(gam_last), (nc, 1, 128)).reshape(nb, r // CHUNK, 1, 128)


def _gdn_scan_kernel(u_ref, w_ref, qd_ref, kd_ref, qk_ref, gl_ref, s0_ref, o_ref, sf_ref, s_ref, *, nb):
    c_idx = pl.program_id(1)

    @pl.when(c_idx == 0)
    def _():
        s_ref[...] = s0_ref[...]

    chains = [(b, h) for b in range(nb) for h in range(GDN_HEADS)]
    s_old = [s_ref[b, h] for b, h in chains]
    s_bf = [s.astype(BF16) for s in s_old]
    v_new = [u_ref[b, h] - jnp.dot(w_ref[b, h], sb, preferred_element_type=F32)
             for (b, h), sb in zip(chains, s_bf)]
    v_bf = [v.astype(BF16) for v in v_new]
    for (b, h), s, sb, vb in zip(chains, s_old, s_bf, v_bf):
        o = (jnp.dot(qd_ref[b, h], sb, preferred_element_type=F32)
             + jnp.dot(qk_ref[b, h], vb, preferred_element_type=F32))
        o_ref[b, :, h * GDN_DV:(h + 1) * GDN_DV] = o
        s_ref[b, h] = gl_ref[b, h, 0] * s + _dot_tn(kd_ref[b, h], vb)

    @pl.when(c_idx == pl.num_programs(1) - 1)
    def _():
        sf_ref[...] = s_ref[...]


def _gdn_prep(p3, conv_w, conv_state8, alog, dt):
    b, t, _ = p3.shape
    r = min(GDN_PREP_ROWS, t)
    nb = max(1, min(b, GDN_PREP_ROWS // r))
    nt = t // r
    nc = t // CHUNK
    blk = lambda j: pl.BlockSpec((nb, r, GDN_QK), lambda i, c: (i, c, COL_GDN // GDN_QK + j))
    vec = pl.BlockSpec((1, 128), lambda i, c: (0, 0))
    head_spec = lambda w: pl.BlockSpec((nb, GDN_HEADS, r, w), lambda i, c: (i, 0, c, 0))
    head_shape = lambda w, dt_: jax.ShapeDtypeStruct((b, GDN_HEADS, t, w), dt_)
    u, w, qd, kd, qk, gl = pl.pallas_call(
        functools.partial(_gdn_prep_kernel, nb=nb, r=r),
        grid=(b // nb, nt),
        in_specs=[blk(0), blk(1), blk(2),
                  pl.BlockSpec((nb, r, 128), lambda i, c: (i, c, COL_GAB // 128)),
                  pl.BlockSpec((CONV_W, 3 * GDN_QK), lambda i, c: (0, 0)),
                  pl.BlockSpec((nb, 3, 8, GDN_QK), lambda i, c: (i, 0, 0, 0)),
                  vec, vec],
        out_specs=[head_spec(GDN_DV), head_spec(GDN_DK), head_spec(GDN_DK), head_spec(GDN_DK),
                   head_spec(CHUNK),
                   pl.BlockSpec((nb, GDN_HEADS, r // CHUNK, 1, 128), lambda i, c: (i, 0, c, 0, 0))],
        out_shape=[head_shape(GDN_DV, F32), head_shape(GDN_DK, BF16), head_shape(GDN_DK, BF16),
                   head_shape(GDN_DK, BF16), head_shape(CHUNK, BF16),
                   jax.ShapeDtypeStruct((b, GDN_HEADS, nc, 1, 128), F32)],
        scratch_shapes=[pltpu.VMEM((nb, 3, 8, GDN_QK), F32)],
        compiler_params=_cparams(("parallel", "arbitrary")),
        name="gdn_prep",
    )(p3, p3, p3, p3, conv_w, conv_state8, alog, dt)
    return u, w, qd, kd, qk, gl


def _hgrn_prep_kernel(q_ref, f_ref, lb_ref, att_ref, qe_ref, ke_ref, dec_ref, *, nb, r):
    n = nb * r
    nc = n // CHUNK
    row = lax.broadcasted_iota(jnp.int32, (1, CHUNK, CHUNK), 1)
    col = lax.broadcasted_iota(jnp.int32, (1, CHUNK, CHUNK), 2)
    uppers = _upper_half_masks((n, HGRN_DK))
    halves = [1 << lvl for lvl in range(len(uppers))]
    pairs = [(row // (2 * m) == col // (2 * m)) & ((row // m) % 2 == 1) & ((col // m) % 2 == 0) for m in halves]

    def bmm_nt(a, b):
        a3, b3 = (x.astype(BF16).reshape(nc, CHUNK, HGRN_DK) for x in (a, b))
        return jnp.einsum('cid,cjd->cij', a3, b3, preferred_element_type=F32)

    for h in range(HGRN_HEADS):
        sl = slice(h * HGRN_DK, (h + 1) * HGRN_DK)
        lb = lb_ref[:, sl]
        f = lb + (1.0 - lb) * _sigmoid(f_ref[:, :, sl].reshape(n, HGRN_DK))
        q = _silu(q_ref[:, :, sl].reshape(n, HGRN_DK)) * (HGRN_DK ** -0.5)
        k = 1.0 - f
        scans = _segment_scans(jnp.log2(f), uppers)
        cb, c_tot = scans[-1]

        att = jnp.where(row == col, bmm_nt(q, k), 0.0)
        for lvl in range(len(halves)):
            pre_m, tot_m = scans[lvl]
            att = att + jnp.where(pairs[lvl], bmm_nt(q * jnp.exp2(pre_m), k * jnp.exp2(tot_m - pre_m)), 0.0)

        att_ref[:, h] = att.astype(BF16).reshape(nb, r, CHUNK)
        qe_ref[:, h] = (q * jnp.exp2(cb)).astype(BF16).reshape(nb, r, HGRN_DK)
        ke_ref[:, h] = (k * jnp.exp2(c_tot - cb)).astype(BF16).reshape(nb, r, HGRN_DK)
        dec_ref[:, h] = jnp.exp2(c_tot).reshape(nc, CHUNK, HGRN_DK)[:, 0:1, :].reshape(nb, r // CHUNK, 1, HGRN_DK)


def _hgrn_scan_kernel(att_ref, qe_ref, ke_ref, dec_ref, v_ref, s0_ref, o_ref, sf_ref, st_ref, *, nb):
    c_idx = pl.program_id(1)

    @pl.when(c_idx == 0)
    def _():
        st_ref[...] = s0_ref[...]

    chains = [(b, h) for b in range(nb) for h in range(HGRN_HEADS)]
    sls = [slice(h * HGRN_DV, (h + 1) * HGRN_DV) for h in range(HGRN_HEADS)]
    st_old = [st_ref[b, h] for b, h in chains]
    vs = [v_ref[b, :, sls[h]].astype(BF16) for b, h in chains]
    o_st = [_dot_nt(qe_ref[b, h], st.astype(BF16)) for (b, h), st in zip(chains, st_old)]
    for (b, h), st, v, o1 in zip(chains, st_old, vs, o_st):
        o_ref[b, :, sls[h]] = o1 + jnp.dot(att_ref[b, h], v, preferred_element_type=F32)
        st_ref[b, h] = st * dec_ref[b, h, 0] + _dot_tn(v, ke_ref[b, h])

    @pl.when(c_idx == pl.num_programs(1) - 1)
    def _():
        sf_ref[...] = st_ref[...]


def _hgrn_prep(p3, lb):
    b, t, _ = p3.shape
    r = min(HGRN_PREP_ROWS, t)
    nb = max(1, min(b, HGRN_PREP_ROWS // r))
    nc = t // CHUNK
    col0 = (COL_GDN + 4 * GDN_QK) // HGRN_KW
    blk = lambda j: pl.BlockSpec((nb, r, HGRN_KW), lambda i, c: (i, c, col0 + j))
    head_spec = lambda w: pl.BlockSpec((nb, HGRN_HEADS, r, w), lambda i, c: (i, 0, c, 0))
    head_shape = lambda w: jax.ShapeDtypeStruct((b, HGRN_HEADS, t, w), BF16)
    att, qe, ke, dec = pl.pallas_call(
        functools.partial(_hgrn_prep_kernel, nb=nb, r=r),
        grid=(b // nb, t // r),
        in_specs=[blk(0), blk(1), pl.BlockSpec((1, HGRN_KW), lambda i, c: (0, 0))],
        out_specs=[head_spec(CHUNK), head_spec(HGRN_DK), head_spec(HGRN_DK),
                   pl.BlockSpec((nb, HGRN_HEADS, r // CHUNK, 1, HGRN_DK), lambda i, c: (i, 0, c, 0, 0))],
        out_shape=[head_shape(CHUNK), head_shape(HGRN_DK), head_shape(HGRN_DK),
                   jax.ShapeDtypeStruct((b, HGRN_HEADS, nc, 1, HGRN_DK), F32)],
        compiler_params=_cparams(("parallel", "parallel")),
        name="hgrn_prep",
    )(p3, p3, lb)
    return att, qe, ke, dec


def _rec_scan_kernel(u_ref, w_ref, qd_ref, kd_ref, qk_ref, gl_ref, sg0_ref,
                     att_ref, qe_ref, ke_ref, dec_ref, v_ref, sh0_ref,
                     og_ref, sgf_ref, oh_ref, shf_ref, sg_ref, sh_ref, *, nb):
    _gdn_scan_kernel(u_ref, w_ref, qd_ref, kd_ref, qk_ref, gl_ref, sg0_ref, og_ref, sgf_ref, sg_ref, nb=nb)
    _hgrn_scan_kernel(att_ref, qe_ref, ke_ref, dec_ref, v_ref, sh0_ref, oh_ref, shf_ref, sh_ref, nb=nb)


def _rec_scan(gdn_parts, hgrn_parts, p3, s0_gdn, s0t_hgrn):
    b, t, _ = p3.shape
    nc = t // CHUNK
    sb = min(b, 8)
    col_v = (COL_GDN + 4 * GDN_QK) // HGRN_KW + 2
    chunk_spec = lambda w: pl.BlockSpec((sb, GDN_HEADS, CHUNK, w), lambda i, c: (i, 0, c, 0))
    per_chunk = pl.BlockSpec((sb, GDN_HEADS, 1, 1, 128), lambda i, c: (i, 0, c, 0, 0))
    state_spec = pl.BlockSpec((sb, GDN_HEADS, 128, 128), lambda i, c: (i, 0, 0, 0))
    out_spec = pl.BlockSpec((sb, CHUNK, GDN_VW), lambda i, c: (i, c, 0))
    state_shape = jax.ShapeDtypeStruct((b, GDN_HEADS, 128, 128), F32)
    return pl.pallas_call(
        functools.partial(_rec_scan_kernel, nb=sb),
        grid=(b // sb, nc),
        in_specs=[chunk_spec(GDN_DV), chunk_spec(GDN_DK), chunk_spec(GDN_DK), chunk_spec(GDN_DK),
                  chunk_spec(CHUNK), per_chunk, state_spec,
                  chunk_spec(CHUNK), chunk_spec(HGRN_DK), chunk_spec(HGRN_DK), per_chunk,
                  pl.BlockSpec((sb, CHUNK, HGRN_KW), lambda i, c: (i, c, col_v)), state_spec],
        out_specs=[out_spec, state_spec, out_spec, state_spec],
        out_shape=[jax.ShapeDtypeStruct((b, t, GDN_VW), F32), state_shape,
                   jax.ShapeDtypeStruct((b, t, HGRN_KW), F32), state_shape],
        scratch_shapes=[pltpu.VMEM((sb, GDN_HEADS, 128, 128), F32), pltpu.VMEM((sb, HGRN_HEADS, 128, 128), F32)],
        compiler_params=_cparams(("parallel", "arbitrary")),
        name="rec_scan",
    )(*gdn_parts, s0_gdn, *hgrn_parts, p3, s0t_hgrn)


def _merge_kernel(x_ref, oa_ref, ob_ref, zb_ref, oc_ref, zc_ref, g0_ref, g1_ref, g2_ref, ngb_ref, ngc_ref,
                  wa_ref, wb_ref, wc_ref, wo_ref, y_ref):
    def branch(o, w_ref, g_ref):
        gate = _sigmoid(g_ref[...].astype(F32))
        return gate * jnp.dot(o.astype(BF16), w_ref[...], preferred_element_type=F32)

    def normed(o_ref, z_ref, ng_ref):
        return jnp.concatenate(
            [_rms(o_ref[:, h * 128:(h + 1) * 128], ng_ref[...]) * _silu(z_ref[:, h * 128:(h + 1) * 128])
             for h in range(GDN_HEADS)], axis=-1)

    mixed = (branch(oa_ref[...], wa_ref, g0_ref) + branch(normed(ob_ref, zb_ref, ngb_ref), wb_ref, g1_ref)
             + branch(normed(oc_ref, zc_ref, ngc_ref), wc_ref, g2_ref))
    y_ref[...] = x_ref[...] + jnp.dot(mixed.astype(BF16), wo_ref[...], preferred_element_type=F32)


def _merge(x, oa, ob, oc, p, pg, ngb, ngc, wa, wb, wc, wo):
    n = x.shape[0]
    tm = _row_tile(n, 512)
    row = lambda w: pl.BlockSpec((tm, w), lambda i: (i, 0))
    gate = lambda j: pl.BlockSpec((tm, D_MODEL), lambda i: (i, j))
    zb_spec = pl.BlockSpec((tm, GDN_VW), lambda i: (i, COL_GDN // GDN_VW + 3))
    zc_spec = pl.BlockSpec((tm, HGRN_KW), lambda i: (i, (COL_GDN + 4 * GDN_QK) // HGRN_KW + 3))
    vec = pl.BlockSpec((1, 128), lambda i: (0, 0))
    wsp = lambda k: pl.BlockSpec((k, D_MODEL), lambda i: (0, 0))
    return pl.pallas_call(
        _merge_kernel,
        grid=(n // tm,),
        in_specs=[row(D_MODEL), row(MLA_VW), row(GDN_VW), zb_spec, row(HGRN_KW), zc_spec,
                  gate(0), gate(1), gate(2), vec, vec,
                  wsp(MLA_VW), wsp(GDN_VW), wsp(HGRN_KW), wsp(D_MODEL)],
        out_specs=row(D_MODEL),
        out_shape=jax.ShapeDtypeStruct((n, D_MODEL), F32),
        compiler_params=_cparams(("parallel",)),
        name="merge_out",
    )(x, oa, ob, p, oc, p, pg, pg, pg, ngb, ngc, wa, wb, wc, wo)


def _router_kernel(x_ref, g_ref, w_ref, b_ref, cw_ref, cwt_ref, cnt_ref):
    h = _rms(x_ref[...], g_ref[...])
    w = w_ref[...]
    h_hi, w_hi = h.astype(BF16), w.astype(BF16)
    h_lo, w_lo = (h - h_hi.astype(F32)).astype(BF16), (w - w_hi.astype(F32)).astype(BF16)
    logits = (jnp.dot(h_hi, w_hi, preferred_element_type=F32) + jnp.dot(h_hi, w_lo, preferred_element_type=F32)
              + jnp.dot(h_lo, w_hi, preferred_element_type=F32)) + b_ref[...]
    lane = lax.broadcasted_iota(jnp.int32, logits.shape, 1)
    valid = lane < N_EXPERTS
    neg = -jnp.inf
    l1 = jnp.where(valid, logits, neg)
    m1 = jnp.max(l1, axis=-1, keepdims=True)
    i1 = jnp.min(jnp.where(l1 == m1, lane, 128), axis=-1, keepdims=True)
    l2 = jnp.where(lane == i1, neg, l1)
    m2 = jnp.max(l2, axis=-1, keepdims=True)
    i2 = jnp.min(jnp.where(l2 == m2, lane, 128), axis=-1, keepdims=True)
    e2 = jnp.exp(m2 - m1)
    den = 1.0 + e2
    cw = jnp.where(lane == i1, 1.0 / den, 0.0) + jnp.where(lane == i2, e2 / den, 0.0)
    cw_ref[...] = cw
    cwt_ref[...] = cw.T
    cnt_ref[0] = jnp.sum((cw > 0.0).astype(F32), axis=0, keepdims=True)


def _router(x, g, w, b, tm):
    n = x.shape[0]
    return pl.pallas_call(
        _router_kernel,
        grid=(n // tm,),
        in_specs=[pl.BlockSpec((tm, D_MODEL), lambda i: (i, 0)),
                  pl.BlockSpec((1, D_MODEL), lambda i: (0, 0)),
                  pl.BlockSpec((D_MODEL, 128), lambda i: (0, 0)),
                  pl.BlockSpec((1, 128), lambda i: (0, 0))],
        out_specs=[pl.BlockSpec((tm, 128), lambda i: (i, 0)),
                   pl.BlockSpec((128, tm), lambda i: (0, i)),
                   pl.BlockSpec((1, 1, 128), lambda i: (i, 0, 0))],
        out_shape=[jax.ShapeDtypeStruct((n, 128), F32),
                   jax.ShapeDtypeStruct((128, n), F32),
                   jax.ShapeDtypeStruct((n // tm, 1, 128), F32)],
        compiler_params=_cparams(("parallel",)),
        name="moe_router",
    )(x, g, w, b)


def _moe_kernel(cnt_ref, x_ref, g_ref, cw_ref, cwt_ref, wg_ref, wu_ref, wd_ref, y_ref,
                h_ref, rcol_ref, rrow_ref, *, t, br):
    i, e = pl.program_id(0), pl.program_id(1)

    @pl.when(e == 0)
    def _():
        x = x_ref[...]
        h_ref[...] = _rms(x, g_ref[...]).astype(BF16)
        y_ref[...] = x
        r = lax.broadcasted_iota(jnp.int32, (t, t), 0)
        c = lax.broadcasted_iota(jnp.int32, (t, t), 1)
        on = cw_ref[...] > 0.0
        rank = jnp.dot((c < r).astype(BF16), on.astype(BF16), preferred_element_type=F32)
        rcol_ref[...] = jnp.where(on, rank.astype(jnp.int32), -1)
        on_t = cwt_ref[...] > 0.0
        rank_t = jnp.dot(on_t.astype(BF16), (r < c).astype(BF16), preferred_element_type=F32)
        rrow_ref[...] = jnp.where(on_t, rank_t.astype(jnp.int32), -1)

    count = cnt_ref[i * N_EXPERTS + e]
    sel = lax.broadcasted_iota(jnp.int32, (1, 128), 1) == e
    rank_c = jnp.sum(jnp.where(sel, rcol_ref[...], 0), axis=-1, keepdims=True)
    rank_r = rrow_ref[pl.ds(e, 1), :]
    w_r = cwt_ref[pl.ds(e, 1), :]

    def expert_block(base, rows_n):
        rows = lax.broadcasted_iota(jnp.int32, (rows_n, t), 0) + base
        pick = rows == rank_r
        xc = jnp.dot(pick.astype(BF16), h_ref[...], preferred_element_type=F32).astype(BF16)
        a = jnp.dot(xc, wg_ref[0], preferred_element_type=F32)
        b = jnp.dot(xc, wu_ref[0], preferred_element_type=F32)
        yc = jnp.dot((_silu(a) * b).astype(BF16), wd_ref[0], preferred_element_type=F32)
        w_rows = jnp.sum(jnp.where(pick, w_r, 0.0), axis=-1, keepdims=True)
        cols = lax.broadcasted_iota(jnp.int32, (t, rows_n), 1) + base
        y_ref[...] += jnp.dot((cols == rank_c).astype(BF16), (yc * w_rows).astype(BF16),
                              preferred_element_type=F32)

    lo = 0
    for size in br:
        @pl.when((count > lo) & (count <= size))
        def _(size=size):
            expert_block(0, size)
        lo = size

    @pl.when(count > br[-1])
    def _():
        def body(j, carry):
            expert_block(j * br[-1], br[-1])
            return carry

        lax.fori_loop(0, lax.div(count + (br[-1] - 1), br[-1]), body, 0)


def _moe(x, g, cw, cwt, counts, wg, wu, wd, t):
    n = x.shape[0]
    ne, _, ff = wg.shape
    br = MOE_BLOCK_ROWS
    grid_spec = pltpu.PrefetchScalarGridSpec(
        num_scalar_prefetch=1,
        grid=(n // t, ne),
        in_specs=[pl.BlockSpec((t, D_MODEL), lambda i, e, cnt: (i, 0)),
                  pl.BlockSpec((1, D_MODEL), lambda i, e, cnt: (0, 0)),
                  pl.BlockSpec((t, 128), lambda i, e, cnt: (i, 0)),
                  pl.BlockSpec((128, t), lambda i, e, cnt: (0, i)),
                  pl.BlockSpec((1, D_MODEL, ff), lambda i, e, cnt: (e, 0, 0)),
                  pl.BlockSpec((1, D_MODEL, ff), lambda i, e, cnt: (e, 0, 0)),
                  pl.BlockSpec((1, ff, D_MODEL), lambda i, e, cnt: (e, 0, 0))],
        out_specs=pl.BlockSpec((t, D_MODEL), lambda i, e, cnt: (i, 0)),
        scratch_shapes=[pltpu.VMEM((t, D_MODEL), BF16),
                        pltpu.VMEM((t, 128), jnp.int32),
                        pltpu.VMEM((128, t), jnp.int32)])
    return pl.pallas_call(
        functools.partial(_moe_kernel, t=t, br=br),
        grid_spec=grid_spec,
        out_shape=jax.ShapeDtypeStruct((n, D_MODEL), F32),
        compiler_params=_cparams(("parallel", "arbitrary")),
        name="moe",
    )(counts, x, g, cw, cwt, wg, wu, wd)


def _ffn_kernel(x_ref, g_ref, wg_ref, wu_ref, wd_ref, y_ref, h_ref):
    e = pl.program_id(1)

    @pl.when(e == 0)
    def _():
        x = x_ref[...]
        h_ref[...] = _rms(x, g_ref[...]).astype(BF16)
        y_ref[...] = x

    h = h_ref[...]
    a = jnp.dot(h, wg_ref[0], preferred_element_type=F32)
    b = jnp.dot(h, wu_ref[0], preferred_element_type=F32)
    y_ref[...] += jnp.dot((_silu(a) * b).astype(BF16), wd_ref[0], preferred_element_type=F32)


def _ffn(x, g, wg, wu, wd):
    n = x.shape[0]
    ne, _, ff = wg.shape
    tm = _row_tile(n, 1024)
    return pl.pallas_call(
        _ffn_kernel,
        grid=(n // tm, ne),
        in_specs=[pl.BlockSpec((tm, D_MODEL), lambda i, e: (i, 0)),
                  pl.BlockSpec((1, D_MODEL), lambda i, e: (0, 0)),
                  pl.BlockSpec((1, D_MODEL, ff), lambda i, e: (e, 0, 0)),
                  pl.BlockSpec((1, D_MODEL, ff), lambda i, e: (e, 0, 0)),
                  pl.BlockSpec((1, ff, D_MODEL), lambda i, e: (e, 0, 0))],
        out_specs=pl.BlockSpec((tm, D_MODEL), lambda i, e: (i, 0)),
        out_shape=jax.ShapeDtypeStruct((n, D_MODEL), F32),
        scratch_shapes=[pltpu.VMEM((tm, D_MODEL), BF16)],
        compiler_params=_cparams(("parallel", "arbitrary")),
        name="ffn",
    )(x, g, wg, wu, wd)


def _pad_lanes(x, left, total):
    return jnp.pad(x, [(0, 0)] * (x.ndim - 1) + [(left, total - left - x.shape[-1])])


def _pack_w_in(w):
    cq, ckv, kr, gq, gk, gv, gz, ga, gb, hq, hf, hi, hg, gates = jnp.split(
        w, np.cumsum(SPLIT_SIZES)[:-1].tolist(), axis=-1)
    kr_blk = _pad_lanes(kr, ROPE_LANE0, 128)
    ab_blk = _pad_lanes(jnp.concatenate([ga, gb], axis=-1), 0, 256)
    return jnp.concatenate([cq, ckv, kr_blk, ab_blk, gq, gk, gv, gz, hq, hf, hi, hg, gates], axis=-1).astype(BF16)


def _rope_tables(n_pos):
    half = MLA_ROPE // 2
    inv = ROPE_THETA ** (-jnp.arange(half, dtype=F32) / half)
    ang = jnp.arange(n_pos, dtype=F32)[:, None] * inv[None, :]
    cos, sin = jnp.cos(ang), jnp.sin(ang)
    one = jnp.ones((n_pos, MLA_NOPE), F32)
    zero = jnp.zeros((n_pos, MLA_NOPE), F32)
    tail = jnp.zeros((n_pos, HEAD_PAD - MLA_QK_HEAD), F32)
    z16 = jnp.zeros((n_pos, half), F32)
    c = jnp.concatenate([one, cos, cos, tail], axis=-1)
    s1 = jnp.concatenate([zero, -sin, z16, tail], axis=-1)
    s2 = jnp.concatenate([zero, z16, sin, tail], axis=-1)
    return c, s1, s2


def _layer_weights(l, a):
    f = {}
    f['mixer_g'] = a['mixer_norm_g'][l][None]
    f['w_in'] = _pack_w_in(a['w_in'][l])
    f['gq'] = a['mla_q_norm_g'][l][None]
    f['gkv'] = a['mla_kv_norm_g'][l][None]
    wq = a['mla_w_q_up'][l].reshape(MLA_Q_LORA, MLA_HEADS, MLA_QK_HEAD)
    f['wq'] = _pad_lanes(wq, 0, HEAD_PAD).reshape(MLA_Q_LORA, QK_W).astype(BF16)
    wkv = a['mla_w_kv_up'][l].reshape(MLA_KV_LORA, MLA_HEADS, MLA_NOPE + MLA_V)
    f['wk'] = _pad_lanes(wkv[:, :, :MLA_NOPE], 0, HEAD_PAD).reshape(MLA_KV_LORA, QK_W).astype(BF16)
    f['wkt'] = f['wk'].T
    f['wv'] = wkv[:, :, MLA_NOPE:].reshape(MLA_KV_LORA, MLA_VW).astype(BF16)
    f['wv_pad'] = _pad_lanes(wkv[:, :, MLA_NOPE:], 0, HEAD_PAD).reshape(MLA_KV_LORA, QK_W).astype(BF16)
    f['hq'] = _pad_lanes(a['mla_q_head_norm_g'][l][None], 0, HEAD_PAD)
    hk = a['mla_k_head_norm_g'][l][None]
    f['hk_nope'] = _pad_lanes(hk[:, :MLA_NOPE], 0, HEAD_PAD)
    f['hk_rope'] = _pad_lanes(hk[:, MLA_NOPE:], ROPE_LANE0, HEAD_PAD)
    f['wo_a'] = a['mla_w_o'][l].astype(BF16)
    f['conv_w'] = a['gdn_conv_w'][l]
    f['alog'] = _pad_lanes(a['gdn_a_log'][l][None], 0, 128)
    f['dt'] = _pad_lanes(a['gdn_dt_bias'][l][None], 0, 128)
    f['gdn_g'] = a['gdn_norm_g'][l][None]
    f['wo_b'] = a['gdn_w_o'][l].astype(BF16)
    f['hgrn_g'] = a['hgrn_norm_g'][l][None]
    f['wo_c'] = a['hgrn_w_o'][l].astype(BF16)
    f['w_out'] = a['w_out'][l].astype(BF16)
    f['ffn_g'] = a['ffn_norm_g'][l][None]
    if l % 2 == 0:
        wg, wu, wd = a['dense_w_gate'][l // 2], a['dense_w_up'][l // 2], a['dense_w_down'][l // 2]
        ff = wg.shape[1]
        half = ff // 2
        f['ffn'] = (jnp.moveaxis(wg.reshape(D_MODEL, 2, half), 1, 0).astype(BF16),
                    jnp.moveaxis(wu.reshape(D_MODEL, 2, half), 1, 0).astype(BF16),
                    wd.reshape(2, half, D_MODEL).astype(BF16))
        f['router'] = None
    else:
        f['ffn'] = (a['moe_w_gate'][l // 2].astype(BF16), a['moe_w_up'][l // 2].astype(BF16),
                    a['moe_w_down'][l // 2].astype(BF16))
        f['router'] = (_pad_lanes(a['moe_w_router'][l // 2], 0, 128),
                       _pad_lanes(a['moe_b_router'][l // 2][None], 0, 128))
    return f


def _trunk_layer(x, b, t, f, lb, tabs, past, layer, depth, ckv_buf):
    n = b * t
    p, p_gate = _in_proj(x, f['mixer_g'], f['w_in'])
    p3 = p.reshape(b, t, P_MAIN)

    fresh = past['ckv'] is None
    q, ckv_buf, kr = _mla_pre(p, b, t, f['gq'], f['gkv'], f['wq'], f['hq'], f['hk_rope'], tabs, fresh,
                              layer, depth, ckv_buf)
    if fresh:
        k_all, vt_all = _kv_up(ckv_buf, layer, kr, b, t, f['wk'], f['wv_pad'], f['hk_nope'])
        o_a = _attention_t(q, k_all.reshape(b, t, QK_W), vt_all)
    else:
        past_len = past['ckv'].shape[2]
        t_pad = -(-t // 128) * 128
        pad_rows = lambda a: jnp.pad(a.reshape(b, t, -1), ((0, 0), (0, t_pad - t), (0, 0)))
        kv = _kv_hist(past['ckv'], past['kr'], past_len + t_pad, 0, f['wkt'], f['wv'], f['hk_nope'],
                      layer=past['layer'])
        kt_all, v_all = _kv_hist(pad_rows(ckv_buf[layer]), pad_rows(kr), past_len + t_pad, past_len,
                                 f['wkt'], f['wv'], f['hk_nope'], into=kv)
        o_a = _attention_hist(q.reshape(b, t, QK_W), kt_all, v_all, past_len, past_len + t)

    conv8 = jnp.pad(past['conv'].reshape(b, CONV_W - 1, 3, GDN_QK).transpose(0, 2, 1, 3),
                    ((0, 0), (0, 0), (8 - (CONV_W - 1), 0), (0, 0)))
    gdn_parts = _gdn_prep(p3, f['conv_w'], conv8, f['alog'], f['dt'])
    gdn_conv = p3[:, t - (CONV_W - 1):, COL_GDN:COL_GDN + 3 * GDN_QK]

    hgrn_parts = _hgrn_prep(p3, lb)
    o_b, gdn_s, o_c, hgrn_st = _rec_scan(gdn_parts, hgrn_parts, p3, past['gdn'],
                                         jnp.swapaxes(past['hgrn'], -1, -2))
    hgrn_s = jnp.swapaxes(hgrn_st, -1, -2)

    x = _merge(x, o_a.reshape(n, -1), o_b.reshape(n, -1), o_c.reshape(n, -1), p, p_gate, f['gdn_g'],
               f['hgrn_g'], f['wo_a'], f['wo_b'], f['wo_c'], f['w_out'])

    wg, wu, wd = f['ffn']
    if f['router'] is None:
        x = _ffn(x, f['ffn_g'], wg, wu, wd)
    else:
        t_moe = _row_tile(n, MOE_TILE)
        cw, cwt, cnt = _router(x, f['ffn_g'], *f['router'], t_moe)
        counts = cnt[:, 0, :N_EXPERTS].astype(jnp.int32).reshape(-1)
        x = _moe(x, f['ffn_g'], cw, cwt, counts, wg, wu, wd, t_moe)

    new_kr = kr.reshape(b, t, HEAD_PAD)[:, :, ROPE_LANE0:ROPE_LANE0 + MLA_ROPE]
    return x, ckv_buf, (new_kr, gdn_s, gdn_conv, hgrn_s)


def kernel(x_prompt, x_sample, cache_mla_ckv, cache_mla_krope, state_gdn, state_gdn_conv, state_hgrn,
           mixer_norm_g, w_in, mla_q_norm_g, mla_w_q_up, mla_kv_norm_g, mla_w_kv_up,
           mla_q_head_norm_g, mla_k_head_norm_g, mla_w_o,
           gdn_conv_w, gdn_a_log, gdn_dt_bias, gdn_norm_g, gdn_w_o,
           hgrn_lb_logits, hgrn_norm_g, hgrn_w_o, w_out, ffn_norm_g,
           dense_w_gate, dense_w_up, dense_w_down,
           moe_w_router, moe_b_router, moe_w_gate, moe_w_up, moe_w_down):
    a = dict(mixer_norm_g=mixer_norm_g, w_in=w_in, mla_q_norm_g=mla_q_norm_g, mla_w_q_up=mla_w_q_up,
             mla_kv_norm_g=mla_kv_norm_g, mla_w_kv_up=mla_w_kv_up, mla_q_head_norm_g=mla_q_head_norm_g,
             mla_k_head_norm_g=mla_k_head_norm_g, mla_w_o=mla_w_o, gdn_conv_w=gdn_conv_w,
             gdn_a_log=gdn_a_log, gdn_dt_bias=gdn_dt_bias, gdn_norm_g=gdn_norm_g, gdn_w_o=gdn_w_o,
             hgrn_norm_g=hgrn_norm_g, hgrn_w_o=hgrn_w_o, w_out=w_out, ffn_norm_g=ffn_norm_g,
             dense_w_gate=dense_w_gate, dense_w_up=dense_w_up, dense_w_down=dense_w_down,
             moe_w_router=moe_w_router, moe_b_router=moe_b_router, moe_w_gate=moe_w_gate,
             moe_w_up=moe_w_up, moe_w_down=moe_w_down)
    depth = w_in.shape[0]
    lb_soft = jax.nn.softmax(hgrn_lb_logits.astype(F32), axis=0)
    hgrn_lb = jnp.cumsum(lb_soft, axis=0) - lb_soft[0]

    b_p, t_p = x_prompt.shape[:2]
    b_s, t_s = x_sample.shape[:2]
    past_len = cache_mla_ckv.shape[2]
    tab_all = _rope_tables(max(t_p, past_len + t_s))
    tabs_p = tuple(tb[:t_p] for tb in tab_all)
    tabs_s = tuple(tb[past_len:past_len + t_s] for tb in tab_all)

    xp = x_prompt.reshape(b_p * t_p, D_MODEL)
    xs = x_sample.reshape(b_s * t_s, D_MODEL)
    past_p = dict(ckv=None, kr=None,
                  gdn=jnp.zeros((b_p, GDN_HEADS, GDN_DK, GDN_DV), F32),
                  conv=jnp.zeros((b_p, CONV_W - 1, 3 * GDN_QK), F32),
                  hgrn=jnp.zeros((b_p, HGRN_HEADS, HGRN_DK, HGRN_DV), F32))
    st_p, st_s = [], []
    ckv_p = ckv_s = None
    for l in range(depth):
        f = _layer_weights(l, a)
        lb = hgrn_lb[l][None]
        past_s = dict(ckv=cache_mla_ckv, kr=cache_mla_krope, layer=l, gdn=state_gdn[l],
                      conv=state_gdn_conv[l], hgrn=state_hgrn[l])
        xp, ckv_p, sp = _trunk_layer(xp, b_p, t_p, f, lb, tabs_p, past_p, l, depth, ckv_p)
        xs, ckv_s, ss = _trunk_layer(xs, b_s, t_s, f, lb, tabs_s, past_s, l, depth, ckv_s)
        st_p.append(sp)
        st_s.append(ss)

    def stack(lst, i):
        return jnp.stack([s[i] for s in lst], axis=0)

    return (xp.reshape(b_p, t_p, D_MODEL), xs.reshape(b_s, t_s, D_MODEL),
            ckv_p.reshape(depth, b_p, t_p, MLA_KV_LORA),
            stack(st_p, 0), stack(st_p, 1), stack(st_p, 2), stack(st_p, 3),
            ckv_s.reshape(depth, b_s, t_s, MLA_KV_LORA),
            stack(st_s, 0), stack(st_s, 1), stack(st_s, 2), stack(st_s, 3))
```

```python
import functools

import jax
import jax.numpy as jnp
import numpy as np
from jax import lax
from jax.experimental import pallas as pl
from jax.experimental.pallas import tpu as pltpu

F32 = jnp.float32
BF16 = jnp.bfloat16

D_MODEL = 1024
CHUNK = 64
NORM_EPS = 1e-6

MLA_HEADS = 8
MLA_NOPE = 64
MLA_ROPE = 32
MLA_V = 64
MLA_Q_LORA = 384
MLA_KV_LORA = 256
MLA_QK_HEAD = MLA_NOPE + MLA_ROPE
MLA_VW = MLA_HEADS * MLA_V
ROPE_THETA = 10000.0
LOG2_E = 1.4426950408889634
HEAD_PAD = 128
QK_W = MLA_HEADS * HEAD_PAD
VT_ROWS = MLA_V + 16

GDN_HEADS = 4
GDN_DK = 128
GDN_DV = 128
GDN_QK = GDN_HEADS * GDN_DK
GDN_VW = GDN_HEADS * GDN_DV
CONV_W = 4
HGRN_PREP_ROWS = 512
GDN_PREP_ROWS = 1024

HGRN_HEADS = 4
HGRN_DK = 128
HGRN_DV = 128
HGRN_KW = HGRN_HEADS * HGRN_DK

N_BRANCH = 3
SPLIT_SIZES = (MLA_Q_LORA, MLA_KV_LORA, MLA_ROPE,
               GDN_QK, GDN_QK, GDN_VW, GDN_VW, GDN_HEADS, GDN_HEADS,
               HGRN_KW, HGRN_KW, HGRN_KW, HGRN_KW,
               N_BRANCH * D_MODEL)

N_EXPERTS = 8
FF_EXPERT = 1408
MOE_TILE = 1024
TOP_K = 2
MOE_BLOCK_ROWS = tuple(MOE_TILE * TOP_K // N_EXPERTS + d for d in (-32, -16, 0, 16, 32, 48))

P_COLS = 8192
P_MAIN = P_COLS - N_BRANCH * D_MODEL
P_TN = 512
COL_CKV = MLA_Q_LORA
COL_KR = MLA_Q_LORA + MLA_KV_LORA
COL_GAB = COL_KR + 128
COL_GDN = 1024
ROPE_LANE0 = MLA_NOPE

VMEM_LIMIT = 56 * 1024 * 1024


def _cparams(sem):
    return pltpu.CompilerParams(dimension_semantics=sem, vmem_limit_bytes=VMEM_LIMIT)


def _row_tile(n, cap):
    for t in (2048, 1024, 512, 256, 128, 64, 32, 16, 8):
        if t <= cap and n % t == 0:
            return t
    raise ValueError(f"no row tile for {n}")


def _sigmoid(x):
    return 1.0 / (1.0 + jnp.exp(-x))


def _silu(x):
    return x * (0.5 * jnp.tanh(0.5 * x) + 0.5)


def _rms(x, g):
    ms = jnp.mean(x * x, axis=-1, keepdims=True)
    return x * lax.rsqrt(ms + NORM_EPS) * g


def _in_proj_kernel(x_ref, g_ref, w_ref, o_ref, gate_ref, h_ref):
    j = pl.program_id(1)

    @pl.when(j == 0)
    def _():
        h_ref[...] = _rms(x_ref[...], g_ref[...]).astype(BF16)

    @pl.when(j < P_MAIN // P_TN)
    def _():
        o_ref[...] = jnp.dot(h_ref[...], w_ref[...], preferred_element_type=F32)

    @pl.when(j >= P_MAIN // P_TN)
    def _():
        gate_ref[...] = jnp.dot(h_ref[...], w_ref[...], preferred_element_type=F32).astype(BF16)


def _in_proj(x, g, w):
    n = x.shape[0]
    tm = _row_tile(n, 2048)
    n_main = P_MAIN // P_TN
    return pl.pallas_call(
        _in_proj_kernel,
        grid=(n // tm, P_COLS // P_TN),
        in_specs=[pl.BlockSpec((tm, D_MODEL), lambda i, j: (i, 0)),
                  pl.BlockSpec((1, D_MODEL), lambda i, j: (0, 0)),
                  pl.BlockSpec((D_MODEL, P_TN), lambda i, j: (0, j))],
        out_specs=[pl.BlockSpec((tm, P_TN), lambda i, j: (i, jnp.minimum(j, n_main - 1))),
                   pl.BlockSpec((tm, P_TN), lambda i, j: (i, jnp.maximum(j - n_main, 0)))],
        out_shape=[jax.ShapeDtypeStruct((n, P_MAIN), F32),
                   jax.ShapeDtypeStruct((n, P_COLS - P_MAIN), BF16)],
        scratch_shapes=[pltpu.VMEM((tm, D_MODEL), BF16)],
        compiler_params=_cparams(("parallel", "arbitrary")),
        name="in_proj",
    )(x, g, w)


def _rope(x, c, s1, s2):
    return x * c + pltpu.roll(x, HEAD_PAD - 16, 1) * s1 + pltpu.roll(x, 16, 1) * s2


def _mla_pre_kernel(p_ref, gq_ref, gkv_ref, wq_ref, hq_ref, hk_ref, c_ref, s1_ref, s2_ref, *rest,
                    scale, transposed):
    q_ref, ckv_ref, kr_ref = rest[-3:]
    if transposed:
        hqt_ref, ct_ref, s1t_ref, s2t_ref = rest[:4]
    c, s1, s2 = c_ref[...], s1_ref[...], s2_ref[...]
    ckv_ref[...] = _rms(p_ref[:, COL_CKV:COL_KR], gkv_ref[...])

    kr = p_ref[:, COL_KR:COL_KR + HEAD_PAD]
    kr_ms = jnp.sum(kr * kr, axis=-1, keepdims=True) * (1.0 / MLA_ROPE)
    kr_ref[...] = _rope(kr * lax.rsqrt(kr_ms + NORM_EPS) * hk_ref[...], c, s1, s2)

    cq = _rms(p_ref[:, 0:MLA_Q_LORA], gq_ref[...]).astype(BF16)
    q = jnp.dot(cq, wq_ref[...], preferred_element_type=F32)
    if transposed:
        row = lax.broadcasted_iota(jnp.int32, (HEAD_PAD, 1), 0)
        hqt, ct, s1t, s2t = hqt_ref[...], ct_ref[...], s1t_ref[...], s2t_ref[...]
        for h in range(MLA_HEADS):
            qt = q[:, h * HEAD_PAD:(h + 1) * HEAD_PAD].T
            sq = qt * qt
            ms_n = jnp.sum(sq[0:MLA_NOPE], axis=0, keepdims=True) * (1.0 / MLA_NOPE)
            ms_r = jnp.sum(sq[MLA_NOPE:MLA_QK_HEAD], axis=0, keepdims=True) * (1.0 / MLA_ROPE)
            inv = jnp.where(row < MLA_NOPE, lax.rsqrt(ms_n + NORM_EPS), lax.rsqrt(ms_r + NORM_EPS))
            x = qt * inv * hqt
            x = x * ct + pltpu.roll(x, HEAD_PAD - 16, 0) * s1t + pltpu.roll(x, 16, 0) * s2t
            q_ref[0, h] = x.astype(BF16)
    else:
        lane = lax.broadcasted_iota(jnp.int32, (1, HEAD_PAD), 1)
        is_nope = lane < MLA_NOPE
        hq = hq_ref[...]
        for h in range(MLA_HEADS):
            qh = q[:, h * HEAD_PAD:(h + 1) * HEAD_PAD]
            sq = qh * qh
            ms_n = jnp.sum(jnp.where(is_nope, sq, 0.0), axis=-1, keepdims=True) * (1.0 / MLA_NOPE)
            ms_r = jnp.sum(jnp.where(is_nope, 0.0, sq), axis=-1, keepdims=True) * (1.0 / MLA_ROPE)
            inv = jnp.where(is_nope, lax.rsqrt(ms_n + NORM_EPS), lax.rsqrt(ms_r + NORM_EPS))
            qh = _rope(qh * inv * hq, c, s1, s2) * scale
            q_ref[:, h * HEAD_PAD:(h + 1) * HEAD_PAD] = qh.astype(BF16)


def _mla_pre(p, b, t_seq, gq, gkv, wq, hq, hk, tabs, transposed, layer, depth, ckv_buf):
    n = p.shape[0]
    tm = _row_tile(n, 512)
    c, s1, s2 = tabs
    if tm > t_seq:
        c, s1, s2 = (jnp.tile(t, (tm // t_seq, 1)) for t in (c, s1, s2))
    n_tab = c.shape[0] // tm
    tab_spec = pl.BlockSpec((tm, HEAD_PAD), lambda i: (i % n_tab, 0))
    vec = lambda w: pl.BlockSpec((1, w), lambda i: (0, 0))
    scale = MLA_QK_HEAD ** -0.5
    extra, extra_specs = (), []
    if transposed:
        scale *= LOG2_E
        q_spec = pl.BlockSpec((1, MLA_HEADS, HEAD_PAD, tm), lambda i: (i // n_tab, 0, 0, i % n_tab))
        q_shape = jax.ShapeDtypeStruct((b, MLA_HEADS, HEAD_PAD, t_seq), BF16)
        hqt = jnp.broadcast_to((hq[0] * scale)[:, None], (HEAD_PAD, tm))
        extra = (hqt, c.T, s1.T, s2.T)
        tab_t_spec = pl.BlockSpec((HEAD_PAD, tm), lambda i: (0, i % n_tab))
        extra_specs = [pl.BlockSpec((HEAD_PAD, tm), lambda i: (0, 0)), tab_t_spec, tab_t_spec, tab_t_spec]
    else:
        q_spec = pl.BlockSpec((tm, QK_W), lambda i: (i, 0))
        q_shape = jax.ShapeDtypeStruct((n, QK_W), BF16)
    aliases = {}
    if ckv_buf is not None:
        aliases = {9 + len(extra): 1}
        extra = (*extra, ckv_buf)
        extra_specs = [*extra_specs, pl.BlockSpec(memory_space=pl.ANY)]
    return pl.pallas_call(
        functools.partial(_mla_pre_kernel, scale=scale, transposed=transposed),
        grid=(n // tm,),
        in_specs=[pl.BlockSpec((tm, 1024), lambda i: (i, 0)),
                  vec(MLA_Q_LORA), vec(MLA_KV_LORA),
                  pl.BlockSpec((MLA_Q_LORA, QK_W), lambda i: (0, 0)),
                  vec(HEAD_PAD), vec(HEAD_PAD), tab_spec, tab_spec, tab_spec, *extra_specs],
        out_specs=[q_spec,
                   pl.BlockSpec((None, tm, MLA_KV_LORA), lambda i: (layer, i, 0)),
                   pl.BlockSpec((tm, HEAD_PAD), lambda i: (i, 0))],
        out_shape=[q_shape,
                   jax.ShapeDtypeStruct((depth, n, MLA_KV_LORA), F32),
                   jax.ShapeDtypeStruct((n, HEAD_PAD), F32)],
        input_output_aliases=aliases,
        compiler_params=_cparams(("parallel",)),
        name="mla_pre",
    )(p, gq, gkv, wq, hq, hk, c, s1, s2, *extra)


def _kv_up_kernel(ckv_ref, kr_ref, wk_ref, wv_ref, hk_ref, k_ref, v_ref):
    c = ckv_ref[...].astype(BF16)
    k = jnp.dot(c, wk_ref[...], preferred_element_type=F32)
    kr = kr_ref[...]
    hk = hk_ref[...]
    for h in range(MLA_HEADS):
        kh = k[:, h * HEAD_PAD:(h + 1) * HEAD_PAD]
        ms = jnp.sum(kh * kh, axis=-1, keepdims=True) * (1.0 / MLA_NOPE)
        k_ref[:, h * HEAD_PAD:(h + 1) * HEAD_PAD] = (kh * lax.rsqrt(ms + NORM_EPS) * hk + kr).astype(BF16)
    v = jnp.dot(c, wv_ref[...], preferred_element_type=F32)
    lane = lax.broadcasted_iota(jnp.int32, (1, HEAD_PAD), 1)
    for h in range(MLA_HEADS):
        vh = jnp.where(lane < MLA_V, v[:, h * HEAD_PAD:(h + 1) * HEAD_PAD], 1.0)
        v_ref[0, h] = vh.T[0:VT_ROWS].astype(BF16)


def _kv_up(ckv_buf, layer, kr, b, s_len, wk, wv, hk):
    n = ckv_buf.shape[1]
    tm = _row_tile(n, 512)
    n_t = s_len // tm
    return pl.pallas_call(
        _kv_up_kernel,
        grid=(n // tm,),
        in_specs=[pl.BlockSpec((None, tm, MLA_KV_LORA), lambda i: (layer, i, 0)),
                  pl.BlockSpec((tm, HEAD_PAD), lambda i: (i, 0)),
                  pl.BlockSpec((MLA_KV_LORA, QK_W), lambda i: (0, 0)),
                  pl.BlockSpec((MLA_KV_LORA, QK_W), lambda i: (0, 0)),
                  pl.BlockSpec((1, HEAD_PAD), lambda i: (0, 0))],
        out_specs=[pl.BlockSpec((tm, QK_W), lambda i: (i, 0)),
                   pl.BlockSpec((1, MLA_HEADS, VT_ROWS, tm), lambda i: (i // n_t, 0, 0, i % n_t))],
        out_shape=[jax.ShapeDtypeStruct((n, QK_W), BF16),
                   jax.ShapeDtypeStruct((b, MLA_HEADS, VT_ROWS, s_len), BF16)],
        compiler_params=_cparams(("parallel",)),
        name="kv_up",
    )(ckv_buf, kr, wk, wv, hk)


def _kv_hist_kernel(ckv_ref, kr_ref, wkt_ref, wv_ref, hkt_ref, *rest):
    kt_ref, v_ref = rest[-2:]
    c = ckv_ref[0].astype(BF16)
    kt = _dot_nt(wkt_ref[...], c)
    kr = kr_ref[0]
    if kr.shape[1] == MLA_ROPE:
        rows = kr.shape[0]
        kr = jnp.concatenate([jnp.zeros((rows, ROPE_LANE0), F32), kr,
                              jnp.zeros((rows, HEAD_PAD - ROPE_LANE0 - MLA_ROPE), F32)], axis=1)
    krt = kr.T
    hkt = hkt_ref[...]
    for h in range(MLA_HEADS):
        kh = kt[h * HEAD_PAD:(h + 1) * HEAD_PAD]
        ms = jnp.sum(kh * kh, axis=0, keepdims=True) * (1.0 / MLA_NOPE)
        kt_ref[0, h * HEAD_PAD:(h + 1) * HEAD_PAD, :] = (kh * lax.rsqrt(ms + NORM_EPS) * hkt + krt).astype(BF16)
    v_ref[0] = jnp.dot(c, wv_ref[...], preferred_element_type=F32).astype(BF16)


def _kv_hist(ckv, kr, s_total, row0, wkt, wv, hk, into=None, layer=None):
    b, s_in, _ = ckv.shape[-3:]
    tm = _row_tile(s_in, 2048)
    blk0 = row0 // tm
    hkt = jnp.broadcast_to(hk[0][:, None], (HEAD_PAD, tm))
    if layer is None:
        rows_spec = lambda w: pl.BlockSpec((1, tm, w), lambda i, j: (i, j, 0))
    else:
        rows_spec = lambda w: pl.BlockSpec((None, 1, tm, w), lambda i, j: (layer, i, j, 0))
    in_specs = [rows_spec(MLA_KV_LORA), rows_spec(kr.shape[-1]),
                pl.BlockSpec((QK_W, MLA_KV_LORA), lambda i, j: (0, 0)),
                pl.BlockSpec((MLA_KV_LORA, MLA_VW), lambda i, j: (0, 0)),
                pl.BlockSpec((HEAD_PAD, tm), lambda i, j: (0, 0))]
    args = [ckv, kr, wkt, wv, hkt]
    aliases = {}
    if into is not None:
        in_specs += [pl.BlockSpec(memory_space=pl.ANY), pl.BlockSpec(memory_space=pl.ANY)]
        args += list(into)
        aliases = {5: 0, 6: 1}
    return pl.pallas_call(
        _kv_hist_kernel,
        grid=(b, s_in // tm),
        in_specs=in_specs,
        out_specs=[pl.BlockSpec((1, QK_W, tm), lambda i, j: (i, 0, blk0 + j)),
                   pl.BlockSpec((1, tm, MLA_VW), lambda i, j: (i, blk0 + j, 0))],
        out_shape=[jax.ShapeDtypeStruct((b, QK_W, s_total), BF16),
                   jax.ShapeDtypeStruct((b, s_total, MLA_VW), BF16)],
        input_output_aliases=aliases,
        compiler_params=_cparams(("parallel", "parallel")),
        name="kv_hist",
    )(*args)


def _attn_hist_kernel(q_ref, kt_ref, v_ref, o_ref, *, q_pos0, s_valid):
    t, s_len = q_ref.shape[1], kt_ref.shape[2]
    k_pos = lax.broadcasted_iota(jnp.int32, (t, s_len), 1)
    q_pos = lax.broadcasted_iota(jnp.int32, (t, s_len), 0) + q_pos0
    allowed = (k_pos < s_valid) & (k_pos // CHUNK <= q_pos // CHUNK)
    scores = [jnp.dot(q_ref[0, :, h * HEAD_PAD:(h + 1) * HEAD_PAD], kt_ref[0, h * HEAD_PAD:(h + 1) * HEAD_PAD, :],
                      preferred_element_type=F32) for h in range(MLA_HEADS)]
    for h in range(MLA_HEADS):
        s = jnp.where(allowed, scores[h], -jnp.inf)
        p = jnp.exp(s - jnp.max(s, axis=-1, keepdims=True))
        pv = jnp.dot(p.astype(BF16), v_ref[0, :, h * MLA_V:(h + 1) * MLA_V], preferred_element_type=F32)
        o_ref[0, :, h * MLA_V:(h + 1) * MLA_V] = pv / jnp.sum(p, axis=-1, keepdims=True)


def _attention_hist(q, kt, v, q_pos0, s_valid):
    b, t, _ = q.shape
    s = kt.shape[2]
    return pl.pallas_call(
        functools.partial(_attn_hist_kernel, q_pos0=q_pos0, s_valid=s_valid),
        grid=(b,),
        in_specs=[pl.BlockSpec((1, t, QK_W), lambda i: (i, 0, 0)),
                  pl.BlockSpec((1, QK_W, s), lambda i: (i, 0, 0)),
                  pl.BlockSpec((1, s, MLA_VW), lambda i: (i, 0, 0))],
        out_specs=pl.BlockSpec((1, t, MLA_VW), lambda i: (i, 0, 0)),
        out_shape=jax.ShapeDtypeStruct((b, t, MLA_VW), F32),
        compiler_params=_cparams(("parallel",)),
        name="mla_attn_hist",
    )(q, kt, v)


def _attn_t_kernel(qt_ref, k_ref, vt_ref, o_ref, m_ref, acc_ref, *, tq, tk):
    m_ref[...] = jnp.full(m_ref.shape, -jnp.inf, F32)
    acc_ref[...] = jnp.zeros(acc_ref.shape, F32)

    def blocks(starts, masked):
        if masked:
            ck = lax.broadcasted_iota(jnp.int32, (tk, tq), 0) // CHUNK
            cq = lax.broadcasted_iota(jnp.int32, (tk, tq), 1) // CHUNK
            allowed = ck <= cq

        def scores(start, h):
            kh = k_ref[0, pl.ds(start, tk), h * HEAD_PAD:(h + 1) * HEAD_PAD]
            return jnp.dot(kh, qt_ref[0, h], preferred_element_type=F32)

        def update(start, h, s):
            if masked:
                s = jnp.where(allowed, s, -jnp.inf)
            m_prev = m_ref[h]
            m_new = jnp.maximum(m_prev, jnp.max(s, axis=0, keepdims=True))
            alpha = jnp.exp2(m_prev - m_new)
            p = jnp.exp2(s - m_new).astype(BF16)
            pv = jnp.dot(vt_ref[0, h, :, pl.ds(start, tk)], p, preferred_element_type=F32)
            acc_ref[h] = alpha * acc_ref[h] + pv
            m_ref[h] = m_new

        items = [(start, h) for start in starts for h in range(MLA_HEADS)]
        ahead = 5
        pending = [scores(*it) for it in items[:ahead]]
        for n, it in enumerate(items):
            s = pending.pop(0)
            if n + ahead < len(items):
                pending.append(scores(*items[n + ahead]))
            update(*it, s)

    qi = pl.program_id(1)

    def body(j, carry):
        first = pl.multiple_of(2 * j * tk, tk)
        blocks([first, pl.multiple_of(first + tk, tk)], False)
        return carry

    lax.fori_loop(0, qi // 2, body, 0)

    @pl.when(qi % 2 == 1)
    def _():
        blocks([pl.multiple_of((qi - 1) * tk, tk)], False)

    blocks([pl.multiple_of(qi * tk, tk)], True)

    ot = jnp.concatenate([acc_ref[h, 0:MLA_V] / acc_ref[h, MLA_V:MLA_V + 1] for h in range(MLA_HEADS)], axis=0)
    o_ref[0] = ot.T


def _attention_t(qt, k, vt):
    b, _, _, t = qt.shape
    s = k.shape[1]
    tq = tk = min(256, t)
    return pl.pallas_call(
        functools.partial(_attn_t_kernel, tq=tq, tk=tk),
        grid=(b, t // tq),
        in_specs=[pl.BlockSpec((1, MLA_HEADS, HEAD_PAD, tq), lambda i, j: (i, 0, 0, j)),
                  pl.BlockSpec((1, s, QK_W), lambda i, j: (i, 0, 0)),
                  pl.BlockSpec((1, MLA_HEADS, VT_ROWS, s), lambda i, j: (i, 0, 0, 0))],
        out_specs=pl.BlockSpec((1, tq, MLA_VW), lambda i, j: (i, j, 0)),
        out_shape=jax.ShapeDtypeStruct((b, t, MLA_VW), F32),
        scratch_shapes=[pltpu.VMEM((MLA_HEADS, 1, tq), F32),
                        pltpu.VMEM((MLA_HEADS, VT_ROWS, tq), F32)],
        compiler_params=_cparams(("parallel", "arbitrary")),
        name="mla_attn_t",
    )(qt, k, vt)


def _row_iota(shape):
    return lax.broadcasted_iota(jnp.int32, shape, 0)


def _upper_half_masks(shape):
    row = _row_iota(shape)
    masks = []
    m = 1
    while m < CHUNK:
        masks.append((row // m) % 2 == 1)
        m *= 2
    return masks


def _segment_scans(g, uppers):
    rows = g.shape[0]
    pre, tot = g, g
    out = [(pre, tot)]
    m = 1
    while m < CHUNK:
        upper = uppers[len(out) - 1]
        from_lower = pltpu.roll(tot, m, 0)
        from_upper = pltpu.roll(tot, rows - m, 0)
        pre = pre + jnp.where(upper, from_lower, 0.0)
        tot = tot + jnp.where(upper, from_lower, from_upper)
        out.append((pre, tot))
        m *= 2
    return out


def _dot_nt(a, b):
    return lax.dot_general(a, b, (((1,), (1,)), ((), ())), preferred_element_type=F32)


def _dot_tn(a, b):
    return lax.dot_general(a, b, (((0,), (0,)), ((), ())), preferred_element_type=F32)


def _gdn_prep_kernel(q_ref, k_ref, v_ref, ab_ref, cw_ref, cs_ref, alog_ref, dt_ref,
                     u_ref, w_ref, qd_ref, kd_ref, qk_ref, gl_ref, carry_ref, *, nb, r):
    @pl.when(pl.program_id(1) == 0)
    def _():
        carry_ref[...] = cs_ref[...]

    n = nb * r
    nc = n // CHUNK

    def conv(x_ref, j):
        w = cw_ref[:, j * GDN_QK:(j + 1) * GDN_QK]
        outs = []
        for b in range(nb):
            x = x_ref[b]
            xp = jnp.concatenate([carry_ref[b, j], x], axis=0)
            y = x * w[3:4]
            for d in range(1, CONV_W):
                y = y + xp[8 - d:8 - d + r] * w[3 - d:4 - d]
            carry_ref[b, j] = x[r - 8:]
            outs.append(_silu(y))
        return outs[0] if nb == 1 else jnp.concatenate(outs, axis=0)

    q_all, k_all, v_all = conv(q_ref, 0), conv(k_ref, 1), conv(v_ref, 2)

    ab = ab_ref[...].reshape(n, 128)
    x = ab + dt_ref[...]
    softplus = jnp.maximum(x, 0.0) + jnp.log(1.0 + jnp.exp(-jnp.abs(x)))
    g_blk = (-LOG2_E) * jnp.exp(alog_ref[...]) * softplus
    gam_blk = _segment_scans(g_blk, _upper_half_masks(g_blk.shape))[-1][0]
    gam_t = gam_blk.T
    gam3_blk = gam_blk.reshape(nc, CHUNK, 128)
    beta3_blk = _sigmoid(ab).reshape(nc, CHUNK, 128)

    row = lax.broadcasted_iota(jnp.int32, (1, CHUNK, CHUNK), 1)
    col = lax.broadcasted_iota(jnp.int32, (1, CHUNK, CHUNK), 2)
    eye = (row == col).astype(F32)

    def bmm(a, b):
        return jnp.einsum('cij,cjk->cik', a.astype(BF16), b.astype(BF16), preferred_element_type=F32)

    def bmm_nt(a, b):
        return jnp.einsum('cid,cjd->cij', a.astype(BF16), b.astype(BF16), preferred_element_type=F32)

    for h in range(GDN_HEADS):
        sl = slice(h * GDN_DK, (h + 1) * GDN_DK)
        q, k, v = q_all[:, sl], k_all[:, sl], v_all[:, sl]
        q = q * lax.rsqrt(jnp.sum(q * q, axis=-1, keepdims=True) + NORM_EPS) * (GDN_DK ** -0.5)
        k = k * lax.rsqrt(jnp.sum(k * k, axis=-1, keepdims=True) + NORM_EPS)
        q, k, v = (a.reshape(nc, CHUNK, GDN_DK) for a in (q, k, v))
        gam = gam3_blk[:, :, h:h + 1]
        beta = beta3_blk[:, :, GDN_HEADS + h:GDN_HEADS + h + 1]
        gam_row = jnp.stack([gam_t[h:h + 1, c * CHUNK:(c + 1) * CHUNK] for c in range(nc)], axis=0)
        decay = jnp.where(row >= col, jnp.exp2(jnp.minimum(gam - gam_row, 0.0)), 0.0)
        a = jnp.where(row > col, beta * bmm_nt(k, k) * decay, 0.0)
        t_inv = eye - a
        pw = a
        m = 1
        while 2 * m < CHUNK:
            pw = bmm(pw, pw)
            t_inv = t_inv + bmm(t_inv, pw)
            m *= 2
        e_gam = jnp.exp2(gam)
        gam_last = gam[:, CHUNK - 1:CHUNK, :]
        u_ref[:, h] = bmm(t_inv, v * beta).reshape(nb, r, GDN_DV)
        w_ref[:, h] = bmm(t_inv, k * (beta * e_gam)).astype(BF16).reshape(nb, r, GDN_DK)
        qd_ref[:, h] = (q * e_gam).astype(BF16).reshape(nb, r, GDN_DK)
        kd_ref[:, h] = (k * jnp.exp2(gam_last - gam)).astype(BF16).reshape(nb, r, GDN_DK)
        qk_ref[:, h] = (bmm_nt(q, k) * decay).astype(BF16).reshape(nb, r, CHUNK)
        gl_ref[:, h] = jnp.broadcast_to(jnp.exp2(gam_last), (nc, 1, 128)).reshape(nb, r // CHUNK, 1, 128)


def _gdn_scan_kernel(u_ref, w_ref, qd_ref, kd_ref, qk_ref, gl_ref, s0_ref, o_ref, sf_ref, s_ref, *, nb):
    c_idx = pl.program_id(1)

    @pl.when(c_idx == 0)
    def _():
        s_ref[...] = s0_ref[...]

    chains = [(b, h) for b in range(nb) for h in range(GDN_HEADS)]
    s_old = [s_ref[b, h] for b, h in chains]
    s_bf = [s.astype(BF16) for s in s_old]
    v_new = [u_ref[b, h] - jnp.dot(w_ref[b, h], sb, preferred_element_type=F32)
             for (b, h), sb in zip(chains, s_bf)]
    v_bf = [v.astype(BF16) for v in v_new]
    for (b, h), s, sb, vb in zip(chains, s_old, s_bf, v_bf):
        o = (jnp.dot(qd_ref[b, h], sb, preferred_element_type=F32)
             + jnp.dot(qk_ref[b, h], vb, preferred_element_type=F32))
        o_ref[b, :, h * GDN_DV:(h + 1) * GDN_DV] = o
        s_ref[b, h] = gl_ref[b, h, 0] * s + _dot_tn(kd_ref[b, h], vb)

    @pl.when(c_idx == pl.num_programs(1) - 1)
    def _():
        sf_ref[...] = s_ref[...]


def _gdn_prep(p3, conv_w, conv_state8, alog, dt):
    b, t, _ = p3.shape
    r = min(GDN_PREP_ROWS, t)
    nb = max(1, min(b, GDN_PREP_ROWS // r))
    nt = t // r
    nc = t // CHUNK
    blk = lambda j: pl.BlockSpec((nb, r, GDN_QK), lambda i, c: (i, c, COL_GDN // GDN_QK + j))
    vec = pl.BlockSpec((1, 128), lambda i, c: (0, 0))
    head_spec = lambda w: pl.BlockSpec((nb, GDN_HEADS, r, w), lambda i, c: (i, 0, c, 0))
    head_shape = lambda w, dt_: jax.ShapeDtypeStruct((b, GDN_HEADS, t, w), dt_)
    u, w, qd, kd, qk, gl = pl.pallas_call(
        functools.partial(_gdn_prep_kernel, nb=nb, r=r),
        grid=(b // nb, nt),
        in_specs=[blk(0), blk(1), blk(2),
                  pl.BlockSpec((nb, r, 128), lambda i, c: (i, c, COL_GAB // 128)),
                  pl.BlockSpec((CONV_W, 3 * GDN_QK), lambda i, c: (0, 0)),
                  pl.BlockSpec((nb, 3, 8, GDN_QK), lambda i, c: (i, 0, 0, 0)),
                  vec, vec],
        out_specs=[head_spec(GDN_DV), head_spec(GDN_DK), head_spec(GDN_DK), head_spec(GDN_DK),
                   head_spec(CHUNK),
                   pl.BlockSpec((nb, GDN_HEADS, r // CHUNK, 1, 128), lambda i, c: (i, 0, c, 0, 0))],
        out_shape=[head_shape(GDN_DV, F32), head_shape(GDN_DK, BF16), head_shape(GDN_DK, BF16),
                   head_shape(GDN_DK, BF16), head_shape(CHUNK, BF16),
                   jax.ShapeDtypeStruct((b, GDN_HEADS, nc, 1, 128), F32)],
        scratch_shapes=[pltpu.VMEM((nb, 3, 8, GDN_QK), F32)],
        compiler_params=_cparams(("parallel", "arbitrary")),
        name="gdn_prep",
    )(p3, p3, p3, p3, conv_w, conv_state8, alog, dt)
    return u, w, qd, kd, qk, gl


def _hgrn_prep_kernel(q_ref, f_ref, lb_ref, att_ref, qe_ref, ke_ref, dec_ref, *, nb, r):
    n = nb * r
    nc = n // CHUNK
    row = lax.broadcasted_iota(jnp.int32, (1, CHUNK, CHUNK), 1)
    col = lax.broadcasted_iota(jnp.int32, (1, CHUNK, CHUNK), 2)
    uppers = _upper_half_masks((n, HGRN_DK))
    halves = [1 << lvl for lvl in range(len(uppers))]
    pairs = [(row // (2 * m) == col // (2 * m)) & ((row // m) % 2 == 1) & ((col // m) % 2 == 0) for m in halves]

    def bmm_nt(a, b):
        a3, b3 = (x.astype(BF16).reshape(nc, CHUNK, HGRN_DK) for x in (a, b))
        return jnp.einsum('cid,cjd->cij', a3, b3, preferred_element_type=F32)

    for h in range(HGRN_HEADS):
        sl = slice(h * HGRN_DK, (h + 1) * HGRN_DK)
        lb = lb_ref[:, sl]
        f = lb + (1.0 - lb) * _sigmoid(f_ref[:, :, sl].reshape(n, HGRN_DK))
        q = _silu(q_ref[:, :, sl].reshape(n, HGRN_DK)) * (HGRN_DK ** -0.5)
        k = 1.0 - f
        scans = _segment_scans(jnp.log2(f), uppers)
        cb, c_tot = scans[-1]

        att = jnp.where(row == col, bmm_nt(q, k), 0.0)
        for lvl in range(len(halves)):
            pre_m, tot_m = scans[lvl]
            att = att + jnp.where(pairs[lvl], bmm_nt(q * jnp.exp2(pre_m), k * jnp.exp2(tot_m - pre_m)), 0.0)

        att_ref[:, h] = att.astype(BF16).reshape(nb, r, CHUNK)
        qe_ref[:, h] = (q * jnp.exp2(cb)).astype(BF16).reshape(nb, r, HGRN_DK)
        ke_ref[:, h] = (k * jnp.exp2(c_tot - cb)).astype(BF16).reshape(nb, r, HGRN_DK)
        dec_ref[:, h] = jnp.exp2(c_tot).reshape(nc, CHUNK, HGRN_DK)[:, 0:1, :].reshape(nb, r // CHUNK, 1, HGRN_DK)


def _hgrn_scan_kernel(att_ref, qe_ref, ke_ref, dec_ref, v_ref, s0_ref, o_ref, sf_ref, st_ref, *, nb):
    c_idx = pl.program_id(1)

    @pl.when(c_idx == 0)
    def _():
        st_ref[...] = s0_ref[...]

    chains = [(b, h) for b in range(nb) for h in range(HGRN_HEADS)]
    sls = [slice(h * HGRN_DV, (h + 1) * HGRN_DV) for h in range(HGRN_HEADS)]
    st_old = [st_ref[b, h] for b, h in chains]
    vs = [v_ref[b, :, sls[h]].astype(BF16) for b, h in chains]
    o_st = [_dot_nt(qe_ref[b, h], st.astype(BF16)) for (b, h), st in zip(chains, st_old)]
    for (b, h), st, v, o1 in zip(chains, st_old, vs, o_st):
        o_ref[b, :, sls[h]] = o1 + jnp.dot(att_ref[b, h], v, preferred_element_type=F32)
        st_ref[b, h] = st * dec_ref[b, h, 0] + _dot_tn(v, ke_ref[b, h])

    @pl.when(c_idx == pl.num_programs(1) - 1)
    def _():
        sf_ref[...] = st_ref[...]


def _hgrn_prep(p3, lb):
    b, t, _ = p3.shape
    r = min(HGRN_PREP_ROWS, t)
    nb = max(1, min(b, HGRN_PREP_ROWS // r))
    nc = t // CHUNK
    col0 = (COL_GDN + 4 * GDN_QK) // HGRN_KW
    blk = lambda j: pl.BlockSpec((nb, r, HGRN_KW), lambda i, c: (i, c, col0 + j))
    head_spec = lambda w: pl.BlockSpec((nb, HGRN_HEADS, r, w), lambda i, c: (i, 0, c, 0))
    head_shape = lambda w: jax.ShapeDtypeStruct((b, HGRN_HEADS, t, w), BF16)
    att, qe, ke, dec = pl.pallas_call(
        functools.partial(_hgrn_prep_kernel, nb=nb, r=r),
        grid=(b // nb, t // r),
        in_specs=[blk(0), blk(1), pl.BlockSpec((1, HGRN_KW), lambda i, c: (0, 0))],
        out_specs=[head_spec(CHUNK), head_spec(HGRN_DK), head_spec(HGRN_DK),
                   pl.BlockSpec((nb, HGRN_HEADS, r // CHUNK, 1, HGRN_DK), lambda i, c: (i, 0, c, 0, 0))],
        out_shape=[head_shape(CHUNK), head_shape(HGRN_DK), head_shape(HGRN_DK),
                   jax.ShapeDtypeStruct((b, HGRN_HEADS, nc, 1, HGRN_DK), F32)],
        compiler_params=_cparams(("parallel", "parallel")),
        name="hgrn_prep",
    )(p3, p3, lb)
    return att, qe, ke, dec


def _rec_scan_kernel(u_ref, w_ref, qd_ref, kd_ref, qk_ref, gl_ref, sg0_ref,
                     att_ref, qe_ref, ke_ref, dec_ref, v_ref, sh0_ref,
                     og_ref, sgf_ref, oh_ref, shf_ref, sg_ref, sh_ref, *, nb):
    _gdn_scan_kernel(u_ref, w_ref, qd_ref, kd_ref, qk_ref, gl_ref, sg0_ref, og_ref, sgf_ref, sg_ref, nb=nb)
    _hgrn_scan_kernel(att_ref, qe_ref, ke_ref, dec_ref, v_ref, sh0_ref, oh_ref, shf_ref, sh_ref, nb=nb)


def _rec_scan(gdn_parts, hgrn_parts, p3, s0_gdn, s0t_hgrn):
    b, t, _ = p3.shape
    nc = t // CHUNK
    sb = min(b, 8)
    col_v = (COL_GDN + 4 * GDN_QK) // HGRN_KW + 2
    chunk_spec = lambda w: pl.BlockSpec((sb, GDN_HEADS, CHUNK, w), lambda i, c: (i, 0, c, 0))
    per_chunk = pl.BlockSpec((sb, GDN_HEADS, 1, 1, 128), lambda i, c: (i, 0, c, 0, 0))
    state_spec = pl.BlockSpec((sb, GDN_HEADS, 128, 128), lambda i, c: (i, 0, 0, 0))
    out_spec = pl.BlockSpec((sb, CHUNK, GDN_VW), lambda i, c: (i, c, 0))
    state_shape = jax.ShapeDtypeStruct((b, GDN_HEADS, 128, 128), F32)
    return pl.pallas_call(
        functools.partial(_rec_scan_kernel, nb=sb),
        grid=(b // sb, nc),
        in_specs=[chunk_spec(GDN_DV), chunk_spec(GDN_DK), chunk_spec(GDN_DK), chunk_spec(GDN_DK),
                  chunk_spec(CHUNK), per_chunk, state_spec,
                  chunk_spec(CHUNK), chunk_spec(HGRN_DK), chunk_spec(HGRN_DK), per_chunk,
                  pl.BlockSpec((sb, CHUNK, HGRN_KW), lambda i, c: (i, c, col_v)), state_spec],
        out_specs=[out_spec, state_spec, out_spec, state_spec],
        out_shape=[jax.ShapeDtypeStruct((b, t, GDN_VW), F32), state_shape,
                   jax.ShapeDtypeStruct((b, t, HGRN_KW), F32), state_shape],
        scratch_shapes=[pltpu.VMEM((sb, GDN_HEADS, 128, 128), F32), pltpu.VMEM((sb, HGRN_HEADS, 128, 128), F32)],
        compiler_params=_cparams(("parallel", "arbitrary")),
        name="rec_scan",
    )(*gdn_parts, s0_gdn, *hgrn_parts, p3, s0t_hgrn)


def _merge_kernel(x_ref, oa_ref, ob_ref, zb_ref, oc_ref, zc_ref, g0_ref, g1_ref, g2_ref, ngb_ref, ngc_ref,
                  wa_ref, wb_ref, wc_ref, wo_ref, y_ref):
    def branch(o, w_ref, g_ref):
        gate = _sigmoid(g_ref[...].astype(F32))
        return gate * jnp.dot(o.astype(BF16), w_ref[...], preferred_element_type=F32)

    def normed(o_ref, z_ref, ng_ref):
        return jnp.concatenate(
            [_rms(o_ref[:, h * 128:(h + 1) * 128], ng_ref[...]) * _silu(z_ref[:, h * 128:(h + 1) * 128])
             for h in range(GDN_HEADS)], axis=-1)

    mixed = (branch(oa_ref[...], wa_ref, g0_ref) + branch(normed(ob_ref, zb_ref, ngb_ref), wb_ref, g1_ref)
             + branch(normed(oc_ref, zc_ref, ngc_ref), wc_ref, g2_ref))
    y_ref[...] = x_ref[...] + jnp.dot(mixed.astype(BF16), wo_ref[...], preferred_element_type=F32)


def _merge(x, oa, ob, oc, p, pg, ngb, ngc, wa, wb, wc, wo):
    n = x.shape[0]
    tm = _row_tile(n, 512)
    row = lambda w: pl.BlockSpec((tm, w), lambda i: (i, 0))
    gate = lambda j: pl.BlockSpec((tm, D_MODEL), lambda i: (i, j))
    zb_spec = pl.BlockSpec((tm, GDN_VW), lambda i: (i, COL_GDN // GDN_VW + 3))
    zc_spec = pl.BlockSpec((tm, HGRN_KW), lambda i: (i, (COL_GDN + 4 * GDN_QK) // HGRN_KW + 3))
    vec = pl.BlockSpec((1, 128), lambda i: (0, 0))
    wsp = lambda k: pl.BlockSpec((k, D_MODEL), lambda i: (0, 0))
    return pl.pallas_call(
        _merge_kernel,
        grid=(n // tm,),
        in_specs=[row(D_MODEL), row(MLA_VW), row(GDN_VW), zb_spec, row(HGRN_KW), zc_spec,
                  gate(0), gate(1), gate(2), vec, vec,
                  wsp(MLA_VW), wsp(GDN_VW), wsp(HGRN_KW), wsp(D_MODEL)],
        out_specs=row(D_MODEL),
        out_shape=jax.ShapeDtypeStruct((n, D_MODEL), F32),
        compiler_params=_cparams(("parallel",)),
        name="merge_out",
    )(x, oa, ob, p, oc, p, pg, pg, pg, ngb, ngc, wa, wb, wc, wo)


def _router_kernel(x_ref, g_ref, w_ref, b_ref, cw_ref, cwt_ref, cnt_ref):
    h = _rms(x_ref[...], g_ref[...])
    w = w_ref[...]
    h_hi, w_hi = h.astype(BF16), w.astype(BF16)
    h_lo, w_lo = (h - h_hi.astype(F32)).astype(BF16), (w - w_hi.astype(F32)).astype(BF16)
    logits = (jnp.dot(h_hi, w_hi, preferred_element_type=F32) + jnp.dot(h_hi, w_lo, preferred_element_type=F32)
              + jnp.dot(h_lo, w_hi, preferred_element_type=F32)) + b_ref[...]
    lane = lax.broadcasted_iota(jnp.int32, logits.shape, 1)
    valid = lane < N_EXPERTS
    neg = -jnp.inf
    l1 = jnp.where(valid, logits, neg)
    m1 = jnp.max(l1, axis=-1, keepdims=True)
    i1 = jnp.min(jnp.where(l1 == m1, lane, 128), axis=-1, keepdims=True)
    l2 = jnp.where(lane == i1, neg, l1)
    m2 = jnp.max(l2, axis=-1, keepdims=True)
    i2 = jnp.min(jnp.where(l2 == m2, lane, 128), axis=-1, keepdims=True)
    e2 = jnp.exp(m2 - m1)
    den = 1.0 + e2
    cw = jnp.where(lane == i1, 1.0 / den, 0.0) + jnp.where(lane == i2, e2 / den, 0.0)
    cw_ref[...] = cw
    cwt_ref[...] = cw.T
    cnt_ref[0] = jnp.sum((cw > 0.0).astype(F32), axis=0, keepdims=True)


def _router(x, g, w, b, tm):
    n = x.shape[0]
    return pl.pallas_call(
        _router_kernel,
        grid=(n // tm,),
        in_specs=[pl.BlockSpec((tm, D_MODEL), lambda i: (i, 0)),
                  pl.BlockSpec((1, D_MODEL), lambda i: (0, 0)),
                  pl.BlockSpec((D_MODEL, 128), lambda i: (0, 0)),
                  pl.BlockSpec((1, 128), lambda i: (0, 0))],
        out_specs=[pl.BlockSpec((tm, 128), lambda i: (i, 0)),
                   pl.BlockSpec((128, tm), lambda i: (0, i)),
                   pl.BlockSpec((1, 1, 128), lambda i: (i, 0, 0))],
        out_shape=[jax.ShapeDtypeStruct((n, 128), F32),
                   jax.ShapeDtypeStruct((128, n), F32),
                   jax.ShapeDtypeStruct((n // tm, 1, 128), F32)],
        compiler_params=_cparams(("parallel",)),
        name="moe_router",
    )(x, g, w, b)


def _moe_kernel(cnt_ref, x_ref, g_ref, cw_ref, cwt_ref, wg_ref, wu_ref, wd_ref, y_ref,
                h_ref, rcol_ref, rrow_ref, *, t, br):
    i, e = pl.program_id(0), pl.program_id(1)

    @pl.when(e == 0)
    def _():
        x = x_ref[...]
        h_ref[...] = _rms(x, g_ref[...]).astype(BF16)
        y_ref[...] = x
        r = lax.broadcasted_iota(jnp.int32, (t, t), 0)
        c = lax.broadcasted_iota(jnp.int32, (t, t), 1)
        on = cw_ref[...] > 0.0
        rank = jnp.dot((c < r).astype(BF16), on.astype(BF16), preferred_element_type=F32)
        rcol_ref[...] = jnp.where(on, rank.astype(jnp.int32), -1)
        on_t = cwt_ref[...] > 0.0
        rank_t = jnp.dot(on_t.astype(BF16), (r < c).astype(BF16), preferred_element_type=F32)
        rrow_ref[...] = jnp.where(on_t, rank_t.astype(jnp.int32), -1)

    count = cnt_ref[i * N_EXPERTS + e]
    sel = lax.broadcasted_iota(jnp.int32, (1, 128), 1) == e
    rank_c = jnp.sum(jnp.where(sel, rcol_ref[...], 0), axis=-1, keepdims=True)
    rank_r = rrow_ref[pl.ds(e, 1), :]
    w_r = cwt_ref[pl.ds(e, 1), :]

    def expert_block(base, rows_n):
        rows = lax.broadcasted_iota(jnp.int32, (rows_n, t), 0) + base
        pick = rows == rank_r
        xc = jnp.dot(pick.astype(BF16), h_ref[...], preferred_element_type=F32).astype(BF16)
        a = jnp.dot(xc, wg_ref[0], preferred_element_type=F32)
        b = jnp.dot(xc, wu_ref[0], preferred_element_type=F32)
        yc = jnp.dot((_silu(a) * b).astype(BF16), wd_ref[0], preferred_element_type=F32)
        w_rows = jnp.sum(jnp.where(pick, w_r, 0.0), axis=-1, keepdims=True)
        cols = lax.broadcasted_iota(jnp.int32, (t, rows_n), 1) + base
        y_ref[...] += jnp.dot((cols == rank_c).astype(BF16), (yc * w_rows).astype(BF16),
                              preferred_element_type=F32)

    lo = 0
    for size in br:
        @pl.when((count > lo) & (count <= size))
        def _(size=size):
            expert_block(0, size)
        lo = size

    @pl.when(count > br[-1])
    def _():
        def body(j, carry):
            expert_block(j * br[-1], br[-1])
            return carry

        lax.fori_loop(0, lax.div(count + (br[-1] - 1), br[-1]), body, 0)


def _moe(x, g, cw, cwt, counts, wg, wu, wd, t):
    n = x.shape[0]
    ne, _, ff = wg.shape
    br = MOE_BLOCK_ROWS
    grid_spec = pltpu.PrefetchScalarGridSpec(
        num_scalar_prefetch=1,
        grid=(n // t, ne),
        in_specs=[pl.BlockSpec((t, D_MODEL), lambda i, e, cnt: (i, 0)),
                  pl.BlockSpec((1, D_MODEL), lambda i, e, cnt: (0, 0)),
                  pl.BlockSpec((t, 128), lambda i, e, cnt: (i, 0)),
                  pl.BlockSpec((128, t), lambda i, e, cnt: (0, i)),
                  pl.BlockSpec((1, D_MODEL, ff), lambda i, e, cnt: (e, 0, 0)),
                  pl.BlockSpec((1, D_MODEL, ff), lambda i, e, cnt: (e, 0, 0)),
                  pl.BlockSpec((1, ff, D_MODEL), lambda i, e, cnt: (e, 0, 0))],
        out_specs=pl.BlockSpec((t, D_MODEL), lambda i, e, cnt: (i, 0)),
        scratch_shapes=[pltpu.VMEM((t, D_MODEL), BF16),
                        pltpu.VMEM((t, 128), jnp.int32),
                        pltpu.VMEM((128, t), jnp.int32)])
    return pl.pallas_call(
        functools.partial(_moe_kernel, t=t, br=br),
        grid_spec=grid_spec,
        out_shape=jax.ShapeDtypeStruct((n, D_MODEL), F32),
        compiler_params=_cparams(("parallel", "arbitrary")),
        name="moe",
    )(counts, x, g, cw, cwt, wg, wu, wd)


def _ffn_kernel(x_ref, g_ref, wg_ref, wu_ref, wd_ref, y_ref, h_ref):
    e = pl.program_id(1)

    @pl.when(e == 0)
    def _():
        x = x_ref[...]
        h_ref[...] = _rms(x, g_ref[...]).astype(BF16)
        y_ref[...] = x

    h = h_ref[...]
    a = jnp.dot(h, wg_ref[0], preferred_element_type=F32)
    b = jnp.dot(h, wu_ref[0], preferred_element_type=F32)
    y_ref[...] += jnp.dot((_silu(a) * b).astype(BF16), wd_ref[0], preferred_element_type=F32)


def _ffn(x, g, wg, wu, wd):
    n = x.shape[0]
    ne, _, ff = wg.shape
    tm = _row_tile(n, 1024)
    return pl.pallas_call(
        _ffn_kernel,
        grid=(n // tm, ne),
        in_specs=[pl.BlockSpec((tm, D_MODEL), lambda i, e: (i, 0)),
                  pl.BlockSpec((1, D_MODEL), lambda i, e: (0, 0)),
                  pl.BlockSpec((1, D_MODEL, ff), lambda i, e: (e, 0, 0)),
                  pl.BlockSpec((1, D_MODEL, ff), lambda i, e: (e, 0, 0)),
                  pl.BlockSpec((1, ff, D_MODEL), lambda i, e: (e, 0, 0))],
        out_specs=pl.BlockSpec((tm, D_MODEL), lambda i, e: (i, 0)),
        out_shape=jax.ShapeDtypeStruct((n, D_MODEL), F32),
        scratch_shapes=[pltpu.VMEM((tm, D_MODEL), BF16)],
        compiler_params=_cparams(("parallel", "arbitrary")),
        name="ffn",
    )(x, g, wg, wu, wd)


def _pad_lanes(x, left, total):
    return jnp.pad(x, [(0, 0)] * (x.ndim - 1) + [(left, total - left - x.shape[-1])])


def _pack_w_in(w):
    cq, ckv, kr, gq, gk, gv, gz, ga, gb, hq, hf, hi, hg, gates = jnp.split(
        w, np.cumsum(SPLIT_SIZES)[:-1].tolist(), axis=-1)
    kr_blk = _pad_lanes(kr, ROPE_LANE0, 128)
    ab_blk = _pad_lanes(jnp.concatenate([ga, gb], axis=-1), 0, 256)
    return jnp.concatenate([cq, ckv, kr_blk, ab_blk, gq, gk, gv, gz, hq, hf, hi, hg, gates], axis=-1).astype(BF16)


def _rope_tables(n_pos):
    half = MLA_ROPE // 2
    inv = ROPE_THETA ** (-jnp.arange(half, dtype=F32) / half)
    ang = jnp.arange(n_pos, dtype=F32)[:, None] * inv[None, :]
    cos, sin = jnp.cos(ang), jnp.sin(ang)
    one = jnp.ones((n_pos, MLA_NOPE), F32)
    zero = jnp.zeros((n_pos, MLA_NOPE), F32)
    tail = jnp.zeros((n_pos, HEAD_PAD - MLA_QK_HEAD), F32)
    z16 = jnp.zeros((n_pos, half), F32)
    c = jnp.concatenate([one, cos, cos, tail], axis=-1)
    s1 = jnp.concatenate([zero, -sin, z16, tail], axis=-1)
    s2 = jnp.concatenate([zero, z16, sin, tail], axis=-1)
    return c, s1, s2


def _layer_weights(l, a):
    f = {}
    f['mixer_g'] = a['mixer_norm_g'][l][None]
    f['w_in'] = _pack_w_in(a['w_in'][l])
    f['gq'] = a['mla_q_norm_g'][l][None]
    f['gkv'] = a['mla_kv_norm_g'][l][None]
    wq = a['mla_w_q_up'][l].reshape(MLA_Q_LORA, MLA_HEADS, MLA_QK_HEAD)
    f['wq'] = _pad_lanes(wq, 0, HEAD_PAD).reshape(MLA_Q_LORA, QK_W).astype(BF16)
    wkv = a['mla_w_kv_up'][l].reshape(MLA_KV_LORA, MLA_HEADS, MLA_NOPE + MLA_V)
    f['wk'] = _pad_lanes(wkv[:, :, :MLA_NOPE], 0, HEAD_PAD).reshape(MLA_KV_LORA, QK_W).astype(BF16)
    f['wkt'] = f['wk'].T
    f['wv'] = wkv[:, :, MLA_NOPE:].reshape(MLA_KV_LORA, MLA_VW).astype(BF16)
    f['wv_pad'] = _pad_lanes(wkv[:, :, MLA_NOPE:], 0, HEAD_PAD).reshape(MLA_KV_LORA, QK_W).astype(BF16)
    f['hq'] = _pad_lanes(a['mla_q_head_norm_g'][l][None], 0, HEAD_PAD)
    hk = a['mla_k_head_norm_g'][l][None]
    f['hk_nope'] = _pad_lanes(hk[:, :MLA_NOPE], 0, HEAD_PAD)
    f['hk_rope'] = _pad_lanes(hk[:, MLA_NOPE:], ROPE_LANE0, HEAD_PAD)
    f['wo_a'] = a['mla_w_o'][l].astype(BF16)
    f['conv_w'] = a['gdn_conv_w'][l]
    f['alog'] = _pad_lanes(a['gdn_a_log'][l][None], 0, 128)
    f['dt'] = _pad_lanes(a['gdn_dt_bias'][l][None], 0, 128)
    f['gdn_g'] = a['gdn_norm_g'][l][None]
    f['wo_b'] = a['gdn_w_o'][l].astype(BF16)
    f['hgrn_g'] = a['hgrn_norm_g'][l][None]
    f['wo_c'] = a['hgrn_w_o'][l].astype(BF16)
    f['w_out'] = a['w_out'][l].astype(BF16)
    f['ffn_g'] = a['ffn_norm_g'][l][None]
    if l % 2 == 0:
        wg, wu, wd = a['dense_w_gate'][l // 2], a['dense_w_up'][l // 2], a['dense_w_down'][l // 2]
        ff = wg.shape[1]
        half = ff // 2
        f['ffn'] = (jnp.moveaxis(wg.reshape(D_MODEL, 2, half), 1, 0).astype(BF16),
                    jnp.moveaxis(wu.reshape(D_MODEL, 2, half), 1, 0).astype(BF16),
                    wd.reshape(2, half, D_MODEL).astype(BF16))
        f['router'] = None
    else:
        f['ffn'] = (a['moe_w_gate'][l // 2].astype(BF16), a['moe_w_up'][l // 2].astype(BF16),
                    a['moe_w_down'][l // 2].astype(BF16))
        f['router'] = (_pad_lanes(a['moe_w_router'][l // 2], 0, 128),
                       _pad_lanes(a['moe_b_router'][l // 2][None], 0, 128))
    return f


def _trunk_layer(x, b, t, f, lb, tabs, past, layer, depth, ckv_buf):
    n = b * t
    p, p_gate = _in_proj(x, f['mixer_g'], f['w_in'])
    p3 = p.reshape(b, t, P_MAIN)

    fresh = past['ckv'] is None
    q, ckv_buf, kr = _mla_pre(p, b, t, f['gq'], f['gkv'], f['wq'], f['hq'], f['hk_rope'], tabs, fresh,
                              layer, depth, ckv_buf)
    if fresh:
        k_all, vt_all = _kv_up(ckv_buf, layer, kr, b, t, f['wk'], f['wv_pad'], f['hk_nope'])
        o_a = _attention_t(q, k_all.reshape(b, t, QK_W), vt_all)
    else:
        past_len = past['ckv'].shape[2]
        t_pad = -(-t // 128) * 128
        pad_rows = lambda a: jnp.pad(a.reshape(b, t, -1), ((0, 0), (0, t_pad - t), (0, 0)))
        kv = _kv_hist(past['ckv'], past['kr'], past_len + t_pad, 0, f['wkt'], f['wv'], f['hk_nope'],
                      layer=past['layer'])
        kt_all, v_all = _kv_hist(pad_rows(ckv_buf[layer]), pad_rows(kr), past_len + t_pad, past_len,
                                 f['wkt'], f['wv'], f['hk_nope'], into=kv)
        o_a = _attention_hist(q.reshape(b, t, QK_W), kt_all, v_all, past_len, past_len + t)

    conv8 = jnp.pad(past['conv'].reshape(b, CONV_W - 1, 3, GDN_QK).transpose(0, 2, 1, 3),
                    ((0, 0), (0, 0), (8 - (CONV_W - 1), 0), (0, 0)))
    gdn_parts = _gdn_prep(p3, f['conv_w'], conv8, f['alog'], f['dt'])
    gdn_conv = p3[:, t - (CONV_W - 1):, COL_GDN:COL_GDN + 3 * GDN_QK]

    hgrn_parts = _hgrn_prep(p3, lb)
    o_b, gdn_s, o_c, hgrn_st = _rec_scan(gdn_parts, hgrn_parts, p3, past['gdn'],
                                         jnp.swapaxes(past['hgrn'], -1, -2))
    hgrn_s = jnp.swapaxes(hgrn_st, -1, -2)

    x = _merge(x, o_a.reshape(n, -1), o_b.reshape(n, -1), o_c.reshape(n, -1), p, p_gate, f['gdn_g'],
               f['hgrn_g'], f['wo_a'], f['wo_b'], f['wo_c'], f['w_out'])

    wg, wu, wd = f['ffn']
    if f['router'] is None:
        x = _ffn(x, f['ffn_g'], wg, wu, wd)
    else:
        t_moe = _row_tile(n, MOE_TILE)
        cw, cwt, cnt = _router(x, f['ffn_g'], *f['router'], t_moe)
        counts = cnt[:, 0, :N_EXPERTS].astype(jnp.int32).reshape(-1)
        x = _moe(x, f['ffn_g'], cw, cwt, counts, wg, wu, wd, t_moe)

    new_kr = kr.reshape(b, t, HEAD_PAD)[:, :, ROPE_LANE0:ROPE_LANE0 + MLA_ROPE]
    return x, ckv_buf, (new_kr, gdn_s, gdn_conv, hgrn_s)


def kernel(x_prompt, x_sample, cache_mla_ckv, cache_mla_krope, state_gdn, state_gdn_conv, state_hgrn,
           mixer_norm_g, w_in, mla_q_norm_g, mla_w_q_up, mla_kv_norm_g, mla_w_kv_up,
           mla_q_head_norm_g, mla_k_head_norm_g, mla_w_o,
           gdn_conv_w, gdn_a_log, gdn_dt_bias, gdn_norm_g, gdn_w_o,
           hgrn_lb_logits, hgrn_norm_g, hgrn_w_o, w_out, ffn_norm_g,
           dense_w_gate, dense_w_up, dense_w_down,
           moe_w_router, moe_b_router, moe_w_gate, moe_w_up, moe_w_down):
    a = dict(mixer_norm_g=mixer_norm_g, w_in=w_in, mla_q_norm_g=mla_q_norm_g, mla_w_q_up=mla_w_q_up,
             mla_kv_norm_g=mla_kv_norm_g, mla_w_kv_up=mla_w_kv_up, mla_q_head_norm_g=mla_q_head_norm_g,
             mla_k_head_norm_g=mla_k_head_norm_g, mla_w_o=mla_w_o, gdn_conv_w=gdn_conv_w,
             gdn_a_log=gdn_a_log, gdn_dt_bias=gdn_dt_bias, gdn_norm_g=gdn_norm_g, gdn_w_o=gdn_w_o,
             hgrn_norm_g=hgrn_norm_g, hgrn_w_o=hgrn_w_o, w_out=w_out, ffn_norm_g=ffn_norm_g,
             dense_w_gate=dense_w_gate, dense_w_up=dense_w_up, dense_w_down=dense_w_down,
             moe_w_router=moe_w_router, moe_b_router=moe_b_router, moe_w_gate=moe_w_gate,
             moe_w_up=moe_w_up, moe_w_down=moe_w_down)
    depth = w_in.shape[0]
    lb_soft = jax.nn.softmax(hgrn_lb_logits.astype(F32), axis=0)
    hgrn_lb = jnp.cumsum(lb_soft, axis=0) - lb_soft[0]

    b_p, t_p = x_prompt.shape[:2]
    b_s, t_s = x_sample.shape[:2]
    past_len = cache_mla_ckv.shape[2]
    tab_all = _rope_tables(max(t_p, past_len + t_s))
    tabs_p = tuple(tb[:t_p] for tb in tab_all)
    tabs_s = tuple(tb[past_len:past_len + t_s] for tb in tab_all)

    xp = x_prompt.reshape(b_p * t_p, D_MODEL)
    xs = x_sample.reshape(b_s * t_s, D_MODEL)
    past_p = dict(ckv=None, kr=None,
                  gdn=jnp.zeros((b_p, GDN_HEADS, GDN_DK, GDN_DV), F32),
                  conv=jnp.zeros((b_p, CONV_W - 1, 3 * GDN_QK), F32),
                  hgrn=jnp.zeros((b_p, HGRN_HEADS, HGRN_DK, HGRN_DV), F32))
    st_p, st_s = [], []
    ckv_p = ckv_s = None
    for l in range(depth):
        f = _layer_weights(l, a)
        lb = hgrn_lb[l][None]
        past_s = dict(ckv=cache_mla_ckv, kr=cache_mla_krope, layer=l, gdn=state_gdn[l],
                      conv=state_gdn_conv[l], hgrn=state_hgrn[l])
        xp, ckv_p, sp = _trunk_layer(xp, b_p, t_p, f, lb, tabs_p, past_p, l, depth, ckv_p)
        xs, ckv_s, ss = _trunk_layer(xs, b_s, t_s, f, lb, tabs_s, past_s, l, depth, ckv_s)
        st_p.append(sp)
        st_s.append(ss)

    def stack(lst, i):
        return jnp.stack([s[i] for s in lst], axis=0)

    return (xp.reshape(b_p, t_p, D_MODEL), xs.reshape(b_s, t_s, D_MODEL),
            ckv_p.reshape(depth, b_p, t_p, MLA_KV_LORA),
            stack(st_p, 0), stack(st_p, 1), stack(st_p, 2), stack(st_p, 3),
            ckv_s.reshape(depth, b_s, t_s, MLA_KV_LORA),
            stack(st_s, 0), stack(st_s, 1), stack(st_s, 2), stack(st_s, 3))
```
